```python
import jax, jax.numpy as jnp
from jax import lax
import numpy as np

D_MODEL = 2048
BATCH = 4
SEQ = 2048
DEPTH = 4

RET_HEADS = 8
RET_DK = D_MODEL // RET_HEADS
RET_DV = 2 * D_MODEL // RET_HEADS
RET_QK = RET_HEADS * RET_DK
RET_V = RET_HEADS * RET_DV
RET_CHUNK = 128
ROPE_BASE = 10000.0
POOL_WINDOWS = (2, 4, 8, 16)
POOL_GROUPS = len(POOL_WINDOWS)
POOL_G = D_MODEL // POOL_GROUPS
D_FF = 5632
N_EXPERTS = 8
TOP_K = 2
D_FF_EXPERT = 1408
NORM_EPS = 1e-6
N_RET = (DEPTH + 1) // 2
N_POOL = DEPTH // 2

kernel_name = "hybrid_retention_pool_moe_trunk"


def rms_norm(x, g):
    xf = x.astype(jnp.float32)
    y = xf * lax.rsqrt(jnp.mean(xf * xf, axis=-1, keepdims=True) + NORM_EPS)
    return (y * g.astype(jnp.float32)).astype(x.dtype)


def rotary_interleaved(x, pos):
    b, s, h, dk = x.shape
    freq = 1.0 / (ROPE_BASE ** jnp.linspace(0.0, 1.0, dk // 2, dtype=jnp.float32))
    ang = pos.astype(jnp.float32)[:, None] * freq[None, :]
    cos = jnp.cos(ang)[None, :, None, :]
    sin = jnp.sin(ang)[None, :, None, :]
    xr = x.astype(jnp.float32).reshape(b, s, h, dk // 2, 2)
    x1, x2 = xr[..., 0], xr[..., 1]
    out = jnp.stack([x1 * cos - x2 * sin, x2 * cos + x1 * sin], axis=-1)
    return out.reshape(b, s, h, dk)


def chunkwise_retention(q, k, v):
    b, s, h, dk = q.shape
    dv = v.shape[-1]
    c = RET_CHUNK
    nc = s // c

    def to_chunks(t):
        return t.reshape(b, nc, c, h, t.shape[-1]).transpose(1, 0, 3, 2, 4)

    log_g = jnp.log1p(-jnp.exp2(-5.0 - jnp.arange(h, dtype=jnp.float32)))
    idx = jnp.arange(c, dtype=jnp.float32)
    rel = idx[:, None] - idx[None, :]
    decay_mask = jnp.where(rel[None] >= 0,
                           jnp.exp(jnp.maximum(rel, 0.0)[None] * log_g[:, None, None]),
                           0.0)
    q_decay = jnp.exp((idx + 1.0)[None, :] * log_g[:, None])[:, :, None]
    k_decay = jnp.exp((c - 1.0 - idx)[None, :] * log_g[:, None])[:, :, None]
    chunk_decay = jnp.exp(c * log_g)[:, None, None]

    def step(state, qkv):
        qc, kc, vc = qkv
        scores = jnp.einsum('bhid,bhjd->bhij', qc, kc) * decay_mask
        inner = jnp.einsum('bhij,bhjv->bhiv', scores, vc)
        cross = jnp.einsum('bhid,bhdv->bhiv', qc, state) * q_decay
        new_state = state * chunk_decay + jnp.einsum('bhjd,bhjv->bhdv', kc * k_decay, vc)
        return new_state, inner + cross

    state0 = jnp.zeros((b, h, dk, dv), jnp.float32)
    _, out = lax.scan(step, state0, (to_chunks(q), to_chunks(k), to_chunks(v)))
    return out.transpose(1, 0, 3, 2, 4).reshape(b, s, h, dv)


def retention_mixer(hn, w_in, norm_g, w_out):
    b, s, _ = hn.shape
    proj = hn @ w_in
    q, k, v, g = jnp.split(proj, [RET_QK, 2 * RET_QK, 2 * RET_QK + RET_V], axis=-1)
    pos = jnp.arange(s)
    q = rotary_interleaved(q.reshape(b, s, RET_HEADS, RET_DK), pos)
    k = rotary_interleaved(k.reshape(b, s, RET_HEADS, RET_DK), pos) * (RET_DK ** -0.5)
    v = v.reshape(b, s, RET_HEADS, RET_DV).astype(jnp.float32)
    o = chunkwise_retention(q, k, v)
    o = o * lax.rsqrt(jnp.mean(o * o, axis=-1, keepdims=True) + NORM_EPS)
    o = o * norm_g.astype(jnp.float32).reshape(RET_HEADS, RET_DV)
    o = o.reshape(b, s, RET_V).astype(hn.dtype) * jax.nn.silu(g)
    return o @ w_out


def pool_mixer(hn, w_groups, scale):
    b, s, _ = hn.shape
    hf = hn.astype(jnp.float32)
    cs = jnp.pad(jnp.cumsum(hf, axis=1), ((0, 0), (1, 0), (0, 0)))
    t = jnp.arange(s)
    outs = []
    for gi, w in enumerate(POOL_WINDOWS):
        lo, hi = gi * POOL_G, (gi + 1) * POOL_G
        c = cs[:, :, lo:hi]
        lag = jnp.pad(c[:, :s - w + 1], ((0, 0), (w - 1, 0), (0, 0)))
        count = jnp.minimum(t + 1, w).astype(jnp.float32)[None, :, None]
        mix = ((c[:, 1:] - lag) / count - hf[:, :, lo:hi]).astype(hn.dtype)
        outs.append(mix @ w_groups[gi])
    return jnp.concatenate(outs, axis=-1) * scale


def swiglu(h, w_gu, w_down):
    a, u = jnp.split(h @ w_gu, 2, axis=-1)
    return (jax.nn.silu(a) * u) @ w_down


def moe_swiglu(hn, router, w_gu, w_down):
    b, s, d = hn.shape
    tok = hn.reshape(b * s, d)
    logits = (tok @ router).astype(jnp.float32)
    top_v, top_i = lax.top_k(logits, TOP_K)
    gates = jax.nn.softmax(top_v, axis=-1)
    combine = jnp.sum(jax.nn.one_hot(top_i, N_EXPERTS, dtype=jnp.float32) * gates[..., None], axis=1)
    combine = combine.astype(tok.dtype)
    out = jnp.zeros_like(tok)
    for e in range(N_EXPERTS):
        out = out + combine[:, e:e + 1] * swiglu(tok, w_gu[e], w_down[e])
    return out.reshape(b, s, d)


def setup_inputs(seed: int = 0) -> dict:
    key = jax.random.key(seed)
    ks = jax.random.split(key, 16)
    f32 = jnp.float32
    D = D_MODEL

    def nrm(k, shape, fan_in):
        return jax.random.normal(k, shape, f32) * (fan_in ** -0.5)

    def gain(k, shape):
        return 1.0 + 0.02 * jax.random.normal(k, shape, f32)

    return {
        "x": jax.random.normal(ks[0], (BATCH, SEQ, D), f32),
        "ret_w_in": nrm(ks[1], (N_RET, D, 2 * RET_QK + 2 * RET_V), D),
        "ret_norm_g": gain(ks[2], (N_RET, RET_V)),
        "ret_w_out": nrm(ks[3], (N_RET, RET_V, D), RET_V),
        "pool_w": nrm(ks[4], (N_POOL, POOL_GROUPS, POOL_G, POOL_G), POOL_G),
        "pool_scale": gain(ks[5], (N_POOL, D)),
        "ffn_w_gu": nrm(ks[6], (N_RET, D, 2 * D_FF), D),
        "ffn_w_down": nrm(ks[7], (N_RET, D_FF, D), D_FF),
        "moe_router": nrm(ks[8], (N_POOL, D, N_EXPERTS), D),
        "moe_w_gu": nrm(ks[9], (N_POOL, N_EXPERTS, D, 2 * D_FF_EXPERT), D),
        "moe_w_down": nrm(ks[10], (N_POOL, N_EXPERTS, D_FF_EXPERT, D), D_FF_EXPERT),
        "norm_mix_g": gain(ks[11], (DEPTH, D)),
        "norm_ffn_g": gain(ks[12], (DEPTH, D)),
        "norm_final_g": gain(ks[13], (D,)),
    }


def reference(x, ret_w_in, ret_norm_g, ret_w_out, pool_w, pool_scale, ffn_w_gu, ffn_w_down,
              moe_router, moe_w_gu, moe_w_down, norm_mix_g, norm_ffn_g, norm_final_g):
    for i in range(DEPTH):
        j = i // 2
        hn = rms_norm(x, norm_mix_g[i])
        if i % 2 == 0:
            x = x + retention_mixer(hn, ret_w_in[j], ret_norm_g[j], ret_w_out[j])
        else:
            x = x + pool_mixer(hn, pool_w[j], pool_scale[j])
        hn = rms_norm(x, norm_ffn_g[i])
        if i % 2 == 0:
            x = x + swiglu(hn, ffn_w_gu[j], ffn_w_down[j])
        else:
            x = x + moe_swiglu(hn, moe_router[j], moe_w_gu[j], moe_w_down[j])
    return rms_norm(x, norm_final_g)
```

```python
import functools

import jax
import jax.numpy as jnp
from jax import lax
from jax.experimental import pallas as pl
from jax.experimental.pallas import tpu as pltpu

BF16 = jnp.bfloat16
F32 = jnp.float32

D_MODEL = 2048
DEPTH = 4
RET_HEADS = 8
RET_DK = D_MODEL // RET_HEADS
RET_DV = 2 * D_MODEL // RET_HEADS
RET_QK = RET_HEADS * RET_DK
RET_V = RET_HEADS * RET_DV
RET_CHUNK = 128
ROPE_BASE = 10000.0
POOL_WINDOWS = (2, 4, 8, 16)
POOL_G = D_MODEL // len(POOL_WINDOWS)
POOL_HALO = 16
D_FF = 5632
N_EXPERTS = 8
D_FF_EXPERT = 1408
NORM_EPS = 1e-6
LANES = 128


def _rms(x, g):
    ms = jnp.mean(x * x, axis=-1, keepdims=True)
    return x * lax.rsqrt(ms + NORM_EPS) * g


def _silu(a):
    return a / (1.0 + jnp.exp(-a))


def _dot(a, b):
    return jnp.dot(a, b, preferred_element_type=F32)


def _rmsnorm_kernel(x_ref, g_ref, o_ref):
    o_ref[...] = _rms(x_ref[...], g_ref[...]).astype(o_ref.dtype)


def rmsnorm(x, g, out_dtype, tm=512):
    n, d = x.shape
    return pl.pallas_call(
        _rmsnorm_kernel,
        grid=(n // tm,),
        in_specs=[pl.BlockSpec((tm, d), lambda i: (i, 0)),
                  pl.BlockSpec((1, d), lambda i: (0, 0))],
        out_specs=pl.BlockSpec((tm, d), lambda i: (i, 0)),
        out_shape=jax.ShapeDtypeStruct((n, d), out_dtype),
        name="rmsnorm",
    )(x, g.reshape(1, d))


def _mm_kernel(x_ref, w_ref, o_ref):
    o_ref[...] = _dot(x_ref[...], w_ref[...]).astype(o_ref.dtype)


def matmul(x, w, out_dtype=BF16, tm=1024, tn=1024):
    n, k = x.shape
    f = w.shape[1]
    tm = min(tm, n)
    return pl.pallas_call(
        _mm_kernel,
        grid=(f // tn, n // tm),
        in_specs=[pl.BlockSpec((tm, k), lambda j, i: (i, 0)),
                  pl.BlockSpec((k, tn), lambda j, i: (0, j))],
        out_specs=pl.BlockSpec((tm, tn), lambda j, i: (i, j)),
        out_shape=jax.ShapeDtypeStruct((n, f), out_dtype),
        name="matmul",
    )(x, w)


def _swiglu_kernel(x_ref, wa_ref, wu_ref, o_ref):
    x = x_ref[...]
    a = _dot(x, wa_ref[...])
    u = _dot(x, wu_ref[...])
    o_ref[...] = (_silu(a) * u).astype(o_ref.dtype)


def swiglu_up(x, w_gu, tm=1024, tn=512):
    n, k = x.shape
    f = w_gu.shape[1] // 2
    tm = min(tm, n)
    nj = f // tn
    return pl.pallas_call(
        _swiglu_kernel,
        grid=(nj, n // tm),
        in_specs=[pl.BlockSpec((tm, k), lambda j, i: (i, 0)),
                  pl.BlockSpec((k, tn), lambda j, i: (0, j)),
                  pl.BlockSpec((k, tn), lambda j, i: (0, nj + j))],
        out_specs=pl.BlockSpec((tm, tn), lambda j, i: (i, j)),
        out_shape=jax.ShapeDtypeStruct((n, f), BF16),
        name="swiglu_up",
    )(x, w_gu, w_gu)


def _moe_up_kernel(x_ref, wa_ref, wu_ref, c_ref, o_ref):
    e = pl.program_id(0)
    x = x_ref[...]
    a = _dot(x, wa_ref[...])
    u = _dot(x, wu_ref[...])
    comb = c_ref[...]
    lane = lax.broadcasted_iota(jnp.int32, comb.shape, 1)
    scale = jnp.sum(jnp.where(lane == e, comb, 0.0), axis=-1, keepdims=True)
    o_ref[...] = (_silu(a) * u * scale).astype(o_ref.dtype)


def moe_up_dense(x, w_gu, comb, tm=1024):
    n, k = x.shape
    ne, _, f2 = w_gu.shape
    f = f2 // 2
    tm = min(tm, n)
    return pl.pallas_call(
        _moe_up_kernel,
        grid=(ne, n // tm),
        in_specs=[pl.BlockSpec((tm, k), lambda e, i: (i, 0)),
                  pl.BlockSpec((None, k, f), lambda e, i: (e, 0, 0)),
                  pl.BlockSpec((None, k, f), lambda e, i: (e, 0, 1)),
                  pl.BlockSpec((tm, LANES), lambda e, i: (i, 0))],
        out_specs=pl.BlockSpec((tm, f), lambda e, i: (i, e)),
        out_shape=jax.ShapeDtypeStruct((n, ne * f), BF16),
        name="moe_up_dense",
    )(x, w_gu, w_gu, comb)


def _down_kernel(x_ref, w_ref, r_ref, g_ref, *refs, nk, emit_x, emit_norm):
    acc_ref = refs[-1]
    k = pl.program_id(1)

    @pl.when(k == 0)
    def _():
        acc_ref[...] = r_ref[...]

    acc_ref[...] += _dot(x_ref[...], w_ref[...])

    @pl.when(k == nk - 1)
    def _():
        xn = acc_ref[...]
        pos = 0
        if emit_x:
            refs[pos][...] = xn
            pos += 1
        if emit_norm:
            refs[pos][...] = _rms(xn, g_ref[...]).astype(refs[pos].dtype)


def down_residual(h, w, resid, g=None, *, emit_x=True, norm_dtype=None, tm=512, tk=1024):
    n, kk = h.shape
    d = w.shape[1]
    tm = min(tm, n)
    nk = kk // tk
    emit_norm = norm_dtype is not None
    if g is None:
        g = jnp.ones((d,), F32)
    out_shape, out_specs = [], []
    if emit_x:
        out_shape.append(jax.ShapeDtypeStruct((n, d), F32))
        out_specs.append(pl.BlockSpec((tm, d), lambda i, k: (i, 0)))
    if emit_norm:
        out_shape.append(jax.ShapeDtypeStruct((n, d), norm_dtype))
        out_specs.append(pl.BlockSpec((tm, d), lambda i, k: (i, 0)))
    outs = pl.pallas_call(
        functools.partial(_down_kernel, nk=nk, emit_x=emit_x, emit_norm=emit_norm),
        grid=(n // tm, nk),
        in_specs=[pl.BlockSpec((tm, tk), lambda i, k: (i, k)),
                  pl.BlockSpec((tk, d), lambda i, k: (k, 0)),
                  pl.BlockSpec((tm, d), lambda i, k: (i, 0)),
                  pl.BlockSpec((1, d), lambda i, k: (0, 0))],
        out_specs=out_specs,
        out_shape=out_shape,
        scratch_shapes=[pltpu.VMEM((tm, d), F32)],
        compiler_params=pltpu.CompilerParams(dimension_semantics=("parallel", "arbitrary")),
        name="down_residual",
    )(h, w, resid, g.reshape(1, d))
    return outs


def _retention_kernel(lg_ref, q_ref, k_ref, v_ref, g_ref, cos_ref, sin_ref, gain_ref,
                      o_ref, state_ref, *, n_chunks):
    c = RET_CHUNK
    lg = lg_ref[pl.program_id(1)]
    ii = lax.broadcasted_iota(jnp.int32, (c, c), 0)
    jj = lax.broadcasted_iota(jnp.int32, (c, c), 1)
    rel = (ii - jj).astype(F32)
    mask = jnp.where(rel >= 0.0, jnp.exp(jnp.maximum(rel, 0.0) * lg), 0.0)
    idx = lax.broadcasted_iota(jnp.int32, (c, 1), 0).astype(F32)
    q_decay = jnp.exp((idx + 1.0) * lg)
    k_decay = jnp.exp((c - 1.0 - idx) * lg)
    chunk_decay = jnp.exp(jnp.full((1, 1), float(c), F32) * lg)
    lane = lax.broadcasted_iota(jnp.int32, (c, RET_DK), 1)
    even = (lane & 1) == 0

    def rotary(x, cos, sin):
        partner = jnp.where(even, pltpu.roll(x, RET_DK - 1, 1), pltpu.roll(x, 1, 1))
        return x * cos + partner * sin

    state_ref[...] = jnp.zeros_like(state_ref)

    def body(ci, carry):
        r = pl.multiple_of(ci * c, c)
        cos = cos_ref[pl.ds(r, c), :]
        sin = sin_ref[pl.ds(r, c), :]
        q = rotary(q_ref[pl.ds(r, c), :].astype(F32), cos, sin)
        k = rotary(k_ref[pl.ds(r, c), :].astype(F32), cos, sin) * (RET_DK ** -0.5)
        v = v_ref[pl.ds(r, c), :]
        qb = q.astype(BF16)
        scores = lax.dot_general(qb, k.astype(BF16), (((1,), (1,)), ((), ())),
                                 preferred_element_type=F32) * mask
        inner = _dot(scores.astype(BF16), v)
        state = state_ref[...]
        cross = _dot(qb, state.astype(BF16)) * q_decay
        kd = (k * k_decay).astype(BF16)
        upd = lax.dot_general(kd, v, (((0,), (0,)), ((), ())), preferred_element_type=F32)
        state_ref[...] = state * chunk_decay + upd
        o = inner + cross
        ms = jnp.mean(o * o, axis=-1, keepdims=True)
        o = o * lax.rsqrt(ms + NORM_EPS) * gain_ref[...]
        gate = g_ref[pl.ds(r, c), :].astype(F32)
        o_ref[pl.ds(r, c), :] = (o * _silu(gate)).astype(o_ref.dtype)
        return carry

    lax.fori_loop(0, n_chunks, body, 0)


def retention(proj, norm_g, batch, seq):
    n = batch * seq
    h = RET_HEADS
    log_g = jnp.log1p(-jnp.exp2(-5.0 - jnp.arange(h, dtype=F32)))
    freq = 1.0 / (ROPE_BASE ** jnp.linspace(0.0, 1.0, RET_DK // 2, dtype=F32))
    ang = jnp.arange(seq).astype(F32)[:, None] * freq[None, :]
    cos = jnp.repeat(jnp.cos(ang), 2, axis=-1)
    sin = jnp.stack([-jnp.sin(ang), jnp.sin(ang)], axis=-1).reshape(seq, RET_DK)
    kq = RET_QK // RET_DK
    kv = 2 * RET_QK // RET_DV
    kg = (2 * RET_QK + RET_V) // RET_DV
    return pl.pallas_call(
        functools.partial(_retention_kernel, n_chunks=seq // RET_CHUNK),
        grid=(batch, h),
        in_specs=[pl.BlockSpec(memory_space=pltpu.SMEM),
                  pl.BlockSpec((seq, RET_DK), lambda b, hh: (b, hh)),
                  pl.BlockSpec((seq, RET_DK), lambda b, hh: (b, kq + hh)),
                  pl.BlockSpec((seq, RET_DV), lambda b, hh: (b, kv + hh)),
                  pl.BlockSpec((seq, RET_DV), lambda b, hh: (b, kg + hh)),
                  pl.BlockSpec((seq, RET_DK), lambda b, hh: (0, 0)),
                  pl.BlockSpec((seq, RET_DK), lambda b, hh: (0, 0)),
                  pl.BlockSpec((1, RET_DV), lambda b, hh: (0, hh))],
        out_specs=pl.BlockSpec((seq, RET_DV), lambda b, hh: (b, hh)),
        out_shape=jax.ShapeDtypeStruct((n, RET_V), BF16),
        scratch_shapes=[pltpu.VMEM((RET_DK, RET_DV), F32)],
        name="retention",
    )(log_g, proj, proj, proj, proj, cos, sin, norm_g.reshape(1, RET_V))


def _pool_kernel(x_ref, gm_ref, pw_ref, sc_ref, gf_ref, rt_ref,
                 xo_ref, hn_ref, comb_ref, hbuf, *, tiles_per_seq):
    tm = x_ref.shape[0]
    ti = pl.program_id(0) % tiles_per_seq
    x = x_ref[...]
    hn = _rms(x, gm_ref[...])

    @pl.when(ti == 0)
    def _():
        hbuf[0:POOL_HALO, :] = jnp.zeros((POOL_HALO, D_MODEL), F32)

    @pl.when(ti != 0)
    def _():
        hbuf[0:POOL_HALO, :] = hbuf[tm:tm + POOL_HALO, :]

    hbuf[POOL_HALO:POOL_HALO + tm, :] = hn

    t = ti * tm + lax.broadcasted_iota(jnp.int32, (tm, 1), 0)
    for gi, w in enumerate(POOL_WINDOWS):
        lo, hi = gi * POOL_G, (gi + 1) * POOL_G
        cur = hn[:, lo:hi]
        acc = cur
        for d in range(1, w):
            acc = acc + hbuf[POOL_HALO - d:POOL_HALO - d + tm, lo:hi]
        inv = 1.0 / jnp.minimum(t + 1, w).astype(F32)
        mix = (acc * inv - cur).astype(BF16)
        o = _dot(mix, pw_ref[gi]) * sc_ref[:, lo:hi]
        xo_ref[:, lo:hi] = x[:, lo:hi] + o

    hn2 = _rms(xo_ref[...], gf_ref[...])
    hn_ref[...] = hn2.astype(hn_ref.dtype)

    logits = jnp.dot(hn2, rt_ref[...], preferred_element_type=F32, precision=lax.Precision.HIGHEST)
    lane = lax.broadcasted_iota(jnp.int32, logits.shape, 1)
    neg = jnp.float32(-jnp.inf)
    lg = jnp.where(lane < N_EXPERTS, logits, neg)
    m1 = jnp.max(lg, axis=-1, keepdims=True)
    i1 = jnp.min(jnp.where(lg == m1, lane, LANES), axis=-1, keepdims=True)
    lg2 = jnp.where(lane == i1, neg, lg)
    m2 = jnp.max(lg2, axis=-1, keepdims=True)
    i2 = jnp.min(jnp.where(lg2 == m2, lane, LANES), axis=-1, keepdims=True)
    e2 = jnp.exp(m2 - m1)
    g1 = 1.0 / (1.0 + e2)
    g2 = e2 * g1
    comb_ref[...] = jnp.where(lane == i1, g1, 0.0) + jnp.where(lane == i2, g2, 0.0)


def pool_router(x, g_mix, pool_w, pool_scale, g_ffn, router, seq, tm=512):
    n, d = x.shape
    tm = min(tm, seq)
    router_p = jnp.zeros((d, LANES), F32).at[:, :N_EXPERTS].set(router)
    return pl.pallas_call(
        functools.partial(_pool_kernel, tiles_per_seq=seq // tm),
        grid=(n // tm,),
        in_specs=[pl.BlockSpec((tm, d), lambda i: (i, 0)),
                  pl.BlockSpec((1, d), lambda i: (0, 0)),
                  pl.BlockSpec((len(POOL_WINDOWS), POOL_G, POOL_G), lambda i: (0, 0, 0)),
                  pl.BlockSpec((1, d), lambda i: (0, 0)),
                  pl.BlockSpec((1, d), lambda i: (0, 0)),
                  pl.BlockSpec((d, LANES), lambda i: (0, 0))],
        out_specs=[pl.BlockSpec((tm, d), lambda i: (i, 0)),
                   pl.BlockSpec((tm, d), lambda i: (i, 0)),
                   pl.BlockSpec((tm, LANES), lambda i: (i, 0))],
        out_shape=[jax.ShapeDtypeStruct((n, d), F32),
                   jax.ShapeDtypeStruct((n, d), BF16),
                   jax.ShapeDtypeStruct((n, LANES), F32)],
        scratch_shapes=[pltpu.VMEM((POOL_HALO + tm, d), F32)],
        compiler_params=pltpu.CompilerParams(dimension_semantics=("arbitrary",)),
        name="pool_router",
    )(x, g_mix.reshape(1, d), pool_w, pool_scale.reshape(1, d), g_ffn.reshape(1, d), router_p)


def kernel(x, ret_w_in, ret_norm_g, ret_w_out, pool_w, pool_scale, ffn_w_gu, ffn_w_down,
           moe_router, moe_w_gu, moe_w_down, norm_mix_g, norm_ffn_g, norm_final_g):
    batch, seq, d = x.shape
    n = batch * seq
    xf = x.reshape(n, d)
    hn = rmsnorm(xf, norm_mix_g[0], BF16)
    for i in range(DEPTH):
        j = i // 2
        last = i == DEPTH - 1
        if i % 2 == 0:
            proj = matmul(hn, ret_w_in[j].astype(BF16))
            y = retention(proj, ret_norm_g[j], batch, seq)
            xf, hn = down_residual(y, ret_w_out[j].astype(BF16), xf, norm_ffn_g[i],
                                   norm_dtype=BF16, tk=1024)
            hid = swiglu_up(hn, ffn_w_gu[j].astype(BF16))
            (xf,) = down_residual(hid, ffn_w_down[j].astype(BF16), xf, tk=D_FF_EXPERT)
        else:
            xf, hn, comb = pool_router(xf, norm_mix_g[i], pool_w[j].astype(BF16), pool_scale[j],
                                       norm_ffn_g[i], moe_router[j], seq)
            hid = moe_up_dense(hn, moe_w_gu[j].astype(BF16), comb)
            w_down = moe_w_down[j].astype(BF16).reshape(N_EXPERTS * D_FF_EXPERT, d)
            if last:
                (out,) = down_residual(hid, w_down, xf, norm_final_g, emit_x=False,
                                       norm_dtype=F32, tk=D_FF_EXPERT)
            else:
                xf, hn = down_residual(hid, w_down, xf, norm_mix_g[i + 1],
                                       norm_dtype=BF16, tk=D_FF_EXPERT)
    return out.reshape(batch, seq, d)
```

```python
import functools

import jax
import jax.numpy as jnp
from jax import lax
from jax.experimental import pallas as pl
from jax.experimental.pallas import tpu as pltpu

BF16 = jnp.bfloat16
F32 = jnp.float32

D_MODEL = 2048
DEPTH = 4
RET_HEADS = 8
RET_DK = D_MODEL // RET_HEADS
RET_DV = 2 * D_MODEL // RET_HEADS
RET_QK = RET_HEADS * RET_DK
RET_V = RET_HEADS * RET_DV
RET_CHUNK = 128
ROPE_BASE = 10000.0
POOL_WINDOWS = (2, 4, 8, 16)
POOL_G = D_MODEL // len(POOL_WINDOWS)
POOL_HALO = 16
D_FF = 5632
N_EXPERTS = 8
TOP_K = 2
D_FF_EXPERT = 1408
NORM_EPS = 1e-6
LANES = 128
SUBLANES = 8
MOE_TILE = 256
META_E1, META_E2, META_R1, META_R2, META_G1, META_G2 = range(6)


def _rms(x, g):
    ms = jnp.mean(x * x, axis=-1, keepdims=True)
    return x * lax.rsqrt(ms + NORM_EPS) * g


def _silu(a):
    return a / (1.0 + jnp.exp(-a))


def _dot(a, b):
    return jnp.dot(a, b, preferred_element_type=F32)


def _rmsnorm_kernel(x_ref, g_ref, o_ref):
    o_ref[...] = _rms(x_ref[...], g_ref[...]).astype(o_ref.dtype)


def rmsnorm(x, g, out_dtype, tm=512):
    n, d = x.shape
    return pl.pallas_call(
        _rmsnorm_kernel,
        grid=(n // tm,),
        in_specs=[pl.BlockSpec((tm, d), lambda i: (i, 0)),
                  pl.BlockSpec((1, d), lambda i: (0, 0))],
        out_specs=pl.BlockSpec((tm, d), lambda i: (i, 0)),
        out_shape=jax.ShapeDtypeStruct((n, d), out_dtype),
        name="rmsnorm",
    )(x, g.reshape(1, d))


def _mm_kernel(x_ref, w_ref, o_ref, wb_ref):
    @pl.when(pl.program_id(1) == 0)
    def _():
        wb_ref[...] = w_ref[...].astype(BF16)

    o_ref[...] = _dot(x_ref[...], wb_ref[...]).astype(o_ref.dtype)


def matmul(x, w, layer, out_dtype=BF16, tm=1024, tn=1024):
    n, k = x.shape
    f = w.shape[2]
    tm = min(tm, n)
    return pl.pallas_call(
        _mm_kernel,
        grid=(f // tn, n // tm),
        in_specs=[pl.BlockSpec((tm, k), lambda j, i: (i, 0)),
                  pl.BlockSpec((None, k, tn), lambda j, i: (layer, 0, j))],
        out_specs=pl.BlockSpec((tm, tn), lambda j, i: (i, j)),
        out_shape=jax.ShapeDtypeStruct((n, f), out_dtype),
        scratch_shapes=[pltpu.VMEM((k, tn), BF16)],
        compiler_params=pltpu.CompilerParams(dimension_semantics=("arbitrary", "arbitrary")),
        name="matmul",
    )(x, w)


def _swiglu_kernel(x_ref, wa_ref, wu_ref, o_ref, wab_ref, wub_ref):
    @pl.when(pl.program_id(1) == 0)
    def _():
        wab_ref[...] = wa_ref[...].astype(BF16)
        wub_ref[...] = wu_ref[...].astype(BF16)

    x = x_ref[...]
    a = _dot(x, wab_ref[...])
    u = _dot(x, wub_ref[...])
    o_ref[...] = (_silu(a) * u).astype(o_ref.dtype)


def swiglu_up(x, w_gu, layer, tm=1024, tn=512):
    n, k = x.shape
    f = w_gu.shape[2] // 2
    tm = min(tm, n)
    nj = f // tn
    return pl.pallas_call(
        _swiglu_kernel,
        grid=(nj, n // tm),
        in_specs=[pl.BlockSpec((tm, k), lambda j, i: (i, 0)),
                  pl.BlockSpec((None, k, tn), lambda j, i: (layer, 0, j)),
                  pl.BlockSpec((None, k, tn), lambda j, i: (layer, 0, nj + j))],
        out_specs=pl.BlockSpec((tm, tn), lambda j, i: (i, j)),
        out_shape=jax.ShapeDtypeStruct((n, f), BF16),
        scratch_shapes=[pltpu.VMEM((k, tn), BF16), pltpu.VMEM((k, tn), BF16)],
        compiler_params=pltpu.CompilerParams(dimension_semantics=("arbitrary", "arbitrary")),
        name="swiglu_up",
    )(x, w_gu, w_gu)


def _emit_residual_outputs(xn, g_ref, refs, emit_x, emit_norm):
    pos = 0
    if emit_x:
        refs[pos][...] = xn
        pos += 1
    if emit_norm:
        refs[pos][...] = _rms(xn, g_ref[...]).astype(refs[pos].dtype)


def _residual_out_shapes(n, d, tm, emit_x, norm_dtype, index_map):
    out_shape, out_specs = [], []
    if emit_x:
        out_shape.append(jax.ShapeDtypeStruct((n, d), F32))
        out_specs.append(pl.BlockSpec((tm, d), index_map))
    if norm_dtype is not None:
        out_shape.append(jax.ShapeDtypeStruct((n, d), norm_dtype))
        out_specs.append(pl.BlockSpec((tm, d), index_map))
    return out_shape, out_specs


def _down_kernel(x_ref, w_ref, r_ref, g_ref, *refs, nk, emit_x, emit_norm):
    acc_ref = refs[-1]
    k = pl.program_id(1)

    @pl.when(k == 0)
    def _():
        acc_ref[...] = r_ref[...]

    acc_ref[...] += _dot(x_ref[...], w_ref[...])

    @pl.when(k == nk - 1)
    def _():
        _emit_residual_outputs(acc_ref[...], g_ref, refs, emit_x, emit_norm)


def down_residual(h, w, layer, resid, g=None, *, emit_x=True, norm_dtype=None, tm=512, tk=1024):
    n, kk = h.shape
    d = w.shape[2]
    tm = min(tm, n)
    nk = kk // tk
    if g is None:
        g = jnp.ones((d,), F32)
    out_shape, out_specs = _residual_out_shapes(n, d, tm, emit_x, norm_dtype, lambda i, k: (i, 0))
    return pl.pallas_call(
        functools.partial(_down_kernel, nk=nk, emit_x=emit_x, emit_norm=norm_dtype is not None),
        grid=(n // tm, nk),
        in_specs=[pl.BlockSpec((tm, tk), lambda i, k: (i, k)),
                  pl.BlockSpec((None, tk, d), lambda i, k: (layer, k, 0)),
                  pl.BlockSpec((tm, d), lambda i, k: (i, 0)),
                  pl.BlockSpec((1, d), lambda i, k: (0, 0))],
        out_specs=out_specs,
        out_shape=out_shape,
        scratch_shapes=[pltpu.VMEM((tm, d), F32)],
        compiler_params=pltpu.CompilerParams(dimension_semantics=("parallel", "arbitrary")),
        name="down_residual",
    )(h, w, resid, g.reshape(1, d))


def _retention_kernel(lg_ref, q_ref, k_ref, v_ref, g_ref, cos_ref, sin_ref, gain_ref,
                      o_ref, state_ref, *, n_chunks):
    c = RET_CHUNK
    lg = lg_ref[pl.program_id(1)]
    ii = lax.broadcasted_iota(jnp.int32, (c, c), 0)
    jj = lax.broadcasted_iota(jnp.int32, (c, c), 1)
    rel = (ii - jj).astype(F32)
    mask = jnp.where(rel >= 0.0, jnp.exp(jnp.maximum(rel, 0.0) * lg), 0.0)
    idx = lax.broadcasted_iota(jnp.int32, (c, 1), 0).astype(F32)
    q_decay = jnp.exp((idx + 1.0) * lg)
    k_decay = jnp.exp((c - 1.0 - idx) * lg)
    chunk_decay = jnp.exp(jnp.full((1, 1), float(c), F32) * lg)
    lane = lax.broadcasted_iota(jnp.int32, (c, RET_DK), 1)
    even = (lane & 1) == 0

    def rotary(x, cos, sin):
        partner = jnp.where(even, pltpu.roll(x, RET_DK - 1, 1), pltpu.roll(x, 1, 1))
        return x * cos + partner * sin

    state_ref[...] = jnp.zeros_like(state_ref)

    def body(ci, carry):
        r = pl.multiple_of(ci * c, c)
        cos = cos_ref[pl.ds(r, c), :]
        sin = sin_ref[pl.ds(r, c), :]
        q = rotary(q_ref[pl.ds(r, c), :].astype(F32), cos, sin)
        k = rotary(k_ref[pl.ds(r, c), :].astype(F32), cos, sin) * (RET_DK ** -0.5)
        v = v_ref[pl.ds(r, c), :]
        qb = q.astype(BF16)
        scores = lax.dot_general(qb, k.astype(BF16), (((1,), (1,)), ((), ())),
                                 preferred_element_type=F32) * mask
        inner = _dot(scores.astype(BF16), v)
        state = state_ref[...]
        cross = _dot(qb, state.astype(BF16)) * q_decay
        kd = (k * k_decay).astype(BF16)
        upd = lax.dot_general(kd, v, (((0,), (0,)), ((), ())), preferred_element_type=F32)
        state_ref[...] = state * chunk_decay + upd
        o = inner + cross
        ms = jnp.mean(o * o, axis=-1, keepdims=True)
        o = o * lax.rsqrt(ms + NORM_EPS) * gain_ref[...]
        gate = g_ref[pl.ds(r, c), :].astype(F32)
        o_ref[pl.ds(r, c), :] = (o * _silu(gate)).astype(o_ref.dtype)
        return carry

    lax.fori_loop(0, n_chunks, body, 0)


def retention(proj, norm_g, layer, batch, seq):
    n = batch * seq
    h = RET_HEADS
    log_g = jnp.log1p(-jnp.exp2(-5.0 - jnp.arange(h, dtype=F32)))
    freq = 1.0 / (ROPE_BASE ** jnp.linspace(0.0, 1.0, RET_DK // 2, dtype=F32))
    ang = jnp.arange(seq).astype(F32)[:, None] * freq[None, :]
    cos = jnp.repeat(jnp.cos(ang), 2, axis=-1)
    sin = jnp.stack([-jnp.sin(ang), jnp.sin(ang)], axis=-1).reshape(seq, RET_DK)
    kq = RET_QK // RET_DK
    kv = 2 * RET_QK // RET_DV
    kg = (2 * RET_QK + RET_V) // RET_DV
    return pl.pallas_call(
        functools.partial(_retention_kernel, n_chunks=seq // RET_CHUNK),
        grid=(batch, h),
        in_specs=[pl.BlockSpec(memory_space=pltpu.SMEM),
                  pl.BlockSpec((seq, RET_DK), lambda b, hh: (b, hh)),
                  pl.BlockSpec((seq, RET_DK), lambda b, hh: (b, kq + hh)),
                  pl.BlockSpec((seq, RET_DV), lambda b, hh: (b, kv + hh)),
                  pl.BlockSpec((seq, RET_DV), lambda b, hh: (b, kg + hh)),
                  pl.BlockSpec((seq, RET_DK), lambda b, hh: (0, 0)),
                  pl.BlockSpec((seq, RET_DK), lambda b, hh: (0, 0)),
                  pl.BlockSpec((None, 1, RET_DV), lambda b, hh: (layer, 0, hh))],
        out_specs=pl.BlockSpec((seq, RET_DV), lambda b, hh: (b, hh)),
        out_shape=jax.ShapeDtypeStruct((n, RET_V), BF16),
        scratch_shapes=[pltpu.VMEM((RET_DK, RET_DV), F32)],
        name="retention",
    )(log_g, proj, proj, proj, proj, cos, sin, norm_g.reshape(norm_g.shape[0], 1, RET_V))


def _pool_kernel(x_ref, gm_ref, pw_ref, sc_ref, gf_ref, rt_ref,
                 xo_ref, hn_ref, meta_ref, cnt_out_ref, hbuf, cnt_ref, *, tiles_per_seq):
    tm = x_ref.shape[0]
    step = pl.program_id(0)
    ti = step % tiles_per_seq
    x = x_ref[...]
    hn = _rms(x, gm_ref[...])

    @pl.when(ti == 0)
    def _():
        hbuf[0:POOL_HALO, :] = jnp.zeros((POOL_HALO, D_MODEL), F32)

    @pl.when(ti != 0)
    def _():
        hbuf[0:POOL_HALO, :] = hbuf[tm:tm + POOL_HALO, :]

    hbuf[POOL_HALO:POOL_HALO + tm, :] = hn

    t = ti * tm + lax.broadcasted_iota(jnp.int32, (tm, 1), 0)
    for gi, w in enumerate(POOL_WINDOWS):
        lo, hi = gi * POOL_G, (gi + 1) * POOL_G
        cur = hn[:, lo:hi]
        acc = cur
        for d in range(1, w):
            acc = acc + hbuf[POOL_HALO - d:POOL_HALO - d + tm, lo:hi]
        inv = 1.0 / jnp.minimum(t + 1, w).astype(F32)
        mix = (acc * inv - cur).astype(BF16)
        o = _dot(mix, pw_ref[gi]) * sc_ref[:, lo:hi]
        xo_ref[:, lo:hi] = x[:, lo:hi] + o

    hn2 = _rms(xo_ref[...], gf_ref[...])
    hn_ref[...] = hn2

    logits = jnp.dot(hn2, rt_ref[...], preferred_element_type=F32, precision=lax.Precision.HIGHEST)
    lane = lax.broadcasted_iota(jnp.int32, logits.shape, 1)
    neg = jnp.float32(-jnp.inf)
    lg = jnp.where(lane < N_EXPERTS, logits, neg)
    m1 = jnp.max(lg, axis=-1, keepdims=True)
    i1 = jnp.min(jnp.where(lg == m1, lane, LANES), axis=-1, keepdims=True)
    lg2 = jnp.where(lane == i1, neg, lg)
    m2 = jnp.max(lg2, axis=-1, keepdims=True)
    i2 = jnp.min(jnp.where(lg2 == m2, lane, LANES), axis=-1, keepdims=True)
    e2 = jnp.exp(m2 - m1)
    g1 = 1.0 / (1.0 + e2)
    g2 = e2 * g1

    @pl.when(step == 0)
    def _():
        cnt_ref[...] = jnp.zeros_like(cnt_ref)

    chosen = jnp.where((lane == i1) | (lane == i2), 1.0, 0.0)
    rr = lax.broadcasted_iota(jnp.int32, (tm, tm), 0)
    cc = lax.broadcasted_iota(jnp.int32, (tm, tm), 1)
    earlier = jnp.where(rr > cc, 1.0, 0.0).astype(BF16)
    before = _dot(earlier, chosen.astype(BF16)) + cnt_ref[...]
    r1 = jnp.sum(jnp.where(lane == i1, before, 0.0), axis=-1, keepdims=True)
    r2 = jnp.sum(jnp.where(lane == i2, before, 0.0), axis=-1, keepdims=True)
    cnt_ref[...] += jnp.sum(chosen, axis=0, keepdims=True)
    cnt_out_ref[...] = jnp.broadcast_to(cnt_ref[...], cnt_out_ref.shape)

    meta = jnp.zeros(logits.shape, F32)
    for li, field in ((META_E1, i1.astype(F32)), (META_E2, i2.astype(F32)), (META_R1, r1),
                      (META_R2, r2), (META_G1, g1), (META_G2, g2)):
        meta = jnp.where(lane == li, field, meta)
    meta_ref[...] = meta


def pool_router(x, g_mix, pool_w, layer, pool_scale, g_ffn, router, seq, tm=512):
    n, d = x.shape
    tm = min(tm, seq)
    router_p = jnp.zeros((d, LANES), F32).at[:, :N_EXPERTS].set(router)
    ng = len(POOL_WINDOWS)
    return pl.pallas_call(
        functools.partial(_pool_kernel, tiles_per_seq=seq // tm),
        grid=(n // tm,),
        in_specs=[pl.BlockSpec((tm, d), lambda i: (i, 0)),
                  pl.BlockSpec((1, d), lambda i: (0, 0)),
                  pl.BlockSpec((None, ng, POOL_G, POOL_G), lambda i: (layer, 0, 0, 0)),
                  pl.BlockSpec((1, d), lambda i: (0, 0)),
                  pl.BlockSpec((1, d), lambda i: (0, 0)),
                  pl.BlockSpec((d, LANES), lambda i: (0, 0))],
        out_specs=[pl.BlockSpec((tm, d), lambda i: (i, 0)),
                   pl.BlockSpec((tm, d), lambda i: (i, 0)),
                   pl.BlockSpec((tm, LANES), lambda i: (i, 0)),
                   pl.BlockSpec((SUBLANES, LANES), lambda i: (0, 0))],
        out_shape=[jax.ShapeDtypeStruct((n, d), F32),
                   jax.ShapeDtypeStruct((n, d), F32),
                   jax.ShapeDtypeStruct((n, LANES), F32),
                   jax.ShapeDtypeStruct((SUBLANES, LANES), F32)],
        scratch_shapes=[pltpu.VMEM((POOL_HALO + tm, d), F32),
                        pltpu.VMEM((1, LANES), F32)],
        compiler_params=pltpu.CompilerParams(dimension_semantics=("arbitrary",)),
        name="pool_router",
    )(x, g_mix.reshape(1, d), pool_w, pool_scale.reshape(1, d), g_ffn.reshape(1, d), router_p)


def moe_num_tiles(n):
    return TOP_K * n // MOE_TILE + N_EXPERTS


def moe_schedule(meta, counts_f, n):
    nt = moe_num_tiles(n)
    counts = counts_f[0, :N_EXPERTS].astype(jnp.int32)
    tiles = (counts + MOE_TILE - 1) // MOE_TILE
    tile_end = jnp.cumsum(tiles)
    starts = (tile_end - tiles) * MOE_TILE
    e1 = meta[:, META_E1].astype(jnp.int32)
    e2 = meta[:, META_E2].astype(jnp.int32)
    r1 = meta[:, META_R1].astype(jnp.int32)
    r2 = meta[:, META_R2].astype(jnp.int32)
    pos = jnp.concatenate([starts[e1] + r1, starts[e2] + r2])
    total = tile_end[-1]
    t = jnp.arange(nt, dtype=jnp.int32)
    t_used = jnp.minimum(t, total - 1)
    tile_expert = jnp.sum((tile_end[None, :] <= t_used[:, None]).astype(jnp.int32), axis=1)
    tile_expert = jnp.minimum(tile_expert, N_EXPERTS - 1)
    tile_valid = (t < total).astype(jnp.int32)
    pad_lo = starts + counts
    pad_hi = starts + tiles * MOE_TILE
    return pos, tile_expert, tile_valid, pad_lo, pad_hi, total.reshape(1)


def _dispatch_kernel(pos_ref, lo_ref, hi_ref, used_ref, hn_hbm, xs_hbm, zeros, sem, *, n, blk,
                     n_tiles):
    zeros[...] = jnp.zeros_like(zeros)

    def token_copies(tok):
        src = hn_hbm.at[pl.ds(tok, 1), :]
        return (pltpu.make_async_copy(src, xs_hbm.at[pl.ds(pos_ref[tok], 1), :], sem),
                pltpu.make_async_copy(src, xs_hbm.at[pl.ds(pos_ref[n + tok], 1), :], sem))

    def block(b, carry):
        def issue(r, c):
            for cp in token_copies(b * blk + r):
                cp.start()
            return c

        def drain(r, c):
            for cp in token_copies(b * blk + r):
                cp.wait()
            return c

        lax.fori_loop(0, blk, issue, 0)
        lax.fori_loop(0, blk, drain, 0)
        return carry

    lax.fori_loop(0, n // blk, block, 0)

    def pad_copy(slot):
        return pltpu.make_async_copy(zeros.at[pl.ds(0, 1), :], xs_hbm.at[pl.ds(slot, 1), :], sem)

    for e in range(N_EXPERTS):
        def issue_pad(s, c):
            pad_copy(s).start()
            return c

        def drain_pad(s, c):
            pad_copy(s).wait()
            return c

        lax.fori_loop(lo_ref[e], hi_ref[e], issue_pad, 0)
        lax.fori_loop(lo_ref[e], hi_ref[e], drain_pad, 0)

    def tile_copy(t):
        return pltpu.make_async_copy(zeros, xs_hbm.at[pl.ds(t * MOE_TILE, MOE_TILE), :], sem)

    def issue_tile(t, c):
        tile_copy(t).start()
        return c

    def drain_tile(t, c):
        tile_copy(t).wait()
        return c

    lax.fori_loop(used_ref[0], n_tiles, issue_tile, 0)
    lax.fori_loop(used_ref[0], n_tiles, drain_tile, 0)


def moe_dispatch(hn, pos, pad_lo, pad_hi, used, blk=512):
    n, d = hn.shape
    nt = moe_num_tiles(n)
    smem = pl.BlockSpec(memory_space=pltpu.SMEM)
    return pl.pallas_call(
        functools.partial(_dispatch_kernel, n=n, blk=min(blk, n), n_tiles=nt),
        in_specs=[smem, smem, smem, smem, pl.BlockSpec(memory_space=pl.ANY)],
        out_specs=pl.BlockSpec(memory_space=pl.ANY),
        out_shape=jax.ShapeDtypeStruct((nt * MOE_TILE, d), F32),
        scratch_shapes=[pltpu.VMEM((MOE_TILE, d), F32), pltpu.SemaphoreType.DMA],
        name="moe_dispatch",
    )(pos, pad_lo, pad_hi, used, hn)


def _expert_kernel(te_ref, tv_ref, x_ref, wa_ref, wu_ref, wd_ref, o_ref):
    valid = tv_ref[pl.program_id(0)] == 1

    @pl.when(valid)
    def _():
        x = x_ref[...].astype(BF16)
        a = _dot(x, wa_ref[...])
        u = _dot(x, wu_ref[...])
        o_ref[...] = _dot((_silu(a) * u).astype(BF16), wd_ref[...])

    @pl.when(jnp.logical_not(valid))
    def _():
        o_ref[...] = jnp.zeros_like(o_ref)


def moe_experts(xs, w_gu, w_down, layer, tile_expert, tile_valid):
    rows, d = xs.shape
    f = w_down.shape[2]
    nt = rows // MOE_TILE
    grid_spec = pltpu.PrefetchScalarGridSpec(
        num_scalar_prefetch=2,
        grid=(nt,),
        in_specs=[pl.BlockSpec((MOE_TILE, d), lambda t, te, tv: (t, 0)),
                  pl.BlockSpec((None, None, d, f), lambda t, te, tv: (layer, te[t], 0, 0)),
                  pl.BlockSpec((None, None, d, f), lambda t, te, tv: (layer, te[t], 0, 1)),
                  pl.BlockSpec((None, None, f, d), lambda t, te, tv: (layer, te[t], 0, 0))],
        out_specs=pl.BlockSpec((MOE_TILE, d), lambda t, te, tv: (t, 0)),
    )
    return pl.pallas_call(
        _expert_kernel,
        grid_spec=grid_spec,
        out_shape=jax.ShapeDtypeStruct((rows, d), F32),
        compiler_params=pltpu.CompilerParams(dimension_semantics=("arbitrary",),
                                             vmem_limit_bytes=56 * 1024 * 1024),
        name="moe_experts",
    )(tile_expert, tile_valid, xs, w_gu, w_gu, w_down)


def _combine_kernel(pos_ref, x_ref, meta_ref, g_ref, y_hbm, *refs, n, emit_x, emit_norm):
    ybuf, sem = refs[-2], refs[-1]
    tc = x_ref.shape[0]
    base = pl.program_id(0) * tc

    def token_copies(r):
        tok = base + r
        return (pltpu.make_async_copy(y_hbm.at[pl.ds(pos_ref[tok], 1), :],
                                      ybuf.at[0, pl.ds(r, 1), :], sem),
                pltpu.make_async_copy(y_hbm.at[pl.ds(pos_ref[n + tok], 1), :],
                                      ybuf.at[1, pl.ds(r, 1), :], sem))

    def issue(r, c):
        for cp in token_copies(r):
            cp.start()
        return c

    def drain(r, c):
        for cp in token_copies(r):
            cp.wait()
        return c

    lax.fori_loop(0, tc, issue, 0)
    lax.fori_loop(0, tc, drain, 0)
    meta = meta_ref[...]
    g1 = meta[:, META_G1:META_G1 + 1]
    g2 = meta[:, META_G2:META_G2 + 1]
    xn = x_ref[...] + g1 * ybuf[0] + g2 * ybuf[1]
    _emit_residual_outputs(xn, g_ref, refs, emit_x, emit_norm)


def moe_combine(x, y, meta, pos, g, *, emit_x=True, norm_dtype=None, tc=256):
    n, d = x.shape
    tc = min(tc, n)
    out_shape, out_specs = _residual_out_shapes(n, d, tc, emit_x, norm_dtype, lambda i, p: (i, 0))
    grid_spec = pltpu.PrefetchScalarGridSpec(
        num_scalar_prefetch=1,
        grid=(n // tc,),
        in_specs=[pl.BlockSpec((tc, d), lambda i, p: (i, 0)),
                  pl.BlockSpec((tc, LANES), lambda i, p: (i, 0)),
                  pl.BlockSpec((1, d), lambda i, p: (0, 0)),
                  pl.BlockSpec(memory_space=pl.ANY)],
        out_specs=out_specs,
        scratch_shapes=[pltpu.VMEM((TOP_K, tc, d), F32), pltpu.SemaphoreType.DMA],
    )
    return pl.pallas_call(
        functools.partial(_combine_kernel, n=n, emit_x=emit_x, emit_norm=norm_dtype is not None),
        grid_spec=grid_spec,
        out_shape=out_shape,
        compiler_params=pltpu.CompilerParams(dimension_semantics=("arbitrary",)),
        name="moe_combine",
    )(pos, x, meta, g.reshape(1, d), y)


def kernel(x, ret_w_in, ret_norm_g, ret_w_out, pool_w, pool_scale, ffn_w_gu, ffn_w_down,
           moe_router, moe_w_gu, moe_w_down, norm_mix_g, norm_ffn_g, norm_final_g):
    batch, seq, d = x.shape
    n = batch * seq
    xf = x.reshape(n, d)
    ret_w_out, pool_w, ffn_w_down, moe_w_gu, moe_w_down = (
        w.astype(BF16) for w in (ret_w_out, pool_w, ffn_w_down, moe_w_gu, moe_w_down))
    hn = rmsnorm(xf, norm_mix_g[0], BF16)
    for i in range(DEPTH):
        j = i // 2
        last = i == DEPTH - 1
        if i % 2 == 0:
            proj = matmul(hn, ret_w_in, j)
            y = retention(proj, ret_norm_g, j, batch, seq)
            xf, hn = down_residual(y, ret_w_out, j, xf, norm_ffn_g[i], norm_dtype=BF16, tk=1024)
            hid = swiglu_up(hn, ffn_w_gu, j)
            (xf,) = down_residual(hid, ffn_w_down, j, xf, tk=D_FF_EXPERT)
        else:
            xf, hn, meta, counts = pool_router(xf, norm_mix_g[i], pool_w, j, pool_scale[j],
                                               norm_ffn_g[i], moe_router[j], seq)
            pos, tile_expert, tile_valid, pad_lo, pad_hi, used = moe_schedule(meta, counts, n)
            xs = moe_dispatch(hn, pos, pad_lo, pad_hi, used)
            ys = moe_experts(xs, moe_w_gu, moe_w_down, j, tile_expert, tile_valid)
            if last:
                (out,) = moe_combine(xf, ys, meta, pos, norm_final_g, emit_x=False, norm_dtype=F32)
            else:
                xf, hn = moe_combine(xf, ys, meta, pos, norm_mix_g[i + 1], norm_dtype=BF16)
    return out.reshape(batch, seq, d)
```

```python
import functools

import jax
import jax.numpy as jnp
from jax import lax
from jax.experimental import pallas as pl
from jax.experimental.pallas import tpu as pltpu

BF16 = jnp.bfloat16
F32 = jnp.float32

D_MODEL = 2048
DEPTH = 4
RET_HEADS = 8
RET_DK = D_MODEL // RET_HEADS
RET_DV = 2 * D_MODEL // RET_HEADS
RET_QK = RET_HEADS * RET_DK
RET_V = RET_HEADS * RET_DV
RET_CHUNK = 256
ROPE_BASE = 10000.0
POOL_WINDOWS = (2, 4, 8, 16)
POOL_G = D_MODEL // len(POOL_WINDOWS)
POOL_HALO = 16
D_FF = 5632
N_EXPERTS = 8
TOP_K = 2
D_FF_EXPERT = 1408
NORM_EPS = 1e-6
LANES = 128
SUBLANES = 8
MOE_TILE = 256
META_E1, META_E2, META_R1, META_R2, META_G1, META_G2 = range(6)


def _rms(x, g):
    ms = jnp.mean(x * x, axis=-1, keepdims=True)
    return x * lax.rsqrt(ms + NORM_EPS) * g


def _silu(a):
    return a / (1.0 + jnp.exp(-a))


def _dot(a, b):
    return jnp.dot(a, b, preferred_element_type=F32)


def _rmsnorm_kernel(x_ref, g_ref, o_ref):
    o_ref[...] = _rms(x_ref[...], g_ref[...]).astype(o_ref.dtype)


def rmsnorm(x, g, out_dtype, tm=512):
    n, d = x.shape
    return pl.pallas_call(
        _rmsnorm_kernel,
        grid=(n // tm,),
        in_specs=[pl.BlockSpec((tm, d), lambda i: (i, 0)),
                  pl.BlockSpec((1, d), lambda i: (0, 0))],
        out_specs=pl.BlockSpec((tm, d), lambda i: (i, 0)),
        out_shape=jax.ShapeDtypeStruct((n, d), out_dtype),
        name="rmsnorm",
    )(x, g.reshape(1, d))


def _mm_kernel(x_ref, w_ref, o_ref, wb_ref):
    @pl.when(pl.program_id(1) == 0)
    def _():
        wb_ref[...] = w_ref[...].astype(BF16)

    o_ref[...] = _dot(x_ref[...], wb_ref[...]).astype(o_ref.dtype)


def matmul(x, w, layer, out_dtype=BF16, tm=1024, tn=1024):
    n, k = x.shape
    f = w.shape[2]
    tm = min(tm, n)
    return pl.pallas_call(
        _mm_kernel,
        grid=(f // tn, n // tm),
        in_specs=[pl.BlockSpec((tm, k), lambda j, i: (i, 0)),
                  pl.BlockSpec((None, k, tn), lambda j, i: (layer, 0, j))],
        out_specs=pl.BlockSpec((tm, tn), lambda j, i: (i, j)),
        out_shape=jax.ShapeDtypeStruct((n, f), out_dtype),
        scratch_shapes=[pltpu.VMEM((k, tn), BF16)],
        compiler_params=pltpu.CompilerParams(dimension_semantics=("arbitrary", "arbitrary")),
        name="matmul",
    )(x, w)


def _swiglu_kernel(x_ref, wa_ref, wu_ref, o_ref, wab_ref, wub_ref):
    @pl.when(pl.program_id(1) == 0)
    def _():
        wab_ref[...] = wa_ref[...].astype(BF16)
        wub_ref[...] = wu_ref[...].astype(BF16)

    x = x_ref[...]
    a = _dot(x, wab_ref[...])
    u = _dot(x, wub_ref[...])
    o_ref[...] = (_silu(a) * u).astype(o_ref.dtype)


def swiglu_up(x, w_gu, layer, tm=1024, tn=512):
    n, k = x.shape
    f = w_gu.shape[2] // 2
    tm = min(tm, n)
    nj = f // tn
    return pl.pallas_call(
        _swiglu_kernel,
        grid=(nj, n // tm),
        in_specs=[pl.BlockSpec((tm, k), lambda j, i: (i, 0)),
                  pl.BlockSpec((None, k, tn), lambda j, i: (layer, 0, j)),
                  pl.BlockSpec((None, k, tn), lambda j, i: (layer, 0, nj + j))],
        out_specs=pl.BlockSpec((tm, tn), lambda j, i: (i, j)),
        out_shape=jax.ShapeDtypeStruct((n, f), BF16),
        scratch_shapes=[pltpu.VMEM((k, tn), BF16), pltpu.VMEM((k, tn), BF16)],
        compiler_params=pltpu.CompilerParams(dimension_semantics=("arbitrary", "arbitrary")),
        name="swiglu_up",
    )(x, w_gu, w_gu)


def _emit_residual_outputs(xn, g_ref, refs, emit_x, emit_norm):
    pos = 0
    if emit_x:
        refs[pos][...] = xn
        pos += 1
    if emit_norm:
        refs[pos][...] = _rms(xn, g_ref[...]).astype(refs[pos].dtype)


def _residual_out_shapes(n, d, tm, emit_x, norm_dtype, index_map):
    out_shape, out_specs = [], []
    if emit_x:
        out_shape.append(jax.ShapeDtypeStruct((n, d), F32))
        out_specs.append(pl.BlockSpec((tm, d), index_map))
    if norm_dtype is not None:
        out_shape.append(jax.ShapeDtypeStruct((n, d), norm_dtype))
        out_specs.append(pl.BlockSpec((tm, d), index_map))
    return out_shape, out_specs


def _down_kernel(x_ref, w_ref, r_ref, g_ref, *refs, nk, emit_x, emit_norm):
    acc_ref = refs[-1]
    k = pl.program_id(1)

    @pl.when(k == 0)
    def _():
        acc_ref[...] = r_ref[...]

    acc_ref[...] += _dot(x_ref[...], w_ref[...])

    @pl.when(k == nk - 1)
    def _():
        _emit_residual_outputs(acc_ref[...], g_ref, refs, emit_x, emit_norm)


def down_residual(h, w, layer, resid, g=None, *, emit_x=True, norm_dtype=None, tm=512, tk=1024):
    n, kk = h.shape
    d = w.shape[2]
    tm = min(tm, n)
    nk = kk // tk
    if g is None:
        g = jnp.ones((d,), F32)
    out_shape, out_specs = _residual_out_shapes(n, d, tm, emit_x, norm_dtype, lambda i, k: (i, 0))
    return pl.pallas_call(
        functools.partial(_down_kernel, nk=nk, emit_x=emit_x, emit_norm=norm_dtype is not None),
        grid=(n // tm, nk),
        in_specs=[pl.BlockSpec((tm, tk), lambda i, k: (i, k)),
                  pl.BlockSpec((None, tk, d), lambda i, k: (layer, k, 0)),
                  pl.BlockSpec((tm, d), lambda i, k: (i, 0)),
                  pl.BlockSpec((1, d), lambda i, k: (0, 0))],
        out_specs=out_specs,
        out_shape=out_shape,
        scratch_shapes=[pltpu.VMEM((tm, d), F32)],
        compiler_params=pltpu.CompilerParams(dimension_semantics=("parallel", "arbitrary")),
        name="down_residual",
    )(h, w, resid, g.reshape(1, d))


def _retention_kernel(lg_ref, q_ref, k_ref, v_ref, g_ref, cos_ref, sin_ref, gain_ref, o_ref,
                      qr_ref, qd_ref, kr_ref, kd_ref, mask_ref, oraw_ref, state_ref, sb_ref,
                      *, n_chunks):
    c = RET_CHUNK
    lg = lg_ref[pl.program_id(1)]
    ii = lax.broadcasted_iota(jnp.int32, (c, c), 0)
    jj = lax.broadcasted_iota(jnp.int32, (c, c), 1)
    rel = (ii - jj).astype(F32)
    mask_ref[...] = jnp.where(rel >= 0.0, jnp.exp(jnp.maximum(rel, 0.0) * lg), 0.0)
    idx = lax.broadcasted_iota(jnp.int32, (c, 1), 0).astype(F32)
    q_decay = jnp.exp((idx + 1.0) * lg)
    k_decay = jnp.exp((c - 1.0 - idx) * lg)
    chunk_decay = jnp.exp(jnp.full((1, 1), float(c), F32) * lg)
    lane = lax.broadcasted_iota(jnp.int32, (c, RET_DK), 1)
    even = (lane & 1) == 0

    def rotary(x, cos, sin):
        partner = jnp.where(even, pltpu.roll(x, RET_DK - 1, 1), pltpu.roll(x, 1, 1))
        return x * cos + partner * sin

    def rows(ci):
        return pl.ds(pl.multiple_of(ci * c, c), c)

    def prepare(ci, carry):
        r = rows(ci)
        cos = cos_ref[r, :]
        sin = sin_ref[r, :]
        q = rotary(q_ref[r, :].astype(F32), cos, sin)
        k = rotary(k_ref[r, :].astype(F32), cos, sin) * (RET_DK ** -0.5)
        qr_ref[r, :] = q.astype(BF16)
        qd_ref[r, :] = (q * q_decay).astype(BF16)
        kr_ref[r, :] = k.astype(BF16)
        kd_ref[r, :] = (k * k_decay).astype(BF16)
        return carry

    lax.fori_loop(0, n_chunks, prepare, 0)

    state_ref[...] = jnp.zeros_like(state_ref)
    sb_ref[...] = jnp.zeros_like(sb_ref)

    def recur(ci, carry):
        r = rows(ci)
        v = v_ref[r, :]
        scores = lax.dot_general(qr_ref[r, :], kr_ref[r, :], (((1,), (1,)), ((), ())),
                                 preferred_element_type=F32) * mask_ref[...]
        oraw_ref[r, :] = _dot(scores.astype(BF16), v) + _dot(qd_ref[r, :], sb_ref[...])
        upd = lax.dot_general(kd_ref[r, :], v, (((0,), (0,)), ((), ())), preferred_element_type=F32)
        state = state_ref[...] * chunk_decay + upd
        state_ref[...] = state
        sb_ref[...] = state.astype(BF16)
        return carry

    lax.fori_loop(0, n_chunks, recur, 0)

    def finish(ci, carry):
        r = rows(ci)
        o = oraw_ref[r, :]
        ms = jnp.mean(o * o, axis=-1, keepdims=True)
        o = o * lax.rsqrt(ms + NORM_EPS) * gain_ref[...]
        gate = g_ref[r, :].astype(F32)
        o_ref[r, :] = (o * _silu(gate)).astype(o_ref.dtype)
        return carry

    lax.fori_loop(0, n_chunks, finish, 0)


def retention(proj, norm_g, layer, batch, seq):
    n = batch * seq
    h = RET_HEADS
    log_g = jnp.log1p(-jnp.exp2(-5.0 - jnp.arange(h, dtype=F32)))
    freq = 1.0 / (ROPE_BASE ** jnp.linspace(0.0, 1.0, RET_DK // 2, dtype=F32))
    ang = jnp.arange(seq).astype(F32)[:, None] * freq[None, :]
    cos = jnp.repeat(jnp.cos(ang), 2, axis=-1)
    sin = jnp.stack([-jnp.sin(ang), jnp.sin(ang)], axis=-1).reshape(seq, RET_DK)
    kq = RET_QK // RET_DK
    kv = 2 * RET_QK // RET_DV
    kg = (2 * RET_QK + RET_V) // RET_DV
    return pl.pallas_call(
        functools.partial(_retention_kernel, n_chunks=seq // RET_CHUNK),
        grid=(batch, h),
        in_specs=[pl.BlockSpec(memory_space=pltpu.SMEM),
                  pl.BlockSpec((seq, RET_DK), lambda b, hh: (b, hh)),
                  pl.BlockSpec((seq, RET_DK), lambda b, hh: (b, kq + hh)),
                  pl.BlockSpec((seq, RET_DV), lambda b, hh: (b, kv + hh)),
                  pl.BlockSpec((seq, RET_DV), lambda b, hh: (b, kg + hh)),
                  pl.BlockSpec((seq, RET_DK), lambda b, hh: (0, 0)),
                  pl.BlockSpec((seq, RET_DK), lambda b, hh: (0, 0)),
                  pl.BlockSpec((None, 1, RET_DV), lambda b, hh: (layer, 0, hh))],
        out_specs=pl.BlockSpec((seq, RET_DV), lambda b, hh: (b, hh)),
        out_shape=jax.ShapeDtypeStruct((n, RET_V), BF16),
        scratch_shapes=[pltpu.VMEM((seq, RET_DK), BF16),
                        pltpu.VMEM((seq, RET_DK), BF16),
                        pltpu.VMEM((seq, RET_DK), BF16),
                        pltpu.VMEM((seq, RET_DK), BF16),
                        pltpu.VMEM((RET_CHUNK, RET_CHUNK), F32),
                        pltpu.VMEM((seq, RET_DV), F32),
                        pltpu.VMEM((RET_DK, RET_DV), F32),
                        pltpu.VMEM((RET_DK, RET_DV), BF16)],
        name="retention",
    )(log_g, proj, proj, proj, proj, cos, sin, norm_g.reshape(norm_g.shape[0], 1, RET_V))


def _pool_kernel(x_ref, gm_ref, pw_ref, sc_ref, gf_ref, rt_ref,
                 xo_ref, hn_ref, meta_ref, cnt_out_ref, hbuf, cnt_ref, *, tiles_per_seq):
    tm = x_ref.shape[0]
    step = pl.program_id(0)
    ti = step % tiles_per_seq
    x = x_ref[...]
    hn = _rms(x, gm_ref[...])

    @pl.when(ti == 0)
    def _():
        hbuf[0:POOL_HALO, :] = jnp.zeros((POOL_HALO, D_MODEL), F32)

    @pl.when(ti != 0)
    def _():
        hbuf[0:POOL_HALO, :] = hbuf[tm:tm + POOL_HALO, :]

    hbuf[POOL_HALO:POOL_HALO + tm, :] = hn

    t = ti * tm + lax.broadcasted_iota(jnp.int32, (tm, 1), 0)
    for gi, w in enumerate(POOL_WINDOWS):
        lo, hi = gi * POOL_G, (gi + 1) * POOL_G
        cur = hn[:, lo:hi]
        acc = cur
        for d in range(1, w):
            acc = acc + hbuf[POOL_HALO - d:POOL_HALO - d + tm, lo:hi]
        inv = 1.0 / jnp.minimum(t + 1, w).astype(F32)
        mix = (acc * inv - cur).astype(BF16)
        o = _dot(mix, pw_ref[gi]) * sc_ref[:, lo:hi]
        xo_ref[:, lo:hi] = x[:, lo:hi] + o

    hn2 = _rms(xo_ref[...], gf_ref[...])
    hn_ref[...] = hn2

    logits = jnp.dot(hn2, rt_ref[...], preferred_element_type=F32, precision=lax.Precision.HIGHEST)
    lane = lax.broadcasted_iota(jnp.int32, logits.shape, 1)
    neg = jnp.float32(-jnp.inf)
    lg = jnp.where(lane < N_EXPERTS, logits, neg)
    m1 = jnp.max(lg, axis=-1, keepdims=True)
    i1 = jnp.min(jnp.where(lg == m1, lane, LANES), axis=-1, keepdims=True)
    lg2 = jnp.where(lane == i1, neg, lg)
    m2 = jnp.max(lg2, axis=-1, keepdims=True)
    i2 = jnp.min(jnp.where(lg2 == m2, lane, LANES), axis=-1, keepdims=True)
    e2 = jnp.exp(m2 - m1)
    g1 = 1.0 / (1.0 + e2)
    g2 = e2 * g1

    @pl.when(step == 0)
    def _():
        cnt_ref[...] = jnp.zeros_like(cnt_ref)

    chosen = jnp.where((lane == i1) | (lane == i2), 1.0, 0.0)
    rr = lax.broadcasted_iota(jnp.int32, (tm, tm), 0)
    cc = lax.broadcasted_iota(jnp.int32, (tm, tm), 1)
    earlier = jnp.where(rr > cc, 1.0, 0.0).astype(BF16)
    before = _dot(earlier, chosen.astype(BF16)) + cnt_ref[...]
    r1 = jnp.sum(jnp.where(lane == i1, before, 0.0), axis=-1, keepdims=True)
    r2 = jnp.sum(jnp.where(lane == i2, before, 0.0), axis=-1, keepdims=True)
    cnt_ref[...] += jnp.sum(chosen, axis=0, keepdims=True)
    cnt_out_ref[...] = jnp.broadcast_to(cnt_ref[...], cnt_out_ref.shape)

    meta = jnp.zeros(logits.shape, F32)
    for li, field in ((META_E1, i1.astype(F32)), (META_E2, i2.astype(F32)), (META_R1, r1),
                      (META_R2, r2), (META_G1, g1), (META_G2, g2)):
        meta = jnp.where(lane == li, field, meta)
    meta_ref[...] = meta


def pool_router(x, g_mix, pool_w, layer, pool_scale, g_ffn, router, seq, tm=512):
    n, d = x.shape
    tm = min(tm, seq)
    router_p = jnp.zeros((d, LANES), F32).at[:, :N_EXPERTS].set(router)
    ng = len(POOL_WINDOWS)
    return pl.pallas_call(
        functools.partial(_pool_kernel, tiles_per_seq=seq // tm),
        grid=(n // tm,),
        in_specs=[pl.BlockSpec((tm, d), lambda i: (i, 0)),
                  pl.BlockSpec((1, d), lambda i: (0, 0)),
                  pl.BlockSpec((None, ng, POOL_G, POOL_G), lambda i: (layer, 0, 0, 0)),
                  pl.BlockSpec((1, d), lambda i: (0, 0)),
                  pl.BlockSpec((1, d), lambda i: (0, 0)),
                  pl.BlockSpec((d, LANES), lambda i: (0, 0))],
        out_specs=[pl.BlockSpec((tm, d), lambda i: (i, 0)),
                   pl.BlockSpec((tm, d), lambda i: (i, 0)),
                   pl.BlockSpec((tm, LANES), lambda i: (i, 0)),
                   pl.BlockSpec((SUBLANES, LANES), lambda i: (0, 0))],
        out_shape=[jax.ShapeDtypeStruct((n, d), F32),
                   jax.ShapeDtypeStruct((n, d), F32),
                   jax.ShapeDtypeStruct((n, LANES), F32),
                   jax.ShapeDtypeStruct((SUBLANES, LANES), F32)],
        scratch_shapes=[pltpu.VMEM((POOL_HALO + tm, d), F32),
                        pltpu.VMEM((1, LANES), F32)],
        compiler_params=pltpu.CompilerParams(dimension_semantics=("arbitrary",)),
        name="pool_router",
    )(x, g_mix.reshape(1, d), pool_w, pool_scale.reshape(1, d), g_ffn.reshape(1, d), router_p)


def moe_num_tiles(n):
    return TOP_K * n // MOE_TILE + N_EXPERTS


def moe_schedule(meta, counts_f, n):
    nt = moe_num_tiles(n)
    counts = counts_f[0, :N_EXPERTS].astype(jnp.int32)
    tiles = (counts + MOE_TILE - 1) // MOE_TILE
    tile_end = jnp.cumsum(tiles)
    starts = (tile_end - tiles) * MOE_TILE
    e1 = meta[:, META_E1].astype(jnp.int32)
    e2 = meta[:, META_E2].astype(jnp.int32)
    r1 = meta[:, META_R1].astype(jnp.int32)
    r2 = meta[:, META_R2].astype(jnp.int32)
    pos = jnp.concatenate([starts[e1] + r1, starts[e2] + r2])
    total = tile_end[-1]
    t = jnp.arange(nt, dtype=jnp.int32)
    t_used = jnp.minimum(t, total - 1)
    tile_expert = jnp.sum((tile_end[None, :] <= t_used[:, None]).astype(jnp.int32), axis=1)
    tile_expert = jnp.minimum(tile_expert, N_EXPERTS - 1)
    tile_valid = (t < total).astype(jnp.int32)
    pad_lo = starts + counts
    pad_hi = starts + tiles * MOE_TILE
    return pos, tile_expert, tile_valid, pad_lo, pad_hi, total.reshape(1)


def _dispatch_kernel(pos_ref, lo_ref, hi_ref, used_ref, hn_ref, xs_hbm, zeros, sem, *, n, n_tiles):
    tb = hn_ref.shape[0]
    step = pl.program_id(0)
    base = step * tb

    def issue(r, c):
        tok = base + r
        src = hn_ref.at[pl.ds(r, 1), :]
        pltpu.make_async_copy(src, xs_hbm.at[pl.ds(pos_ref[tok], 1), :], sem).start()
        pltpu.make_async_copy(src, xs_hbm.at[pl.ds(pos_ref[n + tok], 1), :], sem).start()
        return c

    lax.fori_loop(0, tb, issue, 0, unroll=8)
    for _ in range(TOP_K):
        pltpu.make_async_copy(hn_ref, xs_hbm.at[pl.ds(0, tb), :], sem).wait()

    @pl.when(step == pl.num_programs(0) - 1)
    def _():
        zeros[...] = jnp.zeros_like(zeros)

        def pad_copy(slot):
            return pltpu.make_async_copy(zeros.at[pl.ds(0, 1), :], xs_hbm.at[pl.ds(slot, 1), :], sem)

        def issue_pad(s, c):
            pad_copy(s).start()
            return c

        def drain_pad(s, c):
            pad_copy(s).wait()
            return c

        for e in range(N_EXPERTS):
            lax.fori_loop(lo_ref[e], hi_ref[e], issue_pad, 0)
            lax.fori_loop(lo_ref[e], hi_ref[e], drain_pad, 0)

        def tile_copy(t):
            return pltpu.make_async_copy(zeros, xs_hbm.at[pl.ds(t * MOE_TILE, MOE_TILE), :], sem)

        def issue_tile(t, c):
            tile_copy(t).start()
            return c

        def drain_tile(t, c):
            tile_copy(t).wait()
            return c

        lax.fori_loop(used_ref[0], n_tiles, issue_tile, 0)
        lax.fori_loop(used_ref[0], n_tiles, drain_tile, 0)


def moe_dispatch(hn, pos, pad_lo, pad_hi, used, tb=512):
    n, d = hn.shape
    tb = min(tb, n)
    nt = moe_num_tiles(n)
    grid_spec = pltpu.PrefetchScalarGridSpec(
        num_scalar_prefetch=4,
        grid=(n // tb,),
        in_specs=[pl.BlockSpec((tb, d), lambda i, *_: (i, 0))],
        out_specs=pl.BlockSpec(memory_space=pl.ANY),
        scratch_shapes=[pltpu.VMEM((MOE_TILE, d), F32), pltpu.SemaphoreType.DMA],
    )
    return pl.pallas_call(
        functools.partial(_dispatch_kernel, n=n, n_tiles=nt),
        grid_spec=grid_spec,
        out_shape=jax.ShapeDtypeStruct((nt * MOE_TILE, d), F32),
        compiler_params=pltpu.CompilerParams(dimension_semantics=("arbitrary",)),
        name="moe_dispatch",
    )(pos, pad_lo, pad_hi, used, hn)


def _expert_kernel(te_ref, tv_ref, x_ref, wa_ref, wu_ref, wd_ref, o_ref):
    valid = tv_ref[pl.program_id(0)] == 1

    @pl.when(valid)
    def _():
        x = x_ref[...].astype(BF16)
        a = _dot(x, wa_ref[...])
        u = _dot(x, wu_ref[...])
        o_ref[...] = _dot((_silu(a) * u).astype(BF16), wd_ref[...])

    @pl.when(jnp.logical_not(valid))
    def _():
        o_ref[...] = jnp.zeros_like(o_ref)


def moe_experts(xs, w_gu, w_down, layer, tile_expert, tile_valid):
    rows, d = xs.shape
    f = w_down.shape[2]
    nt = rows // MOE_TILE
    grid_spec = pltpu.PrefetchScalarGridSpec(
        num_scalar_prefetch=2,
        grid=(nt,),
        in_specs=[pl.BlockSpec((MOE_TILE, d), lambda t, te, tv: (t, 0)),
                  pl.BlockSpec((None, None, d, f), lambda t, te, tv: (layer, te[t], 0, 0)),
                  pl.BlockSpec((None, None, d, f), lambda t, te, tv: (layer, te[t], 0, 1)),
                  pl.BlockSpec((None, None, f, d), lambda t, te, tv: (layer, te[t], 0, 0))],
        out_specs=pl.BlockSpec((MOE_TILE, d), lambda t, te, tv: (t, 0)),
    )
    return pl.pallas_call(
        _expert_kernel,
        grid_spec=grid_spec,
        out_shape=jax.ShapeDtypeStruct((rows, d), F32),
        compiler_params=pltpu.CompilerParams(dimension_semantics=("arbitrary",),
                                             vmem_limit_bytes=56 * 1024 * 1024),
        name="moe_experts",
    )(tile_expert, tile_valid, xs, w_gu, w_gu, w_down)


def _combine_kernel(pos_ref, x_ref, meta_ref, g_ref, y_hbm, *refs, n, emit_x, emit_norm):
    ybuf, sem = refs[-2], refs[-1]
    tc = x_ref.shape[0]
    step = pl.program_id(0)
    slot = step % 2

    def start_gather(s, into):
        def issue(r, c):
            tok = s * tc + r
            for k in range(TOP_K):
                pltpu.make_async_copy(y_hbm.at[pl.ds(pos_ref[k * n + tok], 1), :],
                                      ybuf.at[into, k, pl.ds(r, 1), :], sem.at[into]).start()
            return c

        lax.fori_loop(0, tc, issue, 0, unroll=8)

    @pl.when(step == 0)
    def _():
        start_gather(0, 0)

    @pl.when(step + 1 < pl.num_programs(0))
    def _():
        start_gather(step + 1, 1 - slot)

    for k in range(TOP_K):
        pltpu.make_async_copy(y_hbm.at[pl.ds(0, tc), :], ybuf.at[slot, k], sem.at[slot]).wait()

    meta = meta_ref[...]
    g1 = meta[:, META_G1:META_G1 + 1]
    g2 = meta[:, META_G2:META_G2 + 1]
    xn = x_ref[...] + g1 * ybuf[slot, 0] + g2 * ybuf[slot, 1]
    _emit_residual_outputs(xn, g_ref, refs, emit_x, emit_norm)


def moe_combine(x, y, meta, pos, g, *, emit_x=True, norm_dtype=None, tc=256):
    n, d = x.shape
    tc = min(tc, n)
    out_shape, out_specs = _residual_out_shapes(n, d, tc, emit_x, norm_dtype, lambda i, p: (i, 0))
    grid_spec = pltpu.PrefetchScalarGridSpec(
        num_scalar_prefetch=1,
        grid=(n // tc,),
        in_specs=[pl.BlockSpec((tc, d), lambda i, p: (i, 0)),
                  pl.BlockSpec((tc, LANES), lambda i, p: (i, 0)),
                  pl.BlockSpec((1, d), lambda i, p: (0, 0)),
                  pl.BlockSpec(memory_space=pl.ANY)],
        out_specs=out_specs,
        scratch_shapes=[pltpu.VMEM((2, TOP_K, tc, d), F32), pltpu.SemaphoreType.DMA((2,))],
    )
    return pl.pallas_call(
        functools.partial(_combine_kernel, n=n, emit_x=emit_x, emit_norm=norm_dtype is not None),
        grid_spec=grid_spec,
        out_shape=out_shape,
        compiler_params=pltpu.CompilerParams(dimension_semantics=("arbitrary",)),
        name="moe_combine",
    )(pos, x, meta, g.reshape(1, d), y)


def kernel(x, ret_w_in, ret_norm_g, ret_w_out, pool_w, pool_scale, ffn_w_gu, ffn_w_down,
           moe_router, moe_w_gu, moe_w_down, norm_mix_g, norm_ffn_g, norm_final_g):
    batch, seq, d = x.shape
    n = batch * seq
    xf = x.reshape(n, d)
    ret_w_out, pool_w, ffn_w_down, moe_w_gu, moe_w_down = (
        w.astype(BF16) for w in (ret_w_out, pool_w, ffn_w_down, moe_w_gu, moe_w_down))
    hn = rmsnorm(xf, norm_mix_g[0], BF16)
    for i in range(DEPTH):
        j = i // 2
        last = i == DEPTH - 1
        if i % 2 == 0:
            proj = matmul(hn, ret_w_in, j)
            y = retention(proj, ret_norm_g, j, batch, seq)
            xf, hn = down_residual(y, ret_w_out, j, xf, norm_ffn_g[i], norm_dtype=BF16, tk=1024)
            hid = swiglu_up(hn, ffn_w_gu, j)
            (xf,) = down_residual(hid, ffn_w_down, j, xf, tk=D_FF_EXPERT)
        else:
            xf, hn, meta, counts = pool_router(xf, norm_mix_g[i], pool_w, j, pool_scale[j],
                                               norm_ffn_g[i], moe_router[j], seq)
            pos, tile_expert, tile_valid, pad_lo, pad_hi, used = moe_schedule(meta, counts, n)
            xs = moe_dispatch(hn, pos, pad_lo, pad_hi, used)
            ys = moe_experts(xs, moe_w_gu, moe_w_down, j, tile_expert, tile_valid)
            if last:
                (out,) = moe_combine(xf, ys, meta, pos, norm_final_g, emit_x=False, norm_dtype=F32)
            else:
                xf, hn = moe_combine(xf, ys, meta, pos, norm_mix_g[i + 1], norm_dtype=BF16)
    return out.reshape(batch, seq, d)
```

```python
import functools

import jax
import jax.numpy as jnp
from jax import lax
from jax.experimental import pallas as pl
from jax.experimental.pallas import tpu as pltpu

BF16 = jnp.bfloat16
F32 = jnp.float32

D_MODEL = 2048
DEPTH = 4
RET_HEADS = 8
RET_DK = D_MODEL // RET_HEADS
RET_DV = 2 * D_MODEL // RET_HEADS
RET_QK = RET_HEADS * RET_DK
RET_V = RET_HEADS * RET_DV
RET_CHUNK = 256
ROPE_BASE = 10000.0
POOL_WINDOWS = (2, 4, 8, 16)
POOL_G = D_MODEL // len(POOL_WINDOWS)
POOL_HALO = 16
D_FF = 5632
N_EXPERTS = 8
TOP_K = 2
D_FF_EXPERT = 1408
NORM_EPS = 1e-6
LANES = 128
SUBLANES = 8
MOE_TILE = 256
META_E1, META_E2, META_R1, META_R2, META_G1, META_G2 = range(6)


def _rms(x, g):
    ms = jnp.mean(x * x, axis=-1, keepdims=True)
    return x * lax.rsqrt(ms + NORM_EPS) * g


def _silu(a):
    return a / (1.0 + jnp.exp(-a))


def _dot(a, b):
    return jnp.dot(a, b, preferred_element_type=F32)


def _rmsnorm_kernel(x_ref, g_ref, o_ref):
    o_ref[...] = _rms(x_ref[...], g_ref[...]).astype(o_ref.dtype)


def rmsnorm(x, g, out_dtype, tm=512):
    n, d = x.shape
    return pl.pallas_call(
        _rmsnorm_kernel,
        grid=(n // tm,),
        in_specs=[pl.BlockSpec((tm, d), lambda i: (i, 0)),
                  pl.BlockSpec((1, d), lambda i: (0, 0))],
        out_specs=pl.BlockSpec((tm, d), lambda i: (i, 0)),
        out_shape=jax.ShapeDtypeStruct((n, d), out_dtype),
        name="rmsnorm",
    )(x, g.reshape(1, d))


def _mm_kernel(x_ref, w_ref, o_ref, wb_ref):
    @pl.when(pl.program_id(1) == 0)
    def _():
        wb_ref[...] = w_ref[...].astype(BF16)

    o_ref[...] = _dot(x_ref[...], wb_ref[...]).astype(o_ref.dtype)


def matmul(x, w, layer, out_dtype=BF16, tm=1024, tn=1024):
    n, k = x.shape
    f = w.shape[2]
    tm = min(tm, n)
    return pl.pallas_call(
        _mm_kernel,
        grid=(f // tn, n // tm),
        in_specs=[pl.BlockSpec((tm, k), lambda j, i: (i, 0)),
                  pl.BlockSpec((None, k, tn), lambda j, i: (layer, 0, j))],
        out_specs=pl.BlockSpec((tm, tn), lambda j, i: (i, j)),
        out_shape=jax.ShapeDtypeStruct((n, f), out_dtype),
        scratch_shapes=[pltpu.VMEM((k, tn), BF16)],
        compiler_params=pltpu.CompilerParams(dimension_semantics=("arbitrary", "arbitrary")),
        name="matmul",
    )(x, w)


def _swiglu_kernel(x_ref, ssq_ref, wa_ref, wu_ref, o_ref, wab_ref, wub_ref):
    @pl.when(pl.program_id(1) == 0)
    def _():
        wab_ref[...] = wa_ref[...].astype(BF16)
        wub_ref[...] = wu_ref[...].astype(BF16)

    x = x_ref[...]
    rowscale = lax.rsqrt(jnp.sum(ssq_ref[...], axis=0) / x.shape[1] + NORM_EPS)
    a = _dot(x, wab_ref[...]) * rowscale
    u = _dot(x, wub_ref[...]) * rowscale
    o_ref[...] = (_silu(a) * u).astype(o_ref.dtype)


def swiglu_up(xg, ssq, w_gu, layer, tm=1024, tn=512):
    n, k = xg.shape
    f = w_gu.shape[2] // 2
    tm = min(tm, n)
    nj = f // tn
    ns = ssq.shape[0]
    return pl.pallas_call(
        _swiglu_kernel,
        grid=(nj, n // tm),
        in_specs=[pl.BlockSpec((tm, k), lambda j, i: (i, 0)),
                  pl.BlockSpec((ns, tm, 1), lambda j, i: (0, i, 0)),
                  pl.BlockSpec((None, k, tn), lambda j, i: (layer, 0, j)),
                  pl.BlockSpec((None, k, tn), lambda j, i: (layer, 0, nj + j))],
        out_specs=pl.BlockSpec((tm, tn), lambda j, i: (i, j)),
        out_shape=jax.ShapeDtypeStruct((n, f), BF16),
        scratch_shapes=[pltpu.VMEM((k, tn), BF16), pltpu.VMEM((k, tn), BF16)],
        compiler_params=pltpu.CompilerParams(dimension_semantics=("arbitrary", "arbitrary")),
        name="swiglu_up",
    )(xg, ssq, w_gu, w_gu)


def _emit_residual_outputs(xn, g_ref, refs, emit_x, emit_norm):
    pos = 0
    if emit_x:
        refs[pos][...] = xn
        pos += 1
    if emit_norm:
        refs[pos][...] = _rms(xn, g_ref[...]).astype(refs[pos].dtype)


def _residual_out_shapes(n, d, tm, emit_x, norm_dtype, index_map):
    out_shape, out_specs = [], []
    if emit_x:
        out_shape.append(jax.ShapeDtypeStruct((n, d), F32))
        out_specs.append(pl.BlockSpec((tm, d), index_map))
    if norm_dtype is not None:
        out_shape.append(jax.ShapeDtypeStruct((n, d), norm_dtype))
        out_specs.append(pl.BlockSpec((tm, d), index_map))
    return out_shape, out_specs


def _down_kernel(x_ref, w_ref, r_ref, g_ref, *refs, emit_scaled):
    wb_ref = refs[-1]

    @pl.when(pl.program_id(1) == 0)
    def _():
        wb_ref[...] = w_ref[...].astype(BF16)

    xn = r_ref[...] + _dot(x_ref[...], wb_ref[...])
    refs[0][...] = xn
    if emit_scaled:
        refs[1][...] = (xn * g_ref[...]).astype(BF16)
        refs[2][...] = jnp.sum(xn * xn, axis=-1, keepdims=True)


def down_residual(h, w, layer, resid, g=None, *, tm=512, tn=512):
    n, kk = h.shape
    d = w.shape[2]
    tm = min(tm, n)
    nj = d // tn
    emit_scaled = g is not None
    if g is None:
        g = jnp.ones((d,), F32)
    out_shape = [jax.ShapeDtypeStruct((n, d), F32)]
    out_specs = [pl.BlockSpec((tm, tn), lambda j, i: (i, j))]
    if emit_scaled:
        out_shape += [jax.ShapeDtypeStruct((n, d), BF16), jax.ShapeDtypeStruct((nj, n, 1), F32)]
        out_specs += [pl.BlockSpec((tm, tn), lambda j, i: (i, j)),
                      pl.BlockSpec((None, tm, 1), lambda j, i: (j, i, 0))]
    return pl.pallas_call(
        functools.partial(_down_kernel, emit_scaled=emit_scaled),
        grid=(nj, n // tm),
        in_specs=[pl.BlockSpec((tm, kk), lambda j, i: (i, 0)),
                  pl.BlockSpec((None, kk, tn), lambda j, i: (layer, 0, j)),
                  pl.BlockSpec((tm, tn), lambda j, i: (i, j)),
                  pl.BlockSpec((1, tn), lambda j, i: (0, j))],
        out_specs=out_specs,
        out_shape=out_shape,
        scratch_shapes=[pltpu.VMEM((kk, tn), BF16)],
        compiler_params=pltpu.CompilerParams(dimension_semantics=("arbitrary", "arbitrary")),
        name="down_residual",
    )(h, w, resid, g.reshape(1, d))


def _retention_kernel(lg_ref, q_ref, k_ref, v_ref, g_ref, cos_ref, sin_ref, gain_ref, o_ref,
                      qr_ref, qd_ref, kr_ref, kd_ref, mask_ref, qdec_ref, kdec_ref, oraw_ref,
                      state_ref, sb_ref, *, n_chunks):
    c = RET_CHUNK
    lg = lg_ref[pl.program_id(1)]
    ii = lax.broadcasted_iota(jnp.int32, (c, c), 0)
    jj = lax.broadcasted_iota(jnp.int32, (c, c), 1)
    rel = (ii - jj).astype(F32)
    mask_ref[...] = jnp.where(rel >= 0.0, jnp.exp(jnp.maximum(rel, 0.0) * lg), 0.0)
    idx = lax.broadcasted_iota(jnp.int32, (c, 1), 0).astype(F32)
    k_scale = RET_DK ** -0.5
    qdec_ref[...] = jnp.broadcast_to(jnp.exp((idx + 1.0) * lg), (c, RET_DK))
    kdec_ref[...] = jnp.broadcast_to(jnp.exp((c - 1.0 - idx) * lg) * k_scale, (c, RET_DK))
    chunk_decay = jnp.exp(jnp.full((1, 1), float(c), F32) * lg)
    lane = lax.broadcasted_iota(jnp.int32, (c // 2, LANES), 1)
    even = (lane & 1) == 0

    def rotary(xb, cos, sin):
        words = pltpu.bitcast(xb, jnp.uint32)
        halves = []
        for lo in range(0, RET_DK, LANES):
            w = words[:, lo:lo + LANES]
            halves.append(jnp.where(even, pltpu.roll(w, LANES - 1, 1), pltpu.roll(w, 1, 1)))
        partner = pltpu.bitcast(jnp.concatenate(halves, axis=1), BF16)
        return xb.astype(F32) * cos + partner.astype(F32) * sin

    def rows(ci):
        return pl.ds(pl.multiple_of(ci * c, c), c)

    def prepare(ci, carry):
        r = rows(ci)
        cos = cos_ref[r, :]
        sin = sin_ref[r, :]
        q = rotary(q_ref[r, :], cos, sin)
        k = rotary(k_ref[r, :], cos, sin)
        qr_ref[r, :] = q.astype(BF16)
        qd_ref[r, :] = (q * qdec_ref[...]).astype(BF16)
        kr_ref[r, :] = (k * k_scale).astype(BF16)
        kd_ref[r, :] = (k * kdec_ref[...]).astype(BF16)
        return carry

    lax.fori_loop(0, n_chunks, prepare, 0)

    state_ref[...] = jnp.zeros_like(state_ref)
    sb_ref[...] = jnp.zeros_like(sb_ref)

    def recur(ci, carry):
        r = rows(ci)
        v = v_ref[r, :]
        scores = lax.dot_general(qr_ref[r, :], kr_ref[r, :], (((1,), (1,)), ((), ())),
                                 preferred_element_type=F32) * mask_ref[...]
        oraw_ref[r, :] = _dot(scores.astype(BF16), v) + _dot(qd_ref[r, :], sb_ref[...])
        upd = lax.dot_general(kd_ref[r, :], v, (((0,), (0,)), ((), ())), preferred_element_type=F32)
        state = state_ref[...] * chunk_decay + upd
        state_ref[...] = state
        sb_ref[...] = state.astype(BF16)
        return carry

    lax.fori_loop(0, n_chunks, recur, 0)

    def finish(ci, carry):
        r = rows(ci)
        o = oraw_ref[r, :]
        ms = jnp.mean(o * o, axis=-1, keepdims=True)
        o = o * lax.rsqrt(ms + NORM_EPS) * gain_ref[...]
        gate = g_ref[r, :].astype(F32)
        o_ref[r, :] = (o * _silu(gate)).astype(o_ref.dtype)
        return carry

    lax.fori_loop(0, n_chunks, finish, 0)


def retention(proj, norm_g, layer, batch, seq):
    n = batch * seq
    h = RET_HEADS
    log_g = jnp.log1p(-jnp.exp2(-5.0 - jnp.arange(h, dtype=F32)))
    freq = 1.0 / (ROPE_BASE ** jnp.linspace(0.0, 1.0, RET_DK // 2, dtype=F32))
    ang = jnp.arange(seq).astype(F32)[:, None] * freq[None, :]
    cos = jnp.repeat(jnp.cos(ang), 2, axis=-1)
    sin = jnp.stack([-jnp.sin(ang), jnp.sin(ang)], axis=-1).reshape(seq, RET_DK)
    kq = RET_QK // RET_DK
    kv = 2 * RET_QK // RET_DV
    kg = (2 * RET_QK + RET_V) // RET_DV
    return pl.pallas_call(
        functools.partial(_retention_kernel, n_chunks=seq // RET_CHUNK),
        grid=(batch, h),
        in_specs=[pl.BlockSpec(memory_space=pltpu.SMEM),
                  pl.BlockSpec((seq, RET_DK), lambda b, hh: (b, hh)),
                  pl.BlockSpec((seq, RET_DK), lambda b, hh: (b, kq + hh)),
                  pl.BlockSpec((seq, RET_DV), lambda b, hh: (b, kv + hh)),
                  pl.BlockSpec((seq, RET_DV), lambda b, hh: (b, kg + hh)),
                  pl.BlockSpec((seq, RET_DK), lambda b, hh: (0, 0)),
                  pl.BlockSpec((seq, RET_DK), lambda b, hh: (0, 0)),
                  pl.BlockSpec((None, 1, RET_DV), lambda b, hh: (layer, 0, hh))],
        out_specs=pl.BlockSpec((seq, RET_DV), lambda b, hh: (b, hh)),
        out_shape=jax.ShapeDtypeStruct((n, RET_V), BF16),
        scratch_shapes=[pltpu.VMEM((seq, RET_DK), BF16),
                        pltpu.VMEM((seq, RET_DK), BF16),
                        pltpu.VMEM((seq, RET_DK), BF16),
                        pltpu.VMEM((seq, RET_DK), BF16),
                        pltpu.VMEM((RET_CHUNK, RET_CHUNK), F32),
                        pltpu.VMEM((RET_CHUNK, RET_DK), F32),
                        pltpu.VMEM((RET_CHUNK, RET_DK), F32),
                        pltpu.VMEM((seq, RET_DV), F32),
                        pltpu.VMEM((RET_DK, RET_DV), F32),
                        pltpu.VMEM((RET_DK, RET_DV), BF16)],
        name="retention",
    )(log_g, proj, proj, proj, proj, cos, sin, norm_g.reshape(norm_g.shape[0], 1, RET_V))


def _pool_kernel(x_ref, gm_ref, pw_ref, sc_ref, gf_ref, rt_ref,
                 xo_ref, hn_ref, meta_ref, cnt_out_ref, hbuf, cnt_ref, *, tiles_per_seq):
    tm = x_ref.shape[0]
    step = pl.program_id(0)
    ti = step % tiles_per_seq
    x = x_ref[...]
    hn = _rms(x, gm_ref[...])

    @pl.when(ti == 0)
    def _():
        hbuf[0:POOL_HALO, :] = jnp.zeros((POOL_HALO, D_MODEL), F32)

    @pl.when(ti != 0)
    def _():
        hbuf[0:POOL_HALO, :] = hbuf[tm:tm + POOL_HALO, :]

    hbuf[POOL_HALO:POOL_HALO + tm, :] = hn

    t = ti * tm + lax.broadcasted_iota(jnp.int32, (tm, 1), 0)
    for gi, w in enumerate(POOL_WINDOWS):
        lo, hi = gi * POOL_G, (gi + 1) * POOL_G
        cur = hn[:, lo:hi]
        acc = hbuf[:, lo:hi]
        shift = 1
        while shift < w:
            acc = acc + pltpu.roll(acc, shift, 0)
            shift *= 2
        acc = acc[POOL_HALO:, :]
        inv = 1.0 / jnp.minimum(t + 1, w).astype(F32)
        mix = (acc * inv - cur).astype(BF16)
        o = _dot(mix, pw_ref[gi]) * sc_ref[:, lo:hi]
        xo_ref[:, lo:hi] = x[:, lo:hi] + o

    hn2 = _rms(xo_ref[...], gf_ref[...])
    hn_ref[...] = hn2

    rt = rt_ref[...]
    hn_hi = hn2.astype(BF16)
    hn_lo = (hn2 - hn_hi.astype(F32)).astype(BF16)
    rt_hi = rt.astype(BF16)
    rt_lo = (rt - rt_hi.astype(F32)).astype(BF16)
    logits = _dot(hn_hi, rt_hi) + (_dot(hn_hi, rt_lo) + _dot(hn_lo, rt_hi))
    lane = lax.broadcasted_iota(jnp.int32, logits.shape, 1)
    neg = jnp.float32(-jnp.inf)
    lg = jnp.where(lane < N_EXPERTS, logits, neg)
    m1 = jnp.max(lg, axis=-1, keepdims=True)
    i1 = jnp.min(jnp.where(lg == m1, lane, LANES), axis=-1, keepdims=True)
    lg2 = jnp.where(lane == i1, neg, lg)
    m2 = jnp.max(lg2, axis=-1, keepdims=True)
    i2 = jnp.min(jnp.where(lg2 == m2, lane, LANES), axis=-1, keepdims=True)
    e2 = jnp.exp(m2 - m1)
    g1 = 1.0 / (1.0 + e2)
    g2 = e2 * g1

    @pl.when(step == 0)
    def _():
        cnt_ref[...] = jnp.zeros_like(cnt_ref)

    chosen = jnp.where((lane == i1) | (lane == i2), 1.0, 0.0)
    rr = lax.broadcasted_iota(jnp.int32, (tm, tm), 0)
    cc = lax.broadcasted_iota(jnp.int32, (tm, tm), 1)
    earlier = jnp.where(rr > cc, 1.0, 0.0).astype(BF16)
    before = _dot(earlier, chosen.astype(BF16)) + cnt_ref[...]
    r1 = jnp.sum(jnp.where(lane == i1, before, 0.0), axis=-1, keepdims=True)
    r2 = jnp.sum(jnp.where(lane == i2, before, 0.0), axis=-1, keepdims=True)
    cnt_ref[...] += jnp.sum(chosen, axis=0, keepdims=True)
    cnt_out_ref[...] = jnp.broadcast_to(cnt_ref[...], cnt_out_ref.shape)

    meta = jnp.zeros(logits.shape, F32)
    for li, field in ((META_E1, i1.astype(F32)), (META_E2, i2.astype(F32)), (META_R1, r1),
                      (META_R2, r2), (META_G1, g1), (META_G2, g2)):
        meta = jnp.where(lane == li, field, meta)
    meta_ref[...] = meta


def pool_router(x, g_mix, pool_w, layer, pool_scale, g_ffn, router, seq, tm=512):
    n, d = x.shape
    tm = min(tm, seq)
    router_p = jnp.zeros((d, LANES), F32).at[:, :N_EXPERTS].set(router)
    ng = len(POOL_WINDOWS)
    return pl.pallas_call(
        functools.partial(_pool_kernel, tiles_per_seq=seq // tm),
        grid=(n // tm,),
        in_specs=[pl.BlockSpec((tm, d), lambda i: (i, 0)),
                  pl.BlockSpec((1, d), lambda i: (0, 0)),
                  pl.BlockSpec((None, ng, POOL_G, POOL_G), lambda i: (layer, 0, 0, 0)),
                  pl.BlockSpec((1, d), lambda i: (0, 0)),
                  pl.BlockSpec((1, d), lambda i: (0, 0)),
                  pl.BlockSpec((d, LANES), lambda i: (0, 0))],
        out_specs=[pl.BlockSpec((tm, d), lambda i: (i, 0)),
                   pl.BlockSpec((tm, d), lambda i: (i, 0)),
                   pl.BlockSpec((tm, LANES), lambda i: (i, 0)),
                   pl.BlockSpec((SUBLANES, LANES), lambda i: (0, 0))],
        out_shape=[jax.ShapeDtypeStruct((n, d), F32),
                   jax.ShapeDtypeStruct((n, d), F32),
                   jax.ShapeDtypeStruct((n, LANES), F32),
                   jax.ShapeDtypeStruct((SUBLANES, LANES), F32)],
        scratch_shapes=[pltpu.VMEM((POOL_HALO + tm, d), F32),
                        pltpu.VMEM((1, LANES), F32)],
        compiler_params=pltpu.CompilerParams(dimension_semantics=("arbitrary",)),
        name="pool_router",
    )(x, g_mix.reshape(1, d), pool_w, pool_scale.reshape(1, d), g_ffn.reshape(1, d), router_p)


def moe_num_tiles(n):
    return TOP_K * n // MOE_TILE + N_EXPERTS


def moe_schedule(meta, counts_f, n):
    nt = moe_num_tiles(n)
    counts = counts_f[0, :N_EXPERTS].astype(jnp.int32)
    tiles = (counts + MOE_TILE - 1) // MOE_TILE
    tile_end = jnp.cumsum(tiles)
    starts = (tile_end - tiles) * MOE_TILE
    e1 = meta[:, META_E1].astype(jnp.int32)
    e2 = meta[:, META_E2].astype(jnp.int32)
    r1 = meta[:, META_R1].astype(jnp.int32)
    r2 = meta[:, META_R2].astype(jnp.int32)
    pos = jnp.concatenate([starts[e1] + r1, starts[e2] + r2])
    total = tile_end[-1]
    t = jnp.arange(nt, dtype=jnp.int32)
    t_used = jnp.minimum(t, total - 1)
    tile_expert = jnp.sum((tile_end[None, :] <= t_used[:, None]).astype(jnp.int32), axis=1)
    tile_expert = jnp.minimum(tile_expert, N_EXPERTS - 1)
    tile_valid = (t < total).astype(jnp.int32)
    pad_lo = starts + counts
    pad_hi = starts + tiles * MOE_TILE
    return pos, tile_expert, tile_valid, pad_lo, pad_hi, total.reshape(1)


def _dispatch_kernel(pos_ref, lo_ref, hi_ref, used_ref, hn_ref, xs_hbm, zeros, sem, *, n, n_tiles):
    tb = hn_ref.shape[0]
    step = pl.program_id(0)
    base = step * tb

    def issue(r, c):
        tok = base + r
        src = hn_ref.at[pl.ds(r, 1), :]
        pltpu.make_async_copy(src, xs_hbm.at[pl.ds(pos_ref[tok], 1), :], sem).start()
        pltpu.make_async_copy(src, xs_hbm.at[pl.ds(pos_ref[n + tok], 1), :], sem).start()
        return c

    lax.fori_loop(0, tb, issue, 0, unroll=8)
    for _ in range(TOP_K):
        pltpu.make_async_copy(hn_ref, xs_hbm.at[pl.ds(0, tb), :], sem).wait()

    @pl.when(step == pl.num_programs(0) - 1)
    def _():
        zeros[...] = jnp.zeros_like(zeros)

        def pad_copy(slot):
            return pltpu.make_async_copy(zeros.at[pl.ds(0, 1), :], xs_hbm.at[pl.ds(slot, 1), :], sem)

        def issue_pad(s, c):
            pad_copy(s).start()
            return c

        def drain_pad(s, c):
            pad_copy(s).wait()
            return c

        for e in range(N_EXPERTS):
            lax.fori_loop(lo_ref[e], hi_ref[e], issue_pad, 0)
            lax.fori_loop(lo_ref[e], hi_ref[e], drain_pad, 0)

        def tile_copy(t):
            return pltpu.make_async_copy(zeros, xs_hbm.at[pl.ds(t * MOE_TILE, MOE_TILE), :], sem)

        def issue_tile(t, c):
            tile_copy(t).start()
            return c

        def drain_tile(t, c):
            tile_copy(t).wait()
            return c

        lax.fori_loop(used_ref[0], n_tiles, issue_tile, 0)
        lax.fori_loop(used_ref[0], n_tiles, drain_tile, 0)


def moe_dispatch(hn, pos, pad_lo, pad_hi, used, tb=512):
    n, d = hn.shape
    tb = min(tb, n)
    nt = moe_num_tiles(n)
    grid_spec = pltpu.PrefetchScalarGridSpec(
        num_scalar_prefetch=4,
        grid=(n // tb,),
        in_specs=[pl.BlockSpec((tb, d), lambda i, *_: (i, 0))],
        out_specs=pl.BlockSpec(memory_space=pl.ANY),
        scratch_shapes=[pltpu.VMEM((MOE_TILE, d), F32), pltpu.SemaphoreType.DMA],
    )
    return pl.pallas_call(
        functools.partial(_dispatch_kernel, n=n, n_tiles=nt),
        grid_spec=grid_spec,
        out_shape=jax.ShapeDtypeStruct((nt * MOE_TILE, d), F32),
        compiler_params=pltpu.CompilerParams(dimension_semantics=("arbitrary",)),
        name="moe_dispatch",
    )(pos, pad_lo, pad_hi, used, hn)


def _expert_kernel(te_ref, tv_ref, x_ref, wa_ref, wu_ref, wd_ref, o_ref):
    valid = tv_ref[pl.program_id(0)] == 1

    @pl.when(valid)
    def _():
        x = x_ref[...].astype(BF16)
        a = _dot(x, wa_ref[...])
        u = _dot(x, wu_ref[...])
        o_ref[...] = _dot((_silu(a) * u).astype(BF16), wd_ref[...])

    @pl.when(jnp.logical_not(valid))
    def _():
        o_ref[...] = jnp.zeros_like(o_ref)


def moe_experts(xs, w_gu, w_down, layer, tile_expert, tile_valid):
    rows, d = xs.shape
    f = w_down.shape[2]
    nt = rows // MOE_TILE
    grid_spec = pltpu.PrefetchScalarGridSpec(
        num_scalar_prefetch=2,
        grid=(nt,),
        in_specs=[pl.BlockSpec((MOE_TILE, d), lambda t, te, tv: (t, 0)),
                  pl.BlockSpec((None, None, d, f), lambda t, te, tv: (layer, te[t], 0, 0)),
                  pl.BlockSpec((None, None, d, f), lambda t, te, tv: (layer, te[t], 0, 1)),
                  pl.BlockSpec((None, None, f, d), lambda t, te, tv: (layer, te[t], 0, 0))],
        out_specs=pl.BlockSpec((MOE_TILE, d), lambda t, te, tv: (t, 0)),
    )
    return pl.pallas_call(
        _expert_kernel,
        grid_spec=grid_spec,
        out_shape=jax.ShapeDtypeStruct((rows, d), F32),
        compiler_params=pltpu.CompilerParams(dimension_semantics=("arbitrary",),
                                             vmem_limit_bytes=56 * 1024 * 1024),
        name="moe_experts",
    )(tile_expert, tile_valid, xs, w_gu, w_gu, w_down)


def _combine_kernel(pos_ref, x_ref, meta_ref, g_ref, y_hbm, *refs, n, emit_x, emit_norm):
    ybuf, sem = refs[-2], refs[-1]
    tc = x_ref.shape[0]
    step = pl.program_id(0)
    slot = step % 2

    def start_gather(s, into):
        def issue(r, c):
            tok = s * tc + r
            for k in range(TOP_K):
                pltpu.make_async_copy(y_hbm.at[pl.ds(pos_ref[k * n + tok], 1), :],
                                      ybuf.at[into, k, pl.ds(r, 1), :], sem.at[into]).start()
            return c

        lax.fori_loop(0, tc, issue, 0, unroll=8)

    @pl.when(step == 0)
    def _():
        start_gather(0, 0)

    @pl.when(step + 1 < pl.num_programs(0))
    def _():
        start_gather(step + 1, 1 - slot)

    for k in range(TOP_K):
        pltpu.make_async_copy(y_hbm.at[pl.ds(0, tc), :], ybuf.at[slot, k], sem.at[slot]).wait()

    meta = meta_ref[...]
    g1 = meta[:, META_G1:META_G1 + 1]
    g2 = meta[:, META_G2:META_G2 + 1]
    xn = x_ref[...] + g1 * ybuf[slot, 0] + g2 * ybuf[slot, 1]
    _emit_residual_outputs(xn, g_ref, refs, emit_x, emit_norm)


def moe_combine(x, y, meta, pos, g, *, emit_x=True, norm_dtype=None, tc=256):
    n, d = x.shape
    tc = min(tc, n)
    out_shape, out_specs = _residual_out_shapes(n, d, tc, emit_x, norm_dtype, lambda i, p: (i, 0))
    grid_spec = pltpu.PrefetchScalarGridSpec(
        num_scalar_prefetch=1,
        grid=(n // tc,),
        in_specs=[pl.BlockSpec((tc, d), lambda i, p: (i, 0)),
                  pl.BlockSpec((tc, LANES), lambda i, p: (i, 0)),
                  pl.BlockSpec((1, d), lambda i, p: (0, 0)),
                  pl.BlockSpec(memory_space=pl.ANY)],
        out_specs=out_specs,
        scratch_shapes=[pltpu.VMEM((2, TOP_K, tc, d), F32), pltpu.SemaphoreType.DMA((2,))],
    )
    return pl.pallas_call(
        functools.partial(_combine_kernel, n=n, emit_x=emit_x, emit_norm=norm_dtype is not None),
        grid_spec=grid_spec,
        out_shape=out_shape,
        compiler_params=pltpu.CompilerParams(dimension_semantics=("arbitrary",)),
        name="moe_combine",
    )(pos, x, meta, g.reshape(1, d), y)


def kernel(x, ret_w_in, ret_norm_g, ret_w_out, pool_w, pool_scale, ffn_w_gu, ffn_w_down,
           moe_router, moe_w_gu, moe_w_down, norm_mix_g, norm_ffn_g, norm_final_g):
    batch, seq, d = x.shape
    n = batch * seq
    xf = x.reshape(n, d)
    pool_w, moe_w_gu, moe_w_down = (w.astype(BF16) for w in (pool_w, moe_w_gu, moe_w_down))
    hn = rmsnorm(xf, norm_mix_g[0], BF16)
    for i in range(DEPTH):
        j = i // 2
        last = i == DEPTH - 1
        if i % 2 == 0:
            proj = matmul(hn, ret_w_in, j)
            y = retention(proj, ret_norm_g, j, batch, seq)
            xf, xg, ssq = down_residual(y, ret_w_out, j, xf, norm_ffn_g[i], tm=1024)
            hid = swiglu_up(xg, ssq, ffn_w_gu, j)
            (xf,) = down_residual(hid, ffn_w_down, j, xf)
        else:
            xf, hn, meta, counts = pool_router(xf, norm_mix_g[i], pool_w, j, pool_scale[j],
                                               norm_ffn_g[i], moe_router[j], seq)
            pos, tile_expert, tile_valid, pad_lo, pad_hi, used = moe_schedule(meta, counts, n)
            xs = moe_dispatch(hn, pos, pad_lo, pad_hi, used)
            ys = moe_experts(xs, moe_w_gu, moe_w_down, j, tile_expert, tile_valid)
            if last:
                (out,) = moe_combine(xf, ys, meta, pos, norm_final_g, emit_x=False, norm_dtype=F32)
            else:
                xf, hn = moe_combine(xf, ys, meta, pos, norm_mix_g[i + 1], norm_dtype=BF16)
    return out.reshape(batch, seq, d)
```

```python
import functools

import jax
import jax.numpy as jnp
from jax import lax
from jax.experimental import pallas as pl
from jax.experimental.pallas import tpu as pltpu

BF16 = jnp.bfloat16
F32 = jnp.float32

D_MODEL = 2048
DEPTH = 4
RET_HEADS = 8
RET_DK = D_MODEL // RET_HEADS
RET_DV = 2 * D_MODEL // RET_HEADS
RET_QK = RET_HEADS * RET_DK
RET_V = RET_HEADS * RET_DV
RET_CHUNK = 256
ROPE_BASE = 10000.0
POOL_WINDOWS = (2, 4, 8, 16)
POOL_G = D_MODEL // len(POOL_WINDOWS)
POOL_HALO = 16
D_FF = 5632
N_EXPERTS = 8
TOP_K = 2
D_FF_EXPERT = 1408
NORM_EPS = 1e-6
LANES = 128
SUBLANES = 8
MOE_TILE = 256
SIDE_CAST_BLOCKS = 64
META_E1, META_E2, META_R1, META_R2, META_G1, META_G2 = range(6)


def _rms(x, g):
    ms = jnp.mean(x * x, axis=-1, keepdims=True)
    return x * lax.rsqrt(ms + NORM_EPS) * g


def _silu(a):
    return a / (1.0 + jnp.exp(-a))


def _dot(a, b):
    return jnp.dot(a, b, preferred_element_type=F32)


def _rmsnorm_kernel(x_ref, g_ref, o_ref):
    o_ref[...] = _rms(x_ref[...], g_ref[...]).astype(o_ref.dtype)


def rmsnorm(x, g, out_dtype, tm=512):
    n, d = x.shape
    return pl.pallas_call(
        _rmsnorm_kernel,
        grid=(n // tm,),
        in_specs=[pl.BlockSpec((tm, d), lambda i: (i, 0)),
                  pl.BlockSpec((1, d), lambda i: (0, 0))],
        out_specs=pl.BlockSpec((tm, d), lambda i: (i, 0)),
        out_shape=jax.ShapeDtypeStruct((n, d), out_dtype),
        name="rmsnorm",
    )(x, g.reshape(1, d))


def _side_cast_specs(side, side_layer, steps, n_inner):
    _, rows, cols = side.shape
    n_blocks = 1
    while n_blocks * 2 <= min(steps, SIDE_CAST_BLOCKS):
        n_blocks *= 2
    br = rows // n_blocks

    def index(j, i):
        return jnp.minimum(j * n_inner + i, n_blocks - 1)

    in_spec = pl.BlockSpec((None, br, cols), lambda j, i: (side_layer, index(j, i), 0))
    out_spec = pl.BlockSpec((br, cols), lambda j, i: (index(j, i), 0))
    return in_spec, out_spec, jax.ShapeDtypeStruct((rows, cols), BF16)


def _mm_kernel(x_ref, w_ref, s_ref, o_ref, so_ref, wb_ref):
    @pl.when(pl.program_id(1) == 0)
    def _():
        wb_ref[...] = w_ref[...].astype(BF16)

    o_ref[...] = _dot(x_ref[...], wb_ref[...]).astype(o_ref.dtype)
    so_ref[...] = s_ref[...].astype(BF16)


def matmul(x, w, layer, side, side_layer, out_dtype=BF16, tm=1024, tn=1024):
    n, k = x.shape
    f = w.shape[2]
    tm = min(tm, n)
    nj, ni = f // tn, n // tm
    s_in, s_out, s_shape = _side_cast_specs(side, side_layer, nj * ni, ni)
    return pl.pallas_call(
        _mm_kernel,
        grid=(nj, ni),
        in_specs=[pl.BlockSpec((tm, k), lambda j, i: (i, 0)),
                  pl.BlockSpec((None, k, tn), lambda j, i: (layer, 0, j)),
                  s_in],
        out_specs=[pl.BlockSpec((tm, tn), lambda j, i: (i, j)), s_out],
        out_shape=[jax.ShapeDtypeStruct((n, f), out_dtype), s_shape],
        scratch_shapes=[pltpu.VMEM((k, tn), BF16)],
        compiler_params=pltpu.CompilerParams(dimension_semantics=("arbitrary", "arbitrary")),
        name="matmul",
    )(x, w, side)


def _swiglu_kernel(x_ref, ssq_ref, wa_ref, wu_ref, s_ref, o_ref, so_ref, wab_ref, wub_ref):
    @pl.when(pl.program_id(1) == 0)
    def _():
        wab_ref[...] = wa_ref[...].astype(BF16)
        wub_ref[...] = wu_ref[...].astype(BF16)

    x = x_ref[...]
    rowscale = lax.rsqrt(jnp.sum(ssq_ref[...], axis=0) / x.shape[1] + NORM_EPS)
    a = _dot(x, wab_ref[...]) * rowscale
    u = _dot(x, wub_ref[...]) * rowscale
    o_ref[...] = (_silu(a) * u).astype(o_ref.dtype)
    so_ref[...] = s_ref[...].astype(BF16)


def swiglu_up(xg, ssq, w_gu, layer, side, side_layer, tm=1024, tn=512):
    n, k = xg.shape
    f = w_gu.shape[2] // 2
    tm = min(tm, n)
    nj, ni = f // tn, n // tm
    ns = ssq.shape[0]
    s_in, s_out, s_shape = _side_cast_specs(side, side_layer, nj * ni, ni)
    return pl.pallas_call(
        _swiglu_kernel,
        grid=(nj, ni),
        in_specs=[pl.BlockSpec((tm, k), lambda j, i: (i, 0)),
                  pl.BlockSpec((ns, tm, 1), lambda j, i: (0, i, 0)),
                  pl.BlockSpec((None, k, tn), lambda j, i: (layer, 0, j)),
                  pl.BlockSpec((None, k, tn), lambda j, i: (layer, 0, nj + j)),
                  s_in],
        out_specs=[pl.BlockSpec((tm, tn), lambda j, i: (i, j)), s_out],
        out_shape=[jax.ShapeDtypeStruct((n, f), BF16), s_shape],
        scratch_shapes=[pltpu.VMEM((k, tn), BF16), pltpu.VMEM((k, tn), BF16)],
        compiler_params=pltpu.CompilerParams(dimension_semantics=("arbitrary", "arbitrary")),
        name="swiglu_up",
    )(xg, ssq, w_gu, w_gu, side)


def _emit_residual_outputs(xn, g_ref, refs, emit_x, emit_norm):
    pos = 0
    if emit_x:
        refs[pos][...] = xn
        pos += 1
    if emit_norm:
        refs[pos][...] = _rms(xn, g_ref[...]).astype(refs[pos].dtype)


def _residual_out_shapes(n, d, tm, emit_x, norm_dtype, index_map):
    out_shape, out_specs = [], []
    if emit_x:
        out_shape.append(jax.ShapeDtypeStruct((n, d), F32))
        out_specs.append(pl.BlockSpec((tm, d), index_map))
    if norm_dtype is not None:
        out_shape.append(jax.ShapeDtypeStruct((n, d), norm_dtype))
        out_specs.append(pl.BlockSpec((tm, d), index_map))
    return out_shape, out_specs


def _down_kernel(x_ref, w_ref, r_ref, g_ref, *refs, emit_scaled):
    wb_ref = refs[-1]

    @pl.when(pl.program_id(1) == 0)
    def _():
        wb_ref[...] = w_ref[...].astype(BF16)

    xn = r_ref[...] + _dot(x_ref[...], wb_ref[...])
    refs[0][...] = xn
    if emit_scaled:
        refs[1][...] = (xn * g_ref[...]).astype(BF16)
        refs[2][...] = jnp.sum(xn * xn, axis=-1, keepdims=True)


def down_residual(h, w, layer, resid, g=None, *, tm=512, tn=512):
    n, kk = h.shape
    d = w.shape[2]
    tm = min(tm, n)
    nj = d // tn
    emit_scaled = g is not None
    if g is None:
        g = jnp.ones((d,), F32)
    out_shape = [jax.ShapeDtypeStruct((n, d), F32)]
    out_specs = [pl.BlockSpec((tm, tn), lambda j, i: (i, j))]
    if emit_scaled:
        out_shape += [jax.ShapeDtypeStruct((n, d), BF16), jax.ShapeDtypeStruct((nj, n, 1), F32)]
        out_specs += [pl.BlockSpec((tm, tn), lambda j, i: (i, j)),
                      pl.BlockSpec((None, tm, 1), lambda j, i: (j, i, 0))]
    return pl.pallas_call(
        functools.partial(_down_kernel, emit_scaled=emit_scaled),
        grid=(nj, n // tm),
        in_specs=[pl.BlockSpec((tm, kk), lambda j, i: (i, 0)),
                  pl.BlockSpec((None, kk, tn), lambda j, i: (layer, 0, j)),
                  pl.BlockSpec((tm, tn), lambda j, i: (i, j)),
                  pl.BlockSpec((1, tn), lambda j, i: (0, j))],
        out_specs=out_specs,
        out_shape=out_shape,
        scratch_shapes=[pltpu.VMEM((kk, tn), BF16)],
        compiler_params=pltpu.CompilerParams(dimension_semantics=("arbitrary", "arbitrary")),
        name="down_residual",
    )(h, w, resid, g.reshape(1, d))


def _retention_kernel(lg_ref, q_ref, k_ref, v_ref, g_ref, cos_ref, sin_ref, gain_ref, o_ref,
                      qr_ref, qd_ref, kr_ref, kd_ref, mask_ref, qdec_ref, kdec_ref, oraw_ref,
                      state_ref, sb_ref, *, n_chunks):
    c = RET_CHUNK
    lg = lg_ref[pl.program_id(1)]
    ii = lax.broadcasted_iota(jnp.int32, (c, c), 0)
    jj = lax.broadcasted_iota(jnp.int32, (c, c), 1)
    rel = (ii - jj).astype(F32)
    mask_ref[...] = jnp.where(rel >= 0.0, jnp.exp(jnp.maximum(rel, 0.0) * lg), 0.0)
    idx = lax.broadcasted_iota(jnp.int32, (c, 1), 0).astype(F32)
    k_scale = RET_DK ** -0.5
    qdec_ref[...] = jnp.broadcast_to(jnp.exp((idx + 1.0) * lg), (c, RET_DK))
    kdec_ref[...] = jnp.broadcast_to(jnp.exp((c - 1.0 - idx) * lg) * k_scale, (c, RET_DK))
    chunk_decay = jnp.exp(jnp.full((1, 1), float(c), F32) * lg)
    lane = lax.broadcasted_iota(jnp.int32, (c // 2, LANES), 1)
    even = (lane & 1) == 0

    def rotary(xb, cos, sin):
        words = pltpu.bitcast(xb, jnp.uint32)
        halves = []
        for lo in range(0, RET_DK, LANES):
            w = words[:, lo:lo + LANES]
            halves.append(jnp.where(even, pltpu.roll(w, LANES - 1, 1), pltpu.roll(w, 1, 1)))
        partner = pltpu.bitcast(jnp.concatenate(halves, axis=1), BF16)
        return xb.astype(F32) * cos + partner.astype(F32) * sin

    def rows(ci):
        return pl.ds(pl.multiple_of(ci * c, c), c)

    def prepare(ci, carry):
        r = rows(ci)
        cos = cos_ref[r, :]
        sin = sin_ref[r, :]
        q = rotary(q_ref[r, :], cos, sin)
        k = rotary(k_ref[r, :], cos, sin)
        qr_ref[r, :] = q.astype(BF16)
        qd_ref[r, :] = (q * qdec_ref[...]).astype(BF16)
        kr_ref[r, :] = (k * k_scale).astype(BF16)
        kd_ref[r, :] = (k * kdec_ref[...]).astype(BF16)
        return carry

    lax.fori_loop(0, n_chunks, prepare, 0)

    state_ref[...] = jnp.zeros_like(state_ref)
    sb_ref[...] = jnp.zeros_like(sb_ref)

    def recur(ci, carry):
        r = rows(ci)
        v = v_ref[r, :]
        scores = lax.dot_general(qr_ref[r, :], kr_ref[r, :], (((1,), (1,)), ((), ())),
                                 preferred_element_type=F32) * mask_ref[...]
        oraw_ref[r, :] = _dot(scores.astype(BF16), v) + _dot(qd_ref[r, :], sb_ref[...])
        upd = lax.dot_general(kd_ref[r, :], v, (((0,), (0,)), ((), ())), preferred_element_type=F32)
        state = state_ref[...] * chunk_decay + upd
        state_ref[...] = state
        sb_ref[...] = state.astype(BF16)
        return carry

    lax.fori_loop(0, n_chunks, recur, 0)

    def finish(ci, carry):
        r = rows(ci)
        o = oraw_ref[r, :]
        ms = jnp.mean(o * o, axis=-1, keepdims=True)
        o = o * lax.rsqrt(ms + NORM_EPS) * gain_ref[...]
        gate = g_ref[r, :].astype(F32)
        o_ref[r, :] = (o * _silu(gate)).astype(o_ref.dtype)
        return carry

    lax.fori_loop(0, n_chunks, finish, 0)


def retention(proj, norm_g, layer, batch, seq):
    n = batch * seq
    h = RET_HEADS
    log_g = jnp.log1p(-jnp.exp2(-5.0 - jnp.arange(h, dtype=F32)))
    freq = 1.0 / (ROPE_BASE ** jnp.linspace(0.0, 1.0, RET_DK // 2, dtype=F32))
    ang = jnp.arange(seq).astype(F32)[:, None] * freq[None, :]
    cos = jnp.repeat(jnp.cos(ang), 2, axis=-1)
    sin = jnp.stack([-jnp.sin(ang), jnp.sin(ang)], axis=-1).reshape(seq, RET_DK)
    kq = RET_QK // RET_DK
    kv = 2 * RET_QK // RET_DV
    kg = (2 * RET_QK + RET_V) // RET_DV
    return pl.pallas_call(
        functools.partial(_retention_kernel, n_chunks=seq // RET_CHUNK),
        grid=(batch, h),
        in_specs=[pl.BlockSpec(memory_space=pltpu.SMEM),
                  pl.BlockSpec((seq, RET_DK), lambda b, hh: (b, hh)),
                  pl.BlockSpec((seq, RET_DK), lambda b, hh: (b, kq + hh)),
                  pl.BlockSpec((seq, RET_DV), lambda b, hh: (b, kv + hh)),
                  pl.BlockSpec((seq, RET_DV), lambda b, hh: (b, kg + hh)),
                  pl.BlockSpec((seq, RET_DK), lambda b, hh: (0, 0)),
                  pl.BlockSpec((seq, RET_DK), lambda b, hh: (0, 0)),
                  pl.BlockSpec((None, 1, RET_DV), lambda b, hh: (layer, 0, hh))],
        out_specs=pl.BlockSpec((seq, RET_DV), lambda b, hh: (b, hh)),
        out_shape=jax.ShapeDtypeStruct((n, RET_V), BF16),
        scratch_shapes=[pltpu.VMEM((seq, RET_DK), BF16),
                        pltpu.VMEM((seq, RET_DK), BF16),
                        pltpu.VMEM((seq, RET_DK), BF16),
                        pltpu.VMEM((seq, RET_DK), BF16),
                        pltpu.VMEM((RET_CHUNK, RET_CHUNK), F32),
                        pltpu.VMEM((RET_CHUNK, RET_DK), F32),
                        pltpu.VMEM((RET_CHUNK, RET_DK), F32),
                        pltpu.VMEM((seq, RET_DV), F32),
                        pltpu.VMEM((RET_DK, RET_DV), F32),
                        pltpu.VMEM((RET_DK, RET_DV), BF16)],
        name="retention",
    )(log_g, proj, proj, proj, proj, cos, sin, norm_g.reshape(norm_g.shape[0], 1, RET_V))


def _pool_kernel(x_ref, gm_ref, pw_ref, sc_ref, gf_ref, rt_ref,
                 xo_ref, hn_ref, meta_ref, meta_t_ref, cnt_out_ref, hbuf, cnt_ref, *, tiles_per_seq):
    tm = x_ref.shape[0]
    step = pl.program_id(0)
    ti = step % tiles_per_seq
    x = x_ref[...]
    hn = _rms(x, gm_ref[...])

    @pl.when(ti == 0)
    def _():
        hbuf[0:POOL_HALO, :] = jnp.zeros((POOL_HALO, D_MODEL), F32)

    @pl.when(ti != 0)
    def _():
        hbuf[0:POOL_HALO, :] = hbuf[tm:tm + POOL_HALO, :]

    hbuf[POOL_HALO:POOL_HALO + tm, :] = hn

    t = ti * tm + lax.broadcasted_iota(jnp.int32, (tm, 1), 0)
    for gi, w in enumerate(POOL_WINDOWS):
        lo, hi = gi * POOL_G, (gi + 1) * POOL_G
        cur = hn[:, lo:hi]
        acc = hbuf[:, lo:hi]
        shift = 1
        while shift < w:
            acc = acc + pltpu.roll(acc, shift, 0)
            shift *= 2
        acc = acc[POOL_HALO:, :]
        inv = 1.0 / jnp.minimum(t + 1, w).astype(F32)
        mix = (acc * inv - cur).astype(BF16)
        o = _dot(mix, pw_ref[gi]) * sc_ref[:, lo:hi]
        xo_ref[:, lo:hi] = x[:, lo:hi] + o

    hn2 = _rms(xo_ref[...], gf_ref[...])
    hn_ref[...] = hn2

    rt = rt_ref[...]
    hn_hi = hn2.astype(BF16)
    hn_lo = (hn2 - hn_hi.astype(F32)).astype(BF16)
    rt_hi = rt.astype(BF16)
    rt_lo = (rt - rt_hi.astype(F32)).astype(BF16)
    logits = _dot(hn_hi, rt_hi) + (_dot(hn_hi, rt_lo) + _dot(hn_lo, rt_hi))
    lane = lax.broadcasted_iota(jnp.int32, logits.shape, 1)
    neg = jnp.float32(-jnp.inf)
    lg = jnp.where(lane < N_EXPERTS, logits, neg)
    m1 = jnp.max(lg, axis=-1, keepdims=True)
    i1 = jnp.min(jnp.where(lg == m1, lane, LANES), axis=-1, keepdims=True)
    lg2 = jnp.where(lane == i1, neg, lg)
    m2 = jnp.max(lg2, axis=-1, keepdims=True)
    i2 = jnp.min(jnp.where(lg2 == m2, lane, LANES), axis=-1, keepdims=True)
    e2 = jnp.exp(m2 - m1)
    g1 = 1.0 / (1.0 + e2)
    g2 = e2 * g1

    @pl.when(step == 0)
    def _():
        cnt_ref[...] = jnp.zeros_like(cnt_ref)

    chosen = jnp.where((lane == i1) | (lane == i2), 1.0, 0.0)
    rr = lax.broadcasted_iota(jnp.int32, (tm, tm), 0)
    cc = lax.broadcasted_iota(jnp.int32, (tm, tm), 1)
    earlier = jnp.where(rr > cc, 1.0, 0.0).astype(BF16)
    before = _dot(earlier, chosen.astype(BF16)) + cnt_ref[...]
    r1 = jnp.sum(jnp.where(lane == i1, before, 0.0), axis=-1, keepdims=True)
    r2 = jnp.sum(jnp.where(lane == i2, before, 0.0), axis=-1, keepdims=True)
    cnt_ref[...] += jnp.sum(chosen, axis=0, keepdims=True)
    cnt_out_ref[...] = jnp.broadcast_to(cnt_ref[...], cnt_out_ref.shape)

    meta = jnp.zeros(logits.shape, F32)
    for li, field in ((META_E1, i1.astype(F32)), (META_E2, i2.astype(F32)), (META_R1, r1),
                      (META_R2, r2), (META_G1, g1), (META_G2, g2)):
        meta = jnp.where(lane == li, field, meta)
    meta_ref[...] = meta
    meta_t_ref[...] = meta.T[:SUBLANES, :]


def pool_router(x, g_mix, pool_w, layer, pool_scale, g_ffn, router, seq, tm=512):
    n, d = x.shape
    tm = min(tm, seq)
    router_p = jnp.zeros((d, LANES), F32).at[:, :N_EXPERTS].set(router)
    ng = len(POOL_WINDOWS)
    return pl.pallas_call(
        functools.partial(_pool_kernel, tiles_per_seq=seq // tm),
        grid=(n // tm,),
        in_specs=[pl.BlockSpec((tm, d), lambda i: (i, 0)),
                  pl.BlockSpec((1, d), lambda i: (0, 0)),
                  pl.BlockSpec((None, ng, POOL_G, POOL_G), lambda i: (layer, 0, 0, 0)),
                  pl.BlockSpec((1, d), lambda i: (0, 0)),
                  pl.BlockSpec((1, d), lambda i: (0, 0)),
                  pl.BlockSpec((d, LANES), lambda i: (0, 0))],
        out_specs=[pl.BlockSpec((tm, d), lambda i: (i, 0)),
                   pl.BlockSpec((tm, d), lambda i: (i, 0)),
                   pl.BlockSpec((tm, LANES), lambda i: (i, 0)),
                   pl.BlockSpec((SUBLANES, tm), lambda i: (0, i)),
                   pl.BlockSpec((SUBLANES, LANES), lambda i: (0, 0))],
        out_shape=[jax.ShapeDtypeStruct((n, d), F32),
                   jax.ShapeDtypeStruct((n, d), F32),
                   jax.ShapeDtypeStruct((n, LANES), F32),
                   jax.ShapeDtypeStruct((SUBLANES, n), F32),
                   jax.ShapeDtypeStruct((SUBLANES, LANES), F32)],
        scratch_shapes=[pltpu.VMEM((POOL_HALO + tm, d), F32),
                        pltpu.VMEM((1, LANES), F32)],
        compiler_params=pltpu.CompilerParams(dimension_semantics=("arbitrary",)),
        name="pool_router",
    )(x, g_mix.reshape(1, d), pool_w, pool_scale.reshape(1, d), g_ffn.reshape(1, d), router_p)


def moe_num_tiles(n):
    return TOP_K * n // MOE_TILE + N_EXPERTS


def moe_schedule(meta_t, counts_f, n):
    nt = moe_num_tiles(n)
    counts = counts_f[0, :N_EXPERTS].astype(jnp.int32)
    tiles = (counts + MOE_TILE - 1) // MOE_TILE
    tile_end = jnp.cumsum(tiles)
    starts = (tile_end - tiles) * MOE_TILE
    expert_ids = jnp.arange(N_EXPERTS, dtype=jnp.int32)[:, None]

    def slots(e_row, r_row):
        e = meta_t[e_row].astype(jnp.int32)
        first = jnp.sum(jnp.where(e[None, :] == expert_ids, starts[:, None], 0), axis=0)
        return first + meta_t[r_row].astype(jnp.int32)

    pos = jnp.concatenate([slots(META_E1, META_R1), slots(META_E2, META_R2)])
    total = tile_end[-1]
    t = jnp.arange(nt, dtype=jnp.int32)
    t_used = jnp.minimum(t, total - 1)
    tile_expert = jnp.sum((tile_end[None, :] <= t_used[:, None]).astype(jnp.int32), axis=1)
    tile_expert = jnp.minimum(tile_expert, N_EXPERTS - 1)
    tile_valid = (t < total).astype(jnp.int32)
    pad_lo = starts + counts
    pad_hi = starts + tiles * MOE_TILE
    return pos, tile_expert, tile_valid, pad_lo, pad_hi, total.reshape(1)


def _dispatch_kernel(pos_ref, lo_ref, hi_ref, used_ref, hn_ref, xs_hbm, zeros, sem, *, n, n_tiles):
    tb = hn_ref.shape[0]
    step = pl.program_id(0)
    base = step * tb

    def issue(r, c):
        tok = base + r
        src = hn_ref.at[pl.ds(r, 1), :]
        pltpu.make_async_copy(src, xs_hbm.at[pl.ds(pos_ref[tok], 1), :], sem).start()
        pltpu.make_async_copy(src, xs_hbm.at[pl.ds(pos_ref[n + tok], 1), :], sem).start()
        return c

    lax.fori_loop(0, tb, issue, 0, unroll=8)
    for _ in range(TOP_K):
        pltpu.make_async_copy(hn_ref, xs_hbm.at[pl.ds(0, tb), :], sem).wait()

    @pl.when(step == pl.num_programs(0) - 1)
    def _():
        zeros[...] = jnp.zeros_like(zeros)

        def pad_copy(slot):
            return pltpu.make_async_copy(zeros.at[pl.ds(0, 1), :], xs_hbm.at[pl.ds(slot, 1), :], sem)

        def issue_pad(s, c):
            pad_copy(s).start()
            return c

        def drain_pad(s, c):
            pad_copy(s).wait()
            return c

        for e in range(N_EXPERTS):
            lax.fori_loop(lo_ref[e], hi_ref[e], issue_pad, 0)
            lax.fori_loop(lo_ref[e], hi_ref[e], drain_pad, 0)

        def tile_copy(t):
            return pltpu.make_async_copy(zeros, xs_hbm.at[pl.ds(t * MOE_TILE, MOE_TILE), :], sem)

        def issue_tile(t, c):
            tile_copy(t).start()
            return c

        def drain_tile(t, c):
            tile_copy(t).wait()
            return c

        lax.fori_loop(used_ref[0], n_tiles, issue_tile, 0)
        lax.fori_loop(used_ref[0], n_tiles, drain_tile, 0)


def moe_dispatch(hn, pos, pad_lo, pad_hi, used, tb=512):
    n, d = hn.shape
    tb = min(tb, n)
    nt = moe_num_tiles(n)
    grid_spec = pltpu.PrefetchScalarGridSpec(
        num_scalar_prefetch=4,
        grid=(n // tb,),
        in_specs=[pl.BlockSpec((tb, d), lambda i, *_: (i, 0))],
        out_specs=pl.BlockSpec(memory_space=pl.ANY),
        scratch_shapes=[pltpu.VMEM((MOE_TILE, d), F32), pltpu.SemaphoreType.DMA],
    )
    return pl.pallas_call(
        functools.partial(_dispatch_kernel, n=n, n_tiles=nt),
        grid_spec=grid_spec,
        out_shape=jax.ShapeDtypeStruct((nt * MOE_TILE, d), F32),
        compiler_params=pltpu.CompilerParams(dimension_semantics=("arbitrary",)),
        name="moe_dispatch",
    )(pos, pad_lo, pad_hi, used, hn)


def _expert_kernel(te_ref, tv_ref, x_ref, wa_ref, wu_ref, wd_ref, o_ref):
    valid = tv_ref[pl.program_id(0)] == 1

    @pl.when(valid)
    def _():
        x = x_ref[...].astype(BF16)
        a = _dot(x, wa_ref[...])
        u = _dot(x, wu_ref[...])
        o_ref[...] = _dot((_silu(a) * u).astype(BF16), wd_ref[...])

    @pl.when(jnp.logical_not(valid))
    def _():
        o_ref[...] = jnp.zeros_like(o_ref)


def moe_experts(xs, w_gu, w_down, tile_expert, tile_valid):
    rows, d = xs.shape
    f = w_down.shape[1]
    nt = rows // MOE_TILE
    grid_spec = pltpu.PrefetchScalarGridSpec(
        num_scalar_prefetch=2,
        grid=(nt,),
        in_specs=[pl.BlockSpec((MOE_TILE, d), lambda t, te, tv: (t, 0)),
                  pl.BlockSpec((None, d, f), lambda t, te, tv: (te[t], 0, 0)),
                  pl.BlockSpec((None, d, f), lambda t, te, tv: (te[t], 0, 1)),
                  pl.BlockSpec((None, f, d), lambda t, te, tv: (te[t], 0, 0))],
        out_specs=pl.BlockSpec((MOE_TILE, d), lambda t, te, tv: (t, 0)),
    )
    return pl.pallas_call(
        _expert_kernel,
        grid_spec=grid_spec,
        out_shape=jax.ShapeDtypeStruct((rows, d), F32),
        compiler_params=pltpu.CompilerParams(dimension_semantics=("arbitrary",),
                                             vmem_limit_bytes=56 * 1024 * 1024),
        name="moe_experts",
    )(tile_expert, tile_valid, xs, w_gu, w_gu, w_down)


def _combine_kernel(pos_ref, x_ref, meta_ref, g_ref, y_hbm, *refs, n, emit_x, emit_norm):
    ybuf, sem = refs[-2], refs[-1]
    tc = x_ref.shape[0]
    step = pl.program_id(0)
    slot = step % 2

    def start_gather(s, into):
        def issue(r, c):
            tok = s * tc + r
            for k in range(TOP_K):
                pltpu.make_async_copy(y_hbm.at[pl.ds(pos_ref[k * n + tok], 1), :],
                                      ybuf.at[into, k, pl.ds(r, 1), :], sem.at[into]).start()
            return c

        lax.fori_loop(0, tc, issue, 0, unroll=8)

    @pl.when(step == 0)
    def _():
        start_gather(0, 0)

    @pl.when(step + 1 < pl.num_programs(0))
    def _():
        start_gather(step + 1, 1 - slot)

    for k in range(TOP_K):
        pltpu.make_async_copy(y_hbm.at[pl.ds(0, tc), :], ybuf.at[slot, k], sem.at[slot]).wait()

    meta = meta_ref[...]
    g1 = meta[:, META_G1:META_G1 + 1]
    g2 = meta[:, META_G2:META_G2 + 1]
    xn = x_ref[...] + g1 * ybuf[slot, 0] + g2 * ybuf[slot, 1]
    _emit_residual_outputs(xn, g_ref, refs, emit_x, emit_norm)


def moe_combine(x, y, meta, pos, g, *, emit_x=True, norm_dtype=None, tc=256):
    n, d = x.shape
    tc = min(tc, n)
    out_shape, out_specs = _residual_out_shapes(n, d, tc, emit_x, norm_dtype, lambda i, p: (i, 0))
    grid_spec = pltpu.PrefetchScalarGridSpec(
        num_scalar_prefetch=1,
        grid=(n // tc,),
        in_specs=[pl.BlockSpec((tc, d), lambda i, p: (i, 0)),
                  pl.BlockSpec((tc, LANES), lambda i, p: (i, 0)),
                  pl.BlockSpec((1, d), lambda i, p: (0, 0)),
                  pl.BlockSpec(memory_space=pl.ANY)],
        out_specs=out_specs,
        scratch_shapes=[pltpu.VMEM((2, TOP_K, tc, d), F32), pltpu.SemaphoreType.DMA((2,))],
    )
    return pl.pallas_call(
        functools.partial(_combine_kernel, n=n, emit_x=emit_x, emit_norm=norm_dtype is not None),
        grid_spec=grid_spec,
        out_shape=out_shape,
        compiler_params=pltpu.CompilerParams(dimension_semantics=("arbitrary",)),
        name="moe_combine",
    )(pos, x, meta, g.reshape(1, d), y)


def kernel(x, ret_w_in, ret_norm_g, ret_w_out, pool_w, pool_scale, ffn_w_gu, ffn_w_down,
           moe_router, moe_w_gu, moe_w_down, norm_mix_g, norm_ffn_g, norm_final_g):
    batch, seq, d = x.shape
    n = batch * seq
    xf = x.reshape(n, d)
    pool_w = pool_w.astype(BF16)
    n_moe, n_exp, _, f2 = moe_w_gu.shape
    f = moe_w_down.shape[2]
    moe_gu_rows = moe_w_gu.reshape(n_moe, n_exp * d, f2)
    moe_down_rows = moe_w_down.reshape(n_moe, n_exp * f, d)
    hn = rmsnorm(xf, norm_mix_g[0], BF16)
    for i in range(DEPTH):
        j = i // 2
        last = i == DEPTH - 1
        if i % 2 == 0:
            proj, moe_gu_b = matmul(hn, ret_w_in, j, moe_gu_rows, j)
            y = retention(proj, ret_norm_g, j, batch, seq)
            xf, xg, ssq = down_residual(y, ret_w_out, j, xf, norm_ffn_g[i], tm=1024)
            hid, moe_down_b = swiglu_up(xg, ssq, ffn_w_gu, j, moe_down_rows, j)
            (xf,) = down_residual(hid, ffn_w_down, j, xf)
            moe_gu_b = moe_gu_b.reshape(n_exp, d, f2)
            moe_down_b = moe_down_b.reshape(n_exp, f, d)
        else:
            xf, hn, meta, meta_t, counts = pool_router(xf, norm_mix_g[i], pool_w, j, pool_scale[j],
                                               norm_ffn_g[i], moe_router[j], seq)
            pos, tile_expert, tile_valid, pad_lo, pad_hi, used = moe_schedule(meta_t, counts, n)
            xs = moe_dispatch(hn, pos, pad_lo, pad_hi, used)
            ys = moe_experts(xs, moe_gu_b, moe_down_b, tile_expert, tile_valid)
            if last:
                (out,) = moe_combine(xf, ys, meta, pos, norm_final_g, emit_x=False, norm_dtype=F32)
            else:
                xf, hn = moe_combine(xf, ys, meta, pos, norm_mix_g[i + 1], norm_dtype=BF16)
    return out.reshape(batch, seq, d)
```

```python
import functools

import jax
import jax.numpy as jnp
from jax import lax
from jax.experimental import pallas as pl
from jax.experimental.pallas import tpu as pltpu

BF16 = jnp.bfloat16
F32 = jnp.float32

D_MODEL = 2048
DEPTH = 4
RET_HEADS = 8
RET_DK = D_MODEL // RET_HEADS
RET_DV = 2 * D_MODEL // RET_HEADS
RET_QK = RET_HEADS * RET_DK
RET_V = RET_HEADS * RET_DV
RET_CHUNK = 256
ROPE_BASE = 10000.0
POOL_WINDOWS = (2, 4, 8, 16)
POOL_G = D_MODEL // len(POOL_WINDOWS)
POOL_HALO = 16
D_FF = 5632
N_EXPERTS = 8
TOP_K = 2
D_FF_EXPERT = 1408
NORM_EPS = 1e-6
LANES = 128
SUBLANES = 8
MOE_TILE = 256
SIDE_CAST_BLOCKS = 64
META_E1, META_E2, META_R1, META_R2, META_G1, META_G2 = range(6)


def _rms(x, g):
    ms = jnp.mean(x * x, axis=-1, keepdims=True)
    return x * lax.rsqrt(ms + NORM_EPS) * g


def _silu(a):
    return a / (1.0 + jnp.exp(-a))


def _dot(a, b):
    return jnp.dot(a, b, preferred_element_type=F32)


def _rmsnorm_kernel(x_ref, g_ref, o_ref):
    o_ref[...] = _rms(x_ref[...], g_ref[...]).astype(o_ref.dtype)


def rmsnorm(x, g, out_dtype, tm=512):
    n, d = x.shape
    return pl.pallas_call(
        _rmsnorm_kernel,
        grid=(n // tm,),
        in_specs=[pl.BlockSpec((tm, d), lambda i: (i, 0)),
                  pl.BlockSpec((1, d), lambda i: (0, 0))],
        out_specs=pl.BlockSpec((tm, d), lambda i: (i, 0)),
        out_shape=jax.ShapeDtypeStruct((n, d), out_dtype),
        name="rmsnorm",
    )(x, g.reshape(1, d))


def _side_cast_specs(side, side_layer, steps, n_inner):
    _, rows, cols = side.shape
    n_blocks = 1
    while n_blocks * 2 <= min(steps, SIDE_CAST_BLOCKS):
        n_blocks *= 2
    br = rows // n_blocks

    def index(j, i):
        return jnp.minimum(j * n_inner + i, n_blocks - 1)

    in_spec = pl.BlockSpec((None, br, cols), lambda j, i: (side_layer, index(j, i), 0))
    out_spec = pl.BlockSpec((br, cols), lambda j, i: (index(j, i), 0))
    return in_spec, out_spec, jax.ShapeDtypeStruct((rows, cols), BF16)


def _mm_kernel(x_ref, w_ref, s_ref, o_ref, so_ref, wb_ref):
    @pl.when(pl.program_id(1) == 0)
    def _():
        wb_ref[...] = w_ref[...].astype(BF16)

    o_ref[...] = _dot(x_ref[...], wb_ref[...]).astype(o_ref.dtype)
    so_ref[...] = s_ref[...].astype(BF16)


def matmul(x, w, layer, side, side_layer, out_dtype=BF16, tm=1024, tn=1024):
    n, k = x.shape
    f = w.shape[2]
    tm = min(tm, n)
    nj, ni = f // tn, n // tm
    s_in, s_out, s_shape = _side_cast_specs(side, side_layer, nj * ni, ni)
    return pl.pallas_call(
        _mm_kernel,
        grid=(nj, ni),
        in_specs=[pl.BlockSpec((tm, k), lambda j, i: (i, 0)),
                  pl.BlockSpec((None, k, tn), lambda j, i: (layer, 0, j)),
                  s_in],
        out_specs=[pl.BlockSpec((tm, tn), lambda j, i: (i, j)), s_out],
        out_shape=[jax.ShapeDtypeStruct((n, f), out_dtype), s_shape],
        scratch_shapes=[pltpu.VMEM((k, tn), BF16)],
        compiler_params=pltpu.CompilerParams(dimension_semantics=("arbitrary", "arbitrary")),
        name="matmul",
    )(x, w, side)


def _swiglu_kernel(x_ref, ssq_ref, wa_ref, wu_ref, s_ref, o_ref, so_ref, wab_ref, wub_ref):
    @pl.when(pl.program_id(1) == 0)
    def _():
        wab_ref[...] = wa_ref[...].astype(BF16)
        wub_ref[...] = wu_ref[...].astype(BF16)

    x = x_ref[...]
    rowscale = lax.rsqrt(jnp.sum(ssq_ref[...], axis=0) / x.shape[1] + NORM_EPS)
    a = _dot(x, wab_ref[...]) * rowscale
    u = _dot(x, wub_ref[...]) * rowscale
    o_ref[...] = (_silu(a) * u).astype(o_ref.dtype)
    so_ref[...] = s_ref[...].astype(BF16)


def swiglu_up(xg, ssq, w_gu, layer, side, side_layer, tm=1024, tn=512):
    n, k = xg.shape
    f = w_gu.shape[2] // 2
    tm = min(tm, n)
    nj, ni = f // tn, n // tm
    ns = ssq.shape[0]
    s_in, s_out, s_shape = _side_cast_specs(side, side_layer, nj * ni, ni)
    return pl.pallas_call(
        _swiglu_kernel,
        grid=(nj, ni),
        in_specs=[pl.BlockSpec((tm, k), lambda j, i: (i, 0)),
                  pl.BlockSpec((ns, tm, 1), lambda j, i: (0, i, 0)),
                  pl.BlockSpec((None, k, tn), lambda j, i: (layer, 0, j)),
                  pl.BlockSpec((None, k, tn), lambda j, i: (layer, 0, nj + j)),
                  s_in],
        out_specs=[pl.BlockSpec((tm, tn), lambda j, i: (i, j)), s_out],
        out_shape=[jax.ShapeDtypeStruct((n, f), BF16), s_shape],
        scratch_shapes=[pltpu.VMEM((k, tn), BF16), pltpu.VMEM((k, tn), BF16)],
        compiler_params=pltpu.CompilerParams(dimension_semantics=("arbitrary", "arbitrary")),
        name="swiglu_up",
    )(xg, ssq, w_gu, w_gu, side)


def _emit_residual_outputs(xn, g_ref, refs, emit_x, emit_norm):
    pos = 0
    if emit_x:
        refs[pos][...] = xn
        pos += 1
    if emit_norm:
        refs[pos][...] = _rms(xn, g_ref[...]).astype(refs[pos].dtype)


def _residual_out_shapes(n, d, tm, emit_x, norm_dtype, index_map):
    out_shape, out_specs = [], []
    if emit_x:
        out_shape.append(jax.ShapeDtypeStruct((n, d), F32))
        out_specs.append(pl.BlockSpec((tm, d), index_map))
    if norm_dtype is not None:
        out_shape.append(jax.ShapeDtypeStruct((n, d), norm_dtype))
        out_specs.append(pl.BlockSpec((tm, d), index_map))
    return out_shape, out_specs


def _down_kernel(x_ref, w_ref, r_ref, g_ref, *refs, emit_scaled):
    wb_ref = refs[-1]

    @pl.when(pl.program_id(1) == 0)
    def _():
        wb_ref[...] = w_ref[...].astype(BF16)

    xn = r_ref[...] + _dot(x_ref[...], wb_ref[...])
    refs[0][...] = xn
    if emit_scaled:
        refs[1][...] = (xn * g_ref[...]).astype(BF16)
        refs[2][...] = jnp.sum(xn * xn, axis=-1, keepdims=True)


def down_residual(h, w, layer, resid, g=None, *, tm=512, tn=512):
    n, kk = h.shape
    d = w.shape[2]
    tm = min(tm, n)
    nj = d // tn
    emit_scaled = g is not None
    if g is None:
        g = jnp.ones((d,), F32)
    out_shape = [jax.ShapeDtypeStruct((n, d), F32)]
    out_specs = [pl.BlockSpec((tm, tn), lambda j, i: (i, j))]
    if emit_scaled:
        out_shape += [jax.ShapeDtypeStruct((n, d), BF16), jax.ShapeDtypeStruct((nj, n, 1), F32)]
        out_specs += [pl.BlockSpec((tm, tn), lambda j, i: (i, j)),
                      pl.BlockSpec((None, tm, 1), lambda j, i: (j, i, 0))]
    return pl.pallas_call(
        functools.partial(_down_kernel, emit_scaled=emit_scaled),
        grid=(nj, n // tm),
        in_specs=[pl.BlockSpec((tm, kk), lambda j, i: (i, 0)),
                  pl.BlockSpec((None, kk, tn), lambda j, i: (layer, 0, j)),
                  pl.BlockSpec((tm, tn), lambda j, i: (i, j)),
                  pl.BlockSpec((1, tn), lambda j, i: (0, j))],
        out_specs=out_specs,
        out_shape=out_shape,
        scratch_shapes=[pltpu.VMEM((kk, tn), BF16)],
        compiler_params=pltpu.CompilerParams(dimension_semantics=("arbitrary", "arbitrary")),
        name="down_residual",
    )(h, w, resid, g.reshape(1, d))


def _retention_kernel(lg_ref, q_ref, k_ref, v_ref, g_ref, cos_ref, sin_ref, gain_ref, o_ref,
                      qr_ref, qd_ref, kr_ref, kd_ref, mask_ref, qdec_ref, kdec_ref, oraw_ref,
                      state_ref, sb_ref, *, n_chunks, n_pairs):
    c = RET_CHUNK
    step = pl.program_id(0)
    slot_prep = step % 2
    slot_recur = 1 - slot_prep
    slot_finish = slot_prep

    @pl.when(step == 0)
    def _():
        for ref in (qr_ref, qd_ref, kr_ref, kd_ref, oraw_ref):
            ref[...] = jnp.zeros_like(ref)

    lg_prep = lg_ref[jnp.minimum(step, n_pairs - 1) % RET_HEADS]
    lg = lg_ref[jnp.clip(step - 1, 0, n_pairs - 1) % RET_HEADS]
    ii = lax.broadcasted_iota(jnp.int32, (c, c), 0)
    jj = lax.broadcasted_iota(jnp.int32, (c, c), 1)
    rel = (ii - jj).astype(F32)
    mask_ref[...] = jnp.where(rel >= 0.0, jnp.exp(jnp.maximum(rel, 0.0) * lg), 0.0)
    idx = lax.broadcasted_iota(jnp.int32, (c, 1), 0).astype(F32)
    k_scale = RET_DK ** -0.5
    qdec_ref[...] = jnp.broadcast_to(jnp.exp((idx + 1.0) * lg_prep), (c, RET_DK))
    kdec_ref[...] = jnp.broadcast_to(jnp.exp((c - 1.0 - idx) * lg_prep) * k_scale, (c, RET_DK))
    chunk_decay = jnp.exp(jnp.full((1, 1), float(c), F32) * lg)
    lane = lax.broadcasted_iota(jnp.int32, (c // 2, LANES), 1)
    even = (lane & 1) == 0

    def rotary(xb, cos, sin):
        words = pltpu.bitcast(xb, jnp.uint32)
        halves = []
        for lo in range(0, RET_DK, LANES):
            w = words[:, lo:lo + LANES]
            halves.append(jnp.where(even, pltpu.roll(w, LANES - 1, 1), pltpu.roll(w, 1, 1)))
        partner = pltpu.bitcast(jnp.concatenate(halves, axis=1), BF16)
        return xb.astype(F32) * cos + partner.astype(F32) * sin

    def rows(ci):
        return pl.ds(pl.multiple_of(ci * c, c), c)

    state_ref[...] = jnp.zeros_like(state_ref)
    sb_ref[...] = jnp.zeros_like(sb_ref)

    def body(ci, carry):
        r = rows(ci)

        o = oraw_ref[slot_finish, r, :]
        ms = jnp.mean(o * o, axis=-1, keepdims=True)
        o = o * lax.rsqrt(ms + NORM_EPS) * gain_ref[...]
        gate = g_ref[r, :].astype(F32)
        o_ref[r, :] = (o * _silu(gate)).astype(o_ref.dtype)

        v = v_ref[r, :]
        scores = lax.dot_general(qr_ref[slot_recur, r, :], kr_ref[slot_recur, r, :],
                                 (((1,), (1,)), ((), ())),
                                 preferred_element_type=F32) * mask_ref[...]
        oraw_ref[slot_recur, r, :] = (_dot(scores.astype(BF16), v)
                                      + _dot(qd_ref[slot_recur, r, :], sb_ref[...]))
        upd = lax.dot_general(kd_ref[slot_recur, r, :], v, (((0,), (0,)), ((), ())),
                              preferred_element_type=F32)
        state = state_ref[...] * chunk_decay + upd
        state_ref[...] = state
        sb_ref[...] = state.astype(BF16)

        cos = cos_ref[r, :]
        sin = sin_ref[r, :]
        q = rotary(q_ref[r, :], cos, sin)
        k = rotary(k_ref[r, :], cos, sin)
        qr_ref[slot_prep, r, :] = q.astype(BF16)
        qd_ref[slot_prep, r, :] = (q * qdec_ref[...]).astype(BF16)
        kr_ref[slot_prep, r, :] = (k * k_scale).astype(BF16)
        kd_ref[slot_prep, r, :] = (k * kdec_ref[...]).astype(BF16)
        return carry

    lax.fori_loop(0, n_chunks, body, 0)


def retention(proj, norm_g, layer, batch, seq):
    n = batch * seq
    h = RET_HEADS
    log_g = jnp.log1p(-jnp.exp2(-5.0 - jnp.arange(h, dtype=F32)))
    freq = 1.0 / (ROPE_BASE ** jnp.linspace(0.0, 1.0, RET_DK // 2, dtype=F32))
    ang = jnp.arange(seq).astype(F32)[:, None] * freq[None, :]
    cos = jnp.repeat(jnp.cos(ang), 2, axis=-1)
    sin = jnp.stack([-jnp.sin(ang), jnp.sin(ang)], axis=-1).reshape(seq, RET_DK)
    kq = RET_QK // RET_DK
    kv = 2 * RET_QK // RET_DV
    kg = (2 * RET_QK + RET_V) // RET_DV
    n_pairs = batch * h
    stages = 3

    def pair(step, stage):
        p = jnp.clip(step - stage, 0, n_pairs - 1)
        return p // h, p % h

    def block(stage, col0):
        def index(s):
            b, hh = pair(s, stage)
            return b, col0 + hh
        return index

    def gain_index(s):
        return layer, 0, pair(s, 2)[1]

    return pl.pallas_call(
        functools.partial(_retention_kernel, n_chunks=seq // RET_CHUNK, n_pairs=n_pairs),
        grid=(n_pairs + stages - 1,),
        in_specs=[pl.BlockSpec(memory_space=pltpu.SMEM),
                  pl.BlockSpec((seq, RET_DK), block(0, 0)),
                  pl.BlockSpec((seq, RET_DK), block(0, kq)),
                  pl.BlockSpec((seq, RET_DV), block(1, kv)),
                  pl.BlockSpec((seq, RET_DV), block(2, kg)),
                  pl.BlockSpec((seq, RET_DK), lambda s: (0, 0)),
                  pl.BlockSpec((seq, RET_DK), lambda s: (0, 0)),
                  pl.BlockSpec((None, 1, RET_DV), gain_index)],
        out_specs=pl.BlockSpec((seq, RET_DV), block(2, 0)),
        out_shape=jax.ShapeDtypeStruct((n, RET_V), BF16),
        scratch_shapes=[pltpu.VMEM((2, seq, RET_DK), BF16),
                        pltpu.VMEM((2, seq, RET_DK), BF16),
                        pltpu.VMEM((2, seq, RET_DK), BF16),
                        pltpu.VMEM((2, seq, RET_DK), BF16),
                        pltpu.VMEM((RET_CHUNK, RET_CHUNK), F32),
                        pltpu.VMEM((RET_CHUNK, RET_DK), F32),
                        pltpu.VMEM((RET_CHUNK, RET_DK), F32),
                        pltpu.VMEM((2, seq, RET_DV), F32),
                        pltpu.VMEM((RET_DK, RET_DV), F32),
                        pltpu.VMEM((RET_DK, RET_DV), BF16)],
        compiler_params=pltpu.CompilerParams(dimension_semantics=("arbitrary",)),
        name="retention",
    )(log_g, proj, proj, proj, proj, cos, sin, norm_g.reshape(norm_g.shape[0], 1, RET_V))


def _pool_kernel(x_ref, gm_ref, pw_ref, sc_ref, gf_ref, rt_ref,
                 xo_ref, hn_ref, meta_ref, meta_t_ref, cnt_out_ref, hbuf, cnt_ref, *, tiles_per_seq):
    tm = x_ref.shape[0]
    step = pl.program_id(0)
    ti = step % tiles_per_seq
    x = x_ref[...]
    hn = _rms(x, gm_ref[...])

    @pl.when(ti == 0)
    def _():
        hbuf[0:POOL_HALO, :] = jnp.zeros((POOL_HALO, D_MODEL), F32)

    @pl.when(ti != 0)
    def _():
        hbuf[0:POOL_HALO, :] = hbuf[tm:tm + POOL_HALO, :]

    hbuf[POOL_HALO:POOL_HALO + tm, :] = hn

    t = ti * tm + lax.broadcasted_iota(jnp.int32, (tm, 1), 0)
    for gi, w in enumerate(POOL_WINDOWS):
        lo, hi = gi * POOL_G, (gi + 1) * POOL_G
        cur = hn[:, lo:hi]
        acc = hbuf[:, lo:hi]
        shift = 1
        while shift < w:
            acc = acc + pltpu.roll(acc, shift, 0)
            shift *= 2
        acc = acc[POOL_HALO:, :]
        inv = 1.0 / jnp.minimum(t + 1, w).astype(F32)
        mix = (acc * inv - cur).astype(BF16)
        o = _dot(mix, pw_ref[gi]) * sc_ref[:, lo:hi]
        xo_ref[:, lo:hi] = x[:, lo:hi] + o

    hn2 = _rms(xo_ref[...], gf_ref[...])
    hn_ref[...] = hn2

    rt = rt_ref[...]
    hn_hi = hn2.astype(BF16)
    hn_lo = (hn2 - hn_hi.astype(F32)).astype(BF16)
    rt_hi = rt.astype(BF16)
    rt_lo = (rt - rt_hi.astype(F32)).astype(BF16)
    logits = _dot(hn_hi, rt_hi) + (_dot(hn_hi, rt_lo) + _dot(hn_lo, rt_hi))
    lane = lax.broadcasted_iota(jnp.int32, logits.shape, 1)
    neg = jnp.float32(-jnp.inf)
    lg = jnp.where(lane < N_EXPERTS, logits, neg)
    m1 = jnp.max(lg, axis=-1, keepdims=True)
    i1 = jnp.min(jnp.where(lg == m1, lane, LANES), axis=-1, keepdims=True)
    lg2 = jnp.where(lane == i1, neg, lg)
    m2 = jnp.max(lg2, axis=-1, keepdims=True)
    i2 = jnp.min(jnp.where(lg2 == m2, lane, LANES), axis=-1, keepdims=True)
    e2 = jnp.exp(m2 - m1)
    g1 = 1.0 / (1.0 + e2)
    g2 = e2 * g1

    @pl.when(step == 0)
    def _():
        cnt_ref[...] = jnp.zeros_like(cnt_ref)

    chosen = jnp.where((lane == i1) | (lane == i2), 1.0, 0.0)
    rr = lax.broadcasted_iota(jnp.int32, (tm, tm), 0)
    cc = lax.broadcasted_iota(jnp.int32, (tm, tm), 1)
    earlier = jnp.where(rr > cc, 1.0, 0.0).astype(BF16)
    before = _dot(earlier, chosen.astype(BF16)) + cnt_ref[...]
    r1 = jnp.sum(jnp.where(lane == i1, before, 0.0), axis=-1, keepdims=True)
    r2 = jnp.sum(jnp.where(lane == i2, before, 0.0), axis=-1, keepdims=True)
    cnt_ref[...] += jnp.sum(chosen, axis=0, keepdims=True)
    cnt_out_ref[...] = jnp.broadcast_to(cnt_ref[...], cnt_out_ref.shape)

    meta = jnp.zeros(logits.shape, F32)
    for li, field in ((META_E1, i1.astype(F32)), (META_E2, i2.astype(F32)), (META_R1, r1),
                      (META_R2, r2), (META_G1, g1), (META_G2, g2)):
        meta = jnp.where(lane == li, field, meta)
    meta_ref[...] = meta
    meta_t_ref[...] = meta.T[:SUBLANES, :]


def pool_router(x, g_mix, pool_w, layer, pool_scale, g_ffn, router, seq, tm=512):
    n, d = x.shape
    tm = min(tm, seq)
    router_p = jnp.zeros((d, LANES), F32).at[:, :N_EXPERTS].set(router)
    ng = len(POOL_WINDOWS)
    return pl.pallas_call(
        functools.partial(_pool_kernel, tiles_per_seq=seq // tm),
        grid=(n // tm,),
        in_specs=[pl.BlockSpec((tm, d), lambda i: (i, 0)),
                  pl.BlockSpec((1, d), lambda i: (0, 0)),
                  pl.BlockSpec((None, ng, POOL_G, POOL_G), lambda i: (layer, 0, 0, 0)),
                  pl.BlockSpec((1, d), lambda i: (0, 0)),
                  pl.BlockSpec((1, d), lambda i: (0, 0)),
                  pl.BlockSpec((d, LANES), lambda i: (0, 0))],
        out_specs=[pl.BlockSpec((tm, d), lambda i: (i, 0)),
                   pl.BlockSpec((tm, d), lambda i: (i, 0)),
                   pl.BlockSpec((tm, LANES), lambda i: (i, 0)),
                   pl.BlockSpec((SUBLANES, tm), lambda i: (0, i)),
                   pl.BlockSpec((SUBLANES, LANES), lambda i: (0, 0))],
        out_shape=[jax.ShapeDtypeStruct((n, d), F32),
                   jax.ShapeDtypeStruct((n, d), F32),
                   jax.ShapeDtypeStruct((n, LANES), F32),
                   jax.ShapeDtypeStruct((SUBLANES, n), F32),
                   jax.ShapeDtypeStruct((SUBLANES, LANES), F32)],
        scratch_shapes=[pltpu.VMEM((POOL_HALO + tm, d), F32),
                        pltpu.VMEM((1, LANES), F32)],
        compiler_params=pltpu.CompilerParams(dimension_semantics=("arbitrary",)),
        name="pool_router",
    )(x, g_mix.reshape(1, d), pool_w, pool_scale.reshape(1, d), g_ffn.reshape(1, d), router_p)


def moe_num_tiles(n):
    return TOP_K * n // MOE_TILE + N_EXPERTS


def moe_schedule(meta_t, counts_f, n):
    nt = moe_num_tiles(n)
    counts = counts_f[0, :N_EXPERTS].astype(jnp.int32)
    tiles = (counts + MOE_TILE - 1) // MOE_TILE
    tile_end = jnp.cumsum(tiles)
    starts = (tile_end - tiles) * MOE_TILE
    expert_ids = jnp.arange(N_EXPERTS, dtype=jnp.int32)[:, None]

    def slots(e_row, r_row):
        e = meta_t[e_row].astype(jnp.int32)
        first = jnp.sum(jnp.where(e[None, :] == expert_ids, starts[:, None], 0), axis=0)
        return first + meta_t[r_row].astype(jnp.int32)

    pos = jnp.concatenate([slots(META_E1, META_R1), slots(META_E2, META_R2)])
    total = tile_end[-1]
    t = jnp.arange(nt, dtype=jnp.int32)
    t_used = jnp.minimum(t, total - 1)
    tile_expert = jnp.sum((tile_end[None, :] <= t_used[:, None]).astype(jnp.int32), axis=1)
    tile_expert = jnp.minimum(tile_expert, N_EXPERTS - 1)
    tile_valid = (t < total).astype(jnp.int32)
    pad_lo = starts + counts
    pad_hi = starts + tiles * MOE_TILE
    return pos, tile_expert, tile_valid, pad_lo, pad_hi, total.reshape(1)


def _dispatch_kernel(pos_ref, lo_ref, hi_ref, used_ref, hn_ref, xs_hbm, zeros, sem, *, n, n_tiles):
    tb = hn_ref.shape[0]
    step = pl.program_id(0)
    base = step * tb

    def issue(r, c):
        tok = base + r
        src = hn_ref.at[pl.ds(r, 1), :]
        pltpu.make_async_copy(src, xs_hbm.at[pl.ds(pos_ref[tok], 1), :], sem).start()
        pltpu.make_async_copy(src, xs_hbm.at[pl.ds(pos_ref[n + tok], 1), :], sem).start()
        return c

    lax.fori_loop(0, tb, issue, 0, unroll=8)
    for _ in range(TOP_K):
        pltpu.make_async_copy(hn_ref, xs_hbm.at[pl.ds(0, tb), :], sem).wait()

    @pl.when(step == pl.num_programs(0) - 1)
    def _():
        zeros[...] = jnp.zeros_like(zeros)

        def pad_copy(slot):
            return pltpu.make_async_copy(zeros.at[pl.ds(0, 1), :], xs_hbm.at[pl.ds(slot, 1), :], sem)

        def issue_pad(s, c):
            pad_copy(s).start()
            return c

        def drain_pad(s, c):
            pad_copy(s).wait()
            return c

        for e in range(N_EXPERTS):
            lax.fori_loop(lo_ref[e], hi_ref[e], issue_pad, 0)
            lax.fori_loop(lo_ref[e], hi_ref[e], drain_pad, 0)

        def tile_copy(t):
            return pltpu.make_async_copy(zeros, xs_hbm.at[pl.ds(t * MOE_TILE, MOE_TILE), :], sem)

        def issue_tile(t, c):
            tile_copy(t).start()
            return c

        def drain_tile(t, c):
            tile_copy(t).wait()
            return c

        lax.fori_loop(used_ref[0], n_tiles, issue_tile, 0)
        lax.fori_loop(used_ref[0], n_tiles, drain_tile, 0)


def moe_dispatch(hn, pos, pad_lo, pad_hi, used, tb=512):
    n, d = hn.shape
    tb = min(tb, n)
    nt = moe_num_tiles(n)
    grid_spec = pltpu.PrefetchScalarGridSpec(
        num_scalar_prefetch=4,
        grid=(n // tb,),
        in_specs=[pl.BlockSpec((tb, d), lambda i, *_: (i, 0))],
        out_specs=pl.BlockSpec(memory_space=pl.ANY),
        scratch_shapes=[pltpu.VMEM((MOE_TILE, d), F32), pltpu.SemaphoreType.DMA],
    )
    return pl.pallas_call(
        functools.partial(_dispatch_kernel, n=n, n_tiles=nt),
        grid_spec=grid_spec,
        out_shape=jax.ShapeDtypeStruct((nt * MOE_TILE, d), F32),
        compiler_params=pltpu.CompilerParams(dimension_semantics=("arbitrary",)),
        name="moe_dispatch",
    )(pos, pad_lo, pad_hi, used, hn)


def _expert_kernel(te_ref, tv_ref, x_ref, wa_ref, wu_ref, wd_ref, o_ref):
    valid = tv_ref[pl.program_id(0)] == 1

    @pl.when(valid)
    def _():
        x = x_ref[...].astype(BF16)
        a = _dot(x, wa_ref[...])
        u = _dot(x, wu_ref[...])
        o_ref[...] = _dot((_silu(a) * u).astype(BF16), wd_ref[...])

    @pl.when(jnp.logical_not(valid))
    def _():
        o_ref[...] = jnp.zeros_like(o_ref)


def moe_experts(xs, w_gu, w_down, tile_expert, tile_valid):
    rows, d = xs.shape
    f = w_down.shape[1]
    nt = rows // MOE_TILE
    grid_spec = pltpu.PrefetchScalarGridSpec(
        num_scalar_prefetch=2,
        grid=(nt,),
        in_specs=[pl.BlockSpec((MOE_TILE, d), lambda t, te, tv: (t, 0)),
                  pl.BlockSpec((None, d, f), lambda t, te, tv: (te[t], 0, 0)),
                  pl.BlockSpec((None, d, f), lambda t, te, tv: (te[t], 0, 1)),
                  pl.BlockSpec((None, f, d), lambda t, te, tv: (te[t], 0, 0))],
        out_specs=pl.BlockSpec((MOE_TILE, d), lambda t, te, tv: (t, 0)),
    )
    return pl.pallas_call(
        _expert_kernel,
        grid_spec=grid_spec,
        out_shape=jax.ShapeDtypeStruct((rows, d), F32),
        compiler_params=pltpu.CompilerParams(dimension_semantics=("arbitrary",),
                                             vmem_limit_bytes=56 * 1024 * 1024),
        name="moe_experts",
    )(tile_expert, tile_valid, xs, w_gu, w_gu, w_down)


def _combine_kernel(pos_ref, x_ref, meta_ref, g_ref, y_hbm, *refs, n, emit_x, emit_norm):
    ybuf, sem = refs[-2], refs[-1]
    tc = x_ref.shape[0]
    step = pl.program_id(0)
    slot = step % 2

    def start_gather(s, into):
        def issue(r, c):
            tok = s * tc + r
            for k in range(TOP_K):
                pltpu.make_async_copy(y_hbm.at[pl.ds(pos_ref[k * n + tok], 1), :],
                                      ybuf.at[into, k, pl.ds(r, 1), :], sem.at[into]).start()
            return c

        lax.fori_loop(0, tc, issue, 0, unroll=8)

    @pl.when(step == 0)
    def _():
        start_gather(0, 0)

    @pl.when(step + 1 < pl.num_programs(0))
    def _():
        start_gather(step + 1, 1 - slot)

    for k in range(TOP_K):
        pltpu.make_async_copy(y_hbm.at[pl.ds(0, tc), :], ybuf.at[slot, k], sem.at[slot]).wait()

    meta = meta_ref[...]
    g1 = meta[:, META_G1:META_G1 + 1]
    g2 = meta[:, META_G2:META_G2 + 1]
    xn = x_ref[...] + g1 * ybuf[slot, 0] + g2 * ybuf[slot, 1]
    _emit_residual_outputs(xn, g_ref, refs, emit_x, emit_norm)


def moe_combine(x, y, meta, pos, g, *, emit_x=True, norm_dtype=None, tc=256):
    n, d = x.shape
    tc = min(tc, n)
    out_shape, out_specs = _residual_out_shapes(n, d, tc, emit_x, norm_dtype, lambda i, p: (i, 0))
    grid_spec = pltpu.PrefetchScalarGridSpec(
        num_scalar_prefetch=1,
        grid=(n // tc,),
        in_specs=[pl.BlockSpec((tc, d), lambda i, p: (i, 0)),
                  pl.BlockSpec((tc, LANES), lambda i, p: (i, 0)),
                  pl.BlockSpec((1, d), lambda i, p: (0, 0)),
                  pl.BlockSpec(memory_space=pl.ANY)],
        out_specs=out_specs,
        scratch_shapes=[pltpu.VMEM((2, TOP_K, tc, d), F32), pltpu.SemaphoreType.DMA((2,))],
    )
    return pl.pallas_call(
        functools.partial(_combine_kernel, n=n, emit_x=emit_x, emit_norm=norm_dtype is not None),
        grid_spec=grid_spec,
        out_shape=out_shape,
        compiler_params=pltpu.CompilerParams(dimension_semantics=("arbitrary",)),
        name="moe_combine",
    )(pos, x, meta, g.reshape(1, d), y)


def kernel(x, ret_w_in, ret_norm_g, ret_w_out, pool_w, pool_scale, ffn_w_gu, ffn_w_down,
           moe_router, moe_w_gu, moe_w_down, norm_mix_g, norm_ffn_g, norm_final_g):
    batch, seq, d = x.shape
    n = batch * seq
    xf = x.reshape(n, d)
    pool_w = pool_w.astype(BF16)
    n_moe, n_exp, _, f2 = moe_w_gu.shape
    f = moe_w_down.shape[2]
    moe_gu_rows = moe_w_gu.reshape(n_moe, n_exp * d, f2)
    moe_down_rows = moe_w_down.reshape(n_moe, n_exp * f, d)
    hn = rmsnorm(xf, norm_mix_g[0], BF16)
    for i in range(DEPTH):
        j = i // 2
        last = i == DEPTH - 1
        if i % 2 == 0:
            proj, moe_gu_b = matmul(hn, ret_w_in, j, moe_gu_rows, j)
            y = retention(proj, ret_norm_g, j, batch, seq)
            xf, xg, ssq = down_residual(y, ret_w_out, j, xf, norm_ffn_g[i], tm=1024)
            hid, moe_down_b = swiglu_up(xg, ssq, ffn_w_gu, j, moe_down_rows, j)
            (xf,) = down_residual(hid, ffn_w_down, j, xf)
            moe_gu_b = moe_gu_b.reshape(n_exp, d, f2)
            moe_down_b = moe_down_b.reshape(n_exp, f, d)
        else:
            xf, hn, meta, meta_t, counts = pool_router(xf, norm_mix_g[i], pool_w, j, pool_scale[j],
                                               norm_ffn_g[i], moe_router[j], seq)
            pos, tile_expert, tile_valid, pad_lo, pad_hi, used = moe_schedule(meta_t, counts, n)
            xs = moe_dispatch(hn, pos, pad_lo, pad_hi, used)
            ys = moe_experts(xs, moe_gu_b, moe_down_b, tile_expert, tile_valid)
            if last:
                (out,) = moe_combine(xf, ys, meta, pos, norm_final_g, emit_x=False, norm_dtype=F32)
            else:
                xf, hn = moe_combine(xf, ys, meta, pos, norm_mix_g[i + 1], norm_dtype=BF16)
    return out.reshape(batch, seq, d)
```

```python
import functools

import jax
import jax.numpy as jnp
from jax import lax
from jax.experimental import pallas as pl
from jax.experimental.pallas import tpu as pltpu

BF16 = jnp.bfloat16
F32 = jnp.float32

D_MODEL = 2048
DEPTH = 4
RET_HEADS = 8
RET_DK = D_MODEL // RET_HEADS
RET_DV = 2 * D_MODEL // RET_HEADS
RET_QK = RET_HEADS * RET_DK
RET_V = RET_HEADS * RET_DV
RET_CHUNK = 256
RET_PREP_ROWS = 32
ROPE_BASE = 10000.0
POOL_WINDOWS = (2, 4, 8, 16)
POOL_G = D_MODEL // len(POOL_WINDOWS)
POOL_HALO = 16
D_FF = 5632
N_EXPERTS = 8
TOP_K = 2
D_FF_EXPERT = 1408
NORM_EPS = 1e-6
LANES = 128
SUBLANES = 8
MOE_TILE = 256
SIDE_CAST_BLOCKS = 64
META_E1, META_E2, META_R1, META_R2, META_G1, META_G2 = range(6)


def _rms(x, g):
    ms = jnp.mean(x * x, axis=-1, keepdims=True)
    return x * lax.rsqrt(ms + NORM_EPS) * g


def _silu(a):
    return a / (1.0 + jnp.exp(-a))


def _dot(a, b):
    return jnp.dot(a, b, preferred_element_type=F32)


def _rmsnorm_kernel(x_ref, g_ref, o_ref):
    o_ref[...] = _rms(x_ref[...], g_ref[...]).astype(o_ref.dtype)


def rmsnorm(x, g, out_dtype, tm=512):
    n, d = x.shape
    return pl.pallas_call(
        _rmsnorm_kernel,
        grid=(n // tm,),
        in_specs=[pl.BlockSpec((tm, d), lambda i: (i, 0)),
                  pl.BlockSpec((1, d), lambda i: (0, 0))],
        out_specs=pl.BlockSpec((tm, d), lambda i: (i, 0)),
        out_shape=jax.ShapeDtypeStruct((n, d), out_dtype),
        name="rmsnorm",
    )(x, g.reshape(1, d))


def _side_cast_specs(side, side_layer, steps, n_inner):
    _, rows, cols = side.shape
    n_blocks = 1
    while n_blocks * 2 <= min(steps, SIDE_CAST_BLOCKS):
        n_blocks *= 2
    br = rows // n_blocks

    def index(j, i):
        return jnp.minimum(j * n_inner + i, n_blocks - 1)

    in_spec = pl.BlockSpec((None, br, cols), lambda j, i: (side_layer, index(j, i), 0))
    out_spec = pl.BlockSpec((br, cols), lambda j, i: (index(j, i), 0))
    return in_spec, out_spec, jax.ShapeDtypeStruct((rows, cols), BF16)


def _in_proj_kernel(x_ref, w_ref, perm_ref, s_ref, o_ref, so_ref, wb_ref, *, n_qk_slabs):
    @pl.when(pl.program_id(1) == 0)
    def _():
        is_qk = pl.program_id(0) < n_qk_slabs

        @pl.when(is_qk)
        def _():
            for lo in range(0, wb_ref.shape[1], RET_DK):
                head = w_ref[:, lo:lo + RET_DK].astype(BF16)
                wb_ref[:, lo:lo + RET_DK] = _dot(head, perm_ref[...]).astype(BF16)

        @pl.when(jnp.logical_not(is_qk))
        def _():
            wb_ref[...] = w_ref[...].astype(BF16)

    o_ref[...] = _dot(x_ref[...], wb_ref[...]).astype(o_ref.dtype)
    so_ref[...] = s_ref[...].astype(BF16)


def retention_in_proj(x, w, layer, side, side_layer, tm=1024, tn=1024):
    n, k = x.shape
    f = w.shape[2]
    tm = min(tm, n)
    nj, ni = f // tn, n // tm
    half = RET_DK // 2
    src = jnp.arange(RET_DK)
    dst = jnp.where(src % 2 == 0, src // 2, half + src // 2)
    perm = (dst[:, None] == jnp.arange(RET_DK)[None, :]).astype(BF16)
    s_in, s_out, s_shape = _side_cast_specs(side, side_layer, nj * ni, ni)
    return pl.pallas_call(
        functools.partial(_in_proj_kernel, n_qk_slabs=2 * RET_QK // tn),
        grid=(nj, ni),
        in_specs=[pl.BlockSpec((tm, k), lambda j, i: (i, 0)),
                  pl.BlockSpec((None, k, tn), lambda j, i: (layer, 0, j)),
                  pl.BlockSpec((RET_DK, RET_DK), lambda j, i: (0, 0)),
                  s_in],
        out_specs=[pl.BlockSpec((tm, tn), lambda j, i: (i, j)), s_out],
        out_shape=[jax.ShapeDtypeStruct((n, f), BF16), s_shape],
        scratch_shapes=[pltpu.VMEM((k, tn), BF16)],
        compiler_params=pltpu.CompilerParams(dimension_semantics=("arbitrary", "arbitrary")),
        name="retention_in_proj",
    )(x, w, perm, side)


def _swiglu_kernel(x_ref, ssq_ref, wa_ref, wu_ref, s_ref, o_ref, so_ref, wab_ref, wub_ref):
    @pl.when(pl.program_id(1) == 0)
    def _():
        wab_ref[...] = wa_ref[...].astype(BF16)
        wub_ref[...] = wu_ref[...].astype(BF16)

    x = x_ref[...]
    rowscale = lax.rsqrt(jnp.sum(ssq_ref[...], axis=0) / x.shape[1] + NORM_EPS)
    a = _dot(x, wab_ref[...]) * rowscale
    u = _dot(x, wub_ref[...]) * rowscale
    o_ref[...] = (_silu(a) * u).astype(o_ref.dtype)
    so_ref[...] = s_ref[...].astype(BF16)


def swiglu_up(xg, ssq, w_gu, layer, side, side_layer, tm=1024, tn=512):
    n, k = xg.shape
    f = w_gu.shape[2] // 2
    tm = min(tm, n)
    nj, ni = f // tn, n // tm
    ns = ssq.shape[0]
    s_in, s_out, s_shape = _side_cast_specs(side, side_layer, nj * ni, ni)
    return pl.pallas_call(
        _swiglu_kernel,
        grid=(nj, ni),
        in_specs=[pl.BlockSpec((tm, k), lambda j, i: (i, 0)),
                  pl.BlockSpec((ns, tm, 1), lambda j, i: (0, i, 0)),
                  pl.BlockSpec((None, k, tn), lambda j, i: (layer, 0, j)),
                  pl.BlockSpec((None, k, tn), lambda j, i: (layer, 0, nj + j)),
                  s_in],
        out_specs=[pl.BlockSpec((tm, tn), lambda j, i: (i, j)), s_out],
        out_shape=[jax.ShapeDtypeStruct((n, f), BF16), s_shape],
        scratch_shapes=[pltpu.VMEM((k, tn), BF16), pltpu.VMEM((k, tn), BF16)],
        compiler_params=pltpu.CompilerParams(dimension_semantics=("arbitrary", "arbitrary")),
        name="swiglu_up",
    )(xg, ssq, w_gu, w_gu, side)


def _emit_residual_outputs(xn, g_ref, refs, emit_x, emit_norm):
    pos = 0
    if emit_x:
        refs[pos][...] = xn
        pos += 1
    if emit_norm:
        refs[pos][...] = _rms(xn, g_ref[...]).astype(refs[pos].dtype)


def _residual_out_shapes(n, d, tm, emit_x, norm_dtype, index_map):
    out_shape, out_specs = [], []
    if emit_x:
        out_shape.append(jax.ShapeDtypeStruct((n, d), F32))
        out_specs.append(pl.BlockSpec((tm, d), index_map))
    if norm_dtype is not None:
        out_shape.append(jax.ShapeDtypeStruct((n, d), norm_dtype))
        out_specs.append(pl.BlockSpec((tm, d), index_map))
    return out_shape, out_specs


def _down_kernel(x_ref, w_ref, r_ref, g_ref, *refs, emit_scaled):
    wb_ref = refs[-1]

    @pl.when(pl.program_id(1) == 0)
    def _():
        wb_ref[...] = w_ref[...].astype(BF16)

    xn = r_ref[...] + _dot(x_ref[...], wb_ref[...])
    refs[0][...] = xn
    if emit_scaled:
        refs[1][...] = (xn * g_ref[...]).astype(BF16)
        refs[2][...] = jnp.sum(xn * xn, axis=-1, keepdims=True)


def down_residual(h, w, layer, resid, g=None, *, tm=512, tn=512):
    n, kk = h.shape
    d = w.shape[2]
    tm = min(tm, n)
    nj = d // tn
    emit_scaled = g is not None
    if g is None:
        g = jnp.ones((d,), F32)
    out_shape = [jax.ShapeDtypeStruct((n, d), F32)]
    out_specs = [pl.BlockSpec((tm, tn), lambda j, i: (i, j))]
    if emit_scaled:
        out_shape += [jax.ShapeDtypeStruct((n, d), BF16), jax.ShapeDtypeStruct((nj, n, 1), F32)]
        out_specs += [pl.BlockSpec((tm, tn), lambda j, i: (i, j)),
                      pl.BlockSpec((None, tm, 1), lambda j, i: (j, i, 0))]
    return pl.pallas_call(
        functools.partial(_down_kernel, emit_scaled=emit_scaled),
        grid=(nj, n // tm),
        in_specs=[pl.BlockSpec((tm, kk), lambda j, i: (i, 0)),
                  pl.BlockSpec((None, kk, tn), lambda j, i: (layer, 0, j)),
                  pl.BlockSpec((tm, tn), lambda j, i: (i, j)),
                  pl.BlockSpec((1, tn), lambda j, i: (0, j))],
        out_specs=out_specs,
        out_shape=out_shape,
        scratch_shapes=[pltpu.VMEM((kk, tn), BF16)],
        compiler_params=pltpu.CompilerParams(dimension_semantics=("arbitrary", "arbitrary")),
        name="down_residual",
    )(h, w, resid, g.reshape(1, d))


def _retention_kernel(lg_ref, q_ref, k_ref, v_ref, g_ref, cos_ref, sin_ref, gain_ref, o_ref,
                      qr_ref, qd_ref, kr_ref, kd_ref, mask_ref, qdec_ref, kdec_ref, oraw_ref,
                      state_ref, sb_ref, *, n_chunks, n_pairs):
    c = RET_CHUNK
    step = pl.program_id(0)
    slot_prep = step % 2
    slot_recur = 1 - slot_prep
    slot_finish = slot_prep

    @pl.when(step == 0)
    def _():
        for ref in (qr_ref, qd_ref, kr_ref, kd_ref, oraw_ref):
            ref[...] = jnp.zeros_like(ref)

    lg_prep = lg_ref[jnp.minimum(step, n_pairs - 1) % RET_HEADS]
    lg = lg_ref[jnp.clip(step - 1, 0, n_pairs - 1) % RET_HEADS]
    ii = lax.broadcasted_iota(jnp.int32, (c, c), 0)
    jj = lax.broadcasted_iota(jnp.int32, (c, c), 1)
    rel = (ii - jj).astype(F32)
    mask_ref[...] = jnp.where(rel >= 0.0, jnp.exp(jnp.maximum(rel, 0.0) * lg), 0.0)
    idx = lax.broadcasted_iota(jnp.int32, (c, 1), 0).astype(F32)
    k_scale = RET_DK ** -0.5
    qdec_ref[...] = jnp.broadcast_to(jnp.exp((idx + 1.0) * lg_prep), (c, RET_DK))
    kdec_ref[...] = jnp.broadcast_to(jnp.exp((c - 1.0 - idx) * lg_prep) * k_scale, (c, RET_DK))
    chunk_decay = jnp.exp(jnp.full((1, 1), float(c), F32) * lg)
    half = RET_DK // 2

    def rotary(xb, cos, sin):
        x = xb.astype(F32)
        x1, x2 = x[:, :half], x[:, half:]
        return jnp.concatenate([x1 * cos - x2 * sin, x2 * cos + x1 * sin], axis=1)

    def rows(ci):
        return pl.ds(pl.multiple_of(ci * c, c), c)

    state_ref[...] = jnp.zeros_like(state_ref)
    sb_ref[...] = jnp.zeros_like(sb_ref)

    def body(ci, carry):
        r = rows(ci)

        o = oraw_ref[slot_finish, r, :]
        ms = jnp.mean(o * o, axis=-1, keepdims=True)
        o = o * lax.rsqrt(ms + NORM_EPS) * gain_ref[...]
        gate = g_ref[r, :].astype(F32)
        o_ref[r, :] = (o * _silu(gate)).astype(o_ref.dtype)

        v = v_ref[r, :]
        scores = lax.dot_general(qr_ref[slot_recur, r, :], kr_ref[slot_recur, r, :],
                                 (((1,), (1,)), ((), ())),
                                 preferred_element_type=F32) * mask_ref[...]
        oraw_ref[slot_recur, r, :] = (_dot(scores.astype(BF16), v)
                                      + _dot(qd_ref[slot_recur, r, :], sb_ref[...]))
        upd = lax.dot_general(kd_ref[slot_recur, r, :], v, (((0,), (0,)), ((), ())),
                              preferred_element_type=F32)
        state = state_ref[...] * chunk_decay + upd
        state_ref[...] = state
        sb_ref[...] = state.astype(BF16)

        for lo in range(0, c, RET_PREP_ROWS):
            rr = pl.ds(pl.multiple_of(ci * c + lo, RET_PREP_ROWS), RET_PREP_ROWS)
            dec = pl.ds(lo, RET_PREP_ROWS)
            cos = cos_ref[rr, :]
            sin = sin_ref[rr, :]
            q = rotary(q_ref[rr, :], cos, sin)
            k = rotary(k_ref[rr, :], cos, sin)
            qr_ref[slot_prep, rr, :] = q.astype(BF16)
            qd_ref[slot_prep, rr, :] = (q * qdec_ref[dec, :]).astype(BF16)
            kr_ref[slot_prep, rr, :] = (k * k_scale).astype(BF16)
            kd_ref[slot_prep, rr, :] = (k * kdec_ref[dec, :]).astype(BF16)
        return carry

    lax.fori_loop(0, n_chunks, body, 0)


def retention(proj, norm_g, layer, batch, seq):
    n = batch * seq
    h = RET_HEADS
    log_g = jnp.log1p(-jnp.exp2(-5.0 - jnp.arange(h, dtype=F32)))
    freq = 1.0 / (ROPE_BASE ** jnp.linspace(0.0, 1.0, RET_DK // 2, dtype=F32))
    ang = jnp.arange(seq).astype(F32)[:, None] * freq[None, :]
    cos = jnp.cos(ang)
    sin = jnp.sin(ang)
    kq = RET_QK // RET_DK
    kv = 2 * RET_QK // RET_DV
    kg = (2 * RET_QK + RET_V) // RET_DV
    n_pairs = batch * h
    stages = 3

    def pair(step, stage):
        p = jnp.clip(step - stage, 0, n_pairs - 1)
        return p // h, p % h

    def block(stage, col0):
        def index(s):
            b, hh = pair(s, stage)
            return b, col0 + hh
        return index

    def gain_index(s):
        return layer, 0, pair(s, 2)[1]

    return pl.pallas_call(
        functools.partial(_retention_kernel, n_chunks=seq // RET_CHUNK, n_pairs=n_pairs),
        grid=(n_pairs + stages - 1,),
        in_specs=[pl.BlockSpec(memory_space=pltpu.SMEM),
                  pl.BlockSpec((seq, RET_DK), block(0, 0)),
                  pl.BlockSpec((seq, RET_DK), block(0, kq)),
                  pl.BlockSpec((seq, RET_DV), block(1, kv)),
                  pl.BlockSpec((seq, RET_DV), block(2, kg)),
                  pl.BlockSpec((seq, RET_DK // 2), lambda s: (0, 0)),
                  pl.BlockSpec((seq, RET_DK // 2), lambda s: (0, 0)),
                  pl.BlockSpec((None, 1, RET_DV), gain_index)],
        out_specs=pl.BlockSpec((seq, RET_DV), block(2, 0)),
        out_shape=jax.ShapeDtypeStruct((n, RET_V), BF16),
        scratch_shapes=[pltpu.VMEM((2, seq, RET_DK), BF16),
                        pltpu.VMEM((2, seq, RET_DK), BF16),
                        pltpu.VMEM((2, seq, RET_DK), BF16),
                        pltpu.VMEM((2, seq, RET_DK), BF16),
                        pltpu.VMEM((RET_CHUNK, RET_CHUNK), F32),
                        pltpu.VMEM((RET_CHUNK, RET_DK), F32),
                        pltpu.VMEM((RET_CHUNK, RET_DK), F32),
                        pltpu.VMEM((2, seq, RET_DV), F32),
                        pltpu.VMEM((RET_DK, RET_DV), F32),
                        pltpu.VMEM((RET_DK, RET_DV), BF16)],
        compiler_params=pltpu.CompilerParams(dimension_semantics=("arbitrary",)),
        name="retention",
    )(log_g, proj, proj, proj, proj, cos, sin, norm_g.reshape(norm_g.shape[0], 1, RET_V))


def _pool_kernel(x_ref, gm_ref, pw_ref, sc_ref, gf_ref, rt_ref,
                 xo_ref, hn_ref, meta_ref, meta_t_ref, cnt_out_ref, hbuf, cnt_ref, *, tiles_per_seq):
    tm = x_ref.shape[0]
    step = pl.program_id(0)
    ti = step % tiles_per_seq
    x = x_ref[...]
    hn = _rms(x, gm_ref[...])

    @pl.when(ti == 0)
    def _():
        hbuf[0:POOL_HALO, :] = jnp.zeros((POOL_HALO, D_MODEL), F32)

    @pl.when(ti != 0)
    def _():
        hbuf[0:POOL_HALO, :] = hbuf[tm:tm + POOL_HALO, :]

    hbuf[POOL_HALO:POOL_HALO + tm, :] = hn

    t = ti * tm + lax.broadcasted_iota(jnp.int32, (tm, 1), 0)
    for gi, w in enumerate(POOL_WINDOWS):
        lo, hi = gi * POOL_G, (gi + 1) * POOL_G
        cur = hn[:, lo:hi]
        acc = hbuf[:, lo:hi]
        shift = 1
        while shift < w:
            acc = acc + pltpu.roll(acc, shift, 0)
            shift *= 2
        acc = acc[POOL_HALO:, :]
        inv = 1.0 / jnp.minimum(t + 1, w).astype(F32)
        mix = (acc * inv - cur).astype(BF16)
        o = _dot(mix, pw_ref[gi]) * sc_ref[:, lo:hi]
        xo_ref[:, lo:hi] = x[:, lo:hi] + o

    hn2 = _rms(xo_ref[...], gf_ref[...])
    hn_ref[...] = hn2

    rt = rt_ref[...]
    hn_hi = hn2.astype(BF16)
    hn_lo = (hn2 - hn_hi.astype(F32)).astype(BF16)
    rt_hi = rt.astype(BF16)
    rt_lo = (rt - rt_hi.astype(F32)).astype(BF16)
    logits = _dot(hn_hi, rt_hi) + (_dot(hn_hi, rt_lo) + _dot(hn_lo, rt_hi))
    lane = lax.broadcasted_iota(jnp.int32, logits.shape, 1)
    neg = jnp.float32(-jnp.inf)
    lg = jnp.where(lane < N_EXPERTS, logits, neg)
    m1 = jnp.max(lg, axis=-1, keepdims=True)
    i1 = jnp.min(jnp.where(lg == m1, lane, LANES), axis=-1, keepdims=True)
    lg2 = jnp.where(lane == i1, neg, lg)
    m2 = jnp.max(lg2, axis=-1, keepdims=True)
    i2 = jnp.min(jnp.where(lg2 == m2, lane, LANES), axis=-1, keepdims=True)
    e2 = jnp.exp(m2 - m1)
    g1 = 1.0 / (1.0 + e2)
    g2 = e2 * g1

    @pl.when(step == 0)
    def _():
        cnt_ref[...] = jnp.zeros_like(cnt_ref)

    chosen = jnp.where((lane == i1) | (lane == i2), 1.0, 0.0)
    rr = lax.broadcasted_iota(jnp.int32, (tm, tm), 0)
    cc = lax.broadcasted_iota(jnp.int32, (tm, tm), 1)
    earlier = jnp.where(rr > cc, 1.0, 0.0).astype(BF16)
    before = _dot(earlier, chosen.astype(BF16)) + cnt_ref[...]
    r1 = jnp.sum(jnp.where(lane == i1, before, 0.0), axis=-1, keepdims=True)
    r2 = jnp.sum(jnp.where(lane == i2, before, 0.0), axis=-1, keepdims=True)
    cnt_ref[...] += jnp.sum(chosen, axis=0, keepdims=True)
    cnt_out_ref[...] = jnp.broadcast_to(cnt_ref[...], cnt_out_ref.shape)

    meta = jnp.zeros(logits.shape, F32)
    for li, field in ((META_E1, i1.astype(F32)), (META_E2, i2.astype(F32)), (META_R1, r1),
                      (META_R2, r2), (META_G1, g1), (META_G2, g2)):
        meta = jnp.where(lane == li, field, meta)
    meta_ref[...] = meta
    meta_t_ref[...] = meta.T[:SUBLANES, :]


def pool_router(x, g_mix, pool_w, layer, pool_scale, g_ffn, router, seq, tm=512):
    n, d = x.shape
    tm = min(tm, seq)
    router_p = jnp.zeros((d, LANES), F32).at[:, :N_EXPERTS].set(router)
    ng = len(POOL_WINDOWS)
    return pl.pallas_call(
        functools.partial(_pool_kernel, tiles_per_seq=seq // tm),
        grid=(n // tm,),
        in_specs=[pl.BlockSpec((tm, d), lambda i: (i, 0)),
                  pl.BlockSpec((1, d), lambda i: (0, 0)),
                  pl.BlockSpec((None, ng, POOL_G, POOL_G), lambda i: (layer, 0, 0, 0)),
                  pl.BlockSpec((1, d), lambda i: (0, 0)),
                  pl.BlockSpec((1, d), lambda i: (0, 0)),
                  pl.BlockSpec((d, LANES), lambda i: (0, 0))],
        out_specs=[pl.BlockSpec((tm, d), lambda i: (i, 0)),
                   pl.BlockSpec((tm, d), lambda i: (i, 0)),
                   pl.BlockSpec((tm, LANES), lambda i: (i, 0)),
                   pl.BlockSpec((SUBLANES, tm), lambda i: (0, i)),
                   pl.BlockSpec((SUBLANES, LANES), lambda i: (0, 0))],
        out_shape=[jax.ShapeDtypeStruct((n, d), F32),
                   jax.ShapeDtypeStruct((n, d), F32),
                   jax.ShapeDtypeStruct((n, LANES), F32),
                   jax.ShapeDtypeStruct((SUBLANES, n), F32),
                   jax.ShapeDtypeStruct((SUBLANES, LANES), F32)],
        scratch_shapes=[pltpu.VMEM((POOL_HALO + tm, d), F32),
                        pltpu.VMEM((1, LANES), F32)],
        compiler_params=pltpu.CompilerParams(dimension_semantics=("arbitrary",)),
        name="pool_router",
    )(x, g_mix.reshape(1, d), pool_w, pool_scale.reshape(1, d), g_ffn.reshape(1, d), router_p)


def moe_num_tiles(n):
    return TOP_K * n // MOE_TILE + N_EXPERTS


def moe_schedule(meta_t, counts_f, n):
    nt = moe_num_tiles(n)
    counts = counts_f[0, :N_EXPERTS].astype(jnp.int32)
    tiles = (counts + MOE_TILE - 1) // MOE_TILE
    tile_end = jnp.cumsum(tiles)
    starts = (tile_end - tiles) * MOE_TILE
    expert_ids = jnp.arange(N_EXPERTS, dtype=jnp.int32)[:, None]

    def slots(e_row, r_row):
        e = meta_t[e_row].astype(jnp.int32)
        first = jnp.sum(jnp.where(e[None, :] == expert_ids, starts[:, None], 0), axis=0)
        return first + meta_t[r_row].astype(jnp.int32)

    pos = jnp.concatenate([slots(META_E1, META_R1), slots(META_E2, META_R2)])
    total = tile_end[-1]
    t = jnp.arange(nt, dtype=jnp.int32)
    t_used = jnp.minimum(t, total - 1)
    tile_expert = jnp.sum((tile_end[None, :] <= t_used[:, None]).astype(jnp.int32), axis=1)
    tile_expert = jnp.minimum(tile_expert, N_EXPERTS - 1)
    tile_valid = (t < total).astype(jnp.int32)
    pad_lo = starts + counts
    pad_hi = starts + tiles * MOE_TILE
    return pos, tile_expert, tile_valid, pad_lo, pad_hi, total.reshape(1)


def _dispatch_kernel(pos_ref, lo_ref, hi_ref, used_ref, hn_ref, xs_hbm, zeros, sem, *, n, n_tiles):
    tb = hn_ref.shape[0]
    step = pl.program_id(0)
    base = step * tb

    def issue(r, c):
        tok = base + r
        src = hn_ref.at[pl.ds(r, 1), :]
        pltpu.make_async_copy(src, xs_hbm.at[pl.ds(pos_ref[tok], 1), :], sem).start()
        pltpu.make_async_copy(src, xs_hbm.at[pl.ds(pos_ref[n + tok], 1), :], sem).start()
        return c

    lax.fori_loop(0, tb, issue, 0, unroll=8)
    for _ in range(TOP_K):
        pltpu.make_async_copy(hn_ref, xs_hbm.at[pl.ds(0, tb), :], sem).wait()

    @pl.when(step == pl.num_programs(0) - 1)
    def _():
        zeros[...] = jnp.zeros_like(zeros)

        def pad_copy(slot):
            return pltpu.make_async_copy(zeros.at[pl.ds(0, 1), :], xs_hbm.at[pl.ds(slot, 1), :], sem)

        def issue_pad(s, c):
            pad_copy(s).start()
            return c

        def drain_pad(s, c):
            pad_copy(s).wait()
            return c

        for e in range(N_EXPERTS):
            lax.fori_loop(lo_ref[e], hi_ref[e], issue_pad, 0)
            lax.fori_loop(lo_ref[e], hi_ref[e], drain_pad, 0)

        def tile_copy(t):
            return pltpu.make_async_copy(zeros, xs_hbm.at[pl.ds(t * MOE_TILE, MOE_TILE), :], sem)

        def issue_tile(t, c):
            tile_copy(t).start()
            return c

        def drain_tile(t, c):
            tile_copy(t).wait()
            return c

        lax.fori_loop(used_ref[0], n_tiles, issue_tile, 0)
        lax.fori_loop(used_ref[0], n_tiles, drain_tile, 0)


def moe_dispatch(hn, pos, pad_lo, pad_hi, used, tb=512):
    n, d = hn.shape
    tb = min(tb, n)
    nt = moe_num_tiles(n)
    grid_spec = pltpu.PrefetchScalarGridSpec(
        num_scalar_prefetch=4,
        grid=(n // tb,),
        in_specs=[pl.BlockSpec((tb, d), lambda i, *_: (i, 0))],
        out_specs=pl.BlockSpec(memory_space=pl.ANY),
        scratch_shapes=[pltpu.VMEM((MOE_TILE, d), F32), pltpu.SemaphoreType.DMA],
    )
    return pl.pallas_call(
        functools.partial(_dispatch_kernel, n=n, n_tiles=nt),
        grid_spec=grid_spec,
        out_shape=jax.ShapeDtypeStruct((nt * MOE_TILE, d), F32),
        compiler_params=pltpu.CompilerParams(dimension_semantics=("arbitrary",)),
        name="moe_dispatch",
    )(pos, pad_lo, pad_hi, used, hn)


def _expert_kernel(te_ref, tv_ref, x_ref, wa_ref, wu_ref, wd_ref, o_ref):
    valid = tv_ref[pl.program_id(0)] == 1

    @pl.when(valid)
    def _():
        x = x_ref[...].astype(BF16)
        a = _dot(x, wa_ref[...])
        u = _dot(x, wu_ref[...])
        o_ref[...] = _dot((_silu(a) * u).astype(BF16), wd_ref[...])

    @pl.when(jnp.logical_not(valid))
    def _():
        o_ref[...] = jnp.zeros_like(o_ref)


def moe_experts(xs, w_gu, w_down, tile_expert, tile_valid):
    rows, d = xs.shape
    f = w_down.shape[1]
    nt = rows // MOE_TILE
    grid_spec = pltpu.PrefetchScalarGridSpec(
        num_scalar_prefetch=2,
        grid=(nt,),
        in_specs=[pl.BlockSpec((MOE_TILE, d), lambda t, te, tv: (t, 0)),
                  pl.BlockSpec((None, d, f), lambda t, te, tv: (te[t], 0, 0)),
                  pl.BlockSpec((None, d, f), lambda t, te, tv: (te[t], 0, 1)),
                  pl.BlockSpec((None, f, d), lambda t, te, tv: (te[t], 0, 0))],
        out_specs=pl.BlockSpec((MOE_TILE, d), lambda t, te, tv: (t, 0)),
    )
    return pl.pallas_call(
        _expert_kernel,
        grid_spec=grid_spec,
        out_shape=jax.ShapeDtypeStruct((rows, d), F32),
        compiler_params=pltpu.CompilerParams(dimension_semantics=("arbitrary",),
                                             vmem_limit_bytes=56 * 1024 * 1024),
        name="moe_experts",
    )(tile_expert, tile_valid, xs, w_gu, w_gu, w_down)


def _combine_kernel(pos_ref, x_ref, meta_ref, g_ref, y_hbm, *refs, n, emit_x, emit_norm):
    ybuf, sem = refs[-2], refs[-1]
    tc = x_ref.shape[0]
    step = pl.program_id(0)
    slot = step % 2

    def start_gather(s, into):
        def issue(r, c):
            tok = s * tc + r
            for k in range(TOP_K):
                pltpu.make_async_copy(y_hbm.at[pl.ds(pos_ref[k * n + tok], 1), :],
                                      ybuf.at[into, k, pl.ds(r, 1), :], sem.at[into]).start()
            return c

        lax.fori_loop(0, tc, issue, 0, unroll=8)

    @pl.when(step == 0)
    def _():
        start_gather(0, 0)

    @pl.when(step + 1 < pl.num_programs(0))
    def _():
        start_gather(step + 1, 1 - slot)

    for k in range(TOP_K):
        pltpu.make_async_copy(y_hbm.at[pl.ds(0, tc), :], ybuf.at[slot, k], sem.at[slot]).wait()

    meta = meta_ref[...]
    g1 = meta[:, META_G1:META_G1 + 1]
    g2 = meta[:, META_G2:META_G2 + 1]
    xn = x_ref[...] + g1 * ybuf[slot, 0] + g2 * ybuf[slot, 1]
    _emit_residual_outputs(xn, g_ref, refs, emit_x, emit_norm)


def moe_combine(x, y, meta, pos, g, *, emit_x=True, norm_dtype=None, tc=256):
    n, d = x.shape
    tc = min(tc, n)
    out_shape, out_specs = _residual_out_shapes(n, d, tc, emit_x, norm_dtype, lambda i, p: (i, 0))
    grid_spec = pltpu.PrefetchScalarGridSpec(
        num_scalar_prefetch=1,
        grid=(n // tc,),
        in_specs=[pl.BlockSpec((tc, d), lambda i, p: (i, 0)),
                  pl.BlockSpec((tc, LANES), lambda i, p: (i, 0)),
                  pl.BlockSpec((1, d), lambda i, p: (0, 0)),
                  pl.BlockSpec(memory_space=pl.ANY)],
        out_specs=out_specs,
        scratch_shapes=[pltpu.VMEM((2, TOP_K, tc, d), F32), pltpu.SemaphoreType.DMA((2,))],
    )
    return pl.pallas_call(
        functools.partial(_combine_kernel, n=n, emit_x=emit_x, emit_norm=norm_dtype is not None),
        grid_spec=grid_spec,
        out_shape=out_shape,
        compiler_params=pltpu.CompilerParams(dimension_semantics=("arbitrary",)),
        name="moe_combine",
    )(pos, x, meta, g.reshape(1, d), y)


def kernel(x, ret_w_in, ret_norm_g, ret_w_out, pool_w, pool_scale, ffn_w_gu, ffn_w_down,
           moe_router, moe_w_gu, moe_w_down, norm_mix_g, norm_ffn_g, norm_final_g):
    batch, seq, d = x.shape
    n = batch * seq
    xf = x.reshape(n, d)
    pool_w = pool_w.astype(BF16)
    n_moe, n_exp, _, f2 = moe_w_gu.shape
    f = moe_w_down.shape[2]
    moe_gu_rows = moe_w_gu.reshape(n_moe, n_exp * d, f2)
    moe_down_rows = moe_w_down.reshape(n_moe, n_exp * f, d)
    hn = rmsnorm(xf, norm_mix_g[0], BF16)
    for i in range(DEPTH):
        j = i // 2
        last = i == DEPTH - 1
        if i % 2 == 0:
            proj, moe_gu_b = retention_in_proj(hn, ret_w_in, j, moe_gu_rows, j)
            y = retention(proj, ret_norm_g, j, batch, seq)
            xf, xg, ssq = down_residual(y, ret_w_out, j, xf, norm_ffn_g[i], tm=1024)
            hid, moe_down_b = swiglu_up(xg, ssq, ffn_w_gu, j, moe_down_rows, j)
            (xf,) = down_residual(hid, ffn_w_down, j, xf)
            moe_gu_b = moe_gu_b.reshape(n_exp, d, f2)
            moe_down_b = moe_down_b.reshape(n_exp, f, d)
        else:
            xf, hn, meta, meta_t, counts = pool_router(xf, norm_mix_g[i], pool_w, j, pool_scale[j],
                                               norm_ffn_g[i], moe_router[j], seq)
            pos, tile_expert, tile_valid, pad_lo, pad_hi, used = moe_schedule(meta_t, counts, n)
            xs = moe_dispatch(hn, pos, pad_lo, pad_hi, used)
            ys = moe_experts(xs, moe_gu_b, moe_down_b, tile_expert, tile_valid)
            if last:
                (out,) = moe_combine(xf, ys, meta, pos, norm_final_g, emit_x=False, norm_dtype=F32)
            else:
                xf, hn = moe_combine(xf, ys, meta, pos, norm_mix_g[i + 1], norm_dtype=BF16)
    return out.reshape(batch, seq, d)
```

```python
import functools

import jax
import jax.numpy as jnp
from jax import lax
from jax.experimental import pallas as pl
from jax.experimental.pallas import tpu as pltpu

BF16 = jnp.bfloat16
F32 = jnp.float32

D_MODEL = 2048
DEPTH = 4
RET_HEADS = 8
RET_DK = D_MODEL // RET_HEADS
RET_DV = 2 * D_MODEL // RET_HEADS
RET_QK = RET_HEADS * RET_DK
RET_V = RET_HEADS * RET_DV
RET_CHUNK = 256
RET_PREP_ROWS = 32
ROPE_BASE = 10000.0
POOL_WINDOWS = (2, 4, 8, 16)
POOL_G = D_MODEL // len(POOL_WINDOWS)
POOL_HALO = 16
D_FF = 5632
N_EXPERTS = 8
TOP_K = 2
D_FF_EXPERT = 1408
NORM_EPS = 1e-6
LANES = 128
SUBLANES = 8
MOE_TILE = 256
SIDE_CAST_BLOCKS = 64
BF16_TILE_ROWS = 16
DOWN_VMEM_LIMIT = 60 * 1024 * 1024
META_E1, META_E2, META_R1, META_R2, META_G1, META_G2 = range(6)


def _rms(x, g):
    ms = jnp.mean(x * x, axis=-1, keepdims=True)
    return x * lax.rsqrt(ms + NORM_EPS) * g


def _silu(a):
    return a / (1.0 + jnp.exp(-a))


def _dot(a, b):
    return jnp.dot(a, b, preferred_element_type=F32)


def _rmsnorm_kernel(x_ref, g_ref, o_ref):
    o_ref[...] = _rms(x_ref[...], g_ref[...]).astype(o_ref.dtype)


def rmsnorm(x, g, out_dtype, tm=512):
    n, d = x.shape
    return pl.pallas_call(
        _rmsnorm_kernel,
        grid=(n // tm,),
        in_specs=[pl.BlockSpec((tm, d), lambda i: (i, 0)),
                  pl.BlockSpec((1, d), lambda i: (0, 0))],
        out_specs=pl.BlockSpec((tm, d), lambda i: (i, 0)),
        out_shape=jax.ShapeDtypeStruct((n, d), out_dtype),
        name="rmsnorm",
    )(x, g.reshape(1, d))


def _side_cast_specs(sides, steps, n_inner):
    in_specs, out_specs, shapes = [], [], []
    for side, side_layer in sides:
        _, rows, cols = side.shape
        n_blocks = 1
        while (n_blocks * 2 <= min(steps, SIDE_CAST_BLOCKS)
               and rows % (n_blocks * 2 * BF16_TILE_ROWS) == 0):
            n_blocks *= 2
        br = rows // n_blocks

        def index(j, i, n_blocks=n_blocks):
            return jnp.minimum(j * n_inner + i, n_blocks - 1)

        in_specs.append(pl.BlockSpec((None, br, cols),
                                     lambda j, i, l=side_layer, ix=index: (l, ix(j, i), 0)))
        out_specs.append(pl.BlockSpec((br, cols), lambda j, i, ix=index: (ix(j, i), 0)))
        shapes.append(jax.ShapeDtypeStruct((rows, cols), BF16))
    return in_specs, out_specs, shapes


def _cast_sides(refs, n_sides):
    for s_ref, so_ref in zip(refs[:n_sides], refs[n_sides + 1:2 * n_sides + 1]):
        so_ref[...] = s_ref[...].astype(BF16)


def _in_proj_kernel(x_ref, w_ref, perm_ref, *refs, n_qk_slabs, n_sides):
    o_ref, wb_ref = refs[n_sides], refs[-1]

    @pl.when(pl.program_id(1) == 0)
    def _():
        is_qk = pl.program_id(0) < n_qk_slabs

        @pl.when(is_qk)
        def _():
            for lo in range(0, wb_ref.shape[1], RET_DK):
                head = w_ref[:, lo:lo + RET_DK].astype(BF16)
                wb_ref[:, lo:lo + RET_DK] = _dot(head, perm_ref[...]).astype(BF16)

        @pl.when(jnp.logical_not(is_qk))
        def _():
            wb_ref[...] = w_ref[...].astype(BF16)

    o_ref[...] = _dot(x_ref[...], wb_ref[...]).astype(o_ref.dtype)
    _cast_sides(refs, n_sides)


def retention_in_proj(x, w, layer, sides, tm=1024, tn=1024):
    n, k = x.shape
    f = w.shape[2]
    tm = min(tm, n)
    nj, ni = f // tn, n // tm
    half = RET_DK // 2
    src = jnp.arange(RET_DK)
    dst = jnp.where(src % 2 == 0, src // 2, half + src // 2)
    perm = (dst[:, None] == jnp.arange(RET_DK)[None, :]).astype(BF16)
    s_in, s_out, s_shapes = _side_cast_specs(sides, nj * ni, ni)
    return pl.pallas_call(
        functools.partial(_in_proj_kernel, n_qk_slabs=2 * RET_QK // tn, n_sides=len(sides)),
        grid=(nj, ni),
        in_specs=[pl.BlockSpec((tm, k), lambda j, i: (i, 0)),
                  pl.BlockSpec((None, k, tn), lambda j, i: (layer, 0, j)),
                  pl.BlockSpec((RET_DK, RET_DK), lambda j, i: (0, 0)),
                  *s_in],
        out_specs=[pl.BlockSpec((tm, tn), lambda j, i: (i, j)), *s_out],
        out_shape=[jax.ShapeDtypeStruct((n, f), BF16), *s_shapes],
        scratch_shapes=[pltpu.VMEM((k, tn), BF16)],
        compiler_params=pltpu.CompilerParams(dimension_semantics=("arbitrary", "arbitrary")),
        name="retention_in_proj",
    )(x, w, perm, *(side for side, _ in sides))


def _swiglu_kernel(x_ref, ssq_ref, wa_ref, wu_ref, *refs, n_sides):
    o_ref, wab_ref, wub_ref = refs[n_sides], refs[-2], refs[-1]

    @pl.when(pl.program_id(1) == 0)
    def _():
        wab_ref[...] = wa_ref[...].astype(BF16)
        wub_ref[...] = wu_ref[...].astype(BF16)

    x = x_ref[...]
    rowscale = lax.rsqrt(jnp.sum(ssq_ref[...], axis=0) / x.shape[1] + NORM_EPS)
    a = _dot(x, wab_ref[...]) * rowscale
    u = _dot(x, wub_ref[...]) * rowscale
    o_ref[...] = (_silu(a) * u).astype(o_ref.dtype)
    _cast_sides(refs, n_sides)


def swiglu_up(xg, ssq, w_gu, layer, sides, tm=1024, tn=512):
    n, k = xg.shape
    f = w_gu.shape[2] // 2
    tm = min(tm, n)
    nj, ni = f // tn, n // tm
    ns = ssq.shape[0]
    s_in, s_out, s_shapes = _side_cast_specs(sides, nj * ni, ni)
    return pl.pallas_call(
        functools.partial(_swiglu_kernel, n_sides=len(sides)),
        grid=(nj, ni),
        in_specs=[pl.BlockSpec((tm, k), lambda j, i: (i, 0)),
                  pl.BlockSpec((ns, tm, 1), lambda j, i: (0, i, 0)),
                  pl.BlockSpec((None, k, tn), lambda j, i: (layer, 0, j)),
                  pl.BlockSpec((None, k, tn), lambda j, i: (layer, 0, nj + j)),
                  *s_in],
        out_specs=[pl.BlockSpec((tm, tn), lambda j, i: (i, j)), *s_out],
        out_shape=[jax.ShapeDtypeStruct((n, f), BF16), *s_shapes],
        scratch_shapes=[pltpu.VMEM((k, tn), BF16), pltpu.VMEM((k, tn), BF16)],
        compiler_params=pltpu.CompilerParams(dimension_semantics=("arbitrary", "arbitrary")),
        name="swiglu_up",
    )(xg, ssq, w_gu, w_gu, *(side for side, _ in sides))


def _emit_residual_outputs(xn, g_ref, refs, emit_x, emit_norm):
    pos = 0
    if emit_x:
        refs[pos][...] = xn
        pos += 1
    if emit_norm:
        refs[pos][...] = _rms(xn, g_ref[...]).astype(refs[pos].dtype)


def _residual_out_shapes(n, d, tm, emit_x, norm_dtype, index_map):
    out_shape, out_specs = [], []
    if emit_x:
        out_shape.append(jax.ShapeDtypeStruct((n, d), F32))
        out_specs.append(pl.BlockSpec((tm, d), index_map))
    if norm_dtype is not None:
        out_shape.append(jax.ShapeDtypeStruct((n, d), norm_dtype))
        out_specs.append(pl.BlockSpec((tm, d), index_map))
    return out_shape, out_specs


def _down_kernel(x_ref, w_ref, r_ref, g_ref, *refs, emit_scaled):
    xn = r_ref[...] + _dot(x_ref[...], w_ref[...])
    refs[0][...] = xn
    if emit_scaled:
        refs[1][...] = (xn * g_ref[...]).astype(BF16)
        refs[2][...] = jnp.sum(xn * xn, axis=-1, keepdims=True)[None]


def down_residual(h, w, resid, g=None, *, tm=512):
    n, kk = h.shape
    d = w.shape[1]
    tm = min(tm, n)
    emit_scaled = g is not None
    if g is None:
        g = jnp.ones((d,), F32)
    out_shape = [jax.ShapeDtypeStruct((n, d), F32)]
    out_specs = [pl.BlockSpec((tm, d), lambda i: (i, 0))]
    if emit_scaled:
        out_shape += [jax.ShapeDtypeStruct((n, d), BF16), jax.ShapeDtypeStruct((1, n, 1), F32)]
        out_specs += [pl.BlockSpec((tm, d), lambda i: (i, 0)),
                      pl.BlockSpec((1, tm, 1), lambda i: (0, i, 0))]
    return pl.pallas_call(
        functools.partial(_down_kernel, emit_scaled=emit_scaled),
        grid=(n // tm,),
        in_specs=[pl.BlockSpec((tm, kk), lambda i: (i, 0)),
                  pl.BlockSpec((kk, d), lambda i: (0, 0), pipeline_mode=pl.Buffered(1)),
                  pl.BlockSpec((tm, d), lambda i: (i, 0)),
                  pl.BlockSpec((1, d), lambda i: (0, 0))],
        out_specs=out_specs,
        out_shape=out_shape,
        compiler_params=pltpu.CompilerParams(dimension_semantics=("arbitrary",),
                                             vmem_limit_bytes=DOWN_VMEM_LIMIT),
        name="down_residual",
    )(h, w, resid, g.reshape(1, d))


def _retention_kernel(lg_ref, q_ref, k_ref, v_ref, g_ref, cos_ref, sin_ref, gain_ref, o_ref,
                      qr_ref, qd_ref, kr_ref, kd_ref, mask_ref, qdec_ref, kdec_ref, oraw_ref,
                      state_ref, sb_ref, *, n_chunks, n_pairs):
    c = RET_CHUNK
    step = pl.program_id(0)
    slot_prep = step % 2
    slot_recur = 1 - slot_prep
    slot_finish = slot_prep

    @pl.when(step == 0)
    def _():
        for ref in (qr_ref, qd_ref, kr_ref, kd_ref, oraw_ref):
            ref[...] = jnp.zeros_like(ref)

    lg_prep = lg_ref[jnp.minimum(step, n_pairs - 1) % RET_HEADS]
    lg = lg_ref[jnp.clip(step - 1, 0, n_pairs - 1) % RET_HEADS]
    ii = lax.broadcasted_iota(jnp.int32, (c, c), 0)
    jj = lax.broadcasted_iota(jnp.int32, (c, c), 1)
    rel = (ii - jj).astype(F32)
    mask_ref[...] = jnp.where(rel >= 0.0, jnp.exp(jnp.maximum(rel, 0.0) * lg), 0.0)
    idx = lax.broadcasted_iota(jnp.int32, (c, 1), 0).astype(F32)
    k_scale = RET_DK ** -0.5
    qdec_ref[...] = jnp.broadcast_to(jnp.exp((idx + 1.0) * lg_prep), (c, RET_DK))
    kdec_ref[...] = jnp.broadcast_to(jnp.exp((c - 1.0 - idx) * lg_prep) * k_scale, (c, RET_DK))
    chunk_decay = jnp.exp(jnp.full((1, 1), float(c), F32) * lg)
    half = RET_DK // 2

    def rotary(xb, cos, sin):
        x = xb.astype(F32)
        x1, x2 = x[:, :half], x[:, half:]
        return jnp.concatenate([x1 * cos - x2 * sin, x2 * cos + x1 * sin], axis=1)

    def rows(ci):
        return pl.ds(pl.multiple_of(ci * c, c), c)

    state_ref[...] = jnp.zeros_like(state_ref)
    sb_ref[...] = jnp.zeros_like(sb_ref)

    def body(ci, carry):
        r = rows(ci)

        o = oraw_ref[slot_finish, r, :]
        ms = jnp.mean(o * o, axis=-1, keepdims=True)
        o = o * lax.rsqrt(ms + NORM_EPS) * gain_ref[...]
        gate = g_ref[r, :].astype(F32)
        o_ref[r, :] = (o * _silu(gate)).astype(o_ref.dtype)

        v = v_ref[r, :]
        scores = lax.dot_general(qr_ref[slot_recur, r, :], kr_ref[slot_recur, r, :],
                                 (((1,), (1,)), ((), ())),
                                 preferred_element_type=F32) * mask_ref[...]
        oraw_ref[slot_recur, r, :] = (_dot(scores.astype(BF16), v)
                                      + _dot(qd_ref[slot_recur, r, :], sb_ref[...]))
        upd = lax.dot_general(kd_ref[slot_recur, r, :], v, (((0,), (0,)), ((), ())),
                              preferred_element_type=F32)
        state = state_ref[...] * chunk_decay + upd
        state_ref[...] = state
        sb_ref[...] = state.astype(BF16)

        for lo in range(0, c, RET_PREP_ROWS):
            rr = pl.ds(pl.multiple_of(ci * c + lo, RET_PREP_ROWS), RET_PREP_ROWS)
            dec = pl.ds(lo, RET_PREP_ROWS)
            cos = cos_ref[rr, :]
            sin = sin_ref[rr, :]
            q = rotary(q_ref[rr, :], cos, sin)
            k = rotary(k_ref[rr, :], cos, sin)
            qr_ref[slot_prep, rr, :] = q.astype(BF16)
            qd_ref[slot_prep, rr, :] = (q * qdec_ref[dec, :]).astype(BF16)
            kr_ref[slot_prep, rr, :] = (k * k_scale).astype(BF16)
            kd_ref[slot_prep, rr, :] = (k * kdec_ref[dec, :]).astype(BF16)
        return carry

    lax.fori_loop(0, n_chunks, body, 0)


def retention(proj, norm_g, layer, batch, seq):
    n = batch * seq
    h = RET_HEADS
    log_g = jnp.log1p(-jnp.exp2(-5.0 - jnp.arange(h, dtype=F32)))
    freq = 1.0 / (ROPE_BASE ** jnp.linspace(0.0, 1.0, RET_DK // 2, dtype=F32))
    ang = jnp.arange(seq).astype(F32)[:, None] * freq[None, :]
    cos = jnp.cos(ang)
    sin = jnp.sin(ang)
    kq = RET_QK // RET_DK
    kv = 2 * RET_QK // RET_DV
    kg = (2 * RET_QK + RET_V) // RET_DV
    n_pairs = batch * h
    stages = 3

    def pair(step, stage):
        p = jnp.clip(step - stage, 0, n_pairs - 1)
        return p // h, p % h

    def block(stage, col0):
        def index(s):
            b, hh = pair(s, stage)
            return b, col0 + hh
        return index

    def gain_index(s):
        return layer, 0, pair(s, 2)[1]

    return pl.pallas_call(
        functools.partial(_retention_kernel, n_chunks=seq // RET_CHUNK, n_pairs=n_pairs),
        grid=(n_pairs + stages - 1,),
        in_specs=[pl.BlockSpec(memory_space=pltpu.SMEM),
                  pl.BlockSpec((seq, RET_DK), block(0, 0)),
                  pl.BlockSpec((seq, RET_DK), block(0, kq)),
                  pl.BlockSpec((seq, RET_DV), block(1, kv)),
                  pl.BlockSpec((seq, RET_DV), block(2, kg)),
                  pl.BlockSpec((seq, RET_DK // 2), lambda s: (0, 0)),
                  pl.BlockSpec((seq, RET_DK // 2), lambda s: (0, 0)),
                  pl.BlockSpec((None, 1, RET_DV), gain_index)],
        out_specs=pl.BlockSpec((seq, RET_DV), block(2, 0)),
        out_shape=jax.ShapeDtypeStruct((n, RET_V), BF16),
        scratch_shapes=[pltpu.VMEM((2, seq, RET_DK), BF16),
                        pltpu.VMEM((2, seq, RET_DK), BF16),
                        pltpu.VMEM((2, seq, RET_DK), BF16),
                        pltpu.VMEM((2, seq, RET_DK), BF16),
                        pltpu.VMEM((RET_CHUNK, RET_CHUNK), F32),
                        pltpu.VMEM((RET_CHUNK, RET_DK), F32),
                        pltpu.VMEM((RET_CHUNK, RET_DK), F32),
                        pltpu.VMEM((2, seq, RET_DV), F32),
                        pltpu.VMEM((RET_DK, RET_DV), F32),
                        pltpu.VMEM((RET_DK, RET_DV), BF16)],
        compiler_params=pltpu.CompilerParams(dimension_semantics=("arbitrary",)),
        name="retention",
    )(log_g, proj, proj, proj, proj, cos, sin, norm_g.reshape(norm_g.shape[0], 1, RET_V))


def _pool_kernel(x_ref, gm_ref, pw_ref, sc_ref, gf_ref, rt_ref,
                 xo_ref, hn_ref, meta_ref, meta_t_ref, cnt_out_ref, hbuf, cnt_ref, *, tiles_per_seq):
    tm = x_ref.shape[0]
    step = pl.program_id(0)
    ti = step % tiles_per_seq
    x = x_ref[...]
    hn = _rms(x, gm_ref[...])

    @pl.when(ti == 0)
    def _():
        hbuf[0:POOL_HALO, :] = jnp.zeros((POOL_HALO, D_MODEL), F32)

    @pl.when(ti != 0)
    def _():
        hbuf[0:POOL_HALO, :] = hbuf[tm:tm + POOL_HALO, :]

    hbuf[POOL_HALO:POOL_HALO + tm, :] = hn

    t = ti * tm + lax.broadcasted_iota(jnp.int32, (tm, 1), 0)
    for gi, w in enumerate(POOL_WINDOWS):
        lo, hi = gi * POOL_G, (gi + 1) * POOL_G
        cur = hn[:, lo:hi]
        acc = hbuf[:, lo:hi]
        shift = 1
        while shift < w:
            acc = acc + pltpu.roll(acc, shift, 0)
            shift *= 2
        acc = acc[POOL_HALO:, :]
        inv = 1.0 / jnp.minimum(t + 1, w).astype(F32)
        mix = (acc * inv - cur).astype(BF16)
        o = _dot(mix, pw_ref[gi]) * sc_ref[:, lo:hi]
        xo_ref[:, lo:hi] = x[:, lo:hi] + o

    hn2 = _rms(xo_ref[...], gf_ref[...])
    hn_ref[...] = hn2

    rt = rt_ref[...]
    hn_hi = hn2.astype(BF16)
    hn_lo = (hn2 - hn_hi.astype(F32)).astype(BF16)
    rt_hi = rt.astype(BF16)
    rt_lo = (rt - rt_hi.astype(F32)).astype(BF16)
    logits = _dot(hn_hi, rt_hi) + (_dot(hn_hi, rt_lo) + _dot(hn_lo, rt_hi))
    lane = lax.broadcasted_iota(jnp.int32, logits.shape, 1)
    neg = jnp.float32(-jnp.inf)
    lg = jnp.where(lane < N_EXPERTS, logits, neg)
    m1 = jnp.max(lg, axis=-1, keepdims=True)
    i1 = jnp.min(jnp.where(lg == m1, lane, LANES), axis=-1, keepdims=True)
    lg2 = jnp.where(lane == i1, neg, lg)
    m2 = jnp.max(lg2, axis=-1, keepdims=True)
    i2 = jnp.min(jnp.where(lg2 == m2, lane, LANES), axis=-1, keepdims=True)
    e2 = jnp.exp(m2 - m1)
    g1 = 1.0 / (1.0 + e2)
    g2 = e2 * g1

    @pl.when(step == 0)
    def _():
        cnt_ref[...] = jnp.zeros_like(cnt_ref)

    chosen = jnp.where((lane == i1) | (lane == i2), 1.0, 0.0)
    rr = lax.broadcasted_iota(jnp.int32, (tm, tm), 0)
    cc = lax.broadcasted_iota(jnp.int32, (tm, tm), 1)
    earlier = jnp.where(rr > cc, 1.0, 0.0).astype(BF16)
    before = _dot(earlier, chosen.astype(BF16)) + cnt_ref[...]
    r1 = jnp.sum(jnp.where(lane == i1, before, 0.0), axis=-1, keepdims=True)
    r2 = jnp.sum(jnp.where(lane == i2, before, 0.0), axis=-1, keepdims=True)
    cnt_ref[...] += jnp.sum(chosen, axis=0, keepdims=True)
    cnt_out_ref[...] = jnp.broadcast_to(cnt_ref[...], cnt_out_ref.shape)

    meta = jnp.zeros(logits.shape, F32)
    for li, field in ((META_E1, i1.astype(F32)), (META_E2, i2.astype(F32)), (META_R1, r1),
                      (META_R2, r2), (META_G1, g1), (META_G2, g2)):
        meta = jnp.where(lane == li, field, meta)
    meta_ref[...] = meta
    meta_t_ref[...] = meta.T[:SUBLANES, :]


def pool_router(x, g_mix, pool_w, layer, pool_scale, g_ffn, router, seq, tm=512):
    n, d = x.shape
    tm = min(tm, seq)
    router_p = jnp.zeros((d, LANES), F32).at[:, :N_EXPERTS].set(router)
    ng = len(POOL_WINDOWS)
    return pl.pallas_call(
        functools.partial(_pool_kernel, tiles_per_seq=seq // tm),
        grid=(n // tm,),
        in_specs=[pl.BlockSpec((tm, d), lambda i: (i, 0)),
                  pl.BlockSpec((1, d), lambda i: (0, 0)),
                  pl.BlockSpec((None, ng, POOL_G, POOL_G), lambda i: (layer, 0, 0, 0)),
                  pl.BlockSpec((1, d), lambda i: (0, 0)),
                  pl.BlockSpec((1, d), lambda i: (0, 0)),
                  pl.BlockSpec((d, LANES), lambda i: (0, 0))],
        out_specs=[pl.BlockSpec((tm, d), lambda i: (i, 0)),
                   pl.BlockSpec((tm, d), lambda i: (i, 0)),
                   pl.BlockSpec((tm, LANES), lambda i: (i, 0)),
                   pl.BlockSpec((SUBLANES, tm), lambda i: (0, i)),
                   pl.BlockSpec((SUBLANES, LANES), lambda i: (0, 0))],
        out_shape=[jax.ShapeDtypeStruct((n, d), F32),
                   jax.ShapeDtypeStruct((n, d), F32),
                   jax.ShapeDtypeStruct((n, LANES), F32),
                   jax.ShapeDtypeStruct((SUBLANES, n), F32),
                   jax.ShapeDtypeStruct((SUBLANES, LANES), F32)],
        scratch_shapes=[pltpu.VMEM((POOL_HALO + tm, d), F32),
                        pltpu.VMEM((1, LANES), F32)],
        compiler_params=pltpu.CompilerParams(dimension_semantics=("arbitrary",)),
        name="pool_router",
    )(x, g_mix.reshape(1, d), pool_w, pool_scale.reshape(1, d), g_ffn.reshape(1, d), router_p)


def moe_num_tiles(n):
    return TOP_K * n // MOE_TILE + N_EXPERTS


def moe_schedule(meta_t, counts_f, n):
    nt = moe_num_tiles(n)
    counts = counts_f[0, :N_EXPERTS].astype(jnp.int32)
    tiles = (counts + MOE_TILE - 1) // MOE_TILE
    tile_end = jnp.cumsum(tiles)
    starts = (tile_end - tiles) * MOE_TILE
    expert_ids = jnp.arange(N_EXPERTS, dtype=jnp.int32)[:, None]

    def slots(e_row, r_row):
        e = meta_t[e_row].astype(jnp.int32)
        first = jnp.sum(jnp.where(e[None, :] == expert_ids, starts[:, None], 0), axis=0)
        return first + meta_t[r_row].astype(jnp.int32)

    pos = jnp.concatenate([slots(META_E1, META_R1), slots(META_E2, META_R2)])
    total = tile_end[-1]
    t = jnp.arange(nt, dtype=jnp.int32)
    t_used = jnp.minimum(t, total - 1)
    tile_expert = jnp.sum((tile_end[None, :] <= t_used[:, None]).astype(jnp.int32), axis=1)
    tile_expert = jnp.minimum(tile_expert, N_EXPERTS - 1)
    tile_valid = (t < total).astype(jnp.int32)
    pad_lo = starts + counts
    pad_hi = starts + tiles * MOE_TILE
    return pos, tile_expert, tile_valid, pad_lo, pad_hi, total.reshape(1)


def _dispatch_kernel(pos_ref, lo_ref, hi_ref, used_ref, hn_ref, xs_hbm, zeros, sem, *, n, n_tiles):
    tb = hn_ref.shape[0]
    step = pl.program_id(0)
    base = step * tb

    def issue(r, c):
        tok = base + r
        src = hn_ref.at[pl.ds(r, 1), :]
        pltpu.make_async_copy(src, xs_hbm.at[pl.ds(pos_ref[tok], 1), :], sem).start()
        pltpu.make_async_copy(src, xs_hbm.at[pl.ds(pos_ref[n + tok], 1), :], sem).start()
        return c

    lax.fori_loop(0, tb, issue, 0, unroll=8)
    for _ in range(TOP_K):
        pltpu.make_async_copy(hn_ref, xs_hbm.at[pl.ds(0, tb), :], sem).wait()

    @pl.when(step == pl.num_programs(0) - 1)
    def _():
        zeros[...] = jnp.zeros_like(zeros)

        def pad_copy(slot):
            return pltpu.make_async_copy(zeros.at[pl.ds(0, 1), :], xs_hbm.at[pl.ds(slot, 1), :], sem)

        def issue_pad(s, c):
            pad_copy(s).start()
            return c

        def drain_pad(s, c):
            pad_copy(s).wait()
            return c

        for e in range(N_EXPERTS):
            lax.fori_loop(lo_ref[e], hi_ref[e], issue_pad, 0)
            lax.fori_loop(lo_ref[e], hi_ref[e], drain_pad, 0)

        def tile_copy(t):
            return pltpu.make_async_copy(zeros, xs_hbm.at[pl.ds(t * MOE_TILE, MOE_TILE), :], sem)

        def issue_tile(t, c):
            tile_copy(t).start()
            return c

        def drain_tile(t, c):
            tile_copy(t).wait()
            return c

        lax.fori_loop(used_ref[0], n_tiles, issue_tile, 0)
        lax.fori_loop(used_ref[0], n_tiles, drain_tile, 0)


def moe_dispatch(hn, pos, pad_lo, pad_hi, used, tb=512):
    n, d = hn.shape
    tb = min(tb, n)
    nt = moe_num_tiles(n)
    grid_spec = pltpu.PrefetchScalarGridSpec(
        num_scalar_prefetch=4,
        grid=(n // tb,),
        in_specs=[pl.BlockSpec((tb, d), lambda i, *_: (i, 0))],
        out_specs=pl.BlockSpec(memory_space=pl.ANY),
        scratch_shapes=[pltpu.VMEM((MOE_TILE, d), F32), pltpu.SemaphoreType.DMA],
    )
    return pl.pallas_call(
        functools.partial(_dispatch_kernel, n=n, n_tiles=nt),
        grid_spec=grid_spec,
        out_shape=jax.ShapeDtypeStruct((nt * MOE_TILE, d), F32),
        compiler_params=pltpu.CompilerParams(dimension_semantics=("arbitrary",)),
        name="moe_dispatch",
    )(pos, pad_lo, pad_hi, used, hn)


def _expert_kernel(te_ref, tv_ref, x_ref, wa_ref, wu_ref, wd_ref, o_ref):
    valid = tv_ref[pl.program_id(0)] == 1

    @pl.when(valid)
    def _():
        x = x_ref[...].astype(BF16)
        a = _dot(x, wa_ref[...])
        u = _dot(x, wu_ref[...])
        o_ref[...] = _dot((_silu(a) * u).astype(BF16), wd_ref[...])

    @pl.when(jnp.logical_not(valid))
    def _():
        o_ref[...] = jnp.zeros_like(o_ref)


def moe_experts(xs, w_gu, w_down, tile_expert, tile_valid):
    rows, d = xs.shape
    f = w_down.shape[1]
    nt = rows // MOE_TILE
    grid_spec = pltpu.PrefetchScalarGridSpec(
        num_scalar_prefetch=2,
        grid=(nt,),
        in_specs=[pl.BlockSpec((MOE_TILE, d), lambda t, te, tv: (t, 0)),
                  pl.BlockSpec((None, d, f), lambda t, te, tv: (te[t], 0, 0)),
                  pl.BlockSpec((None, d, f), lambda t, te, tv: (te[t], 0, 1)),
                  pl.BlockSpec((None, f, d), lambda t, te, tv: (te[t], 0, 0))],
        out_specs=pl.BlockSpec((MOE_TILE, d), lambda t, te, tv: (t, 0)),
    )
    return pl.pallas_call(
        _expert_kernel,
        grid_spec=grid_spec,
        out_shape=jax.ShapeDtypeStruct((rows, d), F32),
        compiler_params=pltpu.CompilerParams(dimension_semantics=("arbitrary",),
                                             vmem_limit_bytes=56 * 1024 * 1024),
        name="moe_experts",
    )(tile_expert, tile_valid, xs, w_gu, w_gu, w_down)


def _combine_kernel(pos_ref, x_ref, meta_ref, g_ref, y_hbm, *refs, n, emit_x, emit_norm):
    ybuf, sem = refs[-2], refs[-1]
    tc = x_ref.shape[0]
    step = pl.program_id(0)
    slot = step % 2

    def start_gather(s, into):
        def issue(r, c):
            tok = s * tc + r
            for k in range(TOP_K):
                pltpu.make_async_copy(y_hbm.at[pl.ds(pos_ref[k * n + tok], 1), :],
                                      ybuf.at[into, k, pl.ds(r, 1), :], sem.at[into]).start()
            return c

        lax.fori_loop(0, tc, issue, 0, unroll=8)

    @pl.when(step == 0)
    def _():
        start_gather(0, 0)

    @pl.when(step + 1 < pl.num_programs(0))
    def _():
        start_gather(step + 1, 1 - slot)

    for k in range(TOP_K):
        pltpu.make_async_copy(y_hbm.at[pl.ds(0, tc), :], ybuf.at[slot, k], sem.at[slot]).wait()

    meta = meta_ref[...]
    g1 = meta[:, META_G1:META_G1 + 1]
    g2 = meta[:, META_G2:META_G2 + 1]
    xn = x_ref[...] + g1 * ybuf[slot, 0] + g2 * ybuf[slot, 1]
    _emit_residual_outputs(xn, g_ref, refs, emit_x, emit_norm)


def moe_combine(x, y, meta, pos, g, *, emit_x=True, norm_dtype=None, tc=256):
    n, d = x.shape
    tc = min(tc, n)
    out_shape, out_specs = _residual_out_shapes(n, d, tc, emit_x, norm_dtype, lambda i, p: (i, 0))
    grid_spec = pltpu.PrefetchScalarGridSpec(
        num_scalar_prefetch=1,
        grid=(n // tc,),
        in_specs=[pl.BlockSpec((tc, d), lambda i, p: (i, 0)),
                  pl.BlockSpec((tc, LANES), lambda i, p: (i, 0)),
                  pl.BlockSpec((1, d), lambda i, p: (0, 0)),
                  pl.BlockSpec(memory_space=pl.ANY)],
        out_specs=out_specs,
        scratch_shapes=[pltpu.VMEM((2, TOP_K, tc, d), F32), pltpu.SemaphoreType.DMA((2,))],
    )
    return pl.pallas_call(
        functools.partial(_combine_kernel, n=n, emit_x=emit_x, emit_norm=norm_dtype is not None),
        grid_spec=grid_spec,
        out_shape=out_shape,
        compiler_params=pltpu.CompilerParams(dimension_semantics=("arbitrary",)),
        name="moe_combine",
    )(pos, x, meta, g.reshape(1, d), y)


def kernel(x, ret_w_in, ret_norm_g, ret_w_out, pool_w, pool_scale, ffn_w_gu, ffn_w_down,
           moe_router, moe_w_gu, moe_w_down, norm_mix_g, norm_ffn_g, norm_final_g):
    batch, seq, d = x.shape
    n = batch * seq
    xf = x.reshape(n, d)
    pool_w = pool_w.astype(BF16)
    n_moe, n_exp, _, f2 = moe_w_gu.shape
    f = moe_w_down.shape[2]
    moe_gu_rows = moe_w_gu.reshape(n_moe, n_exp * d, f2)
    moe_down_rows = moe_w_down.reshape(n_moe, n_exp * f, d)
    hn = rmsnorm(xf, norm_mix_g[0], BF16)
    for i in range(DEPTH):
        j = i // 2
        last = i == DEPTH - 1
        if i % 2 == 0:
            proj, moe_gu_b, w_out_b = retention_in_proj(hn, ret_w_in, j,
                                                        [(moe_gu_rows, j), (ret_w_out, j)])
            y = retention(proj, ret_norm_g, j, batch, seq)
            xf, xg, ssq = down_residual(y, w_out_b, xf, norm_ffn_g[i])
            hid, moe_down_b, ffn_down_b = swiglu_up(xg, ssq, ffn_w_gu, j,
                                                    [(moe_down_rows, j), (ffn_w_down, j)])
            (xf,) = down_residual(hid, ffn_down_b, xf)
            moe_gu_b = moe_gu_b.reshape(n_exp, d, f2)
            moe_down_b = moe_down_b.reshape(n_exp, f, d)
        else:
            xf, hn, meta, meta_t, counts = pool_router(xf, norm_mix_g[i], pool_w, j, pool_scale[j],
                                               norm_ffn_g[i], moe_router[j], seq)
            pos, tile_expert, tile_valid, pad_lo, pad_hi, used = moe_schedule(meta_t, counts, n)
            xs = moe_dispatch(hn, pos, pad_lo, pad_hi, used)
            ys = moe_experts(xs, moe_gu_b, moe_down_b, tile_expert, tile_valid)
            if last:
                (out,) = moe_combine(xf, ys, meta, pos, norm_final_g, emit_x=False, norm_dtype=F32)
            else:
                xf, hn = moe_combine(xf, ys, meta, pos, norm_mix_g[i + 1], norm_dtype=BF16)
    return out.reshape(batch, seq, d)
```

```python
import functools

import jax
import jax.numpy as jnp
from jax import lax
from jax.experimental import pallas as pl
from jax.experimental.pallas import tpu as pltpu

BF16 = jnp.bfloat16
F32 = jnp.float32

D_MODEL = 2048
DEPTH = 4
RET_HEADS = 8
RET_DK = D_MODEL // RET_HEADS
RET_DV = 2 * D_MODEL // RET_HEADS
RET_QK = RET_HEADS * RET_DK
RET_V = RET_HEADS * RET_DV
RET_CHUNK = 256
RET_PREP_ROWS = 32
ROPE_BASE = 10000.0
POOL_WINDOWS = (2, 4, 8, 16)
POOL_G = D_MODEL // len(POOL_WINDOWS)
POOL_HALO = 16
D_FF = 5632
N_EXPERTS = 8
TOP_K = 2
D_FF_EXPERT = 1408
NORM_EPS = 1e-6
LANES = 128
SUBLANES = 8
MOE_TILE = 256
SIDE_CAST_BLOCKS = 64
BF16_TILE_ROWS = 16
DOWN_VMEM_LIMIT = 60 * 1024 * 1024
META_E1, META_E2, META_R1, META_R2, META_G1, META_G2 = range(6)


def _rms(x, g):
    ms = jnp.mean(x * x, axis=-1, keepdims=True)
    return x * lax.rsqrt(ms + NORM_EPS) * g


def _silu(a):
    return a / (1.0 + jnp.exp(-a))


def _dot(a, b):
    return jnp.dot(a, b, preferred_element_type=F32)


def _rmsnorm_kernel(x_ref, g_ref, o_ref):
    o_ref[...] = _rms(x_ref[...], g_ref[...]).astype(o_ref.dtype)


def rmsnorm(x, g, out_dtype, tm=512):
    n, d = x.shape
    return pl.pallas_call(
        _rmsnorm_kernel,
        grid=(n // tm,),
        in_specs=[pl.BlockSpec((tm, d), lambda i: (i, 0)),
                  pl.BlockSpec((1, d), lambda i: (0, 0))],
        out_specs=pl.BlockSpec((tm, d), lambda i: (i, 0)),
        out_shape=jax.ShapeDtypeStruct((n, d), out_dtype),
        name="rmsnorm",
    )(x, g.reshape(1, d))


def _side_cast_specs(sides, steps, n_inner):
    in_specs, out_specs, shapes = [], [], []
    for side, side_layer in sides:
        _, rows, cols = side.shape
        n_blocks = 1
        while (n_blocks * 2 <= min(steps, SIDE_CAST_BLOCKS)
               and rows % (n_blocks * 2 * BF16_TILE_ROWS) == 0):
            n_blocks *= 2
        br = rows // n_blocks

        def index(j, i, n_blocks=n_blocks):
            return jnp.minimum(j * n_inner + i, n_blocks - 1)

        in_specs.append(pl.BlockSpec((None, br, cols),
                                     lambda j, i, l=side_layer, ix=index: (l, ix(j, i), 0)))
        out_specs.append(pl.BlockSpec((br, cols), lambda j, i, ix=index: (ix(j, i), 0)))
        shapes.append(jax.ShapeDtypeStruct((rows, cols), BF16))
    return in_specs, out_specs, shapes


def _cast_sides(refs, n_sides):
    for s_ref, so_ref in zip(refs[:n_sides], refs[n_sides + 1:2 * n_sides + 1]):
        so_ref[...] = s_ref[...].astype(BF16)


def _in_proj_kernel(x_ref, w_ref, perm_ref, *refs, n_qk_slabs, n_sides):
    o_ref, wb_ref = refs[n_sides], refs[-1]

    @pl.when(pl.program_id(1) == 0)
    def _():
        is_qk = pl.program_id(0) < n_qk_slabs

        @pl.when(is_qk)
        def _():
            for lo in range(0, wb_ref.shape[1], RET_DK):
                head = w_ref[:, lo:lo + RET_DK].astype(BF16)
                wb_ref[:, lo:lo + RET_DK] = _dot(head, perm_ref[...]).astype(BF16)

        @pl.when(jnp.logical_not(is_qk))
        def _():
            wb_ref[...] = w_ref[...].astype(BF16)

    o_ref[...] = _dot(x_ref[...], wb_ref[...]).astype(o_ref.dtype)
    _cast_sides(refs, n_sides)


def retention_in_proj(x, w, layer, sides, tm=1024, tn=1024):
    n, k = x.shape
    f = w.shape[2]
    tm = min(tm, n)
    nj, ni = f // tn, n // tm
    half = RET_DK // 2
    src = jnp.arange(RET_DK)
    dst = jnp.where(src % 2 == 0, src // 2, half + src // 2)
    perm = (dst[:, None] == jnp.arange(RET_DK)[None, :]).astype(BF16)
    s_in, s_out, s_shapes = _side_cast_specs(sides, nj * ni, ni)
    return pl.pallas_call(
        functools.partial(_in_proj_kernel, n_qk_slabs=2 * RET_QK // tn, n_sides=len(sides)),
        grid=(nj, ni),
        in_specs=[pl.BlockSpec((tm, k), lambda j, i: (i, 0)),
                  pl.BlockSpec((None, k, tn), lambda j, i: (layer, 0, j)),
                  pl.BlockSpec((RET_DK, RET_DK), lambda j, i: (0, 0)),
                  *s_in],
        out_specs=[pl.BlockSpec((tm, tn), lambda j, i: (i, j)), *s_out],
        out_shape=[jax.ShapeDtypeStruct((n, f), BF16), *s_shapes],
        scratch_shapes=[pltpu.VMEM((k, tn), BF16)],
        compiler_params=pltpu.CompilerParams(dimension_semantics=("arbitrary", "arbitrary")),
        name="retention_in_proj",
    )(x, w, perm, *(side for side, _ in sides))


def _swiglu_kernel(x_ref, ssq_ref, wa_ref, wu_ref, *refs, n_sides):
    o_ref, wab_ref, wub_ref = refs[n_sides], refs[-2], refs[-1]

    @pl.when(pl.program_id(1) == 0)
    def _():
        wab_ref[...] = wa_ref[...].astype(BF16)
        wub_ref[...] = wu_ref[...].astype(BF16)

    x = x_ref[...]
    rowscale = lax.rsqrt(jnp.sum(ssq_ref[...], axis=0) / x.shape[1] + NORM_EPS)
    a = _dot(x, wab_ref[...]) * rowscale
    u = _dot(x, wub_ref[...]) * rowscale
    o_ref[...] = (_silu(a) * u).astype(o_ref.dtype)
    _cast_sides(refs, n_sides)


def swiglu_up(xg, ssq, w_gu, layer, sides, tm=1024, tn=512):
    n, k = xg.shape
    f = w_gu.shape[2] // 2
    tm = min(tm, n)
    nj, ni = f // tn, n // tm
    ns = ssq.shape[0]
    s_in, s_out, s_shapes = _side_cast_specs(sides, nj * ni, ni)
    return pl.pallas_call(
        functools.partial(_swiglu_kernel, n_sides=len(sides)),
        grid=(nj, ni),
        in_specs=[pl.BlockSpec((tm, k), lambda j, i: (i, 0)),
                  pl.BlockSpec((ns, tm, 1), lambda j, i: (0, i, 0)),
                  pl.BlockSpec((None, k, tn), lambda j, i: (layer, 0, j)),
                  pl.BlockSpec((None, k, tn), lambda j, i: (layer, 0, nj + j)),
                  *s_in],
        out_specs=[pl.BlockSpec((tm, tn), lambda j, i: (i, j)), *s_out],
        out_shape=[jax.ShapeDtypeStruct((n, f), BF16), *s_shapes],
        scratch_shapes=[pltpu.VMEM((k, tn), BF16), pltpu.VMEM((k, tn), BF16)],
        compiler_params=pltpu.CompilerParams(dimension_semantics=("arbitrary", "arbitrary")),
        name="swiglu_up",
    )(xg, ssq, w_gu, w_gu, *(side for side, _ in sides))


def _emit_residual_outputs(xn, g_ref, refs, emit_x, emit_norm):
    pos = 0
    if emit_x:
        refs[pos][...] = xn
        pos += 1
    if emit_norm:
        refs[pos][...] = _rms(xn, g_ref[...]).astype(refs[pos].dtype)


def _residual_out_shapes(n, d, tm, emit_x, norm_dtype, index_map):
    out_shape, out_specs = [], []
    if emit_x:
        out_shape.append(jax.ShapeDtypeStruct((n, d), F32))
        out_specs.append(pl.BlockSpec((tm, d), index_map))
    if norm_dtype is not None:
        out_shape.append(jax.ShapeDtypeStruct((n, d), norm_dtype))
        out_specs.append(pl.BlockSpec((tm, d), index_map))
    return out_shape, out_specs


def _down_kernel(x_ref, w_ref, r_ref, g_ref, *refs, emit_scaled):
    xn = r_ref[...] + _dot(x_ref[...], w_ref[...])
    refs[0][...] = xn
    if emit_scaled:
        refs[1][...] = (xn * g_ref[...]).astype(BF16)
        refs[2][...] = jnp.sum(xn * xn, axis=-1, keepdims=True)[None]


def down_residual(h, w, resid, g=None, *, tm=512):
    n, kk = h.shape
    d = w.shape[1]
    tm = min(tm, n)
    emit_scaled = g is not None
    if g is None:
        g = jnp.ones((d,), F32)
    out_shape = [jax.ShapeDtypeStruct((n, d), F32)]
    out_specs = [pl.BlockSpec((tm, d), lambda i: (i, 0))]
    if emit_scaled:
        out_shape += [jax.ShapeDtypeStruct((n, d), BF16), jax.ShapeDtypeStruct((1, n, 1), F32)]
        out_specs += [pl.BlockSpec((tm, d), lambda i: (i, 0)),
                      pl.BlockSpec((1, tm, 1), lambda i: (0, i, 0))]
    return pl.pallas_call(
        functools.partial(_down_kernel, emit_scaled=emit_scaled),
        grid=(n // tm,),
        in_specs=[pl.BlockSpec((tm, kk), lambda i: (i, 0)),
                  pl.BlockSpec((kk, d), lambda i: (0, 0), pipeline_mode=pl.Buffered(1)),
                  pl.BlockSpec((tm, d), lambda i: (i, 0)),
                  pl.BlockSpec((1, d), lambda i: (0, 0))],
        out_specs=out_specs,
        out_shape=out_shape,
        compiler_params=pltpu.CompilerParams(dimension_semantics=("arbitrary",),
                                             vmem_limit_bytes=DOWN_VMEM_LIMIT),
        name="down_residual",
    )(h, w, resid, g.reshape(1, d))


def _retention_kernel(lg_ref, q_ref, k_ref, v_ref, g_ref, cos_ref, sin_ref, gain_ref, o_ref,
                      qr_ref, qd_ref, kr_ref, kd_ref, mask_ref, qdec_ref, kdec_ref, oraw_ref,
                      state_ref, sb_ref, *, n_chunks, n_pairs):
    c = RET_CHUNK
    step = pl.program_id(0)
    slot_prep = step % 2
    slot_recur = 1 - slot_prep
    slot_finish = slot_prep

    @pl.when(step == 0)
    def _():
        for ref in (qr_ref, qd_ref, kr_ref, kd_ref, oraw_ref):
            ref[...] = jnp.zeros_like(ref)

    lg_prep = lg_ref[jnp.minimum(step, n_pairs - 1) % RET_HEADS]
    lg = lg_ref[jnp.clip(step - 1, 0, n_pairs - 1) % RET_HEADS]
    ii = lax.broadcasted_iota(jnp.int32, (c, c), 0)
    jj = lax.broadcasted_iota(jnp.int32, (c, c), 1)
    rel = (ii - jj).astype(F32)
    mask_ref[...] = jnp.where(rel >= 0.0, jnp.exp(jnp.maximum(rel, 0.0) * lg), 0.0)
    idx = lax.broadcasted_iota(jnp.int32, (c, 1), 0).astype(F32)
    k_scale = RET_DK ** -0.5
    qdec_ref[...] = jnp.broadcast_to(jnp.exp((idx + 1.0) * lg_prep), (c, RET_DK))
    kdec_ref[...] = jnp.broadcast_to(jnp.exp((c - 1.0 - idx) * lg_prep) * k_scale, (c, RET_DK))
    chunk_decay = jnp.exp(jnp.full((1, 1), float(c), F32) * lg)
    half = RET_DK // 2

    def rotary(xb, cos, sin):
        x = xb.astype(F32)
        x1, x2 = x[:, :half], x[:, half:]
        return jnp.concatenate([x1 * cos - x2 * sin, x2 * cos + x1 * sin], axis=1)

    def rows(ci):
        return pl.ds(pl.multiple_of(ci * c, c), c)

    state_ref[...] = jnp.zeros_like(state_ref)
    sb_ref[...] = jnp.zeros_like(sb_ref)

    def body(ci, carry):
        r = rows(ci)

        o = oraw_ref[slot_finish, r, :]
        ms = jnp.mean(o * o, axis=-1, keepdims=True)
        o = o * lax.rsqrt(ms + NORM_EPS) * gain_ref[...]
        gate = g_ref[r, :].astype(F32)
        o_ref[r, :] = (o * _silu(gate)).astype(o_ref.dtype)

        v = v_ref[r, :]
        scores = lax.dot_general(qr_ref[slot_recur, r, :], kr_ref[slot_recur, r, :],
                                 (((1,), (1,)), ((), ())),
                                 preferred_element_type=F32) * mask_ref[...]
        oraw_ref[slot_recur, r, :] = (_dot(scores.astype(BF16), v)
                                      + _dot(qd_ref[slot_recur, r, :], sb_ref[...]))
        upd = lax.dot_general(kd_ref[slot_recur, r, :], v, (((0,), (0,)), ((), ())),
                              preferred_element_type=F32)
        state = state_ref[...] * chunk_decay + upd
        state_ref[...] = state
        sb_ref[...] = state.astype(BF16)

        for lo in range(0, c, RET_PREP_ROWS):
            rr = pl.ds(pl.multiple_of(ci * c + lo, RET_PREP_ROWS), RET_PREP_ROWS)
            dec = pl.ds(lo, RET_PREP_ROWS)
            cos = cos_ref[rr, :]
            sin = sin_ref[rr, :]
            q = rotary(q_ref[rr, :], cos, sin)
            k = rotary(k_ref[rr, :], cos, sin)
            qr_ref[slot_prep, rr, :] = q.astype(BF16)
            qd_ref[slot_prep, rr, :] = (q * qdec_ref[dec, :]).astype(BF16)
            kr_ref[slot_prep, rr, :] = (k * k_scale).astype(BF16)
            kd_ref[slot_prep, rr, :] = (k * kdec_ref[dec, :]).astype(BF16)
        return carry

    lax.fori_loop(0, n_chunks, body, 0)


def retention(proj, norm_g, layer, batch, seq):
    n = batch * seq
    h = RET_HEADS
    log_g = jnp.log1p(-jnp.exp2(-5.0 - jnp.arange(h, dtype=F32)))
    freq = 1.0 / (ROPE_BASE ** jnp.linspace(0.0, 1.0, RET_DK // 2, dtype=F32))
    ang = jnp.arange(seq).astype(F32)[:, None] * freq[None, :]
    cos = jnp.cos(ang)
    sin = jnp.sin(ang)
    kq = RET_QK // RET_DK
    kv = 2 * RET_QK // RET_DV
    kg = (2 * RET_QK + RET_V) // RET_DV
    n_pairs = batch * h
    stages = 3

    def pair(step, stage):
        p = jnp.clip(step - stage, 0, n_pairs - 1)
        return p // h, p % h

    def block(stage, col0):
        def index(s):
            b, hh = pair(s, stage)
            return b, col0 + hh
        return index

    def gain_index(s):
        return layer, 0, pair(s, 2)[1]

    return pl.pallas_call(
        functools.partial(_retention_kernel, n_chunks=seq // RET_CHUNK, n_pairs=n_pairs),
        grid=(n_pairs + stages - 1,),
        in_specs=[pl.BlockSpec(memory_space=pltpu.SMEM),
                  pl.BlockSpec((seq, RET_DK), block(0, 0)),
                  pl.BlockSpec((seq, RET_DK), block(0, kq)),
                  pl.BlockSpec((seq, RET_DV), block(1, kv)),
                  pl.BlockSpec((seq, RET_DV), block(2, kg)),
                  pl.BlockSpec((seq, RET_DK // 2), lambda s: (0, 0)),
                  pl.BlockSpec((seq, RET_DK // 2), lambda s: (0, 0)),
                  pl.BlockSpec((None, 1, RET_DV), gain_index)],
        out_specs=pl.BlockSpec((seq, RET_DV), block(2, 0)),
        out_shape=jax.ShapeDtypeStruct((n, RET_V), BF16),
        scratch_shapes=[pltpu.VMEM((2, seq, RET_DK), BF16),
                        pltpu.VMEM((2, seq, RET_DK), BF16),
                        pltpu.VMEM((2, seq, RET_DK), BF16),
                        pltpu.VMEM((2, seq, RET_DK), BF16),
                        pltpu.VMEM((RET_CHUNK, RET_CHUNK), F32),
                        pltpu.VMEM((RET_CHUNK, RET_DK), F32),
                        pltpu.VMEM((RET_CHUNK, RET_DK), F32),
                        pltpu.VMEM((2, seq, RET_DV), F32),
                        pltpu.VMEM((RET_DK, RET_DV), F32),
                        pltpu.VMEM((RET_DK, RET_DV), BF16)],
        compiler_params=pltpu.CompilerParams(dimension_semantics=("arbitrary",)),
        name="retention",
    )(log_g, proj, proj, proj, proj, cos, sin, norm_g.reshape(norm_g.shape[0], 1, RET_V))


def _pool_kernel(x_ref, gm_ref, pw_ref, sc_ref, gf_ref, rt_ref,
                 xo_ref, hn_ref, meta_ref, meta_t_ref, cnt_out_ref, hbuf, cnt_ref, *, tiles_per_seq):
    tm = x_ref.shape[0]
    step = pl.program_id(0)
    ti = step % tiles_per_seq
    x = x_ref[...]
    hn = _rms(x, gm_ref[...])

    @pl.when(ti == 0)
    def _():
        hbuf[0:POOL_HALO, :] = jnp.zeros((POOL_HALO, D_MODEL), F32)

    @pl.when(ti != 0)
    def _():
        hbuf[0:POOL_HALO, :] = hbuf[tm:tm + POOL_HALO, :]

    hbuf[POOL_HALO:POOL_HALO + tm, :] = hn

    t = ti * tm + lax.broadcasted_iota(jnp.int32, (tm, 1), 0)
    for gi, w in enumerate(POOL_WINDOWS):
        lo, hi = gi * POOL_G, (gi + 1) * POOL_G
        cur = hn[:, lo:hi]
        acc = hbuf[:, lo:hi]
        shift = 1
        while shift < w:
            acc = acc + pltpu.roll(acc, shift, 0)
            shift *= 2
        acc = acc[POOL_HALO:, :]
        inv = 1.0 / jnp.minimum(t + 1, w).astype(F32)
        mix = (acc * inv - cur).astype(BF16)
        o = _dot(mix, pw_ref[gi]) * sc_ref[:, lo:hi]
        xo_ref[:, lo:hi] = x[:, lo:hi] + o

    hn2 = _rms(xo_ref[...], gf_ref[...])
    hn_ref[...] = hn2

    rt = rt_ref[...]
    hn_hi = hn2.astype(BF16)
    hn_lo = (hn2 - hn_hi.astype(F32)).astype(BF16)
    rt_hi = rt.astype(BF16)
    rt_lo = (rt - rt_hi.astype(F32)).astype(BF16)
    logits = _dot(hn_hi, rt_hi) + (_dot(hn_hi, rt_lo) + _dot(hn_lo, rt_hi))
    lt = logits.T[:N_EXPERTS, :]
    expert = lax.broadcasted_iota(jnp.int32, lt.shape, 0)
    neg = jnp.float32(-jnp.inf)
    m1 = jnp.max(lt, axis=0, keepdims=True)
    i1 = jnp.min(jnp.where(lt == m1, expert, N_EXPERTS), axis=0, keepdims=True)
    lt2 = jnp.where(expert == i1, neg, lt)
    m2 = jnp.max(lt2, axis=0, keepdims=True)
    i2 = jnp.min(jnp.where(lt2 == m2, expert, N_EXPERTS), axis=0, keepdims=True)
    e2 = jnp.exp(m2 - m1)
    g1 = 1.0 / (1.0 + e2)
    g2 = e2 * g1

    @pl.when(step == 0)
    def _():
        cnt_ref[...] = jnp.zeros_like(cnt_ref)

    chosen = jnp.where((expert == i1) | (expert == i2), 1.0, 0.0)
    src = lax.broadcasted_iota(jnp.int32, (tm, tm), 0)
    dst = lax.broadcasted_iota(jnp.int32, (tm, tm), 1)
    earlier = jnp.where(src < dst, 1.0, 0.0).astype(BF16)
    before = _dot(chosen.astype(BF16), earlier) + cnt_ref[:, 0:1]
    r1 = jnp.sum(jnp.where(expert == i1, before, 0.0), axis=0, keepdims=True)
    r2 = jnp.sum(jnp.where(expert == i2, before, 0.0), axis=0, keepdims=True)
    cnt_ref[...] += jnp.sum(chosen, axis=1, keepdims=True)
    cnt_out_ref[...] = cnt_ref[...]

    fields = {META_E1: i1.astype(F32), META_E2: i2.astype(F32), META_R1: r1, META_R2: r2,
              META_G1: g1, META_G2: g2}
    zero_row = jnp.zeros_like(g1)
    meta_t = jnp.concatenate([fields.get(row, zero_row) for row in range(SUBLANES)], axis=0)
    meta_t_ref[...] = meta_t
    meta_ref[...] = jnp.concatenate([meta_t, jnp.zeros((LANES - SUBLANES, tm), F32)], axis=0).T


def pool_router(x, g_mix, pool_w, layer, pool_scale, g_ffn, router, seq, tm=512):
    n, d = x.shape
    tm = min(tm, seq)
    router_p = jnp.zeros((d, LANES), F32).at[:, :N_EXPERTS].set(router)
    ng = len(POOL_WINDOWS)
    return pl.pallas_call(
        functools.partial(_pool_kernel, tiles_per_seq=seq // tm),
        grid=(n // tm,),
        in_specs=[pl.BlockSpec((tm, d), lambda i: (i, 0)),
                  pl.BlockSpec((1, d), lambda i: (0, 0)),
                  pl.BlockSpec((None, ng, POOL_G, POOL_G), lambda i: (layer, 0, 0, 0)),
                  pl.BlockSpec((1, d), lambda i: (0, 0)),
                  pl.BlockSpec((1, d), lambda i: (0, 0)),
                  pl.BlockSpec((d, LANES), lambda i: (0, 0))],
        out_specs=[pl.BlockSpec((tm, d), lambda i: (i, 0)),
                   pl.BlockSpec((tm, d), lambda i: (i, 0)),
                   pl.BlockSpec((tm, LANES), lambda i: (i, 0)),
                   pl.BlockSpec((SUBLANES, tm), lambda i: (0, i)),
                   pl.BlockSpec((SUBLANES, LANES), lambda i: (0, 0))],
        out_shape=[jax.ShapeDtypeStruct((n, d), F32),
                   jax.ShapeDtypeStruct((n, d), F32),
                   jax.ShapeDtypeStruct((n, LANES), F32),
                   jax.ShapeDtypeStruct((SUBLANES, n), F32),
                   jax.ShapeDtypeStruct((SUBLANES, LANES), F32)],
        scratch_shapes=[pltpu.VMEM((POOL_HALO + tm, d), F32),
                        pltpu.VMEM((N_EXPERTS, LANES), F32)],
        compiler_params=pltpu.CompilerParams(dimension_semantics=("arbitrary",)),
        name="pool_router",
    )(x, g_mix.reshape(1, d), pool_w, pool_scale.reshape(1, d), g_ffn.reshape(1, d), router_p)


def moe_num_tiles(n):
    return TOP_K * n // MOE_TILE + N_EXPERTS


def moe_schedule(meta_t, counts_f, n):
    nt = moe_num_tiles(n)
    counts = counts_f[:N_EXPERTS, 0].astype(jnp.int32)
    tiles = (counts + MOE_TILE - 1) // MOE_TILE
    tile_end = jnp.cumsum(tiles)
    starts = (tile_end - tiles) * MOE_TILE
    expert_ids = jnp.arange(N_EXPERTS, dtype=jnp.int32)[:, None]

    def slots(e_row, r_row):
        e = meta_t[e_row].astype(jnp.int32)
        first = jnp.sum(jnp.where(e[None, :] == expert_ids, starts[:, None], 0), axis=0)
        return first + meta_t[r_row].astype(jnp.int32)

    pos = jnp.concatenate([slots(META_E1, META_R1), slots(META_E2, META_R2)])
    total = tile_end[-1]
    t = jnp.arange(nt, dtype=jnp.int32)
    t_used = jnp.minimum(t, total - 1)
    tile_expert = jnp.sum((tile_end[None, :] <= t_used[:, None]).astype(jnp.int32), axis=1)
    tile_expert = jnp.minimum(tile_expert, N_EXPERTS - 1)
    tile_valid = (t < total).astype(jnp.int32)
    pad_lo = starts + counts
    pad_hi = starts + tiles * MOE_TILE
    return pos, tile_expert, tile_valid, pad_lo, pad_hi, total.reshape(1)


def _dispatch_kernel(pos_ref, lo_ref, hi_ref, used_ref, hn_ref, xs_hbm, stage, zeros, sems,
                     *, n, n_tiles):
    tb = hn_ref.shape[0]
    step = pl.program_id(0)
    last = pl.num_programs(0) - 1
    slot = step % 2
    base = step * tb
    sem = sems.at[2]

    stage[slot] = hn_ref[...]

    def issue(r, c):
        tok = base + r
        src = stage.at[slot, pl.ds(r, 1), :]
        pltpu.make_async_copy(src, xs_hbm.at[pl.ds(pos_ref[tok], 1), :], sems.at[slot]).start()
        pltpu.make_async_copy(src, xs_hbm.at[pl.ds(pos_ref[n + tok], 1), :], sems.at[slot]).start()
        return c

    lax.fori_loop(0, tb, issue, 0, unroll=8)

    def wait_rows(s):
        for _ in range(TOP_K):
            pltpu.make_async_copy(stage.at[s], xs_hbm.at[pl.ds(0, tb), :], sems.at[s]).wait()

    @pl.when(step > 0)
    def _():
        wait_rows(1 - slot)

    @pl.when(step == last)
    def _():
        wait_rows(slot)
        zeros[...] = jnp.zeros_like(zeros)

        def pad_copy(slot):
            return pltpu.make_async_copy(zeros.at[pl.ds(0, 1), :], xs_hbm.at[pl.ds(slot, 1), :], sem)

        def issue_pad(s, c):
            pad_copy(s).start()
            return c

        def drain_pad(s, c):
            pad_copy(s).wait()
            return c

        for e in range(N_EXPERTS):
            lax.fori_loop(lo_ref[e], hi_ref[e], issue_pad, 0)
            lax.fori_loop(lo_ref[e], hi_ref[e], drain_pad, 0)

        def tile_copy(t):
            return pltpu.make_async_copy(zeros, xs_hbm.at[pl.ds(t * MOE_TILE, MOE_TILE), :], sem)

        def issue_tile(t, c):
            tile_copy(t).start()
            return c

        def drain_tile(t, c):
            tile_copy(t).wait()
            return c

        lax.fori_loop(used_ref[0], n_tiles, issue_tile, 0)
        lax.fori_loop(used_ref[0], n_tiles, drain_tile, 0)


def moe_dispatch(hn, pos, pad_lo, pad_hi, used, tb=512):
    n, d = hn.shape
    tb = min(tb, n)
    nt = moe_num_tiles(n)
    grid_spec = pltpu.PrefetchScalarGridSpec(
        num_scalar_prefetch=4,
        grid=(n // tb,),
        in_specs=[pl.BlockSpec((tb, d), lambda i, *_: (i, 0))],
        out_specs=pl.BlockSpec(memory_space=pl.ANY),
        scratch_shapes=[pltpu.VMEM((2, tb, d), F32), pltpu.VMEM((MOE_TILE, d), F32),
                        pltpu.SemaphoreType.DMA((3,))],
    )
    return pl.pallas_call(
        functools.partial(_dispatch_kernel, n=n, n_tiles=nt),
        grid_spec=grid_spec,
        out_shape=jax.ShapeDtypeStruct((nt * MOE_TILE, d), F32),
        compiler_params=pltpu.CompilerParams(dimension_semantics=("arbitrary",)),
        name="moe_dispatch",
    )(pos, pad_lo, pad_hi, used, hn)


def _expert_kernel(te_ref, tv_ref, x_ref, wa_ref, wu_ref, wd_ref, o_ref):
    valid = tv_ref[pl.program_id(0)] == 1

    @pl.when(valid)
    def _():
        x = x_ref[...].astype(BF16)
        a = _dot(x, wa_ref[...])
        u = _dot(x, wu_ref[...])
        o_ref[...] = _dot((_silu(a) * u).astype(BF16), wd_ref[...])

    @pl.when(jnp.logical_not(valid))
    def _():
        o_ref[...] = jnp.zeros_like(o_ref)


def moe_experts(xs, w_gu, w_down, tile_expert, tile_valid):
    rows, d = xs.shape
    f = w_down.shape[1]
    nt = rows // MOE_TILE
    grid_spec = pltpu.PrefetchScalarGridSpec(
        num_scalar_prefetch=2,
        grid=(nt,),
        in_specs=[pl.BlockSpec((MOE_TILE, d), lambda t, te, tv: (t, 0)),
                  pl.BlockSpec((None, d, f), lambda t, te, tv: (te[t], 0, 0)),
                  pl.BlockSpec((None, d, f), lambda t, te, tv: (te[t], 0, 1)),
                  pl.BlockSpec((None, f, d), lambda t, te, tv: (te[t], 0, 0))],
        out_specs=pl.BlockSpec((MOE_TILE, d), lambda t, te, tv: (t, 0)),
    )
    return pl.pallas_call(
        _expert_kernel,
        grid_spec=grid_spec,
        out_shape=jax.ShapeDtypeStruct((rows, d), F32),
        compiler_params=pltpu.CompilerParams(dimension_semantics=("arbitrary",),
                                             vmem_limit_bytes=56 * 1024 * 1024),
        name="moe_experts",
    )(tile_expert, tile_valid, xs, w_gu, w_gu, w_down)


def _combine_kernel(pos_ref, x_ref, meta_ref, g_ref, y_hbm, *refs, n, emit_x, emit_norm):
    ybuf, sem = refs[-2], refs[-1]
    tc = x_ref.shape[0]
    step = pl.program_id(0)
    slot = step % 2

    def start_gather(s, into):
        def issue(r, c):
            tok = s * tc + r
            for k in range(TOP_K):
                pltpu.make_async_copy(y_hbm.at[pl.ds(pos_ref[k * n + tok], 1), :],
                                      ybuf.at[into, k, pl.ds(r, 1), :], sem.at[into]).start()
            return c

        lax.fori_loop(0, tc, issue, 0, unroll=8)

    @pl.when(step == 0)
    def _():
        start_gather(0, 0)

    @pl.when(step + 1 < pl.num_programs(0))
    def _():
        start_gather(step + 1, 1 - slot)

    for k in range(TOP_K):
        pltpu.make_async_copy(y_hbm.at[pl.ds(0, tc), :], ybuf.at[slot, k], sem.at[slot]).wait()

    meta = meta_ref[...]
    g1 = meta[:, META_G1:META_G1 + 1]
    g2 = meta[:, META_G2:META_G2 + 1]
    xn = x_ref[...] + g1 * ybuf[slot, 0] + g2 * ybuf[slot, 1]
    _emit_residual_outputs(xn, g_ref, refs, emit_x, emit_norm)


def moe_combine(x, y, meta, pos, g, *, emit_x=True, norm_dtype=None, tc=256):
    n, d = x.shape
    tc = min(tc, n)
    out_shape, out_specs = _residual_out_shapes(n, d, tc, emit_x, norm_dtype, lambda i, p: (i, 0))
    grid_spec = pltpu.PrefetchScalarGridSpec(
        num_scalar_prefetch=1,
        grid=(n // tc,),
        in_specs=[pl.BlockSpec((tc, d), lambda i, p: (i, 0)),
                  pl.BlockSpec((tc, LANES), lambda i, p: (i, 0)),
                  pl.BlockSpec((1, d), lambda i, p: (0, 0)),
                  pl.BlockSpec(memory_space=pl.ANY)],
        out_specs=out_specs,
        scratch_shapes=[pltpu.VMEM((2, TOP_K, tc, d), F32), pltpu.SemaphoreType.DMA((2,))],
    )
    return pl.pallas_call(
        functools.partial(_combine_kernel, n=n, emit_x=emit_x, emit_norm=norm_dtype is not None),
        grid_spec=grid_spec,
        out_shape=out_shape,
        compiler_params=pltpu.CompilerParams(dimension_semantics=("arbitrary",)),
        name="moe_combine",
    )(pos, x, meta, g.reshape(1, d), y)


def kernel(x, ret_w_in, ret_norm_g, ret_w_out, pool_w, pool_scale, ffn_w_gu, ffn_w_down,
           moe_router, moe_w_gu, moe_w_down, norm_mix_g, norm_ffn_g, norm_final_g):
    batch, seq, d = x.shape
    n = batch * seq
    xf = x.reshape(n, d)
    pool_w = pool_w.astype(BF16)
    n_moe, n_exp, _, f2 = moe_w_gu.shape
    f = moe_w_down.shape[2]
    moe_gu_rows = moe_w_gu.reshape(n_moe, n_exp * d, f2)
    moe_down_rows = moe_w_down.reshape(n_moe, n_exp * f, d)
    hn = rmsnorm(xf, norm_mix_g[0], BF16)
    for i in range(DEPTH):
        j = i // 2
        last = i == DEPTH - 1
        if i % 2 == 0:
            proj, moe_gu_b, w_out_b = retention_in_proj(hn, ret_w_in, j,
                                                        [(moe_gu_rows, j), (ret_w_out, j)])
            y = retention(proj, ret_norm_g, j, batch, seq)
            xf, xg, ssq = down_residual(y, w_out_b, xf, norm_ffn_g[i])
            hid, moe_down_b, ffn_down_b = swiglu_up(xg, ssq, ffn_w_gu, j,
                                                    [(moe_down_rows, j), (ffn_w_down, j)])
            (xf,) = down_residual(hid, ffn_down_b, xf)
            moe_gu_b = moe_gu_b.reshape(n_exp, d, f2)
            moe_down_b = moe_down_b.reshape(n_exp, f, d)
        else:
            xf, hn, meta, meta_t, counts = pool_router(xf, norm_mix_g[i], pool_w, j, pool_scale[j],
                                               norm_ffn_g[i], moe_router[j], seq)
            pos, tile_expert, tile_valid, pad_lo, pad_hi, used = moe_schedule(meta_t, counts, n)
            xs = moe_dispatch(hn, pos, pad_lo, pad_hi, used)
            ys = moe_experts(xs, moe_gu_b, moe_down_b, tile_expert, tile_valid)
            if last:
                (out,) = moe_combine(xf, ys, meta, pos, norm_final_g, emit_x=False, norm_dtype=F32)
            else:
                xf, hn = moe_combine(xf, ys, meta, pos, norm_mix_g[i + 1], norm_dtype=BF16)
    return out.reshape(batch, seq, d)
```

```python
import functools

import jax
import jax.numpy as jnp
from jax import lax
from jax.experimental import pallas as pl
from jax.experimental.pallas import tpu as pltpu

BF16 = jnp.bfloat16
F32 = jnp.float32

D_MODEL = 2048
DEPTH = 4
RET_HEADS = 8
RET_DK = D_MODEL // RET_HEADS
RET_DV = 2 * D_MODEL // RET_HEADS
RET_QK = RET_HEADS * RET_DK
RET_V = RET_HEADS * RET_DV
RET_CHUNK = 256
RET_PREP_ROWS = 32
ROPE_BASE = 10000.0
POOL_WINDOWS = (2, 4, 8, 16)
POOL_G = D_MODEL // len(POOL_WINDOWS)
POOL_HALO = 16
D_FF = 5632
N_EXPERTS = 8
TOP_K = 2
D_FF_EXPERT = 1408
NORM_EPS = 1e-6
LANES = 128
SUBLANES = 8
MOE_TILE = 256
SIDE_CAST_BLOCKS = 64
X_RING = 3
BF16_TILE_ROWS = 16
DOWN_VMEM_LIMIT = 60 * 1024 * 1024
META_E1, META_E2, META_R1, META_R2, META_G1, META_G2 = range(6)


def _rms(x, g):
    ms = jnp.mean(x * x, axis=-1, keepdims=True)
    return x * lax.rsqrt(ms + NORM_EPS) * g


def _silu(a):
    return a / (1.0 + jnp.exp(-a))


def _dot(a, b):
    return jnp.dot(a, b, preferred_element_type=F32)


def _rmsnorm_kernel(x_ref, g_ref, o_ref):
    o_ref[...] = _rms(x_ref[...], g_ref[...]).astype(o_ref.dtype)


def rmsnorm(x, g, out_dtype, tm=512):
    n, d = x.shape
    return pl.pallas_call(
        _rmsnorm_kernel,
        grid=(n // tm,),
        in_specs=[pl.BlockSpec((tm, d), lambda i: (i, 0)),
                  pl.BlockSpec((1, d), lambda i: (0, 0))],
        out_specs=pl.BlockSpec((tm, d), lambda i: (i, 0)),
        out_shape=jax.ShapeDtypeStruct((n, d), out_dtype),
        name="rmsnorm",
    )(x, g.reshape(1, d))


def _side_cast_specs(sides, steps, n_inner):
    in_specs, out_specs, shapes = [], [], []
    for side, side_layer in sides:
        _, rows, cols = side.shape
        n_blocks = 1
        while (n_blocks * 2 <= min(steps, SIDE_CAST_BLOCKS)
               and rows % (n_blocks * 2 * BF16_TILE_ROWS) == 0):
            n_blocks *= 2
        br = rows // n_blocks

        def index(j, i, n_blocks=n_blocks):
            return jnp.minimum(j * n_inner + i, n_blocks - 1)

        in_specs.append(pl.BlockSpec((None, br, cols),
                                     lambda j, i, l=side_layer, ix=index: (l, ix(j, i), 0)))
        out_specs.append(pl.BlockSpec((br, cols), lambda j, i, ix=index: (ix(j, i), 0)))
        shapes.append(jax.ShapeDtypeStruct((rows, cols), BF16))
    return in_specs, out_specs, shapes


def _cast_sides(refs, n_sides):
    for s_ref, so_ref in zip(refs[:n_sides], refs[n_sides + 1:2 * n_sides + 1]):
        so_ref[...] = s_ref[...].astype(BF16)


def _in_proj_kernel(x_hbm, w_ref, perm_ref, *refs, n_qk_slabs, n_sides):
    o_ref = refs[n_sides]
    wb_ref, xbuf, xsem = refs[-3:]
    tm = xbuf.shape[1]
    ni = pl.num_programs(1)
    total = pl.num_programs(0) * ni
    step = pl.program_id(0) * ni + pl.program_id(1)

    def x_copy(s):
        rows = pl.ds(pl.multiple_of((s % ni) * tm, tm), tm)
        return pltpu.make_async_copy(x_hbm.at[rows, :], xbuf.at[s % X_RING], xsem.at[s % X_RING])

    @pl.when(step == 0)
    def _():
        for s in range(X_RING - 1):
            x_copy(s).start()

    @pl.when(step + X_RING - 1 < total)
    def _():
        x_copy(step + X_RING - 1).start()

    @pl.when(pl.program_id(1) == 0)
    def _():
        is_qk = pl.program_id(0) < n_qk_slabs

        @pl.when(is_qk)
        def _():
            for lo in range(0, wb_ref.shape[1], RET_DK):
                head = w_ref[:, lo:lo + RET_DK].astype(BF16)
                wb_ref[:, lo:lo + RET_DK] = _dot(head, perm_ref[...]).astype(BF16)

        @pl.when(jnp.logical_not(is_qk))
        def _():
            wb_ref[...] = w_ref[...].astype(BF16)

    x_copy(step).wait()
    o_ref[...] = _dot(xbuf[step % X_RING], wb_ref[...]).astype(o_ref.dtype)
    _cast_sides(refs, n_sides)


def retention_in_proj(x, w, layer, sides, tm=1024, tn=1024):
    n, k = x.shape
    f = w.shape[2]
    tm = min(tm, n)
    nj, ni = f // tn, n // tm
    half = RET_DK // 2
    src = jnp.arange(RET_DK)
    dst = jnp.where(src % 2 == 0, src // 2, half + src // 2)
    perm = (dst[:, None] == jnp.arange(RET_DK)[None, :]).astype(BF16)
    s_in, s_out, s_shapes = _side_cast_specs(sides, nj * ni, ni)
    return pl.pallas_call(
        functools.partial(_in_proj_kernel, n_qk_slabs=2 * RET_QK // tn, n_sides=len(sides)),
        grid=(nj, ni),
        in_specs=[pl.BlockSpec(memory_space=pl.ANY),
                  pl.BlockSpec((None, k, tn), lambda j, i: (layer, 0, j)),
                  pl.BlockSpec((RET_DK, RET_DK), lambda j, i: (0, 0)),
                  *s_in],
        out_specs=[pl.BlockSpec((tm, tn), lambda j, i: (i, j)), *s_out],
        out_shape=[jax.ShapeDtypeStruct((n, f), BF16), *s_shapes],
        scratch_shapes=[pltpu.VMEM((k, tn), BF16), pltpu.VMEM((X_RING, tm, k), BF16),
                        pltpu.SemaphoreType.DMA((X_RING,))],
        compiler_params=pltpu.CompilerParams(dimension_semantics=("arbitrary", "arbitrary")),
        name="retention_in_proj",
    )(x, w, perm, *(side for side, _ in sides))


def _swiglu_kernel(x_ref, ssq_ref, wa_ref, wu_ref, *refs, n_sides):
    o_ref, wab_ref, wub_ref = refs[n_sides], refs[-2], refs[-1]

    @pl.when(pl.program_id(1) == 0)
    def _():
        wab_ref[...] = wa_ref[...].astype(BF16)
        wub_ref[...] = wu_ref[...].astype(BF16)

    x = x_ref[...]
    rowscale = lax.rsqrt(jnp.sum(ssq_ref[...], axis=0) / x.shape[1] + NORM_EPS)
    a = _dot(x, wab_ref[...]) * rowscale
    u = _dot(x, wub_ref[...]) * rowscale
    o_ref[...] = (_silu(a) * u).astype(o_ref.dtype)
    _cast_sides(refs, n_sides)


def swiglu_up(xg, ssq, w_gu, layer, sides, tm=1024, tn=512):
    n, k = xg.shape
    f = w_gu.shape[2] // 2
    tm = min(tm, n)
    nj, ni = f // tn, n // tm
    ns = ssq.shape[0]
    s_in, s_out, s_shapes = _side_cast_specs(sides, nj * ni, ni)
    return pl.pallas_call(
        functools.partial(_swiglu_kernel, n_sides=len(sides)),
        grid=(nj, ni),
        in_specs=[pl.BlockSpec((tm, k), lambda j, i: (i, 0)),
                  pl.BlockSpec((ns, tm, 1), lambda j, i: (0, i, 0)),
                  pl.BlockSpec((None, k, tn), lambda j, i: (layer, 0, j)),
                  pl.BlockSpec((None, k, tn), lambda j, i: (layer, 0, nj + j)),
                  *s_in],
        out_specs=[pl.BlockSpec((tm, tn), lambda j, i: (i, j)), *s_out],
        out_shape=[jax.ShapeDtypeStruct((n, f), BF16), *s_shapes],
        scratch_shapes=[pltpu.VMEM((k, tn), BF16), pltpu.VMEM((k, tn), BF16)],
        compiler_params=pltpu.CompilerParams(dimension_semantics=("arbitrary", "arbitrary")),
        name="swiglu_up",
    )(xg, ssq, w_gu, w_gu, *(side for side, _ in sides))


def _emit_residual_outputs(xn, g_ref, refs, emit_x, emit_norm):
    pos = 0
    if emit_x:
        refs[pos][...] = xn
        pos += 1
    if emit_norm:
        refs[pos][...] = _rms(xn, g_ref[...]).astype(refs[pos].dtype)


def _residual_out_shapes(n, d, tm, emit_x, norm_dtype, index_map):
    out_shape, out_specs = [], []
    if emit_x:
        out_shape.append(jax.ShapeDtypeStruct((n, d), F32))
        out_specs.append(pl.BlockSpec((tm, d), index_map))
    if norm_dtype is not None:
        out_shape.append(jax.ShapeDtypeStruct((n, d), norm_dtype))
        out_specs.append(pl.BlockSpec((tm, d), index_map))
    return out_shape, out_specs


def _down_kernel(x_ref, w_ref, r_ref, g_ref, *refs, emit_scaled):
    xn = r_ref[...] + _dot(x_ref[...], w_ref[...])
    refs[0][...] = xn
    if emit_scaled:
        refs[1][...] = (xn * g_ref[...]).astype(BF16)
        refs[2][...] = jnp.sum(xn * xn, axis=-1, keepdims=True)[None]


def down_residual(h, w, resid, g=None, *, tm=512):
    n, kk = h.shape
    d = w.shape[1]
    tm = min(tm, n)
    emit_scaled = g is not None
    if g is None:
        g = jnp.ones((d,), F32)
    out_shape = [jax.ShapeDtypeStruct((n, d), F32)]
    out_specs = [pl.BlockSpec((tm, d), lambda i: (i, 0))]
    if emit_scaled:
        out_shape += [jax.ShapeDtypeStruct((n, d), BF16), jax.ShapeDtypeStruct((1, n, 1), F32)]
        out_specs += [pl.BlockSpec((tm, d), lambda i: (i, 0)),
                      pl.BlockSpec((1, tm, 1), lambda i: (0, i, 0))]
    return pl.pallas_call(
        functools.partial(_down_kernel, emit_scaled=emit_scaled),
        grid=(n // tm,),
        in_specs=[pl.BlockSpec((tm, kk), lambda i: (i, 0)),
                  pl.BlockSpec((kk, d), lambda i: (0, 0), pipeline_mode=pl.Buffered(1)),
                  pl.BlockSpec((tm, d), lambda i: (i, 0)),
                  pl.BlockSpec((1, d), lambda i: (0, 0))],
        out_specs=out_specs,
        out_shape=out_shape,
        compiler_params=pltpu.CompilerParams(dimension_semantics=("arbitrary",),
                                             vmem_limit_bytes=DOWN_VMEM_LIMIT),
        name="down_residual",
    )(h, w, resid, g.reshape(1, d))


def _retention_kernel(lg_ref, q_ref, k_ref, v_ref, g_ref, cos_ref, sin_ref, gain_ref, o_ref,
                      qr_ref, qd_ref, kr_ref, kd_ref, mask_ref, qdec_ref, kdec_ref, oraw_ref,
                      state_ref, sb_ref, *, n_chunks, n_pairs):
    c = RET_CHUNK
    step = pl.program_id(0)
    slot_prep = step % 2
    slot_recur = 1 - slot_prep
    slot_finish = slot_prep

    @pl.when(step == 0)
    def _():
        for ref in (qr_ref, qd_ref, kr_ref, kd_ref, oraw_ref):
            ref[...] = jnp.zeros_like(ref)

    lg_prep = lg_ref[jnp.minimum(step, n_pairs - 1) % RET_HEADS]
    lg = lg_ref[jnp.clip(step - 1, 0, n_pairs - 1) % RET_HEADS]
    ii = lax.broadcasted_iota(jnp.int32, (c, c), 0)
    jj = lax.broadcasted_iota(jnp.int32, (c, c), 1)
    rel = (ii - jj).astype(F32)
    mask_ref[...] = jnp.where(rel >= 0.0, jnp.exp(jnp.maximum(rel, 0.0) * lg), 0.0)
    idx = lax.broadcasted_iota(jnp.int32, (c, 1), 0).astype(F32)
    k_scale = RET_DK ** -0.5
    qdec_ref[...] = jnp.broadcast_to(jnp.exp((idx + 1.0) * lg_prep), (c, RET_DK))
    kdec_ref[...] = jnp.broadcast_to(jnp.exp((c - 1.0 - idx) * lg_prep) * k_scale, (c, RET_DK))
    chunk_decay = jnp.exp(jnp.full((1, 1), float(c), F32) * lg)
    half = RET_DK // 2

    def rotary(xb, cos, sin):
        x = xb.astype(F32)
        x1, x2 = x[:, :half], x[:, half:]
        return jnp.concatenate([x1 * cos - x2 * sin, x2 * cos + x1 * sin], axis=1)

    def rows(ci):
        return pl.ds(pl.multiple_of(ci * c, c), c)

    state_ref[...] = jnp.zeros_like(state_ref)
    sb_ref[...] = jnp.zeros_like(sb_ref)

    def body(ci, carry):
        r = rows(ci)

        o = oraw_ref[slot_finish, r, :]
        ms = jnp.mean(o * o, axis=-1, keepdims=True)
        o = o * lax.rsqrt(ms + NORM_EPS) * gain_ref[...]
        gate = g_ref[r, :].astype(F32)
        o_ref[r, :] = (o * _silu(gate)).astype(o_ref.dtype)

        v = v_ref[r, :]
        scores = lax.dot_general(qr_ref[slot_recur, r, :], kr_ref[slot_recur, r, :],
                                 (((1,), (1,)), ((), ())),
                                 preferred_element_type=F32) * mask_ref[...]
        oraw_ref[slot_recur, r, :] = (_dot(scores.astype(BF16), v)
                                      + _dot(qd_ref[slot_recur, r, :], sb_ref[...]))
        upd = lax.dot_general(kd_ref[slot_recur, r, :], v, (((0,), (0,)), ((), ())),
                              preferred_element_type=F32)
        state = state_ref[...] * chunk_decay + upd
        state_ref[...] = state
        sb_ref[...] = state.astype(BF16)

        for lo in range(0, c, RET_PREP_ROWS):
            rr = pl.ds(pl.multiple_of(ci * c + lo, RET_PREP_ROWS), RET_PREP_ROWS)
            dec = pl.ds(lo, RET_PREP_ROWS)
            cos = cos_ref[rr, :]
            sin = sin_ref[rr, :]
            q = rotary(q_ref[rr, :], cos, sin)
            k = rotary(k_ref[rr, :], cos, sin)
            qr_ref[slot_prep, rr, :] = q.astype(BF16)
            qd_ref[slot_prep, rr, :] = (q * qdec_ref[dec, :]).astype(BF16)
            kr_ref[slot_prep, rr, :] = (k * k_scale).astype(BF16)
            kd_ref[slot_prep, rr, :] = (k * kdec_ref[dec, :]).astype(BF16)
        return carry

    lax.fori_loop(0, n_chunks, body, 0, unroll=2)


def retention(proj, norm_g, layer, batch, seq):
    n = batch * seq
    h = RET_HEADS
    log_g = jnp.log1p(-jnp.exp2(-5.0 - jnp.arange(h, dtype=F32)))
    freq = 1.0 / (ROPE_BASE ** jnp.linspace(0.0, 1.0, RET_DK // 2, dtype=F32))
    ang = jnp.arange(seq).astype(F32)[:, None] * freq[None, :]
    cos = jnp.cos(ang)
    sin = jnp.sin(ang)
    kq = RET_QK // RET_DK
    kv = 2 * RET_QK // RET_DV
    kg = (2 * RET_QK + RET_V) // RET_DV
    n_pairs = batch * h
    stages = 3

    def pair(step, stage):
        p = jnp.clip(step - stage, 0, n_pairs - 1)
        return p // h, p % h

    def block(stage, col0):
        def index(s):
            b, hh = pair(s, stage)
            return b, col0 + hh
        return index

    def gain_index(s):
        return layer, 0, pair(s, 2)[1]

    return pl.pallas_call(
        functools.partial(_retention_kernel, n_chunks=seq // RET_CHUNK, n_pairs=n_pairs),
        grid=(n_pairs + stages - 1,),
        in_specs=[pl.BlockSpec(memory_space=pltpu.SMEM),
                  pl.BlockSpec((seq, RET_DK), block(0, 0)),
                  pl.BlockSpec((seq, RET_DK), block(0, kq)),
                  pl.BlockSpec((seq, RET_DV), block(1, kv)),
                  pl.BlockSpec((seq, RET_DV), block(2, kg)),
                  pl.BlockSpec((seq, RET_DK // 2), lambda s: (0, 0)),
                  pl.BlockSpec((seq, RET_DK // 2), lambda s: (0, 0)),
                  pl.BlockSpec((None, 1, RET_DV), gain_index)],
        out_specs=pl.BlockSpec((seq, RET_DV), block(2, 0)),
        out_shape=jax.ShapeDtypeStruct((n, RET_V), BF16),
        scratch_shapes=[pltpu.VMEM((2, seq, RET_DK), BF16),
                        pltpu.VMEM((2, seq, RET_DK), BF16),
                        pltpu.VMEM((2, seq, RET_DK), BF16),
                        pltpu.VMEM((2, seq, RET_DK), BF16),
                        pltpu.VMEM((RET_CHUNK, RET_CHUNK), F32),
                        pltpu.VMEM((RET_CHUNK, RET_DK), F32),
                        pltpu.VMEM((RET_CHUNK, RET_DK), F32),
                        pltpu.VMEM((2, seq, RET_DV), F32),
                        pltpu.VMEM((RET_DK, RET_DV), F32),
                        pltpu.VMEM((RET_DK, RET_DV), BF16)],
        compiler_params=pltpu.CompilerParams(dimension_semantics=("arbitrary",)),
        name="retention",
    )(log_g, proj, proj, proj, proj, cos, sin, norm_g.reshape(norm_g.shape[0], 1, RET_V))


def _pool_kernel(x_ref, gm_ref, pw_ref, sc_ref, gf_ref, rt_ref,
                 xo_ref, hn_ref, meta_ref, meta_t_ref, cnt_out_ref, hbuf, cnt_ref, *, tiles_per_seq):
    tm = x_ref.shape[0]
    step = pl.program_id(0)
    ti = step % tiles_per_seq
    x = x_ref[...]
    hn = _rms(x, gm_ref[...])

    @pl.when(ti == 0)
    def _():
        hbuf[0:POOL_HALO, :] = jnp.zeros((POOL_HALO, D_MODEL), F32)

    @pl.when(ti != 0)
    def _():
        hbuf[0:POOL_HALO, :] = hbuf[tm:tm + POOL_HALO, :]

    hbuf[POOL_HALO:POOL_HALO + tm, :] = hn

    t = ti * tm + lax.broadcasted_iota(jnp.int32, (tm, 1), 0)
    for gi, w in enumerate(POOL_WINDOWS):
        lo, hi = gi * POOL_G, (gi + 1) * POOL_G
        cur = hn[:, lo:hi]
        acc = hbuf[:, lo:hi]
        shift = 1
        while shift < w:
            acc = acc + pltpu.roll(acc, shift, 0)
            shift *= 2
        acc = acc[POOL_HALO:, :]
        inv = 1.0 / jnp.minimum(t + 1, w).astype(F32)
        mix = (acc * inv - cur).astype(BF16)
        o = _dot(mix, pw_ref[gi]) * sc_ref[:, lo:hi]
        xo_ref[:, lo:hi] = x[:, lo:hi] + o

    hn2 = _rms(xo_ref[...], gf_ref[...])
    hn_ref[...] = hn2

    rt = rt_ref[...]
    hn_hi = hn2.astype(BF16)
    hn_lo = (hn2 - hn_hi.astype(F32)).astype(BF16)
    rt_hi = rt.astype(BF16)
    rt_lo = (rt - rt_hi.astype(F32)).astype(BF16)
    logits = _dot(hn_hi, rt_hi) + (_dot(hn_hi, rt_lo) + _dot(hn_lo, rt_hi))
    lt = logits.T[:N_EXPERTS, :]
    expert = lax.broadcasted_iota(jnp.int32, lt.shape, 0)
    neg = jnp.float32(-jnp.inf)
    m1 = jnp.max(lt, axis=0, keepdims=True)
    i1 = jnp.min(jnp.where(lt == m1, expert, N_EXPERTS), axis=0, keepdims=True)
    lt2 = jnp.where(expert == i1, neg, lt)
    m2 = jnp.max(lt2, axis=0, keepdims=True)
    i2 = jnp.min(jnp.where(lt2 == m2, expert, N_EXPERTS), axis=0, keepdims=True)
    e2 = jnp.exp(m2 - m1)
    g1 = 1.0 / (1.0 + e2)
    g2 = e2 * g1

    @pl.when(step == 0)
    def _():
        cnt_ref[...] = jnp.zeros_like(cnt_ref)

    chosen = jnp.where((expert == i1) | (expert == i2), 1.0, 0.0)
    src = lax.broadcasted_iota(jnp.int32, (tm, tm), 0)
    dst = lax.broadcasted_iota(jnp.int32, (tm, tm), 1)
    earlier = jnp.where(src < dst, 1.0, 0.0).astype(BF16)
    before = _dot(chosen.astype(BF16), earlier) + cnt_ref[:, 0:1]
    r1 = jnp.sum(jnp.where(expert == i1, before, 0.0), axis=0, keepdims=True)
    r2 = jnp.sum(jnp.where(expert == i2, before, 0.0), axis=0, keepdims=True)
    cnt_ref[...] += jnp.sum(chosen, axis=1, keepdims=True)
    cnt_out_ref[...] = cnt_ref[...]

    fields = {META_E1: i1.astype(F32), META_E2: i2.astype(F32), META_R1: r1, META_R2: r2,
              META_G1: g1, META_G2: g2}
    zero_row = jnp.zeros_like(g1)
    meta_t = jnp.concatenate([fields.get(row, zero_row) for row in range(SUBLANES)], axis=0)
    meta_t_ref[...] = meta_t
    meta_ref[...] = jnp.concatenate([meta_t, jnp.zeros((LANES - SUBLANES, tm), F32)], axis=0).T


def pool_router(x, g_mix, pool_w, layer, pool_scale, g_ffn, router, seq, tm=512):
    n, d = x.shape
    tm = min(tm, seq)
    router_p = jnp.zeros((d, LANES), F32).at[:, :N_EXPERTS].set(router)
    ng = len(POOL_WINDOWS)
    return pl.pallas_call(
        functools.partial(_pool_kernel, tiles_per_seq=seq // tm),
        grid=(n // tm,),
        in_specs=[pl.BlockSpec((tm, d), lambda i: (i, 0)),
                  pl.BlockSpec((1, d), lambda i: (0, 0)),
                  pl.BlockSpec((None, ng, POOL_G, POOL_G), lambda i: (layer, 0, 0, 0)),
                  pl.BlockSpec((1, d), lambda i: (0, 0)),
                  pl.BlockSpec((1, d), lambda i: (0, 0)),
                  pl.BlockSpec((d, LANES), lambda i: (0, 0))],
        out_specs=[pl.BlockSpec((tm, d), lambda i: (i, 0)),
                   pl.BlockSpec((tm, d), lambda i: (i, 0)),
                   pl.BlockSpec((tm, LANES), lambda i: (i, 0)),
                   pl.BlockSpec((SUBLANES, tm), lambda i: (0, i)),
                   pl.BlockSpec((SUBLANES, LANES), lambda i: (0, 0))],
        out_shape=[jax.ShapeDtypeStruct((n, d), F32),
                   jax.ShapeDtypeStruct((n, d), F32),
                   jax.ShapeDtypeStruct((n, LANES), F32),
                   jax.ShapeDtypeStruct((SUBLANES, n), F32),
                   jax.ShapeDtypeStruct((SUBLANES, LANES), F32)],
        scratch_shapes=[pltpu.VMEM((POOL_HALO + tm, d), F32),
                        pltpu.VMEM((N_EXPERTS, LANES), F32)],
        compiler_params=pltpu.CompilerParams(dimension_semantics=("arbitrary",)),
        name="pool_router",
    )(x, g_mix.reshape(1, d), pool_w, pool_scale.reshape(1, d), g_ffn.reshape(1, d), router_p)


def moe_num_tiles(n):
    return TOP_K * n // MOE_TILE + N_EXPERTS


def moe_schedule(meta_t, counts_f, n):
    nt = moe_num_tiles(n)
    counts = counts_f[:N_EXPERTS, 0].astype(jnp.int32)
    tiles = (counts + MOE_TILE - 1) // MOE_TILE
    tile_end = jnp.cumsum(tiles)
    starts = (tile_end - tiles) * MOE_TILE
    expert_ids = jnp.arange(N_EXPERTS, dtype=jnp.int32)[:, None]

    def slots(e_row, r_row):
        e = meta_t[e_row].astype(jnp.int32)
        first = jnp.sum(jnp.where(e[None, :] == expert_ids, starts[:, None], 0), axis=0)
        return first + meta_t[r_row].astype(jnp.int32)

    pos = jnp.concatenate([slots(META_E1, META_R1), slots(META_E2, META_R2)])
    total = tile_end[-1]
    t = jnp.arange(nt, dtype=jnp.int32)
    t_used = jnp.minimum(t, total - 1)
    tile_expert = jnp.sum((tile_end[None, :] <= t_used[:, None]).astype(jnp.int32), axis=1)
    tile_expert = jnp.minimum(tile_expert, N_EXPERTS - 1)
    tile_valid = (t < total).astype(jnp.int32)
    pad_lo = starts + counts
    pad_hi = starts + tiles * MOE_TILE
    return pos, tile_expert, tile_valid, pad_lo, pad_hi, total.reshape(1)


def _dispatch_kernel(pos_ref, lo_ref, hi_ref, used_ref, hn_ref, xs_hbm, zeros, sem, *, n, n_tiles):
    tb = hn_ref.shape[0]
    step = pl.program_id(0)
    base = step * tb

    def issue(r, c):
        tok = base + r
        src = hn_ref.at[pl.ds(r, 1), :]
        pltpu.make_async_copy(src, xs_hbm.at[pl.ds(pos_ref[tok], 1), :], sem).start()
        pltpu.make_async_copy(src, xs_hbm.at[pl.ds(pos_ref[n + tok], 1), :], sem).start()
        return c

    lax.fori_loop(0, tb, issue, 0, unroll=8)
    for _ in range(TOP_K):
        pltpu.make_async_copy(hn_ref, xs_hbm.at[pl.ds(0, tb), :], sem).wait()

    @pl.when(step == pl.num_programs(0) - 1)
    def _():
        zeros[...] = jnp.zeros_like(zeros)

        def pad_copy(slot):
            return pltpu.make_async_copy(zeros.at[pl.ds(0, 1), :], xs_hbm.at[pl.ds(slot, 1), :], sem)

        def issue_pad(s, c):
            pad_copy(s).start()
            return c

        def drain_pad(s, c):
            pad_copy(s).wait()
            return c

        for e in range(N_EXPERTS):
            lax.fori_loop(lo_ref[e], hi_ref[e], issue_pad, 0)
            lax.fori_loop(lo_ref[e], hi_ref[e], drain_pad, 0)

        def tile_copy(t):
            return pltpu.make_async_copy(zeros, xs_hbm.at[pl.ds(t * MOE_TILE, MOE_TILE), :], sem)

        def issue_tile(t, c):
            tile_copy(t).start()
            return c

        def drain_tile(t, c):
            tile_copy(t).wait()
            return c

        lax.fori_loop(used_ref[0], n_tiles, issue_tile, 0)
        lax.fori_loop(used_ref[0], n_tiles, drain_tile, 0)


def moe_dispatch(hn, pos, pad_lo, pad_hi, used, tb=512):
    n, d = hn.shape
    tb = min(tb, n)
    nt = moe_num_tiles(n)
    grid_spec = pltpu.PrefetchScalarGridSpec(
        num_scalar_prefetch=4,
        grid=(n // tb,),
        in_specs=[pl.BlockSpec((tb, d), lambda i, *_: (i, 0))],
        out_specs=pl.BlockSpec(memory_space=pl.ANY),
        scratch_shapes=[pltpu.VMEM((MOE_TILE, d), F32), pltpu.SemaphoreType.DMA],
    )
    return pl.pallas_call(
        functools.partial(_dispatch_kernel, n=n, n_tiles=nt),
        grid_spec=grid_spec,
        out_shape=jax.ShapeDtypeStruct((nt * MOE_TILE, d), F32),
        compiler_params=pltpu.CompilerParams(dimension_semantics=("arbitrary",)),
        name="moe_dispatch",
    )(pos, pad_lo, pad_hi, used, hn)


def _expert_kernel(te_ref, tv_ref, x_ref, wa_ref, wu_ref, wd_ref, o_ref):
    valid = tv_ref[pl.program_id(0)] == 1

    @pl.when(valid)
    def _():
        x = x_ref[...].astype(BF16)
        a = _dot(x, wa_ref[...])
        u = _dot(x, wu_ref[...])
        o_ref[...] = _dot((_silu(a) * u).astype(BF16), wd_ref[...])

    @pl.when(jnp.logical_not(valid))
    def _():
        o_ref[...] = jnp.zeros_like(o_ref)


def moe_experts(xs, w_gu, w_down, tile_expert, tile_valid):
    rows, d = xs.shape
    f = w_down.shape[1]
    nt = rows // MOE_TILE
    grid_spec = pltpu.PrefetchScalarGridSpec(
        num_scalar_prefetch=2,
        grid=(nt,),
        in_specs=[pl.BlockSpec((MOE_TILE, d), lambda t, te, tv: (t, 0)),
                  pl.BlockSpec((None, d, f), lambda t, te, tv: (te[t], 0, 0)),
                  pl.BlockSpec((None, d, f), lambda t, te, tv: (te[t], 0, 1)),
                  pl.BlockSpec((None, f, d), lambda t, te, tv: (te[t], 0, 0))],
        out_specs=pl.BlockSpec((MOE_TILE, d), lambda t, te, tv: (t, 0)),
    )
    return pl.pallas_call(
        _expert_kernel,
        grid_spec=grid_spec,
        out_shape=jax.ShapeDtypeStruct((rows, d), F32),
        compiler_params=pltpu.CompilerParams(dimension_semantics=("arbitrary",),
                                             vmem_limit_bytes=56 * 1024 * 1024),
        name="moe_experts",
    )(tile_expert, tile_valid, xs, w_gu, w_gu, w_down)


def _combine_kernel(pos_ref, x_ref, meta_ref, g_ref, y_hbm, *refs, n, emit_x, emit_norm):
    ybuf, sem = refs[-2], refs[-1]
    tc = x_ref.shape[0]
    step = pl.program_id(0)
    slot = step % 2

    def start_gather(s, into):
        def issue(r, c):
            tok = s * tc + r
            for k in range(TOP_K):
                pltpu.make_async_copy(y_hbm.at[pl.ds(pos_ref[k * n + tok], 1), :],
                                      ybuf.at[into, k, pl.ds(r, 1), :], sem.at[into]).start()
            return c

        lax.fori_loop(0, tc, issue, 0, unroll=8)

    @pl.when(step == 0)
    def _():
        start_gather(0, 0)

    @pl.when(step + 1 < pl.num_programs(0))
    def _():
        start_gather(step + 1, 1 - slot)

    for k in range(TOP_K):
        pltpu.make_async_copy(y_hbm.at[pl.ds(0, tc), :], ybuf.at[slot, k], sem.at[slot]).wait()

    meta = meta_ref[...]
    g1 = meta[:, META_G1:META_G1 + 1]
    g2 = meta[:, META_G2:META_G2 + 1]
    xn = x_ref[...] + g1 * ybuf[slot, 0] + g2 * ybuf[slot, 1]
    _emit_residual_outputs(xn, g_ref, refs, emit_x, emit_norm)


def moe_combine(x, y, meta, pos, g, *, emit_x=True, norm_dtype=None, tc=256):
    n, d = x.shape
    tc = min(tc, n)
    out_shape, out_specs = _residual_out_shapes(n, d, tc, emit_x, norm_dtype, lambda i, p: (i, 0))
    grid_spec = pltpu.PrefetchScalarGridSpec(
        num_scalar_prefetch=1,
        grid=(n // tc,),
        in_specs=[pl.BlockSpec((tc, d), lambda i, p: (i, 0)),
                  pl.BlockSpec((tc, LANES), lambda i, p: (i, 0)),
                  pl.BlockSpec((1, d), lambda i, p: (0, 0)),
                  pl.BlockSpec(memory_space=pl.ANY)],
        out_specs=out_specs,
        scratch_shapes=[pltpu.VMEM((2, TOP_K, tc, d), F32), pltpu.SemaphoreType.DMA((2,))],
    )
    return pl.pallas_call(
        functools.partial(_combine_kernel, n=n, emit_x=emit_x, emit_norm=norm_dtype is not None),
        grid_spec=grid_spec,
        out_shape=out_shape,
        compiler_params=pltpu.CompilerParams(dimension_semantics=("arbitrary",)),
        name="moe_combine",
    )(pos, x, meta, g.reshape(1, d), y)


def kernel(x, ret_w_in, ret_norm_g, ret_w_out, pool_w, pool_scale, ffn_w_gu, ffn_w_down,
           moe_router, moe_w_gu, moe_w_down, norm_mix_g, norm_ffn_g, norm_final_g):
    batch, seq, d = x.shape
    n = batch * seq
    xf = x.reshape(n, d)
    pool_w = pool_w.astype(BF16)
    n_moe, n_exp, _, f2 = moe_w_gu.shape
    f = moe_w_down.shape[2]
    moe_gu_rows = moe_w_gu.reshape(n_moe, n_exp * d, f2)
    moe_down_rows = moe_w_down.reshape(n_moe, n_exp * f, d)
    hn = rmsnorm(xf, norm_mix_g[0], BF16)
    for i in range(DEPTH):
        j = i // 2
        last = i == DEPTH - 1
        if i % 2 == 0:
            proj, moe_gu_b, w_out_b = retention_in_proj(hn, ret_w_in, j,
                                                        [(moe_gu_rows, j), (ret_w_out, j)])
            y = retention(proj, ret_norm_g, j, batch, seq)
            xf, xg, ssq = down_residual(y, w_out_b, xf, norm_ffn_g[i])
            hid, moe_down_b, ffn_down_b = swiglu_up(xg, ssq, ffn_w_gu, j,
                                                    [(moe_down_rows, j), (ffn_w_down, j)])
            (xf,) = down_residual(hid, ffn_down_b, xf)
            moe_gu_b = moe_gu_b.reshape(n_exp, d, f2)
            moe_down_b = moe_down_b.reshape(n_exp, f, d)
        else:
            xf, hn, meta, meta_t, counts = pool_router(xf, norm_mix_g[i], pool_w, j, pool_scale[j],
                                               norm_ffn_g[i], moe_router[j], seq)
            pos, tile_expert, tile_valid, pad_lo, pad_hi, used = moe_schedule(meta_t, counts, n)
            xs = moe_dispatch(hn, pos, pad_lo, pad_hi, used)
            ys = moe_experts(xs, moe_gu_b, moe_down_b, tile_expert, tile_valid)
            if last:
                (out,) = moe_combine(xf, ys, meta, pos, norm_final_g, emit_x=False, norm_dtype=F32)
            else:
                xf, hn = moe_combine(xf, ys, meta, pos, norm_mix_g[i + 1], norm_dtype=BF16)
    return out.reshape(batch, seq, d)
```

```python
import functools

import jax
import jax.numpy as jnp
from jax import lax
from jax.experimental import pallas as pl
from jax.experimental.pallas import tpu as pltpu

BF16 = jnp.bfloat16
F32 = jnp.float32

D_MODEL = 2048
DEPTH = 4
RET_HEADS = 8
RET_DK = D_MODEL // RET_HEADS
RET_DV = 2 * D_MODEL // RET_HEADS
RET_QK = RET_HEADS * RET_DK
RET_V = RET_HEADS * RET_DV
RET_CHUNK = 256
RET_PREP_ROWS = 32
ROPE_BASE = 10000.0
POOL_WINDOWS = (2, 4, 8, 16)
POOL_G = D_MODEL // len(POOL_WINDOWS)
POOL_HALO = 16
D_FF = 5632
N_EXPERTS = 8
TOP_K = 2
D_FF_EXPERT = 1408
NORM_EPS = 1e-6
LANES = 128
SUBLANES = 8
MOE_TILE = 256
SIDE_CAST_BLOCKS = 64
X_RING = 3
BF16_TILE_ROWS = 16
DOWN_VMEM_LIMIT = 60 * 1024 * 1024
META_E1, META_E2, META_R1, META_R2, META_G1, META_G2 = range(6)


def _rms(x, g):
    ms = jnp.mean(x * x, axis=-1, keepdims=True)
    return x * lax.rsqrt(ms + NORM_EPS) * g


def _silu(a):
    return a / (1.0 + jnp.exp(-a))


def _dot(a, b):
    return jnp.dot(a, b, preferred_element_type=F32)


def _rmsnorm_kernel(x_ref, g_ref, o_ref):
    o_ref[...] = _rms(x_ref[...], g_ref[...]).astype(o_ref.dtype)


def rmsnorm(x, g, out_dtype, tm=512):
    n, d = x.shape
    return pl.pallas_call(
        _rmsnorm_kernel,
        grid=(n // tm,),
        in_specs=[pl.BlockSpec((tm, d), lambda i: (i, 0)),
                  pl.BlockSpec((1, d), lambda i: (0, 0))],
        out_specs=pl.BlockSpec((tm, d), lambda i: (i, 0)),
        out_shape=jax.ShapeDtypeStruct((n, d), out_dtype),
        name="rmsnorm",
    )(x, g.reshape(1, d))


def _side_cast_specs(sides, steps, n_inner):
    in_specs, out_specs, shapes = [], [], []
    for side, side_layer in sides:
        _, rows, cols = side.shape
        n_blocks = 1
        while (n_blocks * 2 <= min(steps, SIDE_CAST_BLOCKS)
               and rows % (n_blocks * 2 * BF16_TILE_ROWS) == 0):
            n_blocks *= 2
        br = rows // n_blocks

        def index(j, i, n_blocks=n_blocks):
            return jnp.minimum(j * n_inner + i, n_blocks - 1)

        in_specs.append(pl.BlockSpec((None, br, cols),
                                     lambda j, i, l=side_layer, ix=index: (l, ix(j, i), 0)))
        out_specs.append(pl.BlockSpec((br, cols), lambda j, i, ix=index: (ix(j, i), 0)))
        shapes.append(jax.ShapeDtypeStruct((rows, cols), BF16))
    return in_specs, out_specs, shapes


def _cast_sides(refs, n_sides):
    for s_ref, so_ref in zip(refs[:n_sides], refs[n_sides + 1:2 * n_sides + 1]):
        so_ref[...] = s_ref[...].astype(BF16)


class _RowTileRing:
    def __init__(self, x_hbm, buf, sems, n_row_tiles):
        self.x_hbm, self.buf, self.sems, self.n_row_tiles = x_hbm, buf, sems, n_row_tiles

    def _copy(self, s):
        tm = self.buf.shape[1]
        rows = pl.ds(pl.multiple_of((s % self.n_row_tiles) * tm, tm), tm)
        return pltpu.make_async_copy(self.x_hbm.at[rows, :], self.buf.at[s % X_RING],
                                     self.sems.at[s % X_RING])

    def advance(self, step, total):
        @pl.when(step == 0)
        def _():
            for s in range(X_RING - 1):
                self._copy(s).start()

        @pl.when(step + X_RING - 1 < total)
        def _():
            self._copy(step + X_RING - 1).start()

    def tile(self, step):
        self._copy(step).wait()
        return self.buf[step % X_RING]


def _ring_scratch(tm, k):
    return [pltpu.VMEM((X_RING, tm, k), BF16), pltpu.SemaphoreType.DMA((X_RING,))]


def _in_proj_kernel(x_hbm, w_ref, perm_ref, *refs, n_qk_slabs, n_sides):
    o_ref = refs[n_sides]
    wb_ref, xbuf, xsem = refs[-3:]
    ni = pl.num_programs(1)
    step = pl.program_id(0) * ni + pl.program_id(1)
    ring = _RowTileRing(x_hbm, xbuf, xsem, ni)
    ring.advance(step, pl.num_programs(0) * ni)

    @pl.when(pl.program_id(1) == 0)
    def _():
        is_qk = pl.program_id(0) < n_qk_slabs

        @pl.when(is_qk)
        def _():
            for lo in range(0, wb_ref.shape[1], RET_DK):
                head = w_ref[:, lo:lo + RET_DK].astype(BF16)
                wb_ref[:, lo:lo + RET_DK] = _dot(head, perm_ref[...]).astype(BF16)

        @pl.when(jnp.logical_not(is_qk))
        def _():
            wb_ref[...] = w_ref[...].astype(BF16)

    o_ref[...] = _dot(ring.tile(step), wb_ref[...]).astype(o_ref.dtype)
    _cast_sides(refs, n_sides)


def retention_in_proj(x, w, layer, sides, tm=1024, tn=1024):
    n, k = x.shape
    f = w.shape[2]
    tm = min(tm, n)
    nj, ni = f // tn, n // tm
    half = RET_DK // 2
    src = jnp.arange(RET_DK)
    dst = jnp.where(src % 2 == 0, src // 2, half + src // 2)
    perm = (dst[:, None] == jnp.arange(RET_DK)[None, :]).astype(BF16)
    s_in, s_out, s_shapes = _side_cast_specs(sides, nj * ni, ni)
    return pl.pallas_call(
        functools.partial(_in_proj_kernel, n_qk_slabs=2 * RET_QK // tn, n_sides=len(sides)),
        grid=(nj, ni),
        in_specs=[pl.BlockSpec(memory_space=pl.ANY),
                  pl.BlockSpec((None, k, tn), lambda j, i: (layer, 0, j)),
                  pl.BlockSpec((RET_DK, RET_DK), lambda j, i: (0, 0)),
                  *s_in],
        out_specs=[pl.BlockSpec((tm, tn), lambda j, i: (i, j)), *s_out],
        out_shape=[jax.ShapeDtypeStruct((n, f), BF16), *s_shapes],
        scratch_shapes=[pltpu.VMEM((k, tn), BF16), *_ring_scratch(tm, k)],
        compiler_params=pltpu.CompilerParams(dimension_semantics=("arbitrary", "arbitrary")),
        name="retention_in_proj",
    )(x, w, perm, *(side for side, _ in sides))


def _swiglu_kernel(x_hbm, ssq_ref, wa_ref, wu_ref, *refs, n_sides):
    o_ref = refs[n_sides]
    wab_ref, wub_ref, xbuf, xsem = refs[-4:]
    ni = pl.num_programs(1)
    step = pl.program_id(0) * ni + pl.program_id(1)
    ring = _RowTileRing(x_hbm, xbuf, xsem, ni)
    ring.advance(step, pl.num_programs(0) * ni)

    @pl.when(pl.program_id(1) == 0)
    def _():
        wab_ref[...] = wa_ref[...].astype(BF16)
        wub_ref[...] = wu_ref[...].astype(BF16)

    x = ring.tile(step)
    rowscale = lax.rsqrt(jnp.sum(ssq_ref[...], axis=0) / x.shape[1] + NORM_EPS)
    a = _dot(x, wab_ref[...]) * rowscale
    u = _dot(x, wub_ref[...]) * rowscale
    o_ref[...] = (_silu(a) * u).astype(o_ref.dtype)
    _cast_sides(refs, n_sides)


def swiglu_up(xg, ssq, w_gu, layer, sides, tm=1024, tn=512):
    n, k = xg.shape
    f = w_gu.shape[2] // 2
    tm = min(tm, n)
    nj, ni = f // tn, n // tm
    ns = ssq.shape[0]
    s_in, s_out, s_shapes = _side_cast_specs(sides, nj * ni, ni)
    return pl.pallas_call(
        functools.partial(_swiglu_kernel, n_sides=len(sides)),
        grid=(nj, ni),
        in_specs=[pl.BlockSpec(memory_space=pl.ANY),
                  pl.BlockSpec((ns, tm, 1), lambda j, i: (0, i, 0)),
                  pl.BlockSpec((None, k, tn), lambda j, i: (layer, 0, j)),
                  pl.BlockSpec((None, k, tn), lambda j, i: (layer, 0, nj + j)),
                  *s_in],
        out_specs=[pl.BlockSpec((tm, tn), lambda j, i: (i, j)), *s_out],
        out_shape=[jax.ShapeDtypeStruct((n, f), BF16), *s_shapes],
        scratch_shapes=[pltpu.VMEM((k, tn), BF16), pltpu.VMEM((k, tn), BF16),
                        *_ring_scratch(tm, k)],
        compiler_params=pltpu.CompilerParams(dimension_semantics=("arbitrary", "arbitrary")),
        name="swiglu_up",
    )(xg, ssq, w_gu, w_gu, *(side for side, _ in sides))


def _emit_residual_outputs(xn, g_ref, refs, emit_x, emit_norm):
    pos = 0
    if emit_x:
        refs[pos][...] = xn
        pos += 1
    if emit_norm:
        refs[pos][...] = _rms(xn, g_ref[...]).astype(refs[pos].dtype)


def _residual_out_shapes(n, d, tm, emit_x, norm_dtype, index_map):
    out_shape, out_specs = [], []
    if emit_x:
        out_shape.append(jax.ShapeDtypeStruct((n, d), F32))
        out_specs.append(pl.BlockSpec((tm, d), index_map))
    if norm_dtype is not None:
        out_shape.append(jax.ShapeDtypeStruct((n, d), norm_dtype))
        out_specs.append(pl.BlockSpec((tm, d), index_map))
    return out_shape, out_specs


def _down_kernel(x_ref, w_ref, r_ref, g_ref, *refs, emit_scaled):
    xn = r_ref[...] + _dot(x_ref[...], w_ref[...])
    refs[0][...] = xn
    if emit_scaled:
        refs[1][...] = (xn * g_ref[...]).astype(BF16)
        refs[2][...] = jnp.sum(xn * xn, axis=-1, keepdims=True)[None]


def down_residual(h, w, resid, g=None, *, tm=512):
    n, kk = h.shape
    d = w.shape[1]
    tm = min(tm, n)
    emit_scaled = g is not None
    if g is None:
        g = jnp.ones((d,), F32)
    out_shape = [jax.ShapeDtypeStruct((n, d), F32)]
    out_specs = [pl.BlockSpec((tm, d), lambda i: (i, 0))]
    if emit_scaled:
        out_shape += [jax.ShapeDtypeStruct((n, d), BF16), jax.ShapeDtypeStruct((1, n, 1), F32)]
        out_specs += [pl.BlockSpec((tm, d), lambda i: (i, 0)),
                      pl.BlockSpec((1, tm, 1), lambda i: (0, i, 0))]
    return pl.pallas_call(
        functools.partial(_down_kernel, emit_scaled=emit_scaled),
        grid=(n // tm,),
        in_specs=[pl.BlockSpec((tm, kk), lambda i: (i, 0)),
                  pl.BlockSpec((kk, d), lambda i: (0, 0), pipeline_mode=pl.Buffered(1)),
                  pl.BlockSpec((tm, d), lambda i: (i, 0)),
                  pl.BlockSpec((1, d), lambda i: (0, 0))],
        out_specs=out_specs,
        out_shape=out_shape,
        compiler_params=pltpu.CompilerParams(dimension_semantics=("arbitrary",),
                                             vmem_limit_bytes=DOWN_VMEM_LIMIT),
        name="down_residual",
    )(h, w, resid, g.reshape(1, d))


def _retention_kernel(lg_ref, q_ref, k_ref, v_ref, g_ref, cos_ref, sin_ref, gain_ref, o_ref,
                      qr_ref, qd_ref, kr_ref, kd_ref, mask_ref, qdec_ref, kdec_ref, oraw_ref,
                      state_ref, sb_ref, *, n_chunks, n_pairs):
    c = RET_CHUNK
    step = pl.program_id(0)
    slot_prep = step % 2
    slot_recur = 1 - slot_prep
    slot_finish = slot_prep

    @pl.when(step == 0)
    def _():
        for ref in (qr_ref, qd_ref, kr_ref, kd_ref, oraw_ref):
            ref[...] = jnp.zeros_like(ref)

    lg_prep = lg_ref[jnp.minimum(step, n_pairs - 1) % RET_HEADS]
    lg = lg_ref[jnp.clip(step - 1, 0, n_pairs - 1) % RET_HEADS]
    ii = lax.broadcasted_iota(jnp.int32, (c, c), 0)
    jj = lax.broadcasted_iota(jnp.int32, (c, c), 1)
    rel = (ii - jj).astype(F32)
    mask_ref[...] = jnp.where(rel >= 0.0, jnp.exp(jnp.maximum(rel, 0.0) * lg), 0.0)
    idx = lax.broadcasted_iota(jnp.int32, (c, 1), 0).astype(F32)
    k_scale = RET_DK ** -0.5
    qdec_ref[...] = jnp.broadcast_to(jnp.exp((idx + 1.0) * lg_prep), (c, RET_DK))
    kdec_ref[...] = jnp.broadcast_to(jnp.exp((c - 1.0 - idx) * lg_prep) * k_scale, (c, RET_DK))
    chunk_decay = jnp.exp(jnp.full((1, 1), float(c), F32) * lg)
    half = RET_DK // 2

    def rotary(xb, cos, sin):
        x = xb.astype(F32)
        x1, x2 = x[:, :half], x[:, half:]
        return jnp.concatenate([x1 * cos - x2 * sin, x2 * cos + x1 * sin], axis=1)

    def rows(ci):
        return pl.ds(pl.multiple_of(ci * c, c), c)

    state_ref[...] = jnp.zeros_like(state_ref)
    sb_ref[...] = jnp.zeros_like(sb_ref)

    def body(ci, carry):
        r = rows(ci)

        o = oraw_ref[slot_finish, r, :]
        ms = jnp.mean(o * o, axis=-1, keepdims=True)
        o = o * lax.rsqrt(ms + NORM_EPS) * gain_ref[...]
        gate = g_ref[r, :].astype(F32)
        o_ref[r, :] = (o * _silu(gate)).astype(o_ref.dtype)

        v = v_ref[r, :]
        scores = lax.dot_general(qr_ref[slot_recur, r, :], kr_ref[slot_recur, r, :],
                                 (((1,), (1,)), ((), ())),
                                 preferred_element_type=F32) * mask_ref[...]
        oraw_ref[slot_recur, r, :] = (_dot(scores.astype(BF16), v)
                                      + _dot(qd_ref[slot_recur, r, :], sb_ref[...]))
        upd = lax.dot_general(kd_ref[slot_recur, r, :], v, (((0,), (0,)), ((), ())),
                              preferred_element_type=F32)
        state = state_ref[...] * chunk_decay + upd
        state_ref[...] = state
        sb_ref[...] = state.astype(BF16)

        for lo in range(0, c, RET_PREP_ROWS):
            rr = pl.ds(pl.multiple_of(ci * c + lo, RET_PREP_ROWS), RET_PREP_ROWS)
            dec = pl.ds(lo, RET_PREP_ROWS)
            cos = cos_ref[rr, :]
            sin = sin_ref[rr, :]
            q = rotary(q_ref[rr, :], cos, sin)
            k = rotary(k_ref[rr, :], cos, sin)
            qr_ref[slot_prep, rr, :] = q.astype(BF16)
            qd_ref[slot_prep, rr, :] = (q * qdec_ref[dec, :]).astype(BF16)
            kr_ref[slot_prep, rr, :] = (k * k_scale).astype(BF16)
            kd_ref[slot_prep, rr, :] = (k * kdec_ref[dec, :]).astype(BF16)
        return carry

    lax.fori_loop(0, n_chunks, body, 0, unroll=2)


def retention(proj, norm_g, layer, batch, seq):
    n = batch * seq
    h = RET_HEADS
    log_g = jnp.log1p(-jnp.exp2(-5.0 - jnp.arange(h, dtype=F32)))
    freq = 1.0 / (ROPE_BASE ** jnp.linspace(0.0, 1.0, RET_DK // 2, dtype=F32))
    ang = jnp.arange(seq).astype(F32)[:, None] * freq[None, :]
    cos = jnp.cos(ang)
    sin = jnp.sin(ang)
    kq = RET_QK // RET_DK
    kv = 2 * RET_QK // RET_DV
    kg = (2 * RET_QK + RET_V) // RET_DV
    n_pairs = batch * h
    stages = 3

    def pair(step, stage):
        p = jnp.clip(step - stage, 0, n_pairs - 1)
        return p // h, p % h

    def block(stage, col0):
        def index(s):
            b, hh = pair(s, stage)
            return b, col0 + hh
        return index

    def gain_index(s):
        return layer, 0, pair(s, 2)[1]

    return pl.pallas_call(
        functools.partial(_retention_kernel, n_chunks=seq // RET_CHUNK, n_pairs=n_pairs),
        grid=(n_pairs + stages - 1,),
        in_specs=[pl.BlockSpec(memory_space=pltpu.SMEM),
                  pl.BlockSpec((seq, RET_DK), block(0, 0)),
                  pl.BlockSpec((seq, RET_DK), block(0, kq)),
                  pl.BlockSpec((seq, RET_DV), block(1, kv)),
                  pl.BlockSpec((seq, RET_DV), block(2, kg)),
                  pl.BlockSpec((seq, RET_DK // 2), lambda s: (0, 0)),
                  pl.BlockSpec((seq, RET_DK // 2), lambda s: (0, 0)),
                  pl.BlockSpec((None, 1, RET_DV), gain_index)],
        out_specs=pl.BlockSpec((seq, RET_DV), block(2, 0)),
        out_shape=jax.ShapeDtypeStruct((n, RET_V), BF16),
        scratch_shapes=[pltpu.VMEM((2, seq, RET_DK), BF16),
                        pltpu.VMEM((2, seq, RET_DK), BF16),
                        pltpu.VMEM((2, seq, RET_DK), BF16),
                        pltpu.VMEM((2, seq, RET_DK), BF16),
                        pltpu.VMEM((RET_CHUNK, RET_CHUNK), F32),
                        pltpu.VMEM((RET_CHUNK, RET_DK), F32),
                        pltpu.VMEM((RET_CHUNK, RET_DK), F32),
                        pltpu.VMEM((2, seq, RET_DV), F32),
                        pltpu.VMEM((RET_DK, RET_DV), F32),
                        pltpu.VMEM((RET_DK, RET_DV), BF16)],
        compiler_params=pltpu.CompilerParams(dimension_semantics=("arbitrary",)),
        name="retention",
    )(log_g, proj, proj, proj, proj, cos, sin, norm_g.reshape(norm_g.shape[0], 1, RET_V))


def _pool_kernel(x_ref, gm_ref, pw_ref, sc_ref, gf_ref, rt_ref,
                 xo_ref, hn_ref, meta_ref, meta_t_ref, cnt_out_ref, hbuf, cnt_ref, *, tiles_per_seq):
    tm = x_ref.shape[0]
    step = pl.program_id(0)
    ti = step % tiles_per_seq
    x = x_ref[...]
    hn = _rms(x, gm_ref[...])

    @pl.when(ti == 0)
    def _():
        hbuf[0:POOL_HALO, :] = jnp.zeros((POOL_HALO, D_MODEL), F32)

    @pl.when(ti != 0)
    def _():
        hbuf[0:POOL_HALO, :] = hbuf[tm:tm + POOL_HALO, :]

    hbuf[POOL_HALO:POOL_HALO + tm, :] = hn

    t = ti * tm + lax.broadcasted_iota(jnp.int32, (tm, 1), 0)
    for gi, w in enumerate(POOL_WINDOWS):
        lo, hi = gi * POOL_G, (gi + 1) * POOL_G
        cur = hn[:, lo:hi]
        acc = hbuf[:, lo:hi]
        shift = 1
        while shift < w:
            acc = acc + pltpu.roll(acc, shift, 0)
            shift *= 2
        acc = acc[POOL_HALO:, :]
        inv = 1.0 / jnp.minimum(t + 1, w).astype(F32)
        mix = (acc * inv - cur).astype(BF16)
        o = _dot(mix, pw_ref[gi]) * sc_ref[:, lo:hi]
        xo_ref[:, lo:hi] = x[:, lo:hi] + o

    hn2 = _rms(xo_ref[...], gf_ref[...])
    hn_ref[...] = hn2

    rt = rt_ref[...]
    hn_hi = hn2.astype(BF16)
    hn_lo = (hn2 - hn_hi.astype(F32)).astype(BF16)
    rt_hi = rt.astype(BF16)
    rt_lo = (rt - rt_hi.astype(F32)).astype(BF16)
    logits = _dot(hn_hi, rt_hi) + (_dot(hn_hi, rt_lo) + _dot(hn_lo, rt_hi))
    lt = logits.T[:N_EXPERTS, :]
    expert = lax.broadcasted_iota(jnp.int32, lt.shape, 0)
    neg = jnp.float32(-jnp.inf)
    m1 = jnp.max(lt, axis=0, keepdims=True)
    i1 = jnp.min(jnp.where(lt == m1, expert, N_EXPERTS), axis=0, keepdims=True)
    lt2 = jnp.where(expert == i1, neg, lt)
    m2 = jnp.max(lt2, axis=0, keepdims=True)
    i2 = jnp.min(jnp.where(lt2 == m2, expert, N_EXPERTS), axis=0, keepdims=True)
    e2 = jnp.exp(m2 - m1)
    g1 = 1.0 / (1.0 + e2)
    g2 = e2 * g1

    @pl.when(step == 0)
    def _():
        cnt_ref[...] = jnp.zeros_like(cnt_ref)

    chosen = jnp.where((expert == i1) | (expert == i2), 1.0, 0.0)
    src = lax.broadcasted_iota(jnp.int32, (tm, tm), 0)
    dst = lax.broadcasted_iota(jnp.int32, (tm, tm), 1)
    earlier = jnp.where(src < dst, 1.0, 0.0).astype(BF16)
    before = _dot(chosen.astype(BF16), earlier) + cnt_ref[:, 0:1]
    r1 = jnp.sum(jnp.where(expert == i1, before, 0.0), axis=0, keepdims=True)
    r2 = jnp.sum(jnp.where(expert == i2, before, 0.0), axis=0, keepdims=True)
    cnt_ref[...] += jnp.sum(chosen, axis=1, keepdims=True)
    cnt_out_ref[...] = cnt_ref[...]

    fields = {META_E1: i1.astype(F32), META_E2: i2.astype(F32), META_R1: r1, META_R2: r2,
              META_G1: g1, META_G2: g2}
    zero_row = jnp.zeros_like(g1)
    meta_t = jnp.concatenate([fields.get(row, zero_row) for row in range(SUBLANES)], axis=0)
    meta_t_ref[...] = meta_t
    meta_ref[...] = jnp.concatenate([meta_t, jnp.zeros((LANES - SUBLANES, tm), F32)], axis=0).T


def pool_router(x, g_mix, pool_w, layer, pool_scale, g_ffn, router, seq, tm=512):
    n, d = x.shape
    tm = min(tm, seq)
    router_p = jnp.zeros((d, LANES), F32).at[:, :N_EXPERTS].set(router)
    ng = len(POOL_WINDOWS)
    return pl.pallas_call(
        functools.partial(_pool_kernel, tiles_per_seq=seq // tm),
        grid=(n // tm,),
        in_specs=[pl.BlockSpec((tm, d), lambda i: (i, 0)),
                  pl.BlockSpec((1, d), lambda i: (0, 0)),
                  pl.BlockSpec((None, ng, POOL_G, POOL_G), lambda i: (layer, 0, 0, 0)),
                  pl.BlockSpec((1, d), lambda i: (0, 0)),
                  pl.BlockSpec((1, d), lambda i: (0, 0)),
                  pl.BlockSpec((d, LANES), lambda i: (0, 0))],
        out_specs=[pl.BlockSpec((tm, d), lambda i: (i, 0)),
                   pl.BlockSpec((tm, d), lambda i: (i, 0)),
                   pl.BlockSpec((tm, LANES), lambda i: (i, 0)),
                   pl.BlockSpec((SUBLANES, tm), lambda i: (0, i)),
                   pl.BlockSpec((SUBLANES, LANES), lambda i: (0, 0))],
        out_shape=[jax.ShapeDtypeStruct((n, d), F32),
                   jax.ShapeDtypeStruct((n, d), F32),
                   jax.ShapeDtypeStruct((n, LANES), F32),
                   jax.ShapeDtypeStruct((SUBLANES, n), F32),
                   jax.ShapeDtypeStruct((SUBLANES, LANES), F32)],
        scratch_shapes=[pltpu.VMEM((POOL_HALO + tm, d), F32),
                        pltpu.VMEM((N_EXPERTS, LANES), F32)],
        compiler_params=pltpu.CompilerParams(dimension_semantics=("arbitrary",)),
        name="pool_router",
    )(x, g_mix.reshape(1, d), pool_w, pool_scale.reshape(1, d), g_ffn.reshape(1, d), router_p)


def moe_num_tiles(n):
    return TOP_K * n // MOE_TILE + N_EXPERTS


def moe_schedule(meta_t, counts_f, n):
    nt = moe_num_tiles(n)
    counts = counts_f[:N_EXPERTS, 0].astype(jnp.int32)
    tiles = (counts + MOE_TILE - 1) // MOE_TILE
    tile_end = jnp.cumsum(tiles)
    starts = (tile_end - tiles) * MOE_TILE
    expert_ids = jnp.arange(N_EXPERTS, dtype=jnp.int32)[:, None]

    def slots(e_row, r_row):
        e = meta_t[e_row].astype(jnp.int32)
        first = jnp.sum(jnp.where(e[None, :] == expert_ids, starts[:, None], 0), axis=0)
        return first + meta_t[r_row].astype(jnp.int32)

    pos = jnp.concatenate([slots(META_E1, META_R1), slots(META_E2, META_R2)])
    total = tile_end[-1]
    t = jnp.arange(nt, dtype=jnp.int32)
    t_used = jnp.minimum(t, total - 1)
    tile_expert = jnp.sum((tile_end[None, :] <= t_used[:, None]).astype(jnp.int32), axis=1)
    tile_expert = jnp.minimum(tile_expert, N_EXPERTS - 1)
    tile_valid = (t < total).astype(jnp.int32)
    pad_lo = starts + counts
    pad_hi = starts + tiles * MOE_TILE
    return pos, tile_expert, tile_valid, pad_lo, pad_hi, total.reshape(1)


def _dispatch_kernel(pos_ref, lo_ref, hi_ref, used_ref, hn_ref, xs_hbm, zeros, sem, *, n, n_tiles):
    tb = hn_ref.shape[0]
    step = pl.program_id(0)
    base = step * tb

    def issue(r, c):
        tok = base + r
        src = hn_ref.at[pl.ds(r, 1), :]
        pltpu.make_async_copy(src, xs_hbm.at[pl.ds(pos_ref[tok], 1), :], sem).start()
        pltpu.make_async_copy(src, xs_hbm.at[pl.ds(pos_ref[n + tok], 1), :], sem).start()
        return c

    lax.fori_loop(0, tb, issue, 0, unroll=8)
    for _ in range(TOP_K):
        pltpu.make_async_copy(hn_ref, xs_hbm.at[pl.ds(0, tb), :], sem).wait()

    @pl.when(step == pl.num_programs(0) - 1)
    def _():
        zeros[...] = jnp.zeros_like(zeros)

        def pad_copy(slot):
            return pltpu.make_async_copy(zeros.at[pl.ds(0, 1), :], xs_hbm.at[pl.ds(slot, 1), :], sem)

        def issue_pad(s, c):
            pad_copy(s).start()
            return c

        def drain_pad(s, c):
            pad_copy(s).wait()
            return c

        for e in range(N_EXPERTS):
            lax.fori_loop(lo_ref[e], hi_ref[e], issue_pad, 0)
            lax.fori_loop(lo_ref[e], hi_ref[e], drain_pad, 0)

        def tile_copy(t):
            return pltpu.make_async_copy(zeros, xs_hbm.at[pl.ds(t * MOE_TILE, MOE_TILE), :], sem)

        def issue_tile(t, c):
            tile_copy(t).start()
            return c

        def drain_tile(t, c):
            tile_copy(t).wait()
            return c

        lax.fori_loop(used_ref[0], n_tiles, issue_tile, 0)
        lax.fori_loop(used_ref[0], n_tiles, drain_tile, 0)


def moe_dispatch(hn, pos, pad_lo, pad_hi, used, tb=512):
    n, d = hn.shape
    tb = min(tb, n)
    nt = moe_num_tiles(n)
    grid_spec = pltpu.PrefetchScalarGridSpec(
        num_scalar_prefetch=4,
        grid=(n // tb,),
        in_specs=[pl.BlockSpec((tb, d), lambda i, *_: (i, 0))],
        out_specs=pl.BlockSpec(memory_space=pl.ANY),
        scratch_shapes=[pltpu.VMEM((MOE_TILE, d), F32), pltpu.SemaphoreType.DMA],
    )
    return pl.pallas_call(
        functools.partial(_dispatch_kernel, n=n, n_tiles=nt),
        grid_spec=grid_spec,
        out_shape=jax.ShapeDtypeStruct((nt * MOE_TILE, d), F32),
        compiler_params=pltpu.CompilerParams(dimension_semantics=("arbitrary",)),
        name="moe_dispatch",
    )(pos, pad_lo, pad_hi, used, hn)


def _expert_kernel(te_ref, tv_ref, x_ref, wa_ref, wu_ref, wd_ref, o_ref):
    valid = tv_ref[pl.program_id(0)] == 1

    @pl.when(valid)
    def _():
        x = x_ref[...].astype(BF16)
        a = _dot(x, wa_ref[...])
        u = _dot(x, wu_ref[...])
        o_ref[...] = _dot((_silu(a) * u).astype(BF16), wd_ref[...])

    @pl.when(jnp.logical_not(valid))
    def _():
        o_ref[...] = jnp.zeros_like(o_ref)


def moe_experts(xs, w_gu, w_down, tile_expert, tile_valid):
    rows, d = xs.shape
    f = w_down.shape[1]
    nt = rows // MOE_TILE
    grid_spec = pltpu.PrefetchScalarGridSpec(
        num_scalar_prefetch=2,
        grid=(nt,),
        in_specs=[pl.BlockSpec((MOE_TILE, d), lambda t, te, tv: (t, 0)),
                  pl.BlockSpec((None, d, f), lambda t, te, tv: (te[t], 0, 0)),
                  pl.BlockSpec((None, d, f), lambda t, te, tv: (te[t], 0, 1)),
                  pl.BlockSpec((None, f, d), lambda t, te, tv: (te[t], 0, 0))],
        out_specs=pl.BlockSpec((MOE_TILE, d), lambda t, te, tv: (t, 0)),
    )
    return pl.pallas_call(
        _expert_kernel,
        grid_spec=grid_spec,
        out_shape=jax.ShapeDtypeStruct((rows, d), F32),
        compiler_params=pltpu.CompilerParams(dimension_semantics=("arbitrary",),
                                             vmem_limit_bytes=56 * 1024 * 1024),
        name="moe_experts",
    )(tile_expert, tile_valid, xs, w_gu, w_gu, w_down)


def _combine_kernel(pos_ref, x_ref, meta_ref, g_ref, y_hbm, *refs, n, emit_x, emit_norm):
    ybuf, sem = refs[-2], refs[-1]
    tc = x_ref.shape[0]
    step = pl.program_id(0)
    slot = step % 2

    def start_gather(s, into):
        def issue(r, c):
            tok = s * tc + r
            for k in range(TOP_K):
                pltpu.make_async_copy(y_hbm.at[pl.ds(pos_ref[k * n + tok], 1), :],
                                      ybuf.at[into, k, pl.ds(r, 1), :], sem.at[into]).start()
            return c

        lax.fori_loop(0, tc, issue, 0, unroll=8)

    @pl.when(step == 0)
    def _():
        start_gather(0, 0)

    @pl.when(step + 1 < pl.num_programs(0))
    def _():
        start_gather(step + 1, 1 - slot)

    for k in range(TOP_K):
        pltpu.make_async_copy(y_hbm.at[pl.ds(0, tc), :], ybuf.at[slot, k], sem.at[slot]).wait()

    meta = meta_ref[...]
    g1 = meta[:, META_G1:META_G1 + 1]
    g2 = meta[:, META_G2:META_G2 + 1]
    xn = x_ref[...] + g1 * ybuf[slot, 0] + g2 * ybuf[slot, 1]
    _emit_residual_outputs(xn, g_ref, refs, emit_x, emit_norm)


def moe_combine(x, y, meta, pos, g, *, emit_x=True, norm_dtype=None, tc=256):
    n, d = x.shape
    tc = min(tc, n)
    out_shape, out_specs = _residual_out_shapes(n, d, tc, emit_x, norm_dtype, lambda i, p: (i, 0))
    grid_spec = pltpu.PrefetchScalarGridSpec(
        num_scalar_prefetch=1,
        grid=(n // tc,),
        in_specs=[pl.BlockSpec((tc, d), lambda i, p: (i, 0)),
                  pl.BlockSpec((tc, LANES), lambda i, p: (i, 0)),
                  pl.BlockSpec((1, d), lambda i, p: (0, 0)),
                  pl.BlockSpec(memory_space=pl.ANY)],
        out_specs=out_specs,
        scratch_shapes=[pltpu.VMEM((2, TOP_K, tc, d), F32), pltpu.SemaphoreType.DMA((2,))],
    )
    return pl.pallas_call(
        functools.partial(_combine_kernel, n=n, emit_x=emit_x, emit_norm=norm_dtype is not None),
        grid_spec=grid_spec,
        out_shape=out_shape,
        compiler_params=pltpu.CompilerParams(dimension_semantics=("arbitrary",)),
        name="moe_combine",
    )(pos, x, meta, g.reshape(1, d), y)


def kernel(x, ret_w_in, ret_norm_g, ret_w_out, pool_w, pool_scale, ffn_w_gu, ffn_w_down,
           moe_router, moe_w_gu, moe_w_down, norm_mix_g, norm_ffn_g, norm_final_g):
    batch, seq, d = x.shape
    n = batch * seq
    xf = x.reshape(n, d)
    pool_w = pool_w.astype(BF16)
    n_moe, n_exp, _, f2 = moe_w_gu.shape
    f = moe_w_down.shape[2]
    moe_gu_rows = moe_w_gu.reshape(n_moe, n_exp * d, f2)
    moe_down_rows = moe_w_down.reshape(n_moe, n_exp * f, d)
    hn = rmsnorm(xf, norm_mix_g[0], BF16)
    for i in range(DEPTH):
        j = i // 2
        last = i == DEPTH - 1
        if i % 2 == 0:
            proj, moe_gu_b, w_out_b = retention_in_proj(hn, ret_w_in, j,
                                                        [(moe_gu_rows, j), (ret_w_out, j)])
            y = retention(proj, ret_norm_g, j, batch, seq)
            xf, xg, ssq = down_residual(y, w_out_b, xf, norm_ffn_g[i])
            hid, moe_down_b, ffn_down_b = swiglu_up(xg, ssq, ffn_w_gu, j,
                                                    [(moe_down_rows, j), (ffn_w_down, j)])
            (xf,) = down_residual(hid, ffn_down_b, xf)
            moe_gu_b = moe_gu_b.reshape(n_exp, d, f2)
            moe_down_b = moe_down_b.reshape(n_exp, f, d)
        else:
            xf, hn, meta, meta_t, counts = pool_router(xf, norm_mix_g[i], pool_w, j, pool_scale[j],
                                               norm_ffn_g[i], moe_router[j], seq)
            pos, tile_expert, tile_valid, pad_lo, pad_hi, used = moe_schedule(meta_t, counts, n)
            xs = moe_dispatch(hn, pos, pad_lo, pad_hi, used)
            ys = moe_experts(xs, moe_gu_b, moe_down_b, tile_expert, tile_valid)
            if last:
                (out,) = moe_combine(xf, ys, meta, pos, norm_final_g, emit_x=False, norm_dtype=F32)
            else:
                xf, hn = moe_combine(xf, ys, meta, pos, norm_mix_g[i + 1], norm_dtype=BF16)
    return out.reshape(batch, seq, d)
```

```python
import functools

import jax
import jax.numpy as jnp
from jax import lax
from jax.experimental import pallas as pl
from jax.experimental.pallas import tpu as pltpu

BF16 = jnp.bfloat16
F32 = jnp.float32

D_MODEL = 2048
DEPTH = 4
RET_HEADS = 8
RET_DK = D_MODEL // RET_HEADS
RET_DV = 2 * D_MODEL // RET_HEADS
RET_QK = RET_HEADS * RET_DK
RET_V = RET_HEADS * RET_DV
RET_CHUNK = 256
RET_PREP_ROWS = 32
ROPE_BASE = 10000.0
POOL_WINDOWS = (2, 4, 8, 16)
POOL_G = D_MODEL // len(POOL_WINDOWS)
POOL_HALO = 16
D_FF = 5632
N_EXPERTS = 8
TOP_K = 2
D_FF_EXPERT = 1408
NORM_EPS = 1e-6
LANES = 128
SUBLANES = 8
MOE_TILE = 256
SIDE_CAST_BLOCKS = 64
X_RING = 3
BF16_TILE_ROWS = 16
DOWN_VMEM_LIMIT = 60 * 1024 * 1024
META_E1, META_E2, META_R1, META_R2, META_G1, META_G2 = range(6)
TILE_EXPERT, TILE_VALID, TILE_FIRST, TILE_SLOT, TILE_NEXT = range(5)


def _rms(x, g):
    ms = jnp.mean(x * x, axis=-1, keepdims=True)
    return x * lax.rsqrt(ms + NORM_EPS) * g


def _silu(a):
    return a / (1.0 + jnp.exp(-a))


def _dot(a, b):
    return jnp.dot(a, b, preferred_element_type=F32)


def _rmsnorm_kernel(x_ref, g_ref, o_ref):
    o_ref[...] = _rms(x_ref[...], g_ref[...]).astype(o_ref.dtype)


def rmsnorm(x, g, out_dtype, tm=512):
    n, d = x.shape
    return pl.pallas_call(
        _rmsnorm_kernel,
        grid=(n // tm,),
        in_specs=[pl.BlockSpec((tm, d), lambda i: (i, 0)),
                  pl.BlockSpec((1, d), lambda i: (0, 0))],
        out_specs=pl.BlockSpec((tm, d), lambda i: (i, 0)),
        out_shape=jax.ShapeDtypeStruct((n, d), out_dtype),
        name="rmsnorm",
    )(x, g.reshape(1, d))


def _side_cast_specs(sides, steps, n_inner):
    in_specs, out_specs, shapes = [], [], []
    for side, side_layer in sides:
        _, rows, cols = side.shape
        n_blocks = 1
        while (n_blocks * 2 <= min(steps, SIDE_CAST_BLOCKS)
               and rows % (n_blocks * 2 * BF16_TILE_ROWS) == 0):
            n_blocks *= 2
        br = rows // n_blocks

        def index(j, i, n_blocks=n_blocks):
            return jnp.minimum(j * n_inner + i, n_blocks - 1)

        in_specs.append(pl.BlockSpec((None, br, cols),
                                     lambda j, i, l=side_layer, ix=index: (l, ix(j, i), 0)))
        out_specs.append(pl.BlockSpec((br, cols), lambda j, i, ix=index: (ix(j, i), 0)))
        shapes.append(jax.ShapeDtypeStruct((rows, cols), BF16))
    return in_specs, out_specs, shapes


def _cast_sides(refs, n_sides):
    for s_ref, so_ref in zip(refs[:n_sides], refs[n_sides + 1:2 * n_sides + 1]):
        so_ref[...] = s_ref[...].astype(BF16)


class _RowTileRing:
    def __init__(self, x_hbm, buf, sems, n_row_tiles):
        self.x_hbm, self.buf, self.sems, self.n_row_tiles = x_hbm, buf, sems, n_row_tiles

    def _copy(self, s):
        tm = self.buf.shape[1]
        rows = pl.ds(pl.multiple_of((s % self.n_row_tiles) * tm, tm), tm)
        return pltpu.make_async_copy(self.x_hbm.at[rows, :], self.buf.at[s % X_RING],
                                     self.sems.at[s % X_RING])

    def advance(self, step, total):
        @pl.when(step == 0)
        def _():
            for s in range(X_RING - 1):
                self._copy(s).start()

        @pl.when(step + X_RING - 1 < total)
        def _():
            self._copy(step + X_RING - 1).start()

    def tile(self, step):
        self._copy(step).wait()
        return self.buf[step % X_RING]


def _ring_scratch(tm, k):
    return [pltpu.VMEM((X_RING, tm, k), BF16), pltpu.SemaphoreType.DMA((X_RING,))]


def _in_proj_kernel(x_hbm, w_ref, perm_ref, *refs, n_qk_slabs, n_sides):
    o_ref = refs[n_sides]
    wb_ref, xbuf, xsem = refs[-3:]
    ni = pl.num_programs(1)
    step = pl.program_id(0) * ni + pl.program_id(1)
    ring = _RowTileRing(x_hbm, xbuf, xsem, ni)
    ring.advance(step, pl.num_programs(0) * ni)

    @pl.when(pl.program_id(1) == 0)
    def _():
        is_qk = pl.program_id(0) < n_qk_slabs

        @pl.when(is_qk)
        def _():
            for lo in range(0, wb_ref.shape[1], RET_DK):
                head = w_ref[:, lo:lo + RET_DK].astype(BF16)
                wb_ref[:, lo:lo + RET_DK] = _dot(head, perm_ref[...]).astype(BF16)

        @pl.when(jnp.logical_not(is_qk))
        def _():
            wb_ref[...] = w_ref[...].astype(BF16)

    o_ref[...] = _dot(ring.tile(step), wb_ref[...]).astype(o_ref.dtype)
    _cast_sides(refs, n_sides)


def retention_in_proj(x, w, layer, sides, tm=1024, tn=1024):
    n, k = x.shape
    f = w.shape[2]
    tm = min(tm, n)
    nj, ni = f // tn, n // tm
    half = RET_DK // 2
    src = jnp.arange(RET_DK)
    dst = jnp.where(src % 2 == 0, src // 2, half + src // 2)
    perm = (dst[:, None] == jnp.arange(RET_DK)[None, :]).astype(BF16)
    s_in, s_out, s_shapes = _side_cast_specs(sides, nj * ni, ni)
    return pl.pallas_call(
        functools.partial(_in_proj_kernel, n_qk_slabs=2 * RET_QK // tn, n_sides=len(sides)),
        grid=(nj, ni),
        in_specs=[pl.BlockSpec(memory_space=pl.ANY),
                  pl.BlockSpec((None, k, tn), lambda j, i: (layer, 0, j)),
                  pl.BlockSpec((RET_DK, RET_DK), lambda j, i: (0, 0)),
                  *s_in],
        out_specs=[pl.BlockSpec((tm, tn), lambda j, i: (i, j)), *s_out],
        out_shape=[jax.ShapeDtypeStruct((n, f), BF16), *s_shapes],
        scratch_shapes=[pltpu.VMEM((k, tn), BF16), *_ring_scratch(tm, k)],
        compiler_params=pltpu.CompilerParams(dimension_semantics=("arbitrary", "arbitrary")),
        name="retention_in_proj",
    )(x, w, perm, *(side for side, _ in sides))


def _swiglu_kernel(x_hbm, ssq_ref, wa_ref, wu_ref, *refs, n_sides):
    o_ref = refs[n_sides]
    wab_ref, wub_ref, xbuf, xsem = refs[-4:]
    ni = pl.num_programs(1)
    step = pl.program_id(0) * ni + pl.program_id(1)
    ring = _RowTileRing(x_hbm, xbuf, xsem, ni)
    ring.advance(step, pl.num_programs(0) * ni)

    @pl.when(pl.program_id(1) == 0)
    def _():
        wab_ref[...] = wa_ref[...].astype(BF16)
        wub_ref[...] = wu_ref[...].astype(BF16)

    x = ring.tile(step)
    rowscale = lax.rsqrt(jnp.sum(ssq_ref[...], axis=0) / x.shape[1] + NORM_EPS)
    a = _dot(x, wab_ref[...]) * rowscale
    u = _dot(x, wub_ref[...]) * rowscale
    o_ref[...] = (_silu(a) * u).astype(o_ref.dtype)
    _cast_sides(refs, n_sides)


def swiglu_up(xg, ssq, w_gu, layer, sides, tm=1024, tn=512):
    n, k = xg.shape
    f = w_gu.shape[2] // 2
    tm = min(tm, n)
    nj, ni = f // tn, n // tm
    ns = ssq.shape[0]
    s_in, s_out, s_shapes = _side_cast_specs(sides, nj * ni, ni)
    return pl.pallas_call(
        functools.partial(_swiglu_kernel, n_sides=len(sides)),
        grid=(nj, ni),
        in_specs=[pl.BlockSpec(memory_space=pl.ANY),
                  pl.BlockSpec((ns, tm, 1), lambda j, i: (0, i, 0)),
                  pl.BlockSpec((None, k, tn), lambda j, i: (layer, 0, j)),
                  pl.BlockSpec((None, k, tn), lambda j, i: (layer, 0, nj + j)),
                  *s_in],
        out_specs=[pl.BlockSpec((tm, tn), lambda j, i: (i, j)), *s_out],
        out_shape=[jax.ShapeDtypeStruct((n, f), BF16), *s_shapes],
        scratch_shapes=[pltpu.VMEM((k, tn), BF16), pltpu.VMEM((k, tn), BF16),
                        *_ring_scratch(tm, k)],
        compiler_params=pltpu.CompilerParams(dimension_semantics=("arbitrary", "arbitrary")),
        name="swiglu_up",
    )(xg, ssq, w_gu, w_gu, *(side for side, _ in sides))


def _emit_residual_outputs(xn, g_ref, refs, emit_x, emit_norm):
    pos = 0
    if emit_x:
        refs[pos][...] = xn
        pos += 1
    if emit_norm:
        refs[pos][...] = _rms(xn, g_ref[...]).astype(refs[pos].dtype)


def _residual_out_shapes(n, d, tm, emit_x, norm_dtype, index_map):
    out_shape, out_specs = [], []
    if emit_x:
        out_shape.append(jax.ShapeDtypeStruct((n, d), F32))
        out_specs.append(pl.BlockSpec((tm, d), index_map))
    if norm_dtype is not None:
        out_shape.append(jax.ShapeDtypeStruct((n, d), norm_dtype))
        out_specs.append(pl.BlockSpec((tm, d), index_map))
    return out_shape, out_specs


def _down_kernel(x_hbm, w_ref, r_ref, g_ref, *refs, emit_scaled):
    step = pl.program_id(0)
    ring = _RowTileRing(x_hbm, refs[-2], refs[-1], pl.num_programs(0))
    ring.advance(step, pl.num_programs(0))
    xn = r_ref[...] + _dot(ring.tile(step), w_ref[...])
    refs[0][...] = xn
    if emit_scaled:
        refs[1][...] = (xn * g_ref[...]).astype(BF16)
        refs[2][...] = jnp.sum(xn * xn, axis=-1, keepdims=True)[None]


def down_residual(h, w, resid, g=None, *, tm=512):
    n, kk = h.shape
    d = w.shape[1]
    tm = min(tm, n)
    emit_scaled = g is not None
    if g is None:
        g = jnp.ones((d,), F32)
    out_shape = [jax.ShapeDtypeStruct((n, d), F32)]
    out_specs = [pl.BlockSpec((tm, d), lambda i: (i, 0))]
    if emit_scaled:
        out_shape += [jax.ShapeDtypeStruct((n, d), BF16), jax.ShapeDtypeStruct((1, n, 1), F32)]
        out_specs += [pl.BlockSpec((tm, d), lambda i: (i, 0)),
                      pl.BlockSpec((1, tm, 1), lambda i: (0, i, 0))]
    return pl.pallas_call(
        functools.partial(_down_kernel, emit_scaled=emit_scaled),
        grid=(n // tm,),
        in_specs=[pl.BlockSpec(memory_space=pl.ANY),
                  pl.BlockSpec((kk, d), lambda i: (0, 0), pipeline_mode=pl.Buffered(1)),
                  pl.BlockSpec((tm, d), lambda i: (i, 0)),
                  pl.BlockSpec((1, d), lambda i: (0, 0))],
        out_specs=out_specs,
        out_shape=out_shape,
        scratch_shapes=_ring_scratch(tm, kk),
        compiler_params=pltpu.CompilerParams(dimension_semantics=("arbitrary",),
                                             vmem_limit_bytes=DOWN_VMEM_LIMIT),
        name="down_residual",
    )(h, w, resid, g.reshape(1, d))


def _retention_kernel(lg_ref, q_ref, k_ref, v_ref, g_ref, cos_ref, sin_ref, gain_ref, o_ref,
                      qr_ref, qd_ref, kr_ref, kd_ref, mask_ref, qdec_ref, kdec_ref, oraw_ref,
                      state_ref, sb_ref, *, n_chunks, n_pairs):
    c = RET_CHUNK
    step = pl.program_id(0)
    slot_prep = step % 2
    slot_recur = 1 - slot_prep
    slot_finish = slot_prep

    @pl.when(step == 0)
    def _():
        for ref in (qr_ref, qd_ref, kr_ref, kd_ref, oraw_ref):
            ref[...] = jnp.zeros_like(ref)

    lg_prep = lg_ref[jnp.minimum(step, n_pairs - 1) % RET_HEADS]
    lg = lg_ref[jnp.clip(step - 1, 0, n_pairs - 1) % RET_HEADS]
    ii = lax.broadcasted_iota(jnp.int32, (c, c), 0)
    jj = lax.broadcasted_iota(jnp.int32, (c, c), 1)
    rel = (ii - jj).astype(F32)
    mask_ref[...] = jnp.where(rel >= 0.0, jnp.exp(jnp.maximum(rel, 0.0) * lg), 0.0)
    idx = lax.broadcasted_iota(jnp.int32, (c, 1), 0).astype(F32)
    k_scale = RET_DK ** -0.5
    qdec_ref[...] = jnp.broadcast_to(jnp.exp((idx + 1.0) * lg_prep), (c, RET_DK))
    kdec_ref[...] = jnp.broadcast_to(jnp.exp((c - 1.0 - idx) * lg_prep) * k_scale, (c, RET_DK))
    chunk_decay = jnp.exp(jnp.full((1, 1), float(c), F32) * lg)
    half = RET_DK // 2

    def rotary(xb, cos, sin):
        x = xb.astype(F32)
        x1, x2 = x[:, :half], x[:, half:]
        return jnp.concatenate([x1 * cos - x2 * sin, x2 * cos + x1 * sin], axis=1)

    def rows(ci):
        return pl.ds(pl.multiple_of(ci * c, c), c)

    state_ref[...] = jnp.zeros_like(state_ref)
    sb_ref[...] = jnp.zeros_like(sb_ref)

    def body(ci, carry):
        r = rows(ci)

        o = oraw_ref[slot_finish, r, :]
        ms = jnp.mean(o * o, axis=-1, keepdims=True)
        o = o * lax.rsqrt(ms + NORM_EPS) * gain_ref[...]
        gate = g_ref[r, :].astype(F32)
        o_ref[r, :] = (o * _silu(gate)).astype(o_ref.dtype)

        v = v_ref[r, :]
        scores = lax.dot_general(qr_ref[slot_recur, r, :], kr_ref[slot_recur, r, :],
                                 (((1,), (1,)), ((), ())),
                                 preferred_element_type=F32) * mask_ref[...]
        oraw_ref[slot_recur, r, :] = (_dot(scores.astype(BF16), v)
                                      + _dot(qd_ref[slot_recur, r, :], sb_ref[...]))
        upd = lax.dot_general(kd_ref[slot_recur, r, :], v, (((0,), (0,)), ((), ())),
                              preferred_element_type=F32)
        state = state_ref[...] * chunk_decay + upd
        state_ref[...] = state
        sb_ref[...] = state.astype(BF16)

        for lo in range(0, c, RET_PREP_ROWS):
            rr = pl.ds(pl.multiple_of(ci * c + lo, RET_PREP_ROWS), RET_PREP_ROWS)
            dec = pl.ds(lo, RET_PREP_ROWS)
            cos = cos_ref[rr, :]
            sin = sin_ref[rr, :]
            q = rotary(q_ref[rr, :], cos, sin)
            k = rotary(k_ref[rr, :], cos, sin)
            qr_ref[slot_prep, rr, :] = q.astype(BF16)
            qd_ref[slot_prep, rr, :] = (q * qdec_ref[dec, :]).astype(BF16)
            kr_ref[slot_prep, rr, :] = (k * k_scale).astype(BF16)
            kd_ref[slot_prep, rr, :] = (k * kdec_ref[dec, :]).astype(BF16)
        return carry

    lax.fori_loop(0, n_chunks, body, 0, unroll=2)


def retention(proj, norm_g, layer, batch, seq):
    n = batch * seq
    h = RET_HEADS
    log_g = jnp.log1p(-jnp.exp2(-5.0 - jnp.arange(h, dtype=F32)))
    freq = 1.0 / (ROPE_BASE ** jnp.linspace(0.0, 1.0, RET_DK // 2, dtype=F32))
    ang = jnp.arange(seq).astype(F32)[:, None] * freq[None, :]
    cos = jnp.cos(ang)
    sin = jnp.sin(ang)
    kq = RET_QK // RET_DK
    kv = 2 * RET_QK // RET_DV
    kg = (2 * RET_QK + RET_V) // RET_DV
    n_pairs = batch * h
    stages = 3

    def pair(step, stage):
        p = jnp.clip(step - stage, 0, n_pairs - 1)
        return p // h, p % h

    def block(stage, col0):
        def index(s):
            b, hh = pair(s, stage)
            return b, col0 + hh
        return index

    def gain_index(s):
        return layer, 0, pair(s, 2)[1]

    return pl.pallas_call(
        functools.partial(_retention_kernel, n_chunks=seq // RET_CHUNK, n_pairs=n_pairs),
        grid=(n_pairs + stages - 1,),
        in_specs=[pl.BlockSpec(memory_space=pltpu.SMEM),
                  pl.BlockSpec((seq, RET_DK), block(0, 0)),
                  pl.BlockSpec((seq, RET_DK), block(0, kq)),
                  pl.BlockSpec((seq, RET_DV), block(1, kv)),
                  pl.BlockSpec((seq, RET_DV), block(2, kg)),
                  pl.BlockSpec((seq, RET_DK // 2), lambda s: (0, 0)),
                  pl.BlockSpec((seq, RET_DK // 2), lambda s: (0, 0)),
                  pl.BlockSpec((None, 1, RET_DV), gain_index)],
        out_specs=pl.BlockSpec((seq, RET_DV), block(2, 0)),
        out_shape=jax.ShapeDtypeStruct((n, RET_V), BF16),
        scratch_shapes=[pltpu.VMEM((2, seq, RET_DK), BF16),
                        pltpu.VMEM((2, seq, RET_DK), BF16),
                        pltpu.VMEM((2, seq, RET_DK), BF16),
                        pltpu.VMEM((2, seq, RET_DK), BF16),
                        pltpu.VMEM((RET_CHUNK, RET_CHUNK), F32),
                        pltpu.VMEM((RET_CHUNK, RET_DK), F32),
                        pltpu.VMEM((RET_CHUNK, RET_DK), F32),
                        pltpu.VMEM((2, seq, RET_DV), F32),
                        pltpu.VMEM((RET_DK, RET_DV), F32),
                        pltpu.VMEM((RET_DK, RET_DV), BF16)],
        compiler_params=pltpu.CompilerParams(dimension_semantics=("arbitrary",)),
        name="retention",
    )(log_g, proj, proj, proj, proj, cos, sin, norm_g.reshape(norm_g.shape[0], 1, RET_V))


def _pool_kernel(x_ref, gm_ref, pw_ref, sc_ref, gf_ref, rt_ref,
                 xo_ref, hn_ref, meta_ref, meta_t_ref, cnt_out_ref, hbuf, cnt_ref, *, tiles_per_seq):
    tm = x_ref.shape[0]
    step = pl.program_id(0)
    ti = step % tiles_per_seq
    x = x_ref[...]
    hn = _rms(x, gm_ref[...])

    @pl.when(ti == 0)
    def _():
        hbuf[0:POOL_HALO, :] = jnp.zeros((POOL_HALO, D_MODEL), F32)

    @pl.when(ti != 0)
    def _():
        hbuf[0:POOL_HALO, :] = hbuf[tm:tm + POOL_HALO, :]

    hbuf[POOL_HALO:POOL_HALO + tm, :] = hn

    t = ti * tm + lax.broadcasted_iota(jnp.int32, (tm, 1), 0)
    for gi, w in enumerate(POOL_WINDOWS):
        lo, hi = gi * POOL_G, (gi + 1) * POOL_G
        cur = hn[:, lo:hi]
        acc = hbuf[:, lo:hi]
        shift = 1
        while shift < w:
            acc = acc + pltpu.roll(acc, shift, 0)
            shift *= 2
        acc = acc[POOL_HALO:, :]
        inv = 1.0 / jnp.minimum(t + 1, w).astype(F32)
        mix = (acc * inv - cur).astype(BF16)
        o = _dot(mix, pw_ref[gi]) * sc_ref[:, lo:hi]
        xo_ref[:, lo:hi] = x[:, lo:hi] + o

    hn2 = _rms(xo_ref[...], gf_ref[...])
    hn_ref[...] = hn2

    rt = rt_ref[...]
    hn_hi = hn2.astype(BF16)
    hn_lo = (hn2 - hn_hi.astype(F32)).astype(BF16)
    rt_hi = rt.astype(BF16)
    rt_lo = (rt - rt_hi.astype(F32)).astype(BF16)
    logits = _dot(hn_hi, rt_hi) + (_dot(hn_hi, rt_lo) + _dot(hn_lo, rt_hi))
    lt = logits.T[:N_EXPERTS, :]
    expert = lax.broadcasted_iota(jnp.int32, lt.shape, 0)
    neg = jnp.float32(-jnp.inf)
    m1 = jnp.max(lt, axis=0, keepdims=True)
    i1 = jnp.min(jnp.where(lt == m1, expert, N_EXPERTS), axis=0, keepdims=True)
    lt2 = jnp.where(expert == i1, neg, lt)
    m2 = jnp.max(lt2, axis=0, keepdims=True)
    i2 = jnp.min(jnp.where(lt2 == m2, expert, N_EXPERTS), axis=0, keepdims=True)
    e2 = jnp.exp(m2 - m1)
    g1 = 1.0 / (1.0 + e2)
    g2 = e2 * g1

    @pl.when(step == 0)
    def _():
        cnt_ref[...] = jnp.zeros_like(cnt_ref)

    chosen = jnp.where((expert == i1) | (expert == i2), 1.0, 0.0)
    src = lax.broadcasted_iota(jnp.int32, (tm, tm), 0)
    dst = lax.broadcasted_iota(jnp.int32, (tm, tm), 1)
    earlier = jnp.where(src < dst, 1.0, 0.0).astype(BF16)
    before = _dot(chosen.astype(BF16), earlier) + cnt_ref[:, 0:1]
    r1 = jnp.sum(jnp.where(expert == i1, before, 0.0), axis=0, keepdims=True)
    r2 = jnp.sum(jnp.where(expert == i2, before, 0.0), axis=0, keepdims=True)
    cnt_ref[...] += jnp.sum(chosen, axis=1, keepdims=True)
    cnt_out_ref[...] = cnt_ref[...]

    fields = {META_E1: i1.astype(F32), META_E2: i2.astype(F32), META_R1: r1, META_R2: r2,
              META_G1: g1, META_G2: g2}
    zero_row = jnp.zeros_like(g1)
    meta_t = jnp.concatenate([fields.get(row, zero_row) for row in range(SUBLANES)], axis=0)
    meta_t_ref[...] = meta_t
    meta_ref[...] = jnp.concatenate([meta_t, jnp.zeros((LANES - SUBLANES, tm), F32)], axis=0).T


def pool_router(x, g_mix, pool_w, layer, pool_scale, g_ffn, router, seq, tm=512):
    n, d = x.shape
    tm = min(tm, seq)
    router_p = jnp.zeros((d, LANES), F32).at[:, :N_EXPERTS].set(router)
    ng = len(POOL_WINDOWS)
    return pl.pallas_call(
        functools.partial(_pool_kernel, tiles_per_seq=seq // tm),
        grid=(n // tm,),
        in_specs=[pl.BlockSpec((tm, d), lambda i: (i, 0)),
                  pl.BlockSpec((1, d), lambda i: (0, 0)),
                  pl.BlockSpec((None, ng, POOL_G, POOL_G), lambda i: (layer, 0, 0, 0)),
                  pl.BlockSpec((1, d), lambda i: (0, 0)),
                  pl.BlockSpec((1, d), lambda i: (0, 0)),
                  pl.BlockSpec((d, LANES), lambda i: (0, 0))],
        out_specs=[pl.BlockSpec((tm, d), lambda i: (i, 0)),
                   pl.BlockSpec((tm, d), lambda i: (i, 0)),
                   pl.BlockSpec((tm, LANES), lambda i: (i, 0)),
                   pl.BlockSpec((SUBLANES, tm), lambda i: (0, i)),
                   pl.BlockSpec((SUBLANES, LANES), lambda i: (0, 0))],
        out_shape=[jax.ShapeDtypeStruct((n, d), F32),
                   jax.ShapeDtypeStruct((n, d), F32),
                   jax.ShapeDtypeStruct((n, LANES), F32),
                   jax.ShapeDtypeStruct((SUBLANES, n), F32),
                   jax.ShapeDtypeStruct((SUBLANES, LANES), F32)],
        scratch_shapes=[pltpu.VMEM((POOL_HALO + tm, d), F32),
                        pltpu.VMEM((N_EXPERTS, LANES), F32)],
        compiler_params=pltpu.CompilerParams(dimension_semantics=("arbitrary",)),
        name="pool_router",
    )(x, g_mix.reshape(1, d), pool_w, pool_scale.reshape(1, d), g_ffn.reshape(1, d), router_p)


def moe_num_tiles(n):
    return TOP_K * n // MOE_TILE + N_EXPERTS


def moe_schedule(meta_t, counts_f, n):
    nt = moe_num_tiles(n)
    counts = counts_f[:N_EXPERTS, 0].astype(jnp.int32)
    tiles = (counts + MOE_TILE - 1) // MOE_TILE
    tile_end = jnp.cumsum(tiles)
    starts = (tile_end - tiles) * MOE_TILE
    expert_ids = jnp.arange(N_EXPERTS, dtype=jnp.int32)[:, None]

    def slots(e_row, r_row):
        e = meta_t[e_row].astype(jnp.int32)
        first = jnp.sum(jnp.where(e[None, :] == expert_ids, starts[:, None], 0), axis=0)
        return first + meta_t[r_row].astype(jnp.int32)

    pos = jnp.concatenate([slots(META_E1, META_R1), slots(META_E2, META_R2)])
    total = tile_end[-1]
    t = jnp.arange(nt, dtype=jnp.int32)
    t_used = jnp.minimum(t, total - 1)
    tile_expert = jnp.sum((tile_end[None, :] <= t_used[:, None]).astype(jnp.int32), axis=1)
    tile_expert = jnp.minimum(tile_expert, N_EXPERTS - 1)
    tile_valid = (t < total).astype(jnp.int32)
    pad_lo = starts + counts
    pad_hi = starts + tiles * MOE_TILE
    prev_expert = jnp.concatenate([jnp.full((1,), -1, jnp.int32), tile_expert[:-1]])
    group_first = (tile_expert != prev_expert).astype(jnp.int32)
    group_slot = (jnp.cumsum(group_first) - 1) % 2
    later = (expert_ids.T > expert_ids) & (tiles[None, :] > 0)
    next_used = jnp.min(jnp.where(later, expert_ids.T, N_EXPERTS), axis=1)
    next_used = jnp.where(next_used == N_EXPERTS, -1, next_used)
    next_expert = jnp.sum(jnp.where(tile_expert[None, :] == expert_ids, next_used[:, None], 0), axis=0)
    table = jnp.concatenate([tile_expert, tile_valid, group_first, group_slot, next_expert])
    return pos, table.astype(jnp.int32), pad_lo, pad_hi, total.reshape(1)


def _dispatch_kernel(pos_ref, lo_ref, hi_ref, used_ref, hn_ref, xs_hbm, zeros, sem, *, n, n_tiles):
    tb = hn_ref.shape[0]
    step = pl.program_id(0)
    base = step * tb

    def issue(r, c):
        tok = base + r
        src = hn_ref.at[pl.ds(r, 1), :]
        pltpu.make_async_copy(src, xs_hbm.at[pl.ds(pos_ref[tok], 1), :], sem).start()
        pltpu.make_async_copy(src, xs_hbm.at[pl.ds(pos_ref[n + tok], 1), :], sem).start()
        return c

    lax.fori_loop(0, tb, issue, 0, unroll=8)
    for _ in range(TOP_K):
        pltpu.make_async_copy(hn_ref, xs_hbm.at[pl.ds(0, tb), :], sem).wait()

    @pl.when(step == pl.num_programs(0) - 1)
    def _():
        zeros[...] = jnp.zeros_like(zeros)

        def pad_copy(slot):
            return pltpu.make_async_copy(zeros.at[pl.ds(0, 1), :], xs_hbm.at[pl.ds(slot, 1), :], sem)

        def issue_pad(s, c):
            pad_copy(s).start()
            return c

        def drain_pad(s, c):
            pad_copy(s).wait()
            return c

        for e in range(N_EXPERTS):
            lax.fori_loop(lo_ref[e], hi_ref[e], issue_pad, 0)
            lax.fori_loop(lo_ref[e], hi_ref[e], drain_pad, 0)

        def tile_copy(t):
            return pltpu.make_async_copy(zeros, xs_hbm.at[pl.ds(t * MOE_TILE, MOE_TILE), :], sem)

        def issue_tile(t, c):
            tile_copy(t).start()
            return c

        def drain_tile(t, c):
            tile_copy(t).wait()
            return c

        lax.fori_loop(used_ref[0], n_tiles, issue_tile, 0)
        lax.fori_loop(used_ref[0], n_tiles, drain_tile, 0)


def moe_dispatch(hn, pos, pad_lo, pad_hi, used, tb=512):
    n, d = hn.shape
    tb = min(tb, n)
    nt = moe_num_tiles(n)
    grid_spec = pltpu.PrefetchScalarGridSpec(
        num_scalar_prefetch=4,
        grid=(n // tb,),
        in_specs=[pl.BlockSpec((tb, d), lambda i, *_: (i, 0))],
        out_specs=pl.BlockSpec(memory_space=pl.ANY),
        scratch_shapes=[pltpu.VMEM((MOE_TILE, d), F32), pltpu.SemaphoreType.DMA],
    )
    return pl.pallas_call(
        functools.partial(_dispatch_kernel, n=n, n_tiles=nt),
        grid_spec=grid_spec,
        out_shape=jax.ShapeDtypeStruct((nt * MOE_TILE, d), F32),
        compiler_params=pltpu.CompilerParams(dimension_semantics=("arbitrary",)),
        name="moe_dispatch",
    )(pos, pad_lo, pad_hi, used, hn)


def _expert_kernel(tab_ref, x_ref, wgu_hbm, wd_hbm, o_ref, wa_buf, wu_buf, wd_buf, sems):
    t = pl.program_id(0)
    nt = pl.num_programs(0)
    f = wd_buf.shape[1]

    def field(k, tile):
        return tab_ref[k * nt + tile]

    def weight_copies(e, slot):
        return (pltpu.make_async_copy(wgu_hbm.at[e, :, pl.ds(0, f)], wa_buf.at[slot], sems.at[slot]),
                pltpu.make_async_copy(wgu_hbm.at[e, :, pl.ds(f, f)], wu_buf.at[slot], sems.at[slot]),
                pltpu.make_async_copy(wd_hbm.at[e], wd_buf.at[slot], sems.at[slot]))

    slot = field(TILE_SLOT, t)

    @pl.when(t == 0)
    def _():
        for cp in weight_copies(field(TILE_EXPERT, 0), field(TILE_SLOT, 0)):
            cp.start()

    @pl.when(field(TILE_FIRST, t) == 1)
    def _():
        for cp in weight_copies(field(TILE_EXPERT, t), slot):
            cp.wait()
        nxt = field(TILE_NEXT, t)

        @pl.when(nxt >= 0)
        def _():
            for cp in weight_copies(nxt, 1 - slot):
                cp.start()

    valid = field(TILE_VALID, t) == 1

    @pl.when(valid)
    def _():
        x = x_ref[...].astype(BF16)
        a = _dot(x, wa_buf[slot])
        u = _dot(x, wu_buf[slot])
        o_ref[...] = _dot((_silu(a) * u).astype(BF16), wd_buf[slot])

    @pl.when(jnp.logical_not(valid))
    def _():
        o_ref[...] = jnp.zeros_like(o_ref)


def moe_experts(xs, w_gu, w_down, table):
    rows, d = xs.shape
    f = w_down.shape[1]
    nt = rows // MOE_TILE
    grid_spec = pltpu.PrefetchScalarGridSpec(
        num_scalar_prefetch=1,
        grid=(nt,),
        in_specs=[pl.BlockSpec((MOE_TILE, d), lambda t, tab: (t, 0)),
                  pl.BlockSpec(memory_space=pl.ANY),
                  pl.BlockSpec(memory_space=pl.ANY)],
        out_specs=pl.BlockSpec((MOE_TILE, d), lambda t, tab: (t, 0)),
        scratch_shapes=[pltpu.VMEM((2, d, f), BF16), pltpu.VMEM((2, d, f), BF16),
                        pltpu.VMEM((2, f, d), BF16), pltpu.SemaphoreType.DMA((2,))],
    )
    return pl.pallas_call(
        _expert_kernel,
        grid_spec=grid_spec,
        out_shape=jax.ShapeDtypeStruct((rows, d), F32),
        compiler_params=pltpu.CompilerParams(dimension_semantics=("arbitrary",),
                                             vmem_limit_bytes=56 * 1024 * 1024),
        name="moe_experts",
    )(table, xs, w_gu, w_down)


def _combine_kernel(pos_ref, x_ref, meta_ref, g_ref, y_hbm, *refs, n, emit_x, emit_norm):
    ybuf, sem = refs[-2], refs[-1]
    tc = x_ref.shape[0]
    step = pl.program_id(0)
    slot = step % 2

    def start_gather(s, into):
        def issue(r, c):
            tok = s * tc + r
            for k in range(TOP_K):
                pltpu.make_async_copy(y_hbm.at[pl.ds(pos_ref[k * n + tok], 1), :],
                                      ybuf.at[into, k, pl.ds(r, 1), :], sem.at[into]).start()
            return c

        lax.fori_loop(0, tc, issue, 0, unroll=8)

    @pl.when(step == 0)
    def _():
        start_gather(0, 0)

    @pl.when(step + 1 < pl.num_programs(0))
    def _():
        start_gather(step + 1, 1 - slot)

    for k in range(TOP_K):
        pltpu.make_async_copy(y_hbm.at[pl.ds(0, tc), :], ybuf.at[slot, k], sem.at[slot]).wait()

    meta = meta_ref[...]
    g1 = meta[:, META_G1:META_G1 + 1]
    g2 = meta[:, META_G2:META_G2 + 1]
    xn = x_ref[...] + g1 * ybuf[slot, 0] + g2 * ybuf[slot, 1]
    _emit_residual_outputs(xn, g_ref, refs, emit_x, emit_norm)


def moe_combine(x, y, meta, pos, g, *, emit_x=True, norm_dtype=None, tc=256):
    n, d = x.shape
    tc = min(tc, n)
    out_shape, out_specs = _residual_out_shapes(n, d, tc, emit_x, norm_dtype, lambda i, p: (i, 0))
    grid_spec = pltpu.PrefetchScalarGridSpec(
        num_scalar_prefetch=1,
        grid=(n // tc,),
        in_specs=[pl.BlockSpec((tc, d), lambda i, p: (i, 0)),
                  pl.BlockSpec((tc, LANES), lambda i, p: (i, 0)),
                  pl.BlockSpec((1, d), lambda i, p: (0, 0)),
                  pl.BlockSpec(memory_space=pl.ANY)],
        out_specs=out_specs,
        scratch_shapes=[pltpu.VMEM((2, TOP_K, tc, d), F32), pltpu.SemaphoreType.DMA((2,))],
    )
    return pl.pallas_call(
        functools.partial(_combine_kernel, n=n, emit_x=emit_x, emit_norm=norm_dtype is not None),
        grid_spec=grid_spec,
        out_shape=out_shape,
        compiler_params=pltpu.CompilerParams(dimension_semantics=("arbitrary",)),
        name="moe_combine",
    )(pos, x, meta, g.reshape(1, d), y)


def kernel(x, ret_w_in, ret_norm_g, ret_w_out, pool_w, pool_scale, ffn_w_gu, ffn_w_down,
           moe_router, moe_w_gu, moe_w_down, norm_mix_g, norm_ffn_g, norm_final_g):
    batch, seq, d = x.shape
    n = batch * seq
    xf = x.reshape(n, d)
    pool_w = pool_w.astype(BF16)
    n_moe, n_exp, _, f2 = moe_w_gu.shape
    f = moe_w_down.shape[2]
    moe_gu_rows = moe_w_gu.reshape(n_moe, n_exp * d, f2)
    moe_down_rows = moe_w_down.reshape(n_moe, n_exp * f, d)
    hn = rmsnorm(xf, norm_mix_g[0], BF16)
    for i in range(DEPTH):
        j = i // 2
        last = i == DEPTH - 1
        if i % 2 == 0:
            proj, moe_gu_b, w_out_b = retention_in_proj(hn, ret_w_in, j,
                                                        [(moe_gu_rows, j), (ret_w_out, j)])
            y = retention(proj, ret_norm_g, j, batch, seq)
            xf, xg, ssq = down_residual(y, w_out_b, xf, norm_ffn_g[i])
            hid, moe_down_b, ffn_down_b = swiglu_up(xg, ssq, ffn_w_gu, j,
                                                    [(moe_down_rows, j), (ffn_w_down, j)])
            (xf,) = down_residual(hid, ffn_down_b, xf)
            moe_gu_b = moe_gu_b.reshape(n_exp, d, f2)
            moe_down_b = moe_down_b.reshape(n_exp, f, d)
        else:
            xf, hn, meta, meta_t, counts = pool_router(xf, norm_mix_g[i], pool_w, j, pool_scale[j],
                                               norm_ffn_g[i], moe_router[j], seq)
            pos, tile_table, pad_lo, pad_hi, used = moe_schedule(meta_t, counts, n)
            xs = moe_dispatch(hn, pos, pad_lo, pad_hi, used)
            ys = moe_experts(xs, moe_gu_b, moe_down_b, tile_table)
            if last:
                (out,) = moe_combine(xf, ys, meta, pos, norm_final_g, emit_x=False, norm_dtype=F32)
            else:
                xf, hn = moe_combine(xf, ys, meta, pos, norm_mix_g[i + 1], norm_dtype=BF16)
    return out.reshape(batch, seq, d)
```

```python
import functools

import jax
import jax.numpy as jnp
from jax import lax
from jax.experimental import pallas as pl
from jax.experimental.pallas import tpu as pltpu

BF16 = jnp.bfloat16
F32 = jnp.float32

D_MODEL = 2048
DEPTH = 4
RET_HEADS = 8
RET_DK = D_MODEL // RET_HEADS
RET_DV = 2 * D_MODEL // RET_HEADS
RET_QK = RET_HEADS * RET_DK
RET_V = RET_HEADS * RET_DV
RET_CHUNK = 256
RET_PREP_ROWS = 32
ROPE_BASE = 10000.0
POOL_WINDOWS = (2, 4, 8, 16)
POOL_G = D_MODEL // len(POOL_WINDOWS)
POOL_HALO = 16
D_FF = 5632
N_EXPERTS = 8
TOP_K = 2
D_FF_EXPERT = 1408
NORM_EPS = 1e-6
LANES = 128
SUBLANES = 8
MOE_TILE = 256
SIDE_CAST_BLOCKS = 64
X_RING = 3
BF16_TILE_ROWS = 16
DOWN_VMEM_LIMIT = 60 * 1024 * 1024
META_E1, META_E2, META_R1, META_R2, META_G1, META_G2 = range(6)
TILE_EXPERT, TILE_VALID, TILE_FIRST, TILE_SLOT, TILE_NEXT = range(5)


def _rms(x, g):
    ms = jnp.mean(x * x, axis=-1, keepdims=True)
    return x * lax.rsqrt(ms + NORM_EPS) * g


def _silu(a):
    return a / (1.0 + jnp.exp(-a))


def _dot(a, b):
    return jnp.dot(a, b, preferred_element_type=F32)


def _rmsnorm_kernel(x_ref, g_ref, o_ref):
    o_ref[...] = _rms(x_ref[...], g_ref[...]).astype(o_ref.dtype)


def rmsnorm(x, g, out_dtype, tm=512):
    n, d = x.shape
    return pl.pallas_call(
        _rmsnorm_kernel,
        grid=(n // tm,),
        in_specs=[pl.BlockSpec((tm, d), lambda i: (i, 0)),
                  pl.BlockSpec((1, d), lambda i: (0, 0))],
        out_specs=pl.BlockSpec((tm, d), lambda i: (i, 0)),
        out_shape=jax.ShapeDtypeStruct((n, d), out_dtype),
        name="rmsnorm",
    )(x, g.reshape(1, d))


def _side_cast_specs(sides, steps, n_inner):
    in_specs, out_specs, shapes = [], [], []
    for side, side_layer in sides:
        _, rows, cols = side.shape
        n_blocks = 1
        while (n_blocks * 2 <= min(steps, SIDE_CAST_BLOCKS)
               and rows % (n_blocks * 2 * BF16_TILE_ROWS) == 0):
            n_blocks *= 2
        br = rows // n_blocks

        def index(j, i, n_blocks=n_blocks):
            return jnp.minimum(j * n_inner + i, n_blocks - 1)

        in_specs.append(pl.BlockSpec((None, br, cols),
                                     lambda j, i, l=side_layer, ix=index: (l, ix(j, i), 0)))
        out_specs.append(pl.BlockSpec((br, cols), lambda j, i, ix=index: (ix(j, i), 0)))
        shapes.append(jax.ShapeDtypeStruct((rows, cols), BF16))
    return in_specs, out_specs, shapes


def _cast_sides(refs, n_sides):
    for s_ref, so_ref in zip(refs[:n_sides], refs[n_sides + 1:2 * n_sides + 1]):
        so_ref[...] = s_ref[...].astype(BF16)


class _RowTileRing:
    def __init__(self, x_hbm, buf, sems, n_row_tiles):
        self.x_hbm, self.buf, self.sems, self.n_row_tiles = x_hbm, buf, sems, n_row_tiles

    def _copy(self, s):
        tm = self.buf.shape[1]
        rows = pl.ds(pl.multiple_of((s % self.n_row_tiles) * tm, tm), tm)
        return pltpu.make_async_copy(self.x_hbm.at[rows, :], self.buf.at[s % X_RING],
                                     self.sems.at[s % X_RING])

    def advance(self, step, total):
        @pl.when(step == 0)
        def _():
            for s in range(X_RING - 1):
                self._copy(s).start()

        @pl.when(step + X_RING - 1 < total)
        def _():
            self._copy(step + X_RING - 1).start()

    def wait(self, step):
        self._copy(step).wait()

    def tile(self, step):
        self.wait(step)
        return self.buf[step % X_RING]


def _ring_scratch(tm, k, dtype=BF16):
    return [pltpu.VMEM((X_RING, tm, k), dtype), pltpu.SemaphoreType.DMA((X_RING,))]


def _in_proj_kernel(x_hbm, w_ref, perm_ref, *refs, n_qk_slabs, n_sides):
    o_ref = refs[n_sides]
    wb_ref, xbuf, xsem = refs[-3:]
    ni = pl.num_programs(1)
    step = pl.program_id(0) * ni + pl.program_id(1)
    ring = _RowTileRing(x_hbm, xbuf, xsem, ni)
    ring.advance(step, pl.num_programs(0) * ni)

    @pl.when(pl.program_id(1) == 0)
    def _():
        is_qk = pl.program_id(0) < n_qk_slabs

        @pl.when(is_qk)
        def _():
            for lo in range(0, wb_ref.shape[1], RET_DK):
                head = w_ref[:, lo:lo + RET_DK].astype(BF16)
                wb_ref[:, lo:lo + RET_DK] = _dot(head, perm_ref[...]).astype(BF16)

        @pl.when(jnp.logical_not(is_qk))
        def _():
            wb_ref[...] = w_ref[...].astype(BF16)

    o_ref[...] = _dot(ring.tile(step), wb_ref[...]).astype(o_ref.dtype)
    _cast_sides(refs, n_sides)


def retention_in_proj(x, w, layer, sides, tm=1024, tn=1024):
    n, k = x.shape
    f = w.shape[2]
    tm = min(tm, n)
    nj, ni = f // tn, n // tm
    half = RET_DK // 2
    src = jnp.arange(RET_DK)
    dst = jnp.where(src % 2 == 0, src // 2, half + src // 2)
    perm = (dst[:, None] == jnp.arange(RET_DK)[None, :]).astype(BF16)
    s_in, s_out, s_shapes = _side_cast_specs(sides, nj * ni, ni)
    return pl.pallas_call(
        functools.partial(_in_proj_kernel, n_qk_slabs=2 * RET_QK // tn, n_sides=len(sides)),
        grid=(nj, ni),
        in_specs=[pl.BlockSpec(memory_space=pl.ANY),
                  pl.BlockSpec((None, k, tn), lambda j, i: (layer, 0, j)),
                  pl.BlockSpec((RET_DK, RET_DK), lambda j, i: (0, 0)),
                  *s_in],
        out_specs=[pl.BlockSpec((tm, tn), lambda j, i: (i, j)), *s_out],
        out_shape=[jax.ShapeDtypeStruct((n, f), BF16), *s_shapes],
        scratch_shapes=[pltpu.VMEM((k, tn), BF16), *_ring_scratch(tm, k)],
        compiler_params=pltpu.CompilerParams(dimension_semantics=("arbitrary", "arbitrary")),
        name="retention_in_proj",
    )(x, w, perm, *(side for side, _ in sides))


def _swiglu_kernel(x_hbm, ssq_ref, wa_ref, wu_ref, *refs, n_sides):
    o_ref = refs[n_sides]
    wab_ref, wub_ref, xbuf, xsem = refs[-4:]
    ni = pl.num_programs(1)
    step = pl.program_id(0) * ni + pl.program_id(1)
    ring = _RowTileRing(x_hbm, xbuf, xsem, ni)
    ring.advance(step, pl.num_programs(0) * ni)

    @pl.when(pl.program_id(1) == 0)
    def _():
        wab_ref[...] = wa_ref[...].astype(BF16)
        wub_ref[...] = wu_ref[...].astype(BF16)

    x = ring.tile(step)
    rowscale = lax.rsqrt(jnp.sum(ssq_ref[...], axis=0) / x.shape[1] + NORM_EPS)
    a = _dot(x, wab_ref[...]) * rowscale
    u = _dot(x, wub_ref[...]) * rowscale
    o_ref[...] = (_silu(a) * u).astype(o_ref.dtype)
    _cast_sides(refs, n_sides)


def swiglu_up(xg, ssq, w_gu, layer, sides, tm=1024, tn=512):
    n, k = xg.shape
    f = w_gu.shape[2] // 2
    tm = min(tm, n)
    nj, ni = f // tn, n // tm
    ns = ssq.shape[0]
    s_in, s_out, s_shapes = _side_cast_specs(sides, nj * ni, ni)
    return pl.pallas_call(
        functools.partial(_swiglu_kernel, n_sides=len(sides)),
        grid=(nj, ni),
        in_specs=[pl.BlockSpec(memory_space=pl.ANY),
                  pl.BlockSpec((ns, tm, 1), lambda j, i: (0, i, 0)),
                  pl.BlockSpec((None, k, tn), lambda j, i: (layer, 0, j)),
                  pl.BlockSpec((None, k, tn), lambda j, i: (layer, 0, nj + j)),
                  *s_in],
        out_specs=[pl.BlockSpec((tm, tn), lambda j, i: (i, j)), *s_out],
        out_shape=[jax.ShapeDtypeStruct((n, f), BF16), *s_shapes],
        scratch_shapes=[pltpu.VMEM((k, tn), BF16), pltpu.VMEM((k, tn), BF16),
                        *_ring_scratch(tm, k)],
        compiler_params=pltpu.CompilerParams(dimension_semantics=("arbitrary", "arbitrary")),
        name="swiglu_up",
    )(xg, ssq, w_gu, w_gu, *(side for side, _ in sides))


def _emit_residual_outputs(xn, g_ref, refs, emit_x, emit_norm):
    pos = 0
    if emit_x:
        refs[pos][...] = xn
        pos += 1
    if emit_norm:
        refs[pos][...] = _rms(xn, g_ref[...]).astype(refs[pos].dtype)


def _residual_out_shapes(n, d, tm, emit_x, norm_dtype, index_map):
    out_shape, out_specs = [], []
    if emit_x:
        out_shape.append(jax.ShapeDtypeStruct((n, d), F32))
        out_specs.append(pl.BlockSpec((tm, d), index_map))
    if norm_dtype is not None:
        out_shape.append(jax.ShapeDtypeStruct((n, d), norm_dtype))
        out_specs.append(pl.BlockSpec((tm, d), index_map))
    return out_shape, out_specs


def _down_kernel(x_ref, w_ref, r_ref, g_ref, *refs, emit_scaled):
    xn = r_ref[...] + _dot(x_ref[...], w_ref[...])
    refs[0][...] = xn
    if emit_scaled:
        refs[1][...] = (xn * g_ref[...]).astype(BF16)
        refs[2][...] = jnp.sum(xn * xn, axis=-1, keepdims=True)[None]


def down_residual(h, w, resid, g=None, *, tm=512):
    n, kk = h.shape
    d = w.shape[1]
    tm = min(tm, n)
    emit_scaled = g is not None
    if g is None:
        g = jnp.ones((d,), F32)
    out_shape = [jax.ShapeDtypeStruct((n, d), F32)]
    out_specs = [pl.BlockSpec((tm, d), lambda i: (i, 0))]
    if emit_scaled:
        out_shape += [jax.ShapeDtypeStruct((n, d), BF16), jax.ShapeDtypeStruct((1, n, 1), F32)]
        out_specs += [pl.BlockSpec((tm, d), lambda i: (i, 0)),
                      pl.BlockSpec((1, tm, 1), lambda i: (0, i, 0))]
    return pl.pallas_call(
        functools.partial(_down_kernel, emit_scaled=emit_scaled),
        grid=(n // tm,),
        in_specs=[pl.BlockSpec((tm, kk), lambda i: (i, 0)),
                  pl.BlockSpec((kk, d), lambda i: (0, 0), pipeline_mode=pl.Buffered(1)),
                  pl.BlockSpec((tm, d), lambda i: (i, 0)),
                  pl.BlockSpec((1, d), lambda i: (0, 0))],
        out_specs=out_specs,
        out_shape=out_shape,
        compiler_params=pltpu.CompilerParams(dimension_semantics=("arbitrary",),
                                             vmem_limit_bytes=DOWN_VMEM_LIMIT),
        name="down_residual",
    )(h, w, resid, g.reshape(1, d))


def _retention_kernel(lg_ref, q_ref, k_ref, v_ref, g_ref, cos_ref, sin_ref, gain_ref, o_ref,
                      qr_ref, qd_ref, kr_ref, kd_ref, mask_ref, qdec_ref, kdec_ref, oraw_ref,
                      state_ref, sb_ref, *, n_chunks, n_pairs):
    c = RET_CHUNK
    step = pl.program_id(0)
    slot_prep = step % 2
    slot_recur = 1 - slot_prep
    slot_finish = slot_prep

    @pl.when(step == 0)
    def _():
        for ref in (qr_ref, qd_ref, kr_ref, kd_ref, oraw_ref):
            ref[...] = jnp.zeros_like(ref)

    lg_prep = lg_ref[jnp.minimum(step, n_pairs - 1) % RET_HEADS]
    lg = lg_ref[jnp.clip(step - 1, 0, n_pairs - 1) % RET_HEADS]
    ii = lax.broadcasted_iota(jnp.int32, (c, c), 0)
    jj = lax.broadcasted_iota(jnp.int32, (c, c), 1)
    rel = (ii - jj).astype(F32)
    mask_ref[...] = jnp.where(rel >= 0.0, jnp.exp(jnp.maximum(rel, 0.0) * lg), 0.0)
    idx = lax.broadcasted_iota(jnp.int32, (c, 1), 0).astype(F32)
    k_scale = RET_DK ** -0.5
    qdec_ref[...] = jnp.broadcast_to(jnp.exp((idx + 1.0) * lg_prep), (c, RET_DK))
    kdec_ref[...] = jnp.broadcast_to(jnp.exp((c - 1.0 - idx) * lg_prep) * k_scale, (c, RET_DK))
    chunk_decay = jnp.exp(jnp.full((1, 1), float(c), F32) * lg)
    half = RET_DK // 2

    def rotary(xb, cos, sin):
        x = xb.astype(F32)
        x1, x2 = x[:, :half], x[:, half:]
        return jnp.concatenate([x1 * cos - x2 * sin, x2 * cos + x1 * sin], axis=1)

    def rows(ci):
        return pl.ds(pl.multiple_of(ci * c, c), c)

    state_ref[...] = jnp.zeros_like(state_ref)
    sb_ref[...] = jnp.zeros_like(sb_ref)

    def body(ci, carry):
        r = rows(ci)

        o = oraw_ref[slot_finish, r, :]
        ms = jnp.mean(o * o, axis=-1, keepdims=True)
        o = o * lax.rsqrt(ms + NORM_EPS) * gain_ref[...]
        gate = g_ref[r, :].astype(F32)
        o_ref[r, :] = (o * _silu(gate)).astype(o_ref.dtype)

        v = v_ref[r, :]
        scores = lax.dot_general(qr_ref[slot_recur, r, :], kr_ref[slot_recur, r, :],
                                 (((1,), (1,)), ((), ())),
                                 preferred_element_type=F32) * mask_ref[...]
        oraw_ref[slot_recur, r, :] = (_dot(scores.astype(BF16), v)
                                      + _dot(qd_ref[slot_recur, r, :], sb_ref[...]))
        upd = lax.dot_general(kd_ref[slot_recur, r, :], v, (((0,), (0,)), ((), ())),
                              preferred_element_type=F32)
        state = state_ref[...] * chunk_decay + upd
        state_ref[...] = state
        sb_ref[...] = state.astype(BF16)

        for lo in range(0, c, RET_PREP_ROWS):
            rr = pl.ds(pl.multiple_of(ci * c + lo, RET_PREP_ROWS), RET_PREP_ROWS)
            dec = pl.ds(lo, RET_PREP_ROWS)
            cos = cos_ref[rr, :]
            sin = sin_ref[rr, :]
            q = rotary(q_ref[rr, :], cos, sin)
            k = rotary(k_ref[rr, :], cos, sin)
            qr_ref[slot_prep, rr, :] = q.astype(BF16)
            qd_ref[slot_prep, rr, :] = (q * qdec_ref[dec, :]).astype(BF16)
            kr_ref[slot_prep, rr, :] = (k * k_scale).astype(BF16)
            kd_ref[slot_prep, rr, :] = (k * kdec_ref[dec, :]).astype(BF16)
        return carry

    lax.fori_loop(0, n_chunks, body, 0, unroll=2)


def retention(proj, norm_g, layer, batch, seq):
    n = batch * seq
    h = RET_HEADS
    log_g = jnp.log1p(-jnp.exp2(-5.0 - jnp.arange(h, dtype=F32)))
    freq = 1.0 / (ROPE_BASE ** jnp.linspace(0.0, 1.0, RET_DK // 2, dtype=F32))
    ang = jnp.arange(seq).astype(F32)[:, None] * freq[None, :]
    cos = jnp.cos(ang)
    sin = jnp.sin(ang)
    kq = RET_QK // RET_DK
    kv = 2 * RET_QK // RET_DV
    kg = (2 * RET_QK + RET_V) // RET_DV
    n_pairs = batch * h
    stages = 3

    def pair(step, stage):
        p = jnp.clip(step - stage, 0, n_pairs - 1)
        return p // h, p % h

    def block(stage, col0):
        def index(s):
            b, hh = pair(s, stage)
            return b, col0 + hh
        return index

    def gain_index(s):
        return layer, 0, pair(s, 2)[1]

    return pl.pallas_call(
        functools.partial(_retention_kernel, n_chunks=seq // RET_CHUNK, n_pairs=n_pairs),
        grid=(n_pairs + stages - 1,),
        in_specs=[pl.BlockSpec(memory_space=pltpu.SMEM),
                  pl.BlockSpec((seq, RET_DK), block(0, 0)),
                  pl.BlockSpec((seq, RET_DK), block(0, kq)),
                  pl.BlockSpec((seq, RET_DV), block(1, kv)),
                  pl.BlockSpec((seq, RET_DV), block(2, kg)),
                  pl.BlockSpec((seq, RET_DK // 2), lambda s: (0, 0)),
                  pl.BlockSpec((seq, RET_DK // 2), lambda s: (0, 0)),
                  pl.BlockSpec((None, 1, RET_DV), gain_index)],
        out_specs=pl.BlockSpec((seq, RET_DV), block(2, 0)),
        out_shape=jax.ShapeDtypeStruct((n, RET_V), BF16),
        scratch_shapes=[pltpu.VMEM((2, seq, RET_DK), BF16),
                        pltpu.VMEM((2, seq, RET_DK), BF16),
                        pltpu.VMEM((2, seq, RET_DK), BF16),
                        pltpu.VMEM((2, seq, RET_DK), BF16),
                        pltpu.VMEM((RET_CHUNK, RET_CHUNK), F32),
                        pltpu.VMEM((RET_CHUNK, RET_DK), F32),
                        pltpu.VMEM((RET_CHUNK, RET_DK), F32),
                        pltpu.VMEM((2, seq, RET_DV), F32),
                        pltpu.VMEM((RET_DK, RET_DV), F32),
                        pltpu.VMEM((RET_DK, RET_DV), BF16)],
        compiler_params=pltpu.CompilerParams(dimension_semantics=("arbitrary",)),
        name="retention",
    )(log_g, proj, proj, proj, proj, cos, sin, norm_g.reshape(norm_g.shape[0], 1, RET_V))


def _pool_kernel(x_ref, gm_ref, pw_ref, sc_ref, gf_ref, rt_ref,
                 xo_ref, hn_ref, meta_ref, meta_t_ref, cnt_out_ref, hbuf, cnt_ref, *, tiles_per_seq):
    tm = x_ref.shape[0]
    step = pl.program_id(0)
    ti = step % tiles_per_seq
    x = x_ref[...]
    hn = _rms(x, gm_ref[...])

    @pl.when(ti == 0)
    def _():
        hbuf[0:POOL_HALO, :] = jnp.zeros((POOL_HALO, D_MODEL), F32)

    @pl.when(ti != 0)
    def _():
        hbuf[0:POOL_HALO, :] = hbuf[tm:tm + POOL_HALO, :]

    hbuf[POOL_HALO:POOL_HALO + tm, :] = hn

    t = ti * tm + lax.broadcasted_iota(jnp.int32, (tm, 1), 0)
    for gi, w in enumerate(POOL_WINDOWS):
        lo, hi = gi * POOL_G, (gi + 1) * POOL_G
        cur = hn[:, lo:hi]
        acc = hbuf[:, lo:hi]
        shift = 1
        while shift < w:
            acc = acc + pltpu.roll(acc, shift, 0)
            shift *= 2
        acc = acc[POOL_HALO:, :]
        inv = 1.0 / jnp.minimum(t + 1, w).astype(F32)
        mix = (acc * inv - cur).astype(BF16)
        o = _dot(mix, pw_ref[gi]) * sc_ref[:, lo:hi]
        xo_ref[:, lo:hi] = x[:, lo:hi] + o

    hn2 = _rms(xo_ref[...], gf_ref[...])
    hn_ref[...] = hn2

    rt = rt_ref[...]
    hn_hi = hn2.astype(BF16)
    hn_lo = (hn2 - hn_hi.astype(F32)).astype(BF16)
    rt_hi = rt.astype(BF16)
    rt_lo = (rt - rt_hi.astype(F32)).astype(BF16)
    logits = _dot(hn_hi, rt_hi) + (_dot(hn_hi, rt_lo) + _dot(hn_lo, rt_hi))
    lt = logits.T[:N_EXPERTS, :]
    expert = lax.broadcasted_iota(jnp.int32, lt.shape, 0)
    neg = jnp.float32(-jnp.inf)
    m1 = jnp.max(lt, axis=0, keepdims=True)
    i1 = jnp.min(jnp.where(lt == m1, expert, N_EXPERTS), axis=0, keepdims=True)
    lt2 = jnp.where(expert == i1, neg, lt)
    m2 = jnp.max(lt2, axis=0, keepdims=True)
    i2 = jnp.min(jnp.where(lt2 == m2, expert, N_EXPERTS), axis=0, keepdims=True)
    e2 = jnp.exp(m2 - m1)
    g1 = 1.0 / (1.0 + e2)
    g2 = e2 * g1

    @pl.when(step == 0)
    def _():
        cnt_ref[...] = jnp.zeros_like(cnt_ref)

    chosen = jnp.where((expert == i1) | (expert == i2), 1.0, 0.0)
    src = lax.broadcasted_iota(jnp.int32, (tm, tm), 0)
    dst = lax.broadcasted_iota(jnp.int32, (tm, tm), 1)
    earlier = jnp.where(src < dst, 1.0, 0.0).astype(BF16)
    before = _dot(chosen.astype(BF16), earlier) + cnt_ref[:, 0:1]
    r1 = jnp.sum(jnp.where(expert == i1, before, 0.0), axis=0, keepdims=True)
    r2 = jnp.sum(jnp.where(expert == i2, before, 0.0), axis=0, keepdims=True)
    cnt_ref[...] += jnp.sum(chosen, axis=1, keepdims=True)
    cnt_out_ref[...] = cnt_ref[...]

    fields = {META_E1: i1.astype(F32), META_E2: i2.astype(F32), META_R1: r1, META_R2: r2,
              META_G1: g1, META_G2: g2}
    zero_row = jnp.zeros_like(g1)
    meta_t = jnp.concatenate([fields.get(row, zero_row) for row in range(SUBLANES)], axis=0)
    meta_t_ref[...] = meta_t
    meta_ref[...] = jnp.concatenate([meta_t, jnp.zeros((LANES - SUBLANES, tm), F32)], axis=0).T


def pool_router(x, g_mix, pool_w, layer, pool_scale, g_ffn, router, seq, tm=512):
    n, d = x.shape
    tm = min(tm, seq)
    router_p = jnp.zeros((d, LANES), F32).at[:, :N_EXPERTS].set(router)
    ng = len(POOL_WINDOWS)
    return pl.pallas_call(
        functools.partial(_pool_kernel, tiles_per_seq=seq // tm),
        grid=(n // tm,),
        in_specs=[pl.BlockSpec((tm, d), lambda i: (i, 0)),
                  pl.BlockSpec((1, d), lambda i: (0, 0)),
                  pl.BlockSpec((None, ng, POOL_G, POOL_G), lambda i: (layer, 0, 0, 0)),
                  pl.BlockSpec((1, d), lambda i: (0, 0)),
                  pl.BlockSpec((1, d), lambda i: (0, 0)),
                  pl.BlockSpec((d, LANES), lambda i: (0, 0))],
        out_specs=[pl.BlockSpec((tm, d), lambda i: (i, 0)),
                   pl.BlockSpec((tm, d), lambda i: (i, 0)),
                   pl.BlockSpec((tm, LANES), lambda i: (i, 0)),
                   pl.BlockSpec((SUBLANES, tm), lambda i: (0, i)),
                   pl.BlockSpec((SUBLANES, LANES), lambda i: (0, 0))],
        out_shape=[jax.ShapeDtypeStruct((n, d), F32),
                   jax.ShapeDtypeStruct((n, d), F32),
                   jax.ShapeDtypeStruct((n, LANES), F32),
                   jax.ShapeDtypeStruct((SUBLANES, n), F32),
                   jax.ShapeDtypeStruct((SUBLANES, LANES), F32)],
        scratch_shapes=[pltpu.VMEM((POOL_HALO + tm, d), F32),
                        pltpu.VMEM((N_EXPERTS, LANES), F32)],
        compiler_params=pltpu.CompilerParams(dimension_semantics=("arbitrary",)),
        name="pool_router",
    )(x, g_mix.reshape(1, d), pool_w, pool_scale.reshape(1, d), g_ffn.reshape(1, d), router_p)


def moe_num_tiles(n):
    return TOP_K * n // MOE_TILE + N_EXPERTS


def moe_schedule(meta_t, counts_f, n):
    nt = moe_num_tiles(n)
    counts = counts_f[:N_EXPERTS, 0].astype(jnp.int32)
    tiles = (counts + MOE_TILE - 1) // MOE_TILE
    tile_end = jnp.cumsum(tiles)
    starts = (tile_end - tiles) * MOE_TILE
    expert_ids = jnp.arange(N_EXPERTS, dtype=jnp.int32)[:, None]

    def slots(e_row, r_row):
        e = meta_t[e_row].astype(jnp.int32)
        first = jnp.sum(jnp.where(e[None, :] == expert_ids, starts[:, None], 0), axis=0)
        return first + meta_t[r_row].astype(jnp.int32)

    pos = jnp.concatenate([slots(META_E1, META_R1), slots(META_E2, META_R2)])
    total = tile_end[-1]
    t = jnp.arange(nt, dtype=jnp.int32)
    t_used = jnp.minimum(t, total - 1)
    tile_expert = jnp.sum((tile_end[None, :] <= t_used[:, None]).astype(jnp.int32), axis=1)
    tile_expert = jnp.minimum(tile_expert, N_EXPERTS - 1)
    tile_valid = (t < total).astype(jnp.int32)
    pad_lo = starts + counts
    pad_hi = starts + tiles * MOE_TILE
    prev_expert = jnp.concatenate([jnp.full((1,), -1, jnp.int32), tile_expert[:-1]])
    group_first = (tile_expert != prev_expert).astype(jnp.int32)
    group_slot = (jnp.cumsum(group_first) - 1) % 2
    later = (expert_ids.T > expert_ids) & (tiles[None, :] > 0)
    next_used = jnp.min(jnp.where(later, expert_ids.T, N_EXPERTS), axis=1)
    next_used = jnp.where(next_used == N_EXPERTS, -1, next_used)
    next_expert = jnp.sum(jnp.where(tile_expert[None, :] == expert_ids, next_used[:, None], 0), axis=0)
    table = jnp.concatenate([tile_expert, tile_valid, group_first, group_slot, next_expert])
    return pos, table.astype(jnp.int32), pad_lo, pad_hi, total.reshape(1)


def _dispatch_kernel(pos_ref, lo_ref, hi_ref, used_ref, hn_ref, xs_hbm, zeros, sem, *, n, n_tiles):
    tb = hn_ref.shape[0]
    step = pl.program_id(0)
    base = step * tb

    def issue(r, c):
        tok = base + r
        src = hn_ref.at[pl.ds(r, 1), :]
        pltpu.make_async_copy(src, xs_hbm.at[pl.ds(pos_ref[tok], 1), :], sem).start()
        pltpu.make_async_copy(src, xs_hbm.at[pl.ds(pos_ref[n + tok], 1), :], sem).start()
        return c

    lax.fori_loop(0, tb, issue, 0, unroll=8)
    for _ in range(TOP_K):
        pltpu.make_async_copy(hn_ref, xs_hbm.at[pl.ds(0, tb), :], sem).wait()

    @pl.when(step == pl.num_programs(0) - 1)
    def _():
        zeros[...] = jnp.zeros_like(zeros)

        def pad_copy(slot):
            return pltpu.make_async_copy(zeros.at[pl.ds(0, 1), :], xs_hbm.at[pl.ds(slot, 1), :], sem)

        def issue_pad(s, c):
            pad_copy(s).start()
            return c

        def drain_pad(s, c):
            pad_copy(s).wait()
            return c

        for e in range(N_EXPERTS):
            lax.fori_loop(lo_ref[e], hi_ref[e], issue_pad, 0)
            lax.fori_loop(lo_ref[e], hi_ref[e], drain_pad, 0)

        def tile_copy(t):
            return pltpu.make_async_copy(zeros, xs_hbm.at[pl.ds(t * MOE_TILE, MOE_TILE), :], sem)

        def issue_tile(t, c):
            tile_copy(t).start()
            return c

        def drain_tile(t, c):
            tile_copy(t).wait()
            return c

        lax.fori_loop(used_ref[0], n_tiles, issue_tile, 0)
        lax.fori_loop(used_ref[0], n_tiles, drain_tile, 0)


def moe_dispatch(hn, pos, pad_lo, pad_hi, used, tb=512):
    n, d = hn.shape
    tb = min(tb, n)
    nt = moe_num_tiles(n)
    grid_spec = pltpu.PrefetchScalarGridSpec(
        num_scalar_prefetch=4,
        grid=(n // tb,),
        in_specs=[pl.BlockSpec((tb, d), lambda i, *_: (i, 0))],
        out_specs=pl.BlockSpec(memory_space=pl.ANY),
        scratch_shapes=[pltpu.VMEM((MOE_TILE, d), F32), pltpu.SemaphoreType.DMA],
    )
    return pl.pallas_call(
        functools.partial(_dispatch_kernel, n=n, n_tiles=nt),
        grid_spec=grid_spec,
        out_shape=jax.ShapeDtypeStruct((nt * MOE_TILE, d), F32),
        compiler_params=pltpu.CompilerParams(dimension_semantics=("arbitrary",)),
        name="moe_dispatch",
    )(pos, pad_lo, pad_hi, used, hn)


def _expert_kernel(tab_ref, x_hbm, wgu_hbm, wd_hbm, o_ref, wa_buf, wu_buf, wd_buf, sems, xbuf, xsem):
    t = pl.program_id(0)
    nt = pl.num_programs(0)
    f = wd_buf.shape[1]
    ring = _RowTileRing(x_hbm, xbuf, xsem, nt)
    ring.advance(t, nt)

    def field(k, tile):
        return tab_ref[k * nt + tile]

    def weight_copies(e, slot):
        return (pltpu.make_async_copy(wgu_hbm.at[e, :, pl.ds(0, f)], wa_buf.at[slot], sems.at[slot]),
                pltpu.make_async_copy(wgu_hbm.at[e, :, pl.ds(f, f)], wu_buf.at[slot], sems.at[slot]),
                pltpu.make_async_copy(wd_hbm.at[e], wd_buf.at[slot], sems.at[slot]))

    slot = field(TILE_SLOT, t)

    @pl.when(t == 0)
    def _():
        for cp in weight_copies(field(TILE_EXPERT, 0), field(TILE_SLOT, 0)):
            cp.start()

    @pl.when(field(TILE_FIRST, t) == 1)
    def _():
        for cp in weight_copies(field(TILE_EXPERT, t), slot):
            cp.wait()
        nxt = field(TILE_NEXT, t)

        @pl.when(nxt >= 0)
        def _():
            for cp in weight_copies(nxt, 1 - slot):
                cp.start()

    valid = field(TILE_VALID, t) == 1
    ring.wait(t)

    @pl.when(valid)
    def _():
        x = xbuf[t % X_RING].astype(BF16)
        a = _dot(x, wa_buf[slot])
        u = _dot(x, wu_buf[slot])
        o_ref[...] = _dot((_silu(a) * u).astype(BF16), wd_buf[slot])

    @pl.when(jnp.logical_not(valid))
    def _():
        o_ref[...] = jnp.zeros_like(o_ref)


def moe_experts(xs, w_gu, w_down, table):
    rows, d = xs.shape
    f = w_down.shape[1]
    nt = rows // MOE_TILE
    grid_spec = pltpu.PrefetchScalarGridSpec(
        num_scalar_prefetch=1,
        grid=(nt,),
        in_specs=[pl.BlockSpec(memory_space=pl.ANY),
                  pl.BlockSpec(memory_space=pl.ANY),
                  pl.BlockSpec(memory_space=pl.ANY)],
        out_specs=pl.BlockSpec((MOE_TILE, d), lambda t, tab: (t, 0)),
        scratch_shapes=[pltpu.VMEM((2, d, f), BF16), pltpu.VMEM((2, d, f), BF16),
                        pltpu.VMEM((2, f, d), BF16), pltpu.SemaphoreType.DMA((2,)),
                        *_ring_scratch(MOE_TILE, d, F32)],
    )
    return pl.pallas_call(
        _expert_kernel,
        grid_spec=grid_spec,
        out_shape=jax.ShapeDtypeStruct((rows, d), F32),
        compiler_params=pltpu.CompilerParams(dimension_semantics=("arbitrary",),
                                             vmem_limit_bytes=56 * 1024 * 1024),
        name="moe_experts",
    )(table, xs, w_gu, w_down)


def _combine_kernel(pos_ref, x_ref, meta_ref, g_ref, y_hbm, *refs, n, emit_x, emit_norm):
    ybuf, sem = refs[-2], refs[-1]
    tc = x_ref.shape[0]
    step = pl.program_id(0)
    slot = step % 2

    def start_gather(s, into):
        def issue(r, c):
            tok = s * tc + r
            for k in range(TOP_K):
                pltpu.make_async_copy(y_hbm.at[pl.ds(pos_ref[k * n + tok], 1), :],
                                      ybuf.at[into, k, pl.ds(r, 1), :], sem.at[into]).start()
            return c

        lax.fori_loop(0, tc, issue, 0, unroll=8)

    @pl.when(step == 0)
    def _():
        start_gather(0, 0)

    @pl.when(step + 1 < pl.num_programs(0))
    def _():
        start_gather(step + 1, 1 - slot)

    for k in range(TOP_K):
        pltpu.make_async_copy(y_hbm.at[pl.ds(0, tc), :], ybuf.at[slot, k], sem.at[slot]).wait()

    meta = meta_ref[...]
    g1 = meta[:, META_G1:META_G1 + 1]
    g2 = meta[:, META_G2:META_G2 + 1]
    xn = x_ref[...] + g1 * ybuf[slot, 0] + g2 * ybuf[slot, 1]
    _emit_residual_outputs(xn, g_ref, refs, emit_x, emit_norm)


def moe_combine(x, y, meta, pos, g, *, emit_x=True, norm_dtype=None, tc=256):
    n, d = x.shape
    tc = min(tc, n)
    out_shape, out_specs = _residual_out_shapes(n, d, tc, emit_x, norm_dtype, lambda i, p: (i, 0))
    grid_spec = pltpu.PrefetchScalarGridSpec(
        num_scalar_prefetch=1,
        grid=(n // tc,),
        in_specs=[pl.BlockSpec((tc, d), lambda i, p: (i, 0)),
                  pl.BlockSpec((tc, LANES), lambda i, p: (i, 0)),
                  pl.BlockSpec((1, d), lambda i, p: (0, 0)),
                  pl.BlockSpec(memory_space=pl.ANY)],
        out_specs=out_specs,
        scratch_shapes=[pltpu.VMEM((2, TOP_K, tc, d), F32), pltpu.SemaphoreType.DMA((2,))],
    )
    return pl.pallas_call(
        functools.partial(_combine_kernel, n=n, emit_x=emit_x, emit_norm=norm_dtype is not None),
        grid_spec=grid_spec,
        out_shape=out_shape,
        compiler_params=pltpu.CompilerParams(dimension_semantics=("arbitrary",)),
        name="moe_combine",
    )(pos, x, meta, g.reshape(1, d), y)


def kernel(x, ret_w_in, ret_norm_g, ret_w_out, pool_w, pool_scale, ffn_w_gu, ffn_w_down,
           moe_router, moe_w_gu, moe_w_down, norm_mix_g, norm_ffn_g, norm_final_g):
    batch, seq, d = x.shape
    n = batch * seq
    xf = x.reshape(n, d)
    pool_w = pool_w.astype(BF16)
    n_moe, n_exp, _, f2 = moe_w_gu.shape
    f = moe_w_down.shape[2]
    moe_gu_rows = moe_w_gu.reshape(n_moe, n_exp * d, f2)
    moe_down_rows = moe_w_down.reshape(n_moe, n_exp * f, d)
    hn = rmsnorm(xf, norm_mix_g[0], BF16)
    for i in range(DEPTH):
        j = i // 2
        last = i == DEPTH - 1
        if i % 2 == 0:
            proj, moe_gu_b, w_out_b = retention_in_proj(hn, ret_w_in, j,
                                                        [(moe_gu_rows, j), (ret_w_out, j)])
            y = retention(proj, ret_norm_g, j, batch, seq)
            xf, xg, ssq = down_residual(y, w_out_b, xf, norm_ffn_g[i])
            hid, moe_down_b, ffn_down_b = swiglu_up(xg, ssq, ffn_w_gu, j,
                                                    [(moe_down_rows, j), (ffn_w_down, j)])
            (xf,) = down_residual(hid, ffn_down_b, xf)
            moe_gu_b = moe_gu_b.reshape(n_exp, d, f2)
            moe_down_b = moe_down_b.reshape(n_exp, f, d)
        else:
            xf, hn, meta, meta_t, counts = pool_router(xf, norm_mix_g[i], pool_w, j, pool_scale[j],
                                               norm_ffn_g[i], moe_router[j], seq)
            pos, tile_table, pad_lo, pad_hi, used = moe_schedule(meta_t, counts, n)
            xs = moe_dispatch(hn, pos, pad_lo, pad_hi, used)
            ys = moe_experts(xs, moe_gu_b, moe_down_b, tile_table)
            if last:
                (out,) = moe_combine(xf, ys, meta, pos, norm_final_g, emit_x=False, norm_dtype=F32)
            else:
                xf, hn = moe_combine(xf, ys, meta, pos, norm_mix_g[i + 1], norm_dtype=BF16)
    return out.reshape(batch, seq, d)
```

```python
import functools

import jax
import jax.numpy as jnp
from jax import lax
from jax.experimental import pallas as pl
from jax.experimental.pallas import tpu as pltpu

BF16 = jnp.bfloat16
F32 = jnp.float32

D_MODEL = 2048
DEPTH = 4
RET_HEADS = 8
RET_DK = D_MODEL // RET_HEADS
RET_DV = 2 * D_MODEL // RET_HEADS
RET_QK = RET_HEADS * RET_DK
RET_V = RET_HEADS * RET_DV
RET_CHUNK = 256
RET_PREP_ROWS = 32
ROPE_BASE = 10000.0
POOL_WINDOWS = (2, 4, 8, 16)
POOL_G = D_MODEL // len(POOL_WINDOWS)
POOL_HALO = 16
D_FF = 5632
N_EXPERTS = 8
TOP_K = 2
D_FF_EXPERT = 1408
NORM_EPS = 1e-6
LANES = 128
SUBLANES = 8
MOE_TILE = 256
SIDE_CAST_BLOCKS = 64
X_RING = 3
BF16_TILE_ROWS = 16
DOWN_VMEM_LIMIT = 60 * 1024 * 1024
META_E1, META_E2, META_R1, META_R2, META_G1, META_G2 = range(6)
TILE_EXPERT, TILE_VALID, TILE_FIRST, TILE_SLOT, TILE_NEXT = range(5)


def _rms(x, g):
    ms = jnp.mean(x * x, axis=-1, keepdims=True)
    return x * lax.rsqrt(ms + NORM_EPS) * g


def _silu(a):
    return a / (1.0 + jnp.exp(-a))


def _dot(a, b):
    return jnp.dot(a, b, preferred_element_type=F32)


def _rmsnorm_kernel(x_ref, g_ref, o_ref):
    o_ref[...] = _rms(x_ref[...], g_ref[...]).astype(o_ref.dtype)


def rmsnorm(x, g, out_dtype, tm=512):
    n, d = x.shape
    return pl.pallas_call(
        _rmsnorm_kernel,
        grid=(n // tm,),
        in_specs=[pl.BlockSpec((tm, d), lambda i: (i, 0)),
                  pl.BlockSpec((1, d), lambda i: (0, 0))],
        out_specs=pl.BlockSpec((tm, d), lambda i: (i, 0)),
        out_shape=jax.ShapeDtypeStruct((n, d), out_dtype),
        name="rmsnorm",
    )(x, g.reshape(1, d))


def _side_cast_specs(sides, steps, n_inner):
    in_specs, out_specs, shapes = [], [], []
    for side, side_layer in sides:
        _, rows, cols = side.shape
        n_blocks = 1
        while (n_blocks * 2 <= min(steps, SIDE_CAST_BLOCKS)
               and rows % (n_blocks * 2 * BF16_TILE_ROWS) == 0):
            n_blocks *= 2
        br = rows // n_blocks

        def index(j, i, n_blocks=n_blocks):
            return jnp.minimum(j * n_inner + i, n_blocks - 1)

        in_specs.append(pl.BlockSpec((None, br, cols),
                                     lambda j, i, l=side_layer, ix=index: (l, ix(j, i), 0)))
        out_specs.append(pl.BlockSpec((br, cols), lambda j, i, ix=index: (ix(j, i), 0)))
        shapes.append(jax.ShapeDtypeStruct((rows, cols), BF16))
    return in_specs, out_specs, shapes


def _cast_sides(refs, n_sides):
    for s_ref, so_ref in zip(refs[:n_sides], refs[n_sides + 1:2 * n_sides + 1]):
        so_ref[...] = s_ref[...].astype(BF16)


class _RowTileRing:
    def __init__(self, x_hbm, buf, sems, n_row_tiles):
        self.x_hbm, self.buf, self.sems, self.n_row_tiles = x_hbm, buf, sems, n_row_tiles

    def _copy(self, s):
        tm = self.buf.shape[1]
        rows = pl.ds(pl.multiple_of((s % self.n_row_tiles) * tm, tm), tm)
        return pltpu.make_async_copy(self.x_hbm.at[rows, :], self.buf.at[s % X_RING],
                                     self.sems.at[s % X_RING])

    def advance(self, step, total):
        @pl.when(step == 0)
        def _():
            for s in range(X_RING - 1):
                self._copy(s).start()

        @pl.when(step + X_RING - 1 < total)
        def _():
            self._copy(step + X_RING - 1).start()

    def tile(self, step):
        self._copy(step).wait()
        return self.buf[step % X_RING]


def _ring_scratch(tm, k):
    return [pltpu.VMEM((X_RING, tm, k), BF16), pltpu.SemaphoreType.DMA((X_RING,))]


def _in_proj_kernel(x_hbm, w_ref, perm_ref, *refs, n_qk_slabs, n_sides):
    o_ref = refs[n_sides]
    wb_ref, xbuf, xsem = refs[-3:]
    ni = pl.num_programs(1)
    step = pl.program_id(0) * ni + pl.program_id(1)
    ring = _RowTileRing(x_hbm, xbuf, xsem, ni)
    ring.advance(step, pl.num_programs(0) * ni)

    @pl.when(pl.program_id(1) == 0)
    def _():
        is_qk = pl.program_id(0) < n_qk_slabs

        @pl.when(is_qk)
        def _():
            for lo in range(0, wb_ref.shape[1], RET_DK):
                head = w_ref[:, lo:lo + RET_DK].astype(BF16)
                wb_ref[:, lo:lo + RET_DK] = _dot(head, perm_ref[...]).astype(BF16)

        @pl.when(jnp.logical_not(is_qk))
        def _():
            wb_ref[...] = w_ref[...].astype(BF16)

    o_ref[...] = _dot(ring.tile(step), wb_ref[...]).astype(o_ref.dtype)
    _cast_sides(refs, n_sides)


def retention_in_proj(x, w, layer, sides, tm=1024, tn=1024):
    n, k = x.shape
    f = w.shape[2]
    tm = min(tm, n)
    nj, ni = f // tn, n // tm
    half = RET_DK // 2
    src = jnp.arange(RET_DK)
    dst = jnp.where(src % 2 == 0, src // 2, half + src // 2)
    perm = (dst[:, None] == jnp.arange(RET_DK)[None, :]).astype(BF16)
    s_in, s_out, s_shapes = _side_cast_specs(sides, nj * ni, ni)
    return pl.pallas_call(
        functools.partial(_in_proj_kernel, n_qk_slabs=2 * RET_QK // tn, n_sides=len(sides)),
        grid=(nj, ni),
        in_specs=[pl.BlockSpec(memory_space=pl.ANY),
                  pl.BlockSpec((None, k, tn), lambda j, i: (layer, 0, j)),
                  pl.BlockSpec((RET_DK, RET_DK), lambda j, i: (0, 0)),
                  *s_in],
        out_specs=[pl.BlockSpec((tm, tn), lambda j, i: (i, j)), *s_out],
        out_shape=[jax.ShapeDtypeStruct((n, f), BF16), *s_shapes],
        scratch_shapes=[pltpu.VMEM((k, tn), BF16), *_ring_scratch(tm, k)],
        compiler_params=pltpu.CompilerParams(dimension_semantics=("arbitrary", "arbitrary")),
        name="retention_in_proj",
    )(x, w, perm, *(side for side, _ in sides))


def _swiglu_kernel(x_hbm, ssq_ref, wa_ref, wu_ref, *refs, n_sides):
    o_ref = refs[n_sides]
    wab_ref, wub_ref, xbuf, xsem = refs[-4:]
    ni = pl.num_programs(1)
    step = pl.program_id(0) * ni + pl.program_id(1)
    ring = _RowTileRing(x_hbm, xbuf, xsem, ni)
    ring.advance(step, pl.num_programs(0) * ni)

    @pl.when(pl.program_id(1) == 0)
    def _():
        wab_ref[...] = wa_ref[...].astype(BF16)
        wub_ref[...] = wu_ref[...].astype(BF16)

    x = ring.tile(step)
    rowscale = lax.rsqrt(jnp.sum(ssq_ref[...], axis=0) / x.shape[1] + NORM_EPS)
    a = _dot(x, wab_ref[...]) * rowscale
    u = _dot(x, wub_ref[...]) * rowscale
    o_ref[...] = (_silu(a) * u).astype(o_ref.dtype)
    _cast_sides(refs, n_sides)


def swiglu_up(xg, ssq, w_gu, layer, sides, tm=1024, tn=512):
    n, k = xg.shape
    f = w_gu.shape[2] // 2
    tm = min(tm, n)
    nj, ni = f // tn, n // tm
    ns = ssq.shape[0]
    s_in, s_out, s_shapes = _side_cast_specs(sides, nj * ni, ni)
    return pl.pallas_call(
        functools.partial(_swiglu_kernel, n_sides=len(sides)),
        grid=(nj, ni),
        in_specs=[pl.BlockSpec(memory_space=pl.ANY),
                  pl.BlockSpec((ns, tm, 1), lambda j, i: (0, i, 0)),
                  pl.BlockSpec((None, k, tn), lambda j, i: (layer, 0, j)),
                  pl.BlockSpec((None, k, tn), lambda j, i: (layer, 0, nj + j)),
                  *s_in],
        out_specs=[pl.BlockSpec((tm, tn), lambda j, i: (i, j)), *s_out],
        out_shape=[jax.ShapeDtypeStruct((n, f), BF16), *s_shapes],
        scratch_shapes=[pltpu.VMEM((k, tn), BF16), pltpu.VMEM((k, tn), BF16),
                        *_ring_scratch(tm, k)],
        compiler_params=pltpu.CompilerParams(dimension_semantics=("arbitrary", "arbitrary")),
        name="swiglu_up",
    )(xg, ssq, w_gu, w_gu, *(side for side, _ in sides))


def _emit_residual_outputs(xn, g_ref, refs, emit_x, emit_norm):
    pos = 0
    if emit_x:
        refs[pos][...] = xn
        pos += 1
    if emit_norm:
        refs[pos][...] = _rms(xn, g_ref[...]).astype(refs[pos].dtype)


def _residual_out_shapes(n, d, tm, emit_x, norm_dtype, index_map):
    out_shape, out_specs = [], []
    if emit_x:
        out_shape.append(jax.ShapeDtypeStruct((n, d), F32))
        out_specs.append(pl.BlockSpec((tm, d), index_map))
    if norm_dtype is not None:
        out_shape.append(jax.ShapeDtypeStruct((n, d), norm_dtype))
        out_specs.append(pl.BlockSpec((tm, d), index_map))
    return out_shape, out_specs


def _down_kernel(x_ref, w_ref, r_ref, g_ref, *refs, emit_scaled):
    xn = r_ref[...] + _dot(x_ref[...], w_ref[...])
    refs[0][...] = xn
    if emit_scaled:
        refs[1][...] = (xn * g_ref[...]).astype(BF16)
        refs[2][...] = jnp.sum(xn * xn, axis=-1, keepdims=True)[None]


def down_residual(h, w, resid, g=None, *, tm=512):
    n, kk = h.shape
    d = w.shape[1]
    tm = min(tm, n)
    emit_scaled = g is not None
    if g is None:
        g = jnp.ones((d,), F32)
    out_shape = [jax.ShapeDtypeStruct((n, d), F32)]
    out_specs = [pl.BlockSpec((tm, d), lambda i: (i, 0))]
    if emit_scaled:
        out_shape += [jax.ShapeDtypeStruct((n, d), BF16), jax.ShapeDtypeStruct((1, n, 1), F32)]
        out_specs += [pl.BlockSpec((tm, d), lambda i: (i, 0)),
                      pl.BlockSpec((1, tm, 1), lambda i: (0, i, 0))]
    return pl.pallas_call(
        functools.partial(_down_kernel, emit_scaled=emit_scaled),
        grid=(n // tm,),
        in_specs=[pl.BlockSpec((tm, kk), lambda i: (i, 0)),
                  pl.BlockSpec((kk, d), lambda i: (0, 0), pipeline_mode=pl.Buffered(1)),
                  pl.BlockSpec((tm, d), lambda i: (i, 0)),
                  pl.BlockSpec((1, d), lambda i: (0, 0))],
        out_specs=out_specs,
        out_shape=out_shape,
        compiler_params=pltpu.CompilerParams(dimension_semantics=("arbitrary",),
                                             vmem_limit_bytes=DOWN_VMEM_LIMIT),
        name="down_residual",
    )(h, w, resid, g.reshape(1, d))


def _retention_kernel(lg_ref, q_ref, k_ref, v_ref, g_ref, cos_ref, sin_ref, gain_ref, o_ref,
                      qr_ref, qd_ref, kr_ref, kd_ref, mask_ref, qdec_ref, kdec_ref, oraw_ref,
                      state_ref, sb_ref, *, n_chunks, n_pairs):
    c = RET_CHUNK
    step = pl.program_id(0)
    slot_prep = step % 2
    slot_recur = 1 - slot_prep
    slot_finish = slot_prep

    @pl.when(step == 0)
    def _():
        for ref in (qr_ref, qd_ref, kr_ref, kd_ref, oraw_ref):
            ref[...] = jnp.zeros_like(ref)

    lg_prep = lg_ref[jnp.minimum(step, n_pairs - 1) % RET_HEADS]
    lg = lg_ref[jnp.clip(step - 1, 0, n_pairs - 1) % RET_HEADS]
    ii = lax.broadcasted_iota(jnp.int32, (c, c), 0)
    jj = lax.broadcasted_iota(jnp.int32, (c, c), 1)
    rel = (ii - jj).astype(F32)
    mask_ref[...] = jnp.where(rel >= 0.0, jnp.exp(jnp.maximum(rel, 0.0) * lg), 0.0)
    idx = lax.broadcasted_iota(jnp.int32, (c, 1), 0).astype(F32)
    k_scale = RET_DK ** -0.5
    qdec_ref[...] = jnp.broadcast_to(jnp.exp((idx + 1.0) * lg_prep), (c, RET_DK))
    kdec_ref[...] = jnp.broadcast_to(jnp.exp((c - 1.0 - idx) * lg_prep) * k_scale, (c, RET_DK))
    chunk_decay = jnp.exp(jnp.full((1, 1), float(c), F32) * lg)
    half = RET_DK // 2

    def rotary(xb, cos, sin):
        x = xb.astype(F32)
        x1, x2 = x[:, :half], x[:, half:]
        return jnp.concatenate([x1 * cos - x2 * sin, x2 * cos + x1 * sin], axis=1)

    def rows(ci):
        return pl.ds(pl.multiple_of(ci * c, c), c)

    state_ref[...] = jnp.zeros_like(state_ref)
    sb_ref[...] = jnp.zeros_like(sb_ref)

    def body(ci, carry):
        r = rows(ci)

        o = oraw_ref[slot_finish, r, :]
        ms = jnp.mean(o * o, axis=-1, keepdims=True)
        o = o * lax.rsqrt(ms + NORM_EPS) * gain_ref[...]
        gate = g_ref[r, :].astype(F32)
        o_ref[r, :] = (o * _silu(gate)).astype(o_ref.dtype)

        v = v_ref[r, :]
        scores = lax.dot_general(qr_ref[slot_recur, r, :], kr_ref[slot_recur, r, :],
                                 (((1,), (1,)), ((), ())),
                                 preferred_element_type=F32) * mask_ref[...]
        oraw_ref[slot_recur, r, :] = (_dot(scores.astype(BF16), v)
                                      + _dot(qd_ref[slot_recur, r, :], sb_ref[...]))
        upd = lax.dot_general(kd_ref[slot_recur, r, :], v, (((0,), (0,)), ((), ())),
                              preferred_element_type=F32)
        state = state_ref[...] * chunk_decay + upd
        state_ref[...] = state
        sb_ref[...] = state.astype(BF16)

        for lo in range(0, c, RET_PREP_ROWS):
            rr = pl.ds(pl.multiple_of(ci * c + lo, RET_PREP_ROWS), RET_PREP_ROWS)
            dec = pl.ds(lo, RET_PREP_ROWS)
            cos = cos_ref[rr, :]
            sin = sin_ref[rr, :]
            q = rotary(q_ref[rr, :], cos, sin)
            k = rotary(k_ref[rr, :], cos, sin)
            qr_ref[slot_prep, rr, :] = q.astype(BF16)
            qd_ref[slot_prep, rr, :] = (q * qdec_ref[dec, :]).astype(BF16)
            kr_ref[slot_prep, rr, :] = (k * k_scale).astype(BF16)
            kd_ref[slot_prep, rr, :] = (k * kdec_ref[dec, :]).astype(BF16)
        return carry

    lax.fori_loop(0, n_chunks, body, 0, unroll=2)


def retention(proj, norm_g, layer, batch, seq):
    n = batch * seq
    h = RET_HEADS
    log_g = jnp.log1p(-jnp.exp2(-5.0 - jnp.arange(h, dtype=F32)))
    freq = 1.0 / (ROPE_BASE ** jnp.linspace(0.0, 1.0, RET_DK // 2, dtype=F32))
    ang = jnp.arange(seq).astype(F32)[:, None] * freq[None, :]
    cos = jnp.cos(ang)
    sin = jnp.sin(ang)
    kq = RET_QK // RET_DK
    kv = 2 * RET_QK // RET_DV
    kg = (2 * RET_QK + RET_V) // RET_DV
    n_pairs = batch * h
    stages = 3

    def pair(step, stage):
        p = jnp.clip(step - stage, 0, n_pairs - 1)
        return p // h, p % h

    def block(stage, col0):
        def index(s):
            b, hh = pair(s, stage)
            return b, col0 + hh
        return index

    def gain_index(s):
        return layer, 0, pair(s, 2)[1]

    return pl.pallas_call(
        functools.partial(_retention_kernel, n_chunks=seq // RET_CHUNK, n_pairs=n_pairs),
        grid=(n_pairs + stages - 1,),
        in_specs=[pl.BlockSpec(memory_space=pltpu.SMEM),
                  pl.BlockSpec((seq, RET_DK), block(0, 0)),
                  pl.BlockSpec((seq, RET_DK), block(0, kq)),
                  pl.BlockSpec((seq, RET_DV), block(1, kv)),
                  pl.BlockSpec((seq, RET_DV), block(2, kg)),
                  pl.BlockSpec((seq, RET_DK // 2), lambda s: (0, 0)),
                  pl.BlockSpec((seq, RET_DK // 2), lambda s: (0, 0)),
                  pl.BlockSpec((None, 1, RET_DV), gain_index)],
        out_specs=pl.BlockSpec((seq, RET_DV), block(2, 0)),
        out_shape=jax.ShapeDtypeStruct((n, RET_V), BF16),
        scratch_shapes=[pltpu.VMEM((2, seq, RET_DK), BF16),
                        pltpu.VMEM((2, seq, RET_DK), BF16),
                        pltpu.VMEM((2, seq, RET_DK), BF16),
                        pltpu.VMEM((2, seq, RET_DK), BF16),
                        pltpu.VMEM((RET_CHUNK, RET_CHUNK), F32),
                        pltpu.VMEM((RET_CHUNK, RET_DK), F32),
                        pltpu.VMEM((RET_CHUNK, RET_DK), F32),
                        pltpu.VMEM((2, seq, RET_DV), F32),
                        pltpu.VMEM((RET_DK, RET_DV), F32),
                        pltpu.VMEM((RET_DK, RET_DV), BF16)],
        compiler_params=pltpu.CompilerParams(dimension_semantics=("arbitrary",)),
        name="retention",
    )(log_g, proj, proj, proj, proj, cos, sin, norm_g.reshape(norm_g.shape[0], 1, RET_V))


def _pool_kernel(x_ref, gm_ref, pw_ref, sc_ref, gf_ref, rt_ref,
                 xo_ref, hn_ref, meta_ref, meta_t_ref, cnt_out_ref, hbuf, cnt_ref, *, tiles_per_seq):
    tm = x_ref.shape[0]
    step = pl.program_id(0)
    ti = step % tiles_per_seq
    x = x_ref[...]
    hn = _rms(x, gm_ref[...])

    @pl.when(ti == 0)
    def _():
        hbuf[0:POOL_HALO, :] = jnp.zeros((POOL_HALO, D_MODEL), F32)

    @pl.when(ti != 0)
    def _():
        hbuf[0:POOL_HALO, :] = hbuf[tm:tm + POOL_HALO, :]

    hbuf[POOL_HALO:POOL_HALO + tm, :] = hn

    t = ti * tm + lax.broadcasted_iota(jnp.int32, (tm, 1), 0)
    for gi, w in enumerate(POOL_WINDOWS):
        lo, hi = gi * POOL_G, (gi + 1) * POOL_G
        cur = hn[:, lo:hi]
        acc = hbuf[:, lo:hi]
        shift = 1
        while shift < w:
            acc = acc + pltpu.roll(acc, shift, 0)
            shift *= 2
        acc = acc[POOL_HALO:, :]
        inv = 1.0 / jnp.minimum(t + 1, w).astype(F32)
        mix = (acc * inv - cur).astype(BF16)
        o = _dot(mix, pw_ref[gi]) * sc_ref[:, lo:hi]
        xo_ref[:, lo:hi] = x[:, lo:hi] + o

    hn2 = _rms(xo_ref[...], gf_ref[...])
    hn_ref[...] = hn2

    rt = rt_ref[...]
    hn_hi = hn2.astype(BF16)
    hn_lo = (hn2 - hn_hi.astype(F32)).astype(BF16)
    rt_hi = rt.astype(BF16)
    rt_lo = (rt - rt_hi.astype(F32)).astype(BF16)
    logits = _dot(hn_hi, rt_hi) + (_dot(hn_hi, rt_lo) + _dot(hn_lo, rt_hi))
    lt = logits.T[:N_EXPERTS, :]
    expert = lax.broadcasted_iota(jnp.int32, lt.shape, 0)
    neg = jnp.float32(-jnp.inf)
    m1 = jnp.max(lt, axis=0, keepdims=True)
    i1 = jnp.min(jnp.where(lt == m1, expert, N_EXPERTS), axis=0, keepdims=True)
    lt2 = jnp.where(expert == i1, neg, lt)
    m2 = jnp.max(lt2, axis=0, keepdims=True)
    i2 = jnp.min(jnp.where(lt2 == m2, expert, N_EXPERTS), axis=0, keepdims=True)
    e2 = jnp.exp(m2 - m1)
    g1 = 1.0 / (1.0 + e2)
    g2 = e2 * g1

    @pl.when(step == 0)
    def _():
        cnt_ref[...] = jnp.zeros_like(cnt_ref)

    chosen = jnp.where((expert == i1) | (expert == i2), 1.0, 0.0)
    src = lax.broadcasted_iota(jnp.int32, (tm, tm), 0)
    dst = lax.broadcasted_iota(jnp.int32, (tm, tm), 1)
    earlier = jnp.where(src < dst, 1.0, 0.0).astype(BF16)
    before = _dot(chosen.astype(BF16), earlier) + cnt_ref[:, 0:1]
    r1 = jnp.sum(jnp.where(expert == i1, before, 0.0), axis=0, keepdims=True)
    r2 = jnp.sum(jnp.where(expert == i2, before, 0.0), axis=0, keepdims=True)
    cnt_ref[...] += jnp.sum(chosen, axis=1, keepdims=True)
    cnt_out_ref[...] = cnt_ref[...]

    fields = {META_E1: i1.astype(F32), META_E2: i2.astype(F32), META_R1: r1, META_R2: r2,
              META_G1: g1, META_G2: g2}
    zero_row = jnp.zeros_like(g1)
    meta_t = jnp.concatenate([fields.get(row, zero_row) for row in range(SUBLANES)], axis=0)
    meta_t_ref[...] = meta_t
    meta_ref[...] = jnp.concatenate([meta_t, jnp.zeros((LANES - SUBLANES, tm), F32)], axis=0).T


def pool_router(x, g_mix, pool_w, layer, pool_scale, g_ffn, router, seq, tm=512):
    n, d = x.shape
    tm = min(tm, seq)
    router_p = jnp.zeros((d, LANES), F32).at[:, :N_EXPERTS].set(router)
    ng = len(POOL_WINDOWS)
    return pl.pallas_call(
        functools.partial(_pool_kernel, tiles_per_seq=seq // tm),
        grid=(n // tm,),
        in_specs=[pl.BlockSpec((tm, d), lambda i: (i, 0)),
                  pl.BlockSpec((1, d), lambda i: (0, 0)),
                  pl.BlockSpec((None, ng, POOL_G, POOL_G), lambda i: (layer, 0, 0, 0)),
                  pl.BlockSpec((1, d), lambda i: (0, 0)),
                  pl.BlockSpec((1, d), lambda i: (0, 0)),
                  pl.BlockSpec((d, LANES), lambda i: (0, 0))],
        out_specs=[pl.BlockSpec((tm, d), lambda i: (i, 0)),
                   pl.BlockSpec((tm, d), lambda i: (i, 0)),
                   pl.BlockSpec((tm, LANES), lambda i: (i, 0)),
                   pl.BlockSpec((SUBLANES, tm), lambda i: (0, i)),
                   pl.BlockSpec((SUBLANES, LANES), lambda i: (0, 0))],
        out_shape=[jax.ShapeDtypeStruct((n, d), F32),
                   jax.ShapeDtypeStruct((n, d), F32),
                   jax.ShapeDtypeStruct((n, LANES), F32),
                   jax.ShapeDtypeStruct((SUBLANES, n), F32),
                   jax.ShapeDtypeStruct((SUBLANES, LANES), F32)],
        scratch_shapes=[pltpu.VMEM((POOL_HALO + tm, d), F32),
                        pltpu.VMEM((N_EXPERTS, LANES), F32)],
        compiler_params=pltpu.CompilerParams(dimension_semantics=("arbitrary",)),
        name="pool_router",
    )(x, g_mix.reshape(1, d), pool_w, pool_scale.reshape(1, d), g_ffn.reshape(1, d), router_p)


def moe_num_tiles(n):
    return TOP_K * n // MOE_TILE + N_EXPERTS


def moe_schedule(meta_t, counts_f, n):
    nt = moe_num_tiles(n)
    counts = counts_f[:N_EXPERTS, 0].astype(jnp.int32)
    tiles = (counts + MOE_TILE - 1) // MOE_TILE
    tile_end = jnp.cumsum(tiles)
    starts = (tile_end - tiles) * MOE_TILE
    expert_ids = jnp.arange(N_EXPERTS, dtype=jnp.int32)[:, None]

    def slots(e_row, r_row):
        e = meta_t[e_row].astype(jnp.int32)
        first = jnp.sum(jnp.where(e[None, :] == expert_ids, starts[:, None], 0), axis=0)
        return first + meta_t[r_row].astype(jnp.int32)

    pos = jnp.concatenate([slots(META_E1, META_R1), slots(META_E2, META_R2)])
    total = tile_end[-1]
    t = jnp.arange(nt, dtype=jnp.int32)
    t_used = jnp.minimum(t, total - 1)
    tile_expert = jnp.sum((tile_end[None, :] <= t_used[:, None]).astype(jnp.int32), axis=1)
    tile_expert = jnp.minimum(tile_expert, N_EXPERTS - 1)
    tile_valid = (t < total).astype(jnp.int32)
    pad_lo = starts + counts
    pad_hi = starts + tiles * MOE_TILE
    prev_expert = jnp.concatenate([jnp.full((1,), -1, jnp.int32), tile_expert[:-1]])
    group_first = (tile_expert != prev_expert).astype(jnp.int32)
    group_slot = (jnp.cumsum(group_first) - 1) % 2
    later = (expert_ids.T > expert_ids) & (tiles[None, :] > 0)
    next_used = jnp.min(jnp.where(later, expert_ids.T, N_EXPERTS), axis=1)
    next_used = jnp.where(next_used == N_EXPERTS, -1, next_used)
    next_expert = jnp.sum(jnp.where(tile_expert[None, :] == expert_ids, next_used[:, None], 0), axis=0)
    table = jnp.concatenate([tile_expert, tile_valid, group_first, group_slot, next_expert])
    return pos, table.astype(jnp.int32), pad_lo, pad_hi, total.reshape(1)


def _dispatch_kernel(pos_ref, lo_ref, hi_ref, used_ref, hn_ref, xs_hbm, zeros, sem, *, n, n_tiles):
    tb = hn_ref.shape[0]
    step = pl.program_id(0)
    base = step * tb

    def issue(r, c):
        tok = base + r
        src = hn_ref.at[pl.ds(r, 1), :]
        pltpu.make_async_copy(src, xs_hbm.at[pl.ds(pos_ref[tok], 1), :], sem).start(priority=0)
        pltpu.make_async_copy(src, xs_hbm.at[pl.ds(pos_ref[n + tok], 1), :], sem).start(priority=1)
        return c

    lax.fori_loop(0, tb, issue, 0, unroll=8)
    for _ in range(TOP_K):
        pltpu.make_async_copy(hn_ref, xs_hbm.at[pl.ds(0, tb), :], sem).wait()

    @pl.when(step == pl.num_programs(0) - 1)
    def _():
        zeros[...] = jnp.zeros_like(zeros)

        def pad_copy(slot):
            return pltpu.make_async_copy(zeros.at[pl.ds(0, 1), :], xs_hbm.at[pl.ds(slot, 1), :], sem)

        def issue_pad(s, c):
            pad_copy(s).start()
            return c

        def drain_pad(s, c):
            pad_copy(s).wait()
            return c

        for e in range(N_EXPERTS):
            lax.fori_loop(lo_ref[e], hi_ref[e], issue_pad, 0)
            lax.fori_loop(lo_ref[e], hi_ref[e], drain_pad, 0)

        def tile_copy(t):
            return pltpu.make_async_copy(zeros, xs_hbm.at[pl.ds(t * MOE_TILE, MOE_TILE), :], sem)

        def issue_tile(t, c):
            tile_copy(t).start()
            return c

        def drain_tile(t, c):
            tile_copy(t).wait()
            return c

        lax.fori_loop(used_ref[0], n_tiles, issue_tile, 0)
        lax.fori_loop(used_ref[0], n_tiles, drain_tile, 0)


def moe_dispatch(hn, pos, pad_lo, pad_hi, used, tb=512):
    n, d = hn.shape
    tb = min(tb, n)
    nt = moe_num_tiles(n)
    grid_spec = pltpu.PrefetchScalarGridSpec(
        num_scalar_prefetch=4,
        grid=(n // tb,),
        in_specs=[pl.BlockSpec((tb, d), lambda i, *_: (i, 0))],
        out_specs=pl.BlockSpec(memory_space=pl.ANY),
        scratch_shapes=[pltpu.VMEM((MOE_TILE, d), F32), pltpu.SemaphoreType.DMA],
    )
    return pl.pallas_call(
        functools.partial(_dispatch_kernel, n=n, n_tiles=nt),
        grid_spec=grid_spec,
        out_shape=jax.ShapeDtypeStruct((nt * MOE_TILE, d), F32),
        compiler_params=pltpu.CompilerParams(dimension_semantics=("arbitrary",)),
        name="moe_dispatch",
    )(pos, pad_lo, pad_hi, used, hn)


def _expert_kernel(tab_ref, x_ref, wgu_hbm, wd_hbm, o_ref, wa_buf, wu_buf, wd_buf, sems):
    t = pl.program_id(0)
    nt = pl.num_programs(0)
    f = wd_buf.shape[1]

    def field(k, tile):
        return tab_ref[k * nt + tile]

    def weight_copies(e, slot):
        return (pltpu.make_async_copy(wgu_hbm.at[e, :, pl.ds(0, f)], wa_buf.at[slot], sems.at[slot]),
                pltpu.make_async_copy(wgu_hbm.at[e, :, pl.ds(f, f)], wu_buf.at[slot], sems.at[slot]),
                pltpu.make_async_copy(wd_hbm.at[e], wd_buf.at[slot], sems.at[slot]))

    slot = field(TILE_SLOT, t)

    @pl.when(t == 0)
    def _():
        for cp in weight_copies(field(TILE_EXPERT, 0), field(TILE_SLOT, 0)):
            cp.start()

    @pl.when(field(TILE_FIRST, t) == 1)
    def _():
        for cp in weight_copies(field(TILE_EXPERT, t), slot):
            cp.wait()
        nxt = field(TILE_NEXT, t)

        @pl.when(nxt >= 0)
        def _():
            for cp in weight_copies(nxt, 1 - slot):
                cp.start()

    valid = field(TILE_VALID, t) == 1

    @pl.when(valid)
    def _():
        x = x_ref[...].astype(BF16)
        a = _dot(x, wa_buf[slot])
        u = _dot(x, wu_buf[slot])
        o_ref[...] = _dot((_silu(a) * u).astype(BF16), wd_buf[slot])

    @pl.when(jnp.logical_not(valid))
    def _():
        o_ref[...] = jnp.zeros_like(o_ref)


def moe_experts(xs, w_gu, w_down, table):
    rows, d = xs.shape
    f = w_down.shape[1]
    nt = rows // MOE_TILE
    grid_spec = pltpu.PrefetchScalarGridSpec(
        num_scalar_prefetch=1,
        grid=(nt,),
        in_specs=[pl.BlockSpec((MOE_TILE, d), lambda t, tab: (t, 0)),
                  pl.BlockSpec(memory_space=pl.ANY),
                  pl.BlockSpec(memory_space=pl.ANY)],
        out_specs=pl.BlockSpec((MOE_TILE, d), lambda t, tab: (t, 0)),
        scratch_shapes=[pltpu.VMEM((2, d, f), BF16), pltpu.VMEM((2, d, f), BF16),
                        pltpu.VMEM((2, f, d), BF16), pltpu.SemaphoreType.DMA((2,))],
    )
    return pl.pallas_call(
        _expert_kernel,
        grid_spec=grid_spec,
        out_shape=jax.ShapeDtypeStruct((rows, d), F32),
        compiler_params=pltpu.CompilerParams(dimension_semantics=("arbitrary",),
                                             vmem_limit_bytes=56 * 1024 * 1024),
        name="moe_experts",
    )(table, xs, w_gu, w_down)


def _combine_kernel(pos_ref, x_ref, meta_ref, g_ref, y_hbm, *refs, n, emit_x, emit_norm):
    ybuf, sem = refs[-2], refs[-1]
    tc = x_ref.shape[0]
    step = pl.program_id(0)
    slot = step % 2

    def start_gather(s, into):
        def issue(r, c):
            tok = s * tc + r
            for k in range(TOP_K):
                pltpu.make_async_copy(y_hbm.at[pl.ds(pos_ref[k * n + tok], 1), :],
                                      ybuf.at[into, k, pl.ds(r, 1), :],
                                      sem.at[into]).start(priority=k)
            return c

        lax.fori_loop(0, tc, issue, 0, unroll=8)

    @pl.when(step == 0)
    def _():
        start_gather(0, 0)

    @pl.when(step + 1 < pl.num_programs(0))
    def _():
        start_gather(step + 1, 1 - slot)

    for k in range(TOP_K):
        pltpu.make_async_copy(y_hbm.at[pl.ds(0, tc), :], ybuf.at[slot, k], sem.at[slot]).wait()

    meta = meta_ref[...]
    g1 = meta[:, META_G1:META_G1 + 1]
    g2 = meta[:, META_G2:META_G2 + 1]
    xn = x_ref[...] + g1 * ybuf[slot, 0] + g2 * ybuf[slot, 1]
    _emit_residual_outputs(xn, g_ref, refs, emit_x, emit_norm)


def moe_combine(x, y, meta, pos, g, *, emit_x=True, norm_dtype=None, tc=256):
    n, d = x.shape
    tc = min(tc, n)
    out_shape, out_specs = _residual_out_shapes(n, d, tc, emit_x, norm_dtype, lambda i, p: (i, 0))
    grid_spec = pltpu.PrefetchScalarGridSpec(
        num_scalar_prefetch=1,
        grid=(n // tc,),
        in_specs=[pl.BlockSpec((tc, d), lambda i, p: (i, 0)),
                  pl.BlockSpec((tc, LANES), lambda i, p: (i, 0)),
                  pl.BlockSpec((1, d), lambda i, p: (0, 0)),
                  pl.BlockSpec(memory_space=pl.ANY)],
        out_specs=out_specs,
        scratch_shapes=[pltpu.VMEM((2, TOP_K, tc, d), F32), pltpu.SemaphoreType.DMA((2,))],
    )
    return pl.pallas_call(
        functools.partial(_combine_kernel, n=n, emit_x=emit_x, emit_norm=norm_dtype is not None),
        grid_spec=grid_spec,
        out_shape=out_shape,
        compiler_params=pltpu.CompilerParams(dimension_semantics=("arbitrary",)),
        name="moe_combine",
    )(pos, x, meta, g.reshape(1, d), y)


def kernel(x, ret_w_in, ret_norm_g, ret_w_out, pool_w, pool_scale, ffn_w_gu, ffn_w_down,
           moe_router, moe_w_gu, moe_w_down, norm_mix_g, norm_ffn_g, norm_final_g):
    batch, seq, d = x.shape
    n = batch * seq
    xf = x.reshape(n, d)
    pool_w = pool_w.astype(BF16)
    n_moe, n_exp, _, f2 = moe_w_gu.shape
    f = moe_w_down.shape[2]
    moe_gu_rows = moe_w_gu.reshape(n_moe, n_exp * d, f2)
    moe_down_rows = moe_w_down.reshape(n_moe, n_exp * f, d)
    hn = rmsnorm(xf, norm_mix_g[0], BF16)
    for i in range(DEPTH):
        j = i // 2
        last = i == DEPTH - 1
        if i % 2 == 0:
            proj, moe_gu_b, w_out_b = retention_in_proj(hn, ret_w_in, j,
                                                        [(moe_gu_rows, j), (ret_w_out, j)])
            y = retention(proj, ret_norm_g, j, batch, seq)
            xf, xg, ssq = down_residual(y, w_out_b, xf, norm_ffn_g[i])
            hid, moe_down_b, ffn_down_b = swiglu_up(xg, ssq, ffn_w_gu, j,
                                                    [(moe_down_rows, j), (ffn_w_down, j)])
            (xf,) = down_residual(hid, ffn_down_b, xf)
            moe_gu_b = moe_gu_b.reshape(n_exp, d, f2)
            moe_down_b = moe_down_b.reshape(n_exp, f, d)
        else:
            xf, hn, meta, meta_t, counts = pool_router(xf, norm_mix_g[i], pool_w, j, pool_scale[j],
                                               norm_ffn_g[i], moe_router[j], seq)
            pos, tile_table, pad_lo, pad_hi, used = moe_schedule(meta_t, counts, n)
            xs = moe_dispatch(hn, pos, pad_lo, pad_hi, used)
            ys = moe_experts(xs, moe_gu_b, moe_down_b, tile_table)
            if last:
                (out,) = moe_combine(xf, ys, meta, pos, norm_final_g, emit_x=False, norm_dtype=F32)
            else:
                xf, hn = moe_combine(xf, ys, meta, pos, norm_mix_g[i + 1], norm_dtype=BF16)
    return out.reshape(batch, seq, d)
```

```python
import functools

import jax
import jax.numpy as jnp
from jax import lax
from jax.experimental import pallas as pl
from jax.experimental.pallas import tpu as pltpu

BF16 = jnp.bfloat16
F32 = jnp.float32

D_MODEL = 2048
DEPTH = 4
RET_HEADS = 8
RET_DK = D_MODEL // RET_HEADS
RET_DV = 2 * D_MODEL // RET_HEADS
RET_QK = RET_HEADS * RET_DK
RET_V = RET_HEADS * RET_DV
RET_CHUNK = 256
RET_PREP_ROWS = 32
ROPE_BASE = 10000.0
POOL_WINDOWS = (2, 4, 8, 16)
POOL_G = D_MODEL // len(POOL_WINDOWS)
POOL_HALO = 16
D_FF = 5632
N_EXPERTS = 8
TOP_K = 2
D_FF_EXPERT = 1408
NORM_EPS = 1e-6
LANES = 128
SUBLANES = 8
MOE_TILE = 256
SIDE_CAST_BLOCKS = 64
X_RING = 3
BF16_TILE_ROWS = 16
DOWN_VMEM_LIMIT = 60 * 1024 * 1024
META_E1, META_E2, META_R1, META_R2, META_G1, META_G2 = range(6)
TILE_EXPERT, TILE_VALID, TILE_FIRST, TILE_SLOT, TILE_NEXT = range(5)


def _rms(x, g):
    ms = jnp.mean(x * x, axis=-1, keepdims=True)
    return x * lax.rsqrt(ms + NORM_EPS) * g


def _silu(a):
    return a / (1.0 + jnp.exp(-a))


def _dot(a, b):
    return jnp.dot(a, b, preferred_element_type=F32)


def _rmsnorm_kernel(x_ref, g_ref, o_ref):
    o_ref[...] = _rms(x_ref[...], g_ref[...]).astype(o_ref.dtype)


def rmsnorm(x, g, out_dtype, tm=512):
    n, d = x.shape
    return pl.pallas_call(
        _rmsnorm_kernel,
        grid=(n // tm,),
        in_specs=[pl.BlockSpec((tm, d), lambda i: (i, 0)),
                  pl.BlockSpec((1, d), lambda i: (0, 0))],
        out_specs=pl.BlockSpec((tm, d), lambda i: (i, 0)),
        out_shape=jax.ShapeDtypeStruct((n, d), out_dtype),
        name="rmsnorm",
    )(x, g.reshape(1, d))


def _side_cast_specs(sides, steps, n_inner):
    in_specs, out_specs, shapes = [], [], []
    for side, side_layer in sides:
        _, rows, cols = side.shape
        n_blocks = 1
        while (n_blocks * 2 <= min(steps, SIDE_CAST_BLOCKS)
               and rows % (n_blocks * 2 * BF16_TILE_ROWS) == 0):
            n_blocks *= 2
        br = rows // n_blocks

        def index(j, i, n_blocks=n_blocks):
            return jnp.minimum(j * n_inner + i, n_blocks - 1)

        in_specs.append(pl.BlockSpec((None, br, cols),
                                     lambda j, i, l=side_layer, ix=index: (l, ix(j, i), 0)))
        out_specs.append(pl.BlockSpec((br, cols), lambda j, i, ix=index: (ix(j, i), 0)))
        shapes.append(jax.ShapeDtypeStruct((rows, cols), BF16))
    return in_specs, out_specs, shapes


def _cast_sides(refs, n_sides):
    for s_ref, so_ref in zip(refs[:n_sides], refs[n_sides + 1:2 * n_sides + 1]):
        so_ref[...] = s_ref[...].astype(BF16)


class _RowTileRing:
    def __init__(self, x_hbm, buf, sems, n_row_tiles):
        self.x_hbm, self.buf, self.sems, self.n_row_tiles = x_hbm, buf, sems, n_row_tiles

    def _copy(self, s):
        tm = self.buf.shape[1]
        rows = pl.ds(pl.multiple_of((s % self.n_row_tiles) * tm, tm), tm)
        return pltpu.make_async_copy(self.x_hbm.at[rows, :], self.buf.at[s % X_RING],
                                     self.sems.at[s % X_RING])

    def advance(self, step, total):
        @pl.when(step == 0)
        def _():
            for s in range(X_RING - 1):
                self._copy(s).start()

        @pl.when(step + X_RING - 1 < total)
        def _():
            self._copy(step + X_RING - 1).start()

    def tile(self, step):
        self._copy(step).wait()
        return self.buf[step % X_RING]


def _ring_scratch(tm, k):
    return [pltpu.VMEM((X_RING, tm, k), BF16), pltpu.SemaphoreType.DMA((X_RING,))]


def _in_proj_kernel(x_hbm, w_ref, perm_ref, *refs, n_qk_slabs, n_sides):
    o_ref = refs[n_sides]
    wb_ref, xbuf, xsem = refs[-3:]
    ni = pl.num_programs(1)
    step = pl.program_id(0) * ni + pl.program_id(1)
    ring = _RowTileRing(x_hbm, xbuf, xsem, ni)
    ring.advance(step, pl.num_programs(0) * ni)

    @pl.when(pl.program_id(1) == 0)
    def _():
        is_qk = pl.program_id(0) < n_qk_slabs

        @pl.when(is_qk)
        def _():
            for lo in range(0, wb_ref.shape[1], RET_DK):
                head = w_ref[:, lo:lo + RET_DK].astype(BF16)
                wb_ref[:, lo:lo + RET_DK] = _dot(head, perm_ref[...]).astype(BF16)

        @pl.when(jnp.logical_not(is_qk))
        def _():
            wb_ref[...] = w_ref[...].astype(BF16)

    o_ref[...] = _dot(ring.tile(step), wb_ref[...]).astype(o_ref.dtype)
    _cast_sides(refs, n_sides)


def retention_in_proj(x, w, layer, sides, tm=1024, tn=1024):
    n, k = x.shape
    f = w.shape[2]
    tm = min(tm, n)
    nj, ni = f // tn, n // tm
    half = RET_DK // 2
    src = jnp.arange(RET_DK)
    dst = jnp.where(src % 2 == 0, src // 2, half + src // 2)
    perm = (dst[:, None] == jnp.arange(RET_DK)[None, :]).astype(BF16)
    s_in, s_out, s_shapes = _side_cast_specs(sides, nj * ni, ni)
    return pl.pallas_call(
        functools.partial(_in_proj_kernel, n_qk_slabs=2 * RET_QK // tn, n_sides=len(sides)),
        grid=(nj, ni),
        in_specs=[pl.BlockSpec(memory_space=pl.ANY),
                  pl.BlockSpec((None, k, tn), lambda j, i: (layer, 0, j)),
                  pl.BlockSpec((RET_DK, RET_DK), lambda j, i: (0, 0)),
                  *s_in],
        out_specs=[pl.BlockSpec((tm, tn), lambda j, i: (i, j)), *s_out],
        out_shape=[jax.ShapeDtypeStruct((n, f), BF16), *s_shapes],
        scratch_shapes=[pltpu.VMEM((k, tn), BF16), *_ring_scratch(tm, k)],
        compiler_params=pltpu.CompilerParams(dimension_semantics=("arbitrary", "arbitrary")),
        name="retention_in_proj",
    )(x, w, perm, *(side for side, _ in sides))


def _swiglu_kernel(x_hbm, ssq_ref, wa_ref, wu_ref, *refs, n_sides):
    o_ref = refs[n_sides]
    wab_ref, wub_ref, xbuf, xsem = refs[-4:]
    ni = pl.num_programs(1)
    step = pl.program_id(0) * ni + pl.program_id(1)
    ring = _RowTileRing(x_hbm, xbuf, xsem, ni)
    ring.advance(step, pl.num_programs(0) * ni)

    @pl.when(pl.program_id(1) == 0)
    def _():
        wab_ref[...] = wa_ref[...].astype(BF16)
        wub_ref[...] = wu_ref[...].astype(BF16)

    x = ring.tile(step)
    rowscale = lax.rsqrt(jnp.sum(ssq_ref[...], axis=0) / x.shape[1] + NORM_EPS)
    a = _dot(x, wab_ref[...]) * rowscale
    u = _dot(x, wub_ref[...]) * rowscale
    o_ref[...] = (_silu(a) * u).astype(o_ref.dtype)
    _cast_sides(refs, n_sides)


def swiglu_up(xg, ssq, w_gu, layer, sides, tm=1024, tn=512):
    n, k = xg.shape
    f = w_gu.shape[2] // 2
    tm = min(tm, n)
    nj, ni = f // tn, n // tm
    ns = ssq.shape[0]
    s_in, s_out, s_shapes = _side_cast_specs(sides, nj * ni, ni)
    return pl.pallas_call(
        functools.partial(_swiglu_kernel, n_sides=len(sides)),
        grid=(nj, ni),
        in_specs=[pl.BlockSpec(memory_space=pl.ANY),
                  pl.BlockSpec((ns, tm, 1), lambda j, i: (0, i, 0)),
                  pl.BlockSpec((None, k, tn), lambda j, i: (layer, 0, j)),
                  pl.BlockSpec((None, k, tn), lambda j, i: (layer, 0, nj + j)),
                  *s_in],
        out_specs=[pl.BlockSpec((tm, tn), lambda j, i: (i, j)), *s_out],
        out_shape=[jax.ShapeDtypeStruct((n, f), BF16), *s_shapes],
        scratch_shapes=[pltpu.VMEM((k, tn), BF16), pltpu.VMEM((k, tn), BF16),
                        *_ring_scratch(tm, k)],
        compiler_params=pltpu.CompilerParams(dimension_semantics=("arbitrary", "arbitrary")),
        name="swiglu_up",
    )(xg, ssq, w_gu, w_gu, *(side for side, _ in sides))


def _emit_residual_outputs(xn, g_ref, refs, emit_x, emit_norm):
    pos = 0
    if emit_x:
        refs[pos][...] = xn
        pos += 1
    if emit_norm:
        refs[pos][...] = _rms(xn, g_ref[...]).astype(refs[pos].dtype)


def _residual_out_shapes(n, d, tm, emit_x, norm_dtype, index_map):
    out_shape, out_specs = [], []
    if emit_x:
        out_shape.append(jax.ShapeDtypeStruct((n, d), F32))
        out_specs.append(pl.BlockSpec((tm, d), index_map))
    if norm_dtype is not None:
        out_shape.append(jax.ShapeDtypeStruct((n, d), norm_dtype))
        out_specs.append(pl.BlockSpec((tm, d), index_map))
    return out_shape, out_specs


def _down_kernel(x_ref, w_ref, r_ref, g_ref, *refs, emit_scaled, head_gated):
    if head_gated:
        gate_ref, gain_ref, xs_ref = refs[0], refs[1], refs[-1]
        refs = refs[2:-1]
        for lo in range(0, x_ref.shape[1], RET_DV):
            o = x_ref[:, lo:lo + RET_DV].astype(F32)
            ms = jnp.mean(o * o, axis=-1, keepdims=True)
            o = o * lax.rsqrt(ms + NORM_EPS) * gain_ref[:, lo:lo + RET_DV]
            gate = gate_ref[:, lo:lo + RET_DV].astype(F32)
            xs_ref[:, lo:lo + RET_DV] = (o * _silu(gate)).astype(BF16)
        x = xs_ref[...]
    else:
        x = x_ref[...]
    xn = r_ref[...] + _dot(x, w_ref[...])
    refs[0][...] = xn
    if emit_scaled:
        refs[1][...] = (xn * g_ref[...]).astype(BF16)
        refs[2][...] = jnp.sum(xn * xn, axis=-1, keepdims=True)[None]


def down_residual(h, w, resid, g=None, *, head_gate=None, tm=512):
    n, kk = h.shape
    d = w.shape[1]
    tm = min(tm, n)
    emit_scaled = g is not None
    if g is None:
        g = jnp.ones((d,), F32)
    extra_in, extra_specs, extra_scratch = [], [], []
    if head_gate is not None:
        proj, gate_block, head_gain = head_gate
        extra_in = [proj, head_gain.reshape(1, kk)]
        extra_specs = [pl.BlockSpec((tm, kk), lambda i: (i, gate_block)),
                       pl.BlockSpec((1, kk), lambda i: (0, 0))]
        extra_scratch = [pltpu.VMEM((tm, kk), BF16)]
    out_shape = [jax.ShapeDtypeStruct((n, d), F32)]
    out_specs = [pl.BlockSpec((tm, d), lambda i: (i, 0))]
    if emit_scaled:
        out_shape += [jax.ShapeDtypeStruct((n, d), BF16), jax.ShapeDtypeStruct((1, n, 1), F32)]
        out_specs += [pl.BlockSpec((tm, d), lambda i: (i, 0)),
                      pl.BlockSpec((1, tm, 1), lambda i: (0, i, 0))]
    return pl.pallas_call(
        functools.partial(_down_kernel, emit_scaled=emit_scaled, head_gated=head_gate is not None),
        grid=(n // tm,),
        in_specs=[pl.BlockSpec((tm, kk), lambda i: (i, 0)),
                  pl.BlockSpec((kk, d), lambda i: (0, 0), pipeline_mode=pl.Buffered(1)),
                  pl.BlockSpec((tm, d), lambda i: (i, 0)),
                  pl.BlockSpec((1, d), lambda i: (0, 0)),
                  *extra_specs],
        out_specs=out_specs,
        out_shape=out_shape,
        scratch_shapes=extra_scratch,
        compiler_params=pltpu.CompilerParams(dimension_semantics=("arbitrary",),
                                             vmem_limit_bytes=DOWN_VMEM_LIMIT),
        name="down_residual",
    )(h, w, resid, g.reshape(1, d), *extra_in)


def _retention_kernel(lg_ref, q_ref, k_ref, v_ref, cos_ref, sin_ref, o_ref,
                      qr_ref, qd_ref, kr_ref, kd_ref, mask_ref, qdec_ref, kdec_ref,
                      state_ref, sb_ref, *, n_chunks, n_pairs):
    c = RET_CHUNK
    step = pl.program_id(0)
    slot_prep = step % 2
    slot_recur = 1 - slot_prep

    @pl.when(step == 0)
    def _():
        for ref in (qr_ref, qd_ref, kr_ref, kd_ref):
            ref[...] = jnp.zeros_like(ref)

    lg_prep = lg_ref[jnp.minimum(step, n_pairs - 1) % RET_HEADS]
    lg = lg_ref[jnp.clip(step - 1, 0, n_pairs - 1) % RET_HEADS]
    ii = lax.broadcasted_iota(jnp.int32, (c, c), 0)
    jj = lax.broadcasted_iota(jnp.int32, (c, c), 1)
    rel = (ii - jj).astype(F32)
    mask_ref[...] = jnp.where(rel >= 0.0, jnp.exp(jnp.maximum(rel, 0.0) * lg), 0.0)
    idx = lax.broadcasted_iota(jnp.int32, (c, 1), 0).astype(F32)
    k_scale = RET_DK ** -0.5
    qdec_ref[...] = jnp.broadcast_to(jnp.exp((idx + 1.0) * lg_prep), (c, RET_DK))
    kdec_ref[...] = jnp.broadcast_to(jnp.exp((c - 1.0 - idx) * lg_prep) * k_scale, (c, RET_DK))
    chunk_decay = jnp.exp(jnp.full((1, 1), float(c), F32) * lg)
    half = RET_DK // 2

    def rotary(xb, cos, sin):
        x = xb.astype(F32)
        x1, x2 = x[:, :half], x[:, half:]
        return jnp.concatenate([x1 * cos - x2 * sin, x2 * cos + x1 * sin], axis=1)

    def rows(ci):
        return pl.ds(pl.multiple_of(ci * c, c), c)

    state_ref[...] = jnp.zeros_like(state_ref)
    sb_ref[...] = jnp.zeros_like(sb_ref)

    def body(ci, carry):
        r = rows(ci)

        v = v_ref[r, :]
        scores = lax.dot_general(qr_ref[slot_recur, r, :], kr_ref[slot_recur, r, :],
                                 (((1,), (1,)), ((), ())),
                                 preferred_element_type=F32) * mask_ref[...]
        o_ref[r, :] = (_dot(scores.astype(BF16), v)
                       + _dot(qd_ref[slot_recur, r, :], sb_ref[...])).astype(o_ref.dtype)
        upd = lax.dot_general(kd_ref[slot_recur, r, :], v, (((0,), (0,)), ((), ())),
                              preferred_element_type=F32)
        state = state_ref[...] * chunk_decay + upd
        state_ref[...] = state
        sb_ref[...] = state.astype(BF16)

        for lo in range(0, c, RET_PREP_ROWS):
            rr = pl.ds(pl.multiple_of(ci * c + lo, RET_PREP_ROWS), RET_PREP_ROWS)
            dec = pl.ds(lo, RET_PREP_ROWS)
            cos = cos_ref[rr, :]
            sin = sin_ref[rr, :]
            q = rotary(q_ref[rr, :], cos, sin)
            k = rotary(k_ref[rr, :], cos, sin)
            qr_ref[slot_prep, rr, :] = q.astype(BF16)
            qd_ref[slot_prep, rr, :] = (q * qdec_ref[dec, :]).astype(BF16)
            kr_ref[slot_prep, rr, :] = (k * k_scale).astype(BF16)
            kd_ref[slot_prep, rr, :] = (k * kdec_ref[dec, :]).astype(BF16)
        return carry

    lax.fori_loop(0, n_chunks, body, 0, unroll=2)


def retention(proj, batch, seq):
    n = batch * seq
    h = RET_HEADS
    log_g = jnp.log1p(-jnp.exp2(-5.0 - jnp.arange(h, dtype=F32)))
    freq = 1.0 / (ROPE_BASE ** jnp.linspace(0.0, 1.0, RET_DK // 2, dtype=F32))
    ang = jnp.arange(seq).astype(F32)[:, None] * freq[None, :]
    cos = jnp.cos(ang)
    sin = jnp.sin(ang)
    kq = RET_QK // RET_DK
    kv = 2 * RET_QK // RET_DV
    n_pairs = batch * h
    stages = 2

    def pair(step, stage):
        p = jnp.clip(step - stage, 0, n_pairs - 1)
        return p // h, p % h

    def block(stage, col0):
        def index(s):
            b, hh = pair(s, stage)
            return b, col0 + hh
        return index

    return pl.pallas_call(
        functools.partial(_retention_kernel, n_chunks=seq // RET_CHUNK, n_pairs=n_pairs),
        grid=(n_pairs + stages - 1,),
        in_specs=[pl.BlockSpec(memory_space=pltpu.SMEM),
                  pl.BlockSpec((seq, RET_DK), block(0, 0)),
                  pl.BlockSpec((seq, RET_DK), block(0, kq)),
                  pl.BlockSpec((seq, RET_DV), block(1, kv)),
                  pl.BlockSpec((seq, RET_DK // 2), lambda s: (0, 0)),
                  pl.BlockSpec((seq, RET_DK // 2), lambda s: (0, 0))],
        out_specs=pl.BlockSpec((seq, RET_DV), block(1, 0)),
        out_shape=jax.ShapeDtypeStruct((n, RET_V), BF16),
        scratch_shapes=[pltpu.VMEM((2, seq, RET_DK), BF16),
                        pltpu.VMEM((2, seq, RET_DK), BF16),
                        pltpu.VMEM((2, seq, RET_DK), BF16),
                        pltpu.VMEM((2, seq, RET_DK), BF16),
                        pltpu.VMEM((RET_CHUNK, RET_CHUNK), F32),
                        pltpu.VMEM((RET_CHUNK, RET_DK), F32),
                        pltpu.VMEM((RET_CHUNK, RET_DK), F32),
                        pltpu.VMEM((RET_DK, RET_DV), F32),
                        pltpu.VMEM((RET_DK, RET_DV), BF16)],
        compiler_params=pltpu.CompilerParams(dimension_semantics=("arbitrary",)),
        name="retention",
    )(log_g, proj, proj, proj, cos, sin)


def _pool_kernel(x_ref, gm_ref, pw_ref, sc_ref, gf_ref, rt_ref,
                 xo_ref, hn_ref, meta_ref, meta_t_ref, cnt_out_ref, hbuf, cnt_ref, *, tiles_per_seq):
    tm = x_ref.shape[0]
    step = pl.program_id(0)
    ti = step % tiles_per_seq
    x = x_ref[...]
    hn = _rms(x, gm_ref[...])

    @pl.when(ti == 0)
    def _():
        hbuf[0:POOL_HALO, :] = jnp.zeros((POOL_HALO, D_MODEL), F32)

    @pl.when(ti != 0)
    def _():
        hbuf[0:POOL_HALO, :] = hbuf[tm:tm + POOL_HALO, :]

    hbuf[POOL_HALO:POOL_HALO + tm, :] = hn

    t = ti * tm + lax.broadcasted_iota(jnp.int32, (tm, 1), 0)
    for gi, w in enumerate(POOL_WINDOWS):
        lo, hi = gi * POOL_G, (gi + 1) * POOL_G
        cur = hn[:, lo:hi]
        acc = hbuf[:, lo:hi]
        shift = 1
        while shift < w:
            acc = acc + pltpu.roll(acc, shift, 0)
            shift *= 2
        acc = acc[POOL_HALO:, :]
        inv = 1.0 / jnp.minimum(t + 1, w).astype(F32)
        mix = (acc * inv - cur).astype(BF16)
        o = _dot(mix, pw_ref[gi]) * sc_ref[:, lo:hi]
        xo_ref[:, lo:hi] = x[:, lo:hi] + o

    hn2 = _rms(xo_ref[...], gf_ref[...])
    hn_ref[...] = hn2

    rt = rt_ref[...]
    hn_hi = hn2.astype(BF16)
    hn_lo = (hn2 - hn_hi.astype(F32)).astype(BF16)
    rt_hi = rt.astype(BF16)
    rt_lo = (rt - rt_hi.astype(F32)).astype(BF16)
    logits = _dot(hn_hi, rt_hi) + (_dot(hn_hi, rt_lo) + _dot(hn_lo, rt_hi))
    lt = logits.T[:N_EXPERTS, :]
    expert = lax.broadcasted_iota(jnp.int32, lt.shape, 0)
    neg = jnp.float32(-jnp.inf)
    m1 = jnp.max(lt, axis=0, keepdims=True)
    i1 = jnp.min(jnp.where(lt == m1, expert, N_EXPERTS), axis=0, keepdims=True)
    lt2 = jnp.where(expert == i1, neg, lt)
    m2 = jnp.max(lt2, axis=0, keepdims=True)
    i2 = jnp.min(jnp.where(lt2 == m2, expert, N_EXPERTS), axis=0, keepdims=True)
    e2 = jnp.exp(m2 - m1)
    g1 = 1.0 / (1.0 + e2)
    g2 = e2 * g1

    @pl.when(step == 0)
    def _():
        cnt_ref[...] = jnp.zeros_like(cnt_ref)

    chosen = jnp.where((expert == i1) | (expert == i2), 1.0, 0.0)
    src = lax.broadcasted_iota(jnp.int32, (tm, tm), 0)
    dst = lax.broadcasted_iota(jnp.int32, (tm, tm), 1)
    earlier = jnp.where(src < dst, 1.0, 0.0).astype(BF16)
    before = _dot(chosen.astype(BF16), earlier) + cnt_ref[:, 0:1]
    r1 = jnp.sum(jnp.where(expert == i1, before, 0.0), axis=0, keepdims=True)
    r2 = jnp.sum(jnp.where(expert == i2, before, 0.0), axis=0, keepdims=True)
    cnt_ref[...] += jnp.sum(chosen, axis=1, keepdims=True)
    cnt_out_ref[...] = cnt_ref[...]

    fields = {META_E1: i1.astype(F32), META_E2: i2.astype(F32), META_R1: r1, META_R2: r2,
              META_G1: g1, META_G2: g2}
    zero_row = jnp.zeros_like(g1)
    meta_t = jnp.concatenate([fields.get(row, zero_row) for row in range(SUBLANES)], axis=0)
    meta_t_ref[...] = meta_t
    meta_ref[...] = jnp.concatenate([meta_t, jnp.zeros((LANES - SUBLANES, tm), F32)], axis=0).T


def pool_router(x, g_mix, pool_w, layer, pool_scale, g_ffn, router, seq, tm=512):
    n, d = x.shape
    tm = min(tm, seq)
    router_p = jnp.zeros((d, LANES), F32).at[:, :N_EXPERTS].set(router)
    ng = len(POOL_WINDOWS)
    return pl.pallas_call(
        functools.partial(_pool_kernel, tiles_per_seq=seq // tm),
        grid=(n // tm,),
        in_specs=[pl.BlockSpec((tm, d), lambda i: (i, 0)),
                  pl.BlockSpec((1, d), lambda i: (0, 0)),
                  pl.BlockSpec((None, ng, POOL_G, POOL_G), lambda i: (layer, 0, 0, 0)),
                  pl.BlockSpec((1, d), lambda i: (0, 0)),
                  pl.BlockSpec((1, d), lambda i: (0, 0)),
                  pl.BlockSpec((d, LANES), lambda i: (0, 0))],
        out_specs=[pl.BlockSpec((tm, d), lambda i: (i, 0)),
                   pl.BlockSpec((tm, d), lambda i: (i, 0)),
                   pl.BlockSpec((tm, LANES), lambda i: (i, 0)),
                   pl.BlockSpec((SUBLANES, tm), lambda i: (0, i)),
                   pl.BlockSpec((SUBLANES, LANES), lambda i: (0, 0))],
        out_shape=[jax.ShapeDtypeStruct((n, d), F32),
                   jax.ShapeDtypeStruct((n, d), F32),
                   jax.ShapeDtypeStruct((n, LANES), F32),
                   jax.ShapeDtypeStruct((SUBLANES, n), F32),
                   jax.ShapeDtypeStruct((SUBLANES, LANES), F32)],
        scratch_shapes=[pltpu.VMEM((POOL_HALO + tm, d), F32),
                        pltpu.VMEM((N_EXPERTS, LANES), F32)],
        compiler_params=pltpu.CompilerParams(dimension_semantics=("arbitrary",)),
        name="pool_router",
    )(x, g_mix.reshape(1, d), pool_w, pool_scale.reshape(1, d), g_ffn.reshape(1, d), router_p)


def moe_num_tiles(n):
    return TOP_K * n // MOE_TILE + N_EXPERTS


def moe_schedule(meta_t, counts_f, n):
    nt = moe_num_tiles(n)
    counts = counts_f[:N_EXPERTS, 0].astype(jnp.int32)
    tiles = (counts + MOE_TILE - 1) // MOE_TILE
    tile_end = jnp.cumsum(tiles)
    starts = (tile_end - tiles) * MOE_TILE
    expert_ids = jnp.arange(N_EXPERTS, dtype=jnp.int32)[:, None]

    def slots(e_row, r_row):
        e = meta_t[e_row].astype(jnp.int32)
        first = jnp.sum(jnp.where(e[None, :] == expert_ids, starts[:, None], 0), axis=0)
        return first + meta_t[r_row].astype(jnp.int32)

    pos = jnp.concatenate([slots(META_E1, META_R1), slots(META_E2, META_R2)])
    total = tile_end[-1]
    t = jnp.arange(nt, dtype=jnp.int32)
    t_used = jnp.minimum(t, total - 1)
    tile_expert = jnp.sum((tile_end[None, :] <= t_used[:, None]).astype(jnp.int32), axis=1)
    tile_expert = jnp.minimum(tile_expert, N_EXPERTS - 1)
    tile_valid = (t < total).astype(jnp.int32)
    pad_lo = starts + counts
    pad_hi = starts + tiles * MOE_TILE
    prev_expert = jnp.concatenate([jnp.full((1,), -1, jnp.int32), tile_expert[:-1]])
    group_first = (tile_expert != prev_expert).astype(jnp.int32)
    group_slot = (jnp.cumsum(group_first) - 1) % 2
    later = (expert_ids.T > expert_ids) & (tiles[None, :] > 0)
    next_used = jnp.min(jnp.where(later, expert_ids.T, N_EXPERTS), axis=1)
    next_used = jnp.where(next_used == N_EXPERTS, -1, next_used)
    next_expert = jnp.sum(jnp.where(tile_expert[None, :] == expert_ids, next_used[:, None], 0), axis=0)
    table = jnp.concatenate([tile_expert, tile_valid, group_first, group_slot, next_expert])
    return pos, table.astype(jnp.int32), pad_lo, pad_hi, total.reshape(1)


def _dispatch_kernel(pos_ref, lo_ref, hi_ref, used_ref, hn_ref, xs_hbm, zeros, sem, *, n, n_tiles):
    tb = hn_ref.shape[0]
    step = pl.program_id(0)
    base = step * tb

    def issue(r, c):
        tok = base + r
        src = hn_ref.at[pl.ds(r, 1), :]
        pltpu.make_async_copy(src, xs_hbm.at[pl.ds(pos_ref[tok], 1), :], sem).start(priority=0)
        pltpu.make_async_copy(src, xs_hbm.at[pl.ds(pos_ref[n + tok], 1), :], sem).start(priority=1)
        return c

    lax.fori_loop(0, tb, issue, 0, unroll=8)
    for _ in range(TOP_K):
        pltpu.make_async_copy(hn_ref, xs_hbm.at[pl.ds(0, tb), :], sem).wait()

    @pl.when(step == pl.num_programs(0) - 1)
    def _():
        zeros[...] = jnp.zeros_like(zeros)

        def pad_copy(slot):
            return pltpu.make_async_copy(zeros.at[pl.ds(0, 1), :], xs_hbm.at[pl.ds(slot, 1), :], sem)

        def issue_pad(s, c):
            pad_copy(s).start()
            return c

        def drain_pad(s, c):
            pad_copy(s).wait()
            return c

        for e in range(N_EXPERTS):
            lax.fori_loop(lo_ref[e], hi_ref[e], issue_pad, 0)
            lax.fori_loop(lo_ref[e], hi_ref[e], drain_pad, 0)

        def tile_copy(t):
            return pltpu.make_async_copy(zeros, xs_hbm.at[pl.ds(t * MOE_TILE, MOE_TILE), :], sem)

        def issue_tile(t, c):
            tile_copy(t).start()
            return c

        def drain_tile(t, c):
            tile_copy(t).wait()
            return c

        lax.fori_loop(used_ref[0], n_tiles, issue_tile, 0)
        lax.fori_loop(used_ref[0], n_tiles, drain_tile, 0)


def moe_dispatch(hn, pos, pad_lo, pad_hi, used, tb=512):
    n, d = hn.shape
    tb = min(tb, n)
    nt = moe_num_tiles(n)
    grid_spec = pltpu.PrefetchScalarGridSpec(
        num_scalar_prefetch=4,
        grid=(n // tb,),
        in_specs=[pl.BlockSpec((tb, d), lambda i, *_: (i, 0))],
        out_specs=pl.BlockSpec(memory_space=pl.ANY),
        scratch_shapes=[pltpu.VMEM((MOE_TILE, d), F32), pltpu.SemaphoreType.DMA],
    )
    return pl.pallas_call(
        functools.partial(_dispatch_kernel, n=n, n_tiles=nt),
        grid_spec=grid_spec,
        out_shape=jax.ShapeDtypeStruct((nt * MOE_TILE, d), F32),
        compiler_params=pltpu.CompilerParams(dimension_semantics=("arbitrary",)),
        name="moe_dispatch",
    )(pos, pad_lo, pad_hi, used, hn)


def _expert_kernel(tab_ref, x_ref, wgu_hbm, wd_hbm, o_ref, wa_buf, wu_buf, wd_buf, sems):
    t = pl.program_id(0)
    nt = pl.num_programs(0)
    f = wd_buf.shape[1]

    def field(k, tile):
        return tab_ref[k * nt + tile]

    def weight_copies(e, slot):
        return (pltpu.make_async_copy(wgu_hbm.at[e, :, pl.ds(0, f)], wa_buf.at[slot], sems.at[slot]),
                pltpu.make_async_copy(wgu_hbm.at[e, :, pl.ds(f, f)], wu_buf.at[slot], sems.at[slot]),
                pltpu.make_async_copy(wd_hbm.at[e], wd_buf.at[slot], sems.at[slot]))

    slot = field(TILE_SLOT, t)

    @pl.when(t == 0)
    def _():
        for cp in weight_copies(field(TILE_EXPERT, 0), field(TILE_SLOT, 0)):
            cp.start()

    @pl.when(field(TILE_FIRST, t) == 1)
    def _():
        for cp in weight_copies(field(TILE_EXPERT, t), slot):
            cp.wait()
        nxt = field(TILE_NEXT, t)

        @pl.when(nxt >= 0)
        def _():
            for cp in weight_copies(nxt, 1 - slot):
                cp.start()

    valid = field(TILE_VALID, t) == 1

    @pl.when(valid)
    def _():
        x = x_ref[...].astype(BF16)
        a = _dot(x, wa_buf[slot])
        u = _dot(x, wu_buf[slot])
        o_ref[...] = _dot((_silu(a) * u).astype(BF16), wd_buf[slot])

    @pl.when(jnp.logical_not(valid))
    def _():
        o_ref[...] = jnp.zeros_like(o_ref)


def moe_experts(xs, w_gu, w_down, table):
    rows, d = xs.shape
    f = w_down.shape[1]
    nt = rows // MOE_TILE
    grid_spec = pltpu.PrefetchScalarGridSpec(
        num_scalar_prefetch=1,
        grid=(nt,),
        in_specs=[pl.BlockSpec((MOE_TILE, d), lambda t, tab: (t, 0)),
                  pl.BlockSpec(memory_space=pl.ANY),
                  pl.BlockSpec(memory_space=pl.ANY)],
        out_specs=pl.BlockSpec((MOE_TILE, d), lambda t, tab: (t, 0)),
        scratch_shapes=[pltpu.VMEM((2, d, f), BF16), pltpu.VMEM((2, d, f), BF16),
                        pltpu.VMEM((2, f, d), BF16), pltpu.SemaphoreType.DMA((2,))],
    )
    return pl.pallas_call(
        _expert_kernel,
        grid_spec=grid_spec,
        out_shape=jax.ShapeDtypeStruct((rows, d), F32),
        compiler_params=pltpu.CompilerParams(dimension_semantics=("arbitrary",),
                                             vmem_limit_bytes=56 * 1024 * 1024),
        name="moe_experts",
    )(table, xs, w_gu, w_down)


def _combine_kernel(pos_ref, x_ref, meta_ref, g_ref, y_hbm, *refs, n, emit_x, emit_norm):
    ybuf, sem = refs[-2], refs[-1]
    tc = x_ref.shape[0]
    step = pl.program_id(0)
    slot = step % 2

    def start_gather(s, into):
        def issue(r, c):
            tok = s * tc + r
            for k in range(TOP_K):
                pltpu.make_async_copy(y_hbm.at[pl.ds(pos_ref[k * n + tok], 1), :],
                                      ybuf.at[into, k, pl.ds(r, 1), :],
                                      sem.at[into]).start(priority=k)
            return c

        lax.fori_loop(0, tc, issue, 0, unroll=8)

    @pl.when(step == 0)
    def _():
        start_gather(0, 0)

    @pl.when(step + 1 < pl.num_programs(0))
    def _():
        start_gather(step + 1, 1 - slot)

    for k in range(TOP_K):
        pltpu.make_async_copy(y_hbm.at[pl.ds(0, tc), :], ybuf.at[slot, k], sem.at[slot]).wait()

    meta = meta_ref[...]
    g1 = meta[:, META_G1:META_G1 + 1]
    g2 = meta[:, META_G2:META_G2 + 1]
    xn = x_ref[...] + g1 * ybuf[slot, 0] + g2 * ybuf[slot, 1]
    _emit_residual_outputs(xn, g_ref, refs, emit_x, emit_norm)


def moe_combine(x, y, meta, pos, g, *, emit_x=True, norm_dtype=None, tc=256):
    n, d = x.shape
    tc = min(tc, n)
    out_shape, out_specs = _residual_out_shapes(n, d, tc, emit_x, norm_dtype, lambda i, p: (i, 0))
    grid_spec = pltpu.PrefetchScalarGridSpec(
        num_scalar_prefetch=1,
        grid=(n // tc,),
        in_specs=[pl.BlockSpec((tc, d), lambda i, p: (i, 0)),
                  pl.BlockSpec((tc, LANES), lambda i, p: (i, 0)),
                  pl.BlockSpec((1, d), lambda i, p: (0, 0)),
                  pl.BlockSpec(memory_space=pl.ANY)],
        out_specs=out_specs,
        scratch_shapes=[pltpu.VMEM((2, TOP_K, tc, d), F32), pltpu.SemaphoreType.DMA((2,))],
    )
    return pl.pallas_call(
        functools.partial(_combine_kernel, n=n, emit_x=emit_x, emit_norm=norm_dtype is not None),
        grid_spec=grid_spec,
        out_shape=out_shape,
        compiler_params=pltpu.CompilerParams(dimension_semantics=("arbitrary",)),
        name="moe_combine",
    )(pos, x, meta, g.reshape(1, d), y)


def kernel(x, ret_w_in, ret_norm_g, ret_w_out, pool_w, pool_scale, ffn_w_gu, ffn_w_down,
           moe_router, moe_w_gu, moe_w_down, norm_mix_g, norm_ffn_g, norm_final_g):
    batch, seq, d = x.shape
    n = batch * seq
    xf = x.reshape(n, d)
    pool_w = pool_w.astype(BF16)
    n_moe, n_exp, _, f2 = moe_w_gu.shape
    f = moe_w_down.shape[2]
    moe_gu_rows = moe_w_gu.reshape(n_moe, n_exp * d, f2)
    moe_down_rows = moe_w_down.reshape(n_moe, n_exp * f, d)
    hn = rmsnorm(xf, norm_mix_g[0], BF16)
    for i in range(DEPTH):
        j = i // 2
        last = i == DEPTH - 1
        if i % 2 == 0:
            proj, moe_gu_b, w_out_b = retention_in_proj(hn, ret_w_in, j,
                                                        [(moe_gu_rows, j), (ret_w_out, j)])
            y = retention(proj, batch, seq)
            gate_block = (2 * RET_QK + RET_V) // RET_V
            xf, xg, ssq = down_residual(y, w_out_b, xf, norm_ffn_g[i],
                                        head_gate=(proj, gate_block, ret_norm_g[j]))
            hid, moe_down_b, ffn_down_b = swiglu_up(xg, ssq, ffn_w_gu, j,
                                                    [(moe_down_rows, j), (ffn_w_down, j)])
            (xf,) = down_residual(hid, ffn_down_b, xf)
            moe_gu_b = moe_gu_b.reshape(n_exp, d, f2)
            moe_down_b = moe_down_b.reshape(n_exp, f, d)
        else:
            xf, hn, meta, meta_t, counts = pool_router(xf, norm_mix_g[i], pool_w, j, pool_scale[j],
                                               norm_ffn_g[i], moe_router[j], seq)
            pos, tile_table, pad_lo, pad_hi, used = moe_schedule(meta_t, counts, n)
            xs = moe_dispatch(hn, pos, pad_lo, pad_hi, used)
            ys = moe_experts(xs, moe_gu_b, moe_down_b, tile_table)
            if last:
                (out,) = moe_combine(xf, ys, meta, pos, norm_final_g, emit_x=False, norm_dtype=F32)
            else:
                xf, hn = moe_combine(xf, ys, meta, pos, norm_mix_g[i + 1], norm_dtype=BF16)
    return out.reshape(batch, seq, d)
```

```python
import functools

import jax
import jax.numpy as jnp
from jax import lax
from jax.experimental import pallas as pl
from jax.experimental.pallas import tpu as pltpu

BF16 = jnp.bfloat16
F32 = jnp.float32

D_MODEL = 2048
DEPTH = 4
RET_HEADS = 8
RET_DK = D_MODEL // RET_HEADS
RET_DV = 2 * D_MODEL // RET_HEADS
RET_QK = RET_HEADS * RET_DK
RET_V = RET_HEADS * RET_DV
RET_CHUNK = 256
RET_PREP_ROWS = 32
ROPE_BASE = 10000.0
POOL_WINDOWS = (2, 4, 8, 16)
POOL_G = D_MODEL // len(POOL_WINDOWS)
POOL_HALO = 16
D_FF = 5632
N_EXPERTS = 8
TOP_K = 2
D_FF_EXPERT = 1408
NORM_EPS = 1e-6
LANES = 128
SUBLANES = 8
MOE_TILE = 256
SIDE_CAST_BLOCKS = 64
X_RING = 3
BF16_TILE_ROWS = 16
DOWN_VMEM_LIMIT = 60 * 1024 * 1024
META_E1, META_E2, META_R1, META_R2, META_G1, META_G2 = range(6)
TILE_EXPERT, TILE_VALID, TILE_FIRST, TILE_SLOT, TILE_NEXT = range(5)


def _rms(x, g):
    ms = jnp.mean(x * x, axis=-1, keepdims=True)
    return x * lax.rsqrt(ms + NORM_EPS) * g


def _silu(a):
    return a / (1.0 + jnp.exp(-a))


def _dot(a, b):
    return jnp.dot(a, b, preferred_element_type=F32)


def _rmsnorm_kernel(x_ref, g_ref, o_ref):
    o_ref[...] = _rms(x_ref[...], g_ref[...]).astype(o_ref.dtype)


def rmsnorm(x, g, out_dtype, tm=512):
    n, d = x.shape
    return pl.pallas_call(
        _rmsnorm_kernel,
        grid=(n // tm,),
        in_specs=[pl.BlockSpec((tm, d), lambda i: (i, 0)),
                  pl.BlockSpec((1, d), lambda i: (0, 0))],
        out_specs=pl.BlockSpec((tm, d), lambda i: (i, 0)),
        out_shape=jax.ShapeDtypeStruct((n, d), out_dtype),
        name="rmsnorm",
    )(x, g.reshape(1, d))


def _side_cast_specs(sides, steps, n_inner):
    in_specs, out_specs, shapes = [], [], []
    for side, side_layer in sides:
        _, rows, cols = side.shape
        n_blocks = 1
        while (n_blocks * 2 <= min(steps, SIDE_CAST_BLOCKS)
               and rows % (n_blocks * 2 * BF16_TILE_ROWS) == 0):
            n_blocks *= 2
        br = rows // n_blocks

        def index(j, i, n_blocks=n_blocks):
            return jnp.minimum(j * n_inner + i, n_blocks - 1)

        in_specs.append(pl.BlockSpec((None, br, cols),
                                     lambda j, i, l=side_layer, ix=index: (l, ix(j, i), 0)))
        out_specs.append(pl.BlockSpec((br, cols), lambda j, i, ix=index: (ix(j, i), 0)))
        shapes.append(jax.ShapeDtypeStruct((rows, cols), BF16))
    return in_specs, out_specs, shapes


def _cast_sides(refs, n_sides):
    for s_ref, so_ref in zip(refs[:n_sides], refs[n_sides + 1:2 * n_sides + 1]):
        so_ref[...] = s_ref[...].astype(BF16)


class _RowTileRing:
    def __init__(self, x_hbm, buf, sems, n_row_tiles):
        self.x_hbm, self.buf, self.sems, self.n_row_tiles = x_hbm, buf, sems, n_row_tiles

    def _copy(self, s):
        tm = self.buf.shape[1]
        rows = pl.ds(pl.multiple_of((s % self.n_row_tiles) * tm, tm), tm)
        return pltpu.make_async_copy(self.x_hbm.at[rows, :], self.buf.at[s % X_RING],
                                     self.sems.at[s % X_RING])

    def advance(self, step, total):
        @pl.when(step == 0)
        def _():
            for s in range(X_RING - 1):
                self._copy(s).start()

        @pl.when(step + X_RING - 1 < total)
        def _():
            self._copy(step + X_RING - 1).start()

    def tile(self, step):
        self._copy(step).wait()
        return self.buf[step % X_RING]


def _ring_scratch(tm, k):
    return [pltpu.VMEM((X_RING, tm, k), BF16), pltpu.SemaphoreType.DMA((X_RING,))]


def _in_proj_kernel(x_hbm, w_ref, perm_ref, *refs, n_qk_slabs, n_sides):
    o_ref = refs[n_sides]
    wb_ref, xbuf, xsem = refs[-3:]
    ni = pl.num_programs(1)
    step = pl.program_id(0) * ni + pl.program_id(1)
    ring = _RowTileRing(x_hbm, xbuf, xsem, ni)
    ring.advance(step, pl.num_programs(0) * ni)

    @pl.when(pl.program_id(1) == 0)
    def _():
        is_qk = pl.program_id(0) < n_qk_slabs

        @pl.when(is_qk)
        def _():
            for lo in range(0, wb_ref.shape[1], RET_DK):
                head = w_ref[:, lo:lo + RET_DK].astype(BF16)
                wb_ref[:, lo:lo + RET_DK] = _dot(head, perm_ref[...]).astype(BF16)

        @pl.when(jnp.logical_not(is_qk))
        def _():
            wb_ref[...] = w_ref[...].astype(BF16)

    o_ref[...] = _dot(ring.tile(step), wb_ref[...]).astype(o_ref.dtype)
    _cast_sides(refs, n_sides)


def retention_in_proj(x, w, layer, sides, tm=1024, tn=1024):
    n, k = x.shape
    f = w.shape[2]
    tm = min(tm, n)
    nj, ni = f // tn, n // tm
    half = RET_DK // 2
    src = jnp.arange(RET_DK)
    dst = jnp.where(src % 2 == 0, src // 2, half + src // 2)
    perm = (dst[:, None] == jnp.arange(RET_DK)[None, :]).astype(BF16)
    s_in, s_out, s_shapes = _side_cast_specs(sides, nj * ni, ni)
    return pl.pallas_call(
        functools.partial(_in_proj_kernel, n_qk_slabs=2 * RET_QK // tn, n_sides=len(sides)),
        grid=(nj, ni),
        in_specs=[pl.BlockSpec(memory_space=pl.ANY),
                  pl.BlockSpec((None, k, tn), lambda j, i: (layer, 0, j)),
                  pl.BlockSpec((RET_DK, RET_DK), lambda j, i: (0, 0)),
                  *s_in],
        out_specs=[pl.BlockSpec((tm, tn), lambda j, i: (i, j)), *s_out],
        out_shape=[jax.ShapeDtypeStruct((n, f), BF16), *s_shapes],
        scratch_shapes=[pltpu.VMEM((k, tn), BF16), *_ring_scratch(tm, k)],
        compiler_params=pltpu.CompilerParams(dimension_semantics=("arbitrary", "arbitrary")),
        name="retention_in_proj",
    )(x, w, perm, *(side for side, _ in sides))


def _swiglu_kernel(x_hbm, ssq_ref, wa_ref, wu_ref, *refs, n_sides):
    o_ref = refs[n_sides]
    wab_ref, wub_ref, xbuf, xsem = refs[-4:]
    ni = pl.num_programs(1)
    step = pl.program_id(0) * ni + pl.program_id(1)
    ring = _RowTileRing(x_hbm, xbuf, xsem, ni)
    ring.advance(step, pl.num_programs(0) * ni)

    @pl.when(pl.program_id(1) == 0)
    def _():
        wab_ref[...] = wa_ref[...].astype(BF16)
        wub_ref[...] = wu_ref[...].astype(BF16)

    x = ring.tile(step)
    rowscale = lax.rsqrt(jnp.sum(ssq_ref[...], axis=0) / x.shape[1] + NORM_EPS)
    a = _dot(x, wab_ref[...]) * rowscale
    u = _dot(x, wub_ref[...]) * rowscale
    o_ref[...] = (_silu(a) * u).astype(o_ref.dtype)
    _cast_sides(refs, n_sides)


def swiglu_up(xg, ssq, w_gu, layer, sides, tm=1024, tn=512):
    n, k = xg.shape
    f = w_gu.shape[2] // 2
    tm = min(tm, n)
    nj, ni = f // tn, n // tm
    ns = ssq.shape[0]
    s_in, s_out, s_shapes = _side_cast_specs(sides, nj * ni, ni)
    return pl.pallas_call(
        functools.partial(_swiglu_kernel, n_sides=len(sides)),
        grid=(nj, ni),
        in_specs=[pl.BlockSpec(memory_space=pl.ANY),
                  pl.BlockSpec((ns, tm, 1), lambda j, i: (0, i, 0)),
                  pl.BlockSpec((None, k, tn), lambda j, i: (layer, 0, j)),
                  pl.BlockSpec((None, k, tn), lambda j, i: (layer, 0, nj + j)),
                  *s_in],
        out_specs=[pl.BlockSpec((tm, tn), lambda j, i: (i, j)), *s_out],
        out_shape=[jax.ShapeDtypeStruct((n, f), BF16), *s_shapes],
        scratch_shapes=[pltpu.VMEM((k, tn), BF16), pltpu.VMEM((k, tn), BF16),
                        *_ring_scratch(tm, k)],
        compiler_params=pltpu.CompilerParams(dimension_semantics=("arbitrary", "arbitrary")),
        name="swiglu_up",
    )(xg, ssq, w_gu, w_gu, *(side for side, _ in sides))


def _emit_residual_outputs(xn, g_ref, refs, emit_x, emit_norm):
    pos = 0
    if emit_x:
        refs[pos][...] = xn
        pos += 1
    if emit_norm:
        refs[pos][...] = _rms(xn, g_ref[...]).astype(refs[pos].dtype)


def _residual_out_shapes(n, d, tm, emit_x, norm_dtype, index_map):
    out_shape, out_specs = [], []
    if emit_x:
        out_shape.append(jax.ShapeDtypeStruct((n, d), F32))
        out_specs.append(pl.BlockSpec((tm, d), index_map))
    if norm_dtype is not None:
        out_shape.append(jax.ShapeDtypeStruct((n, d), norm_dtype))
        out_specs.append(pl.BlockSpec((tm, d), index_map))
    return out_shape, out_specs


def _down_kernel(x_ref, w_ref, r_ref, g_ref, *refs, emit_scaled, head_gated):
    if head_gated:
        gate_ref, gain_ref, xs_ref = refs[0], refs[1], refs[-1]
        refs = refs[2:-1]
        for lo in range(0, x_ref.shape[1], RET_DV):
            o = x_ref[:, lo:lo + RET_DV].astype(F32)
            ms = jnp.mean(o * o, axis=-1, keepdims=True)
            o = o * lax.rsqrt(ms + NORM_EPS) * gain_ref[:, lo:lo + RET_DV]
            gate = gate_ref[:, lo:lo + RET_DV].astype(F32)
            xs_ref[:, lo:lo + RET_DV] = (o * _silu(gate)).astype(BF16)
        x = xs_ref[...]
    else:
        x = x_ref[...]
    xn = r_ref[...] + _dot(x, w_ref[...])
    refs[0][...] = xn
    if emit_scaled:
        refs[1][...] = (xn * g_ref[...]).astype(BF16)
        refs[2][...] = jnp.sum(xn * xn, axis=-1, keepdims=True)[None]


def down_residual(h, w, resid, g=None, *, head_gate=None, tm=512):
    n, kk = h.shape
    d = w.shape[1]
    tm = min(tm, n)
    emit_scaled = g is not None
    if g is None:
        g = jnp.ones((d,), F32)
    extra_in, extra_specs, extra_scratch = [], [], []
    if head_gate is not None:
        proj, gate_block, head_gain = head_gate
        extra_in = [proj, head_gain.reshape(1, kk)]
        extra_specs = [pl.BlockSpec((tm, kk), lambda i: (i, gate_block)),
                       pl.BlockSpec((1, kk), lambda i: (0, 0))]
        extra_scratch = [pltpu.VMEM((tm, kk), BF16)]
    out_shape = [jax.ShapeDtypeStruct((n, d), F32)]
    out_specs = [pl.BlockSpec((tm, d), lambda i: (i, 0))]
    if emit_scaled:
        out_shape += [jax.ShapeDtypeStruct((n, d), BF16), jax.ShapeDtypeStruct((1, n, 1), F32)]
        out_specs += [pl.BlockSpec((tm, d), lambda i: (i, 0)),
                      pl.BlockSpec((1, tm, 1), lambda i: (0, i, 0))]
    return pl.pallas_call(
        functools.partial(_down_kernel, emit_scaled=emit_scaled, head_gated=head_gate is not None),
        grid=(n // tm,),
        in_specs=[pl.BlockSpec((tm, kk), lambda i: (i, 0)),
                  pl.BlockSpec((kk, d), lambda i: (0, 0), pipeline_mode=pl.Buffered(1)),
                  pl.BlockSpec((tm, d), lambda i: (i, 0)),
                  pl.BlockSpec((1, d), lambda i: (0, 0)),
                  *extra_specs],
        out_specs=out_specs,
        out_shape=out_shape,
        scratch_shapes=extra_scratch,
        compiler_params=pltpu.CompilerParams(dimension_semantics=("arbitrary",),
                                             vmem_limit_bytes=DOWN_VMEM_LIMIT),
        name="down_residual",
    )(h, w, resid, g.reshape(1, d), *extra_in)


def _retention_kernel(lg_ref, q_ref, k_ref, v_ref, cos_ref, sin_ref, side_ref, o_ref, side_out_ref,
                      qr_ref, qd_ref, kr_ref, kd_ref, mask_ref, qdec_ref, kdec_ref,
                      state_ref, sb_ref, *, n_chunks, n_pairs):
    c = RET_CHUNK
    step = pl.program_id(0)
    slot_prep = step % 2
    slot_recur = 1 - slot_prep

    @pl.when(step == 0)
    def _():
        for ref in (qr_ref, qd_ref, kr_ref, kd_ref):
            ref[...] = jnp.zeros_like(ref)

    lg_prep = lg_ref[jnp.minimum(step, n_pairs - 1) % RET_HEADS]
    lg = lg_ref[jnp.clip(step - 1, 0, n_pairs - 1) % RET_HEADS]
    ii = lax.broadcasted_iota(jnp.int32, (c, c), 0)
    jj = lax.broadcasted_iota(jnp.int32, (c, c), 1)
    rel = (ii - jj).astype(F32)
    mask_ref[...] = jnp.where(rel >= 0.0, jnp.exp(jnp.maximum(rel, 0.0) * lg), 0.0)
    idx = lax.broadcasted_iota(jnp.int32, (c, 1), 0).astype(F32)
    k_scale = RET_DK ** -0.5
    qdec_ref[...] = jnp.broadcast_to(jnp.exp((idx + 1.0) * lg_prep), (c, RET_DK))
    kdec_ref[...] = jnp.broadcast_to(jnp.exp((c - 1.0 - idx) * lg_prep) * k_scale, (c, RET_DK))
    chunk_decay = jnp.exp(jnp.full((1, 1), float(c), F32) * lg)
    half = RET_DK // 2

    def rotary(xb, cos, sin):
        x = xb.astype(F32)
        x1, x2 = x[:, :half], x[:, half:]
        return jnp.concatenate([x1 * cos - x2 * sin, x2 * cos + x1 * sin], axis=1)

    def rows(ci):
        return pl.ds(pl.multiple_of(ci * c, c), c)

    state_ref[...] = jnp.zeros_like(state_ref)
    sb_ref[...] = jnp.zeros_like(sb_ref)

    def body(ci, carry):
        r = rows(ci)

        v = v_ref[r, :]
        scores = lax.dot_general(qr_ref[slot_recur, r, :], kr_ref[slot_recur, r, :],
                                 (((1,), (1,)), ((), ())),
                                 preferred_element_type=F32) * mask_ref[...]
        o_ref[r, :] = (_dot(scores.astype(BF16), v)
                       + _dot(qd_ref[slot_recur, r, :], sb_ref[...])).astype(o_ref.dtype)
        upd = lax.dot_general(kd_ref[slot_recur, r, :], v, (((0,), (0,)), ((), ())),
                              preferred_element_type=F32)
        state = state_ref[...] * chunk_decay + upd
        state_ref[...] = state
        sb_ref[...] = state.astype(BF16)

        for lo in range(0, c, RET_PREP_ROWS):
            rr = pl.ds(pl.multiple_of(ci * c + lo, RET_PREP_ROWS), RET_PREP_ROWS)
            dec = pl.ds(lo, RET_PREP_ROWS)
            cos = cos_ref[rr, :]
            sin = sin_ref[rr, :]
            q = rotary(q_ref[rr, :], cos, sin)
            k = rotary(k_ref[rr, :], cos, sin)
            qr_ref[slot_prep, rr, :] = q.astype(BF16)
            qd_ref[slot_prep, rr, :] = (q * qdec_ref[dec, :]).astype(BF16)
            kr_ref[slot_prep, rr, :] = (k * k_scale).astype(BF16)
            kd_ref[slot_prep, rr, :] = (k * kdec_ref[dec, :]).astype(BF16)

        side_rows = pl.ds(pl.multiple_of(ci * side_chunk, side_chunk), side_chunk)
        side_out_ref[side_rows, :] = side_ref[side_rows, :].astype(BF16)
        return carry

    side_chunk = side_ref.shape[0] // n_chunks
    lax.fori_loop(0, n_chunks, body, 0, unroll=2)


def retention(proj, batch, seq, side, side_layer):
    n = batch * seq
    h = RET_HEADS
    log_g = jnp.log1p(-jnp.exp2(-5.0 - jnp.arange(h, dtype=F32)))
    freq = 1.0 / (ROPE_BASE ** jnp.linspace(0.0, 1.0, RET_DK // 2, dtype=F32))
    ang = jnp.arange(seq).astype(F32)[:, None] * freq[None, :]
    cos = jnp.cos(ang)
    sin = jnp.sin(ang)
    kq = RET_QK // RET_DK
    kv = 2 * RET_QK // RET_DV
    n_pairs = batch * h
    stages = 2

    def pair(step, stage):
        p = jnp.clip(step - stage, 0, n_pairs - 1)
        return p // h, p % h

    def block(stage, col0):
        def index(s):
            b, hh = pair(s, stage)
            return b, col0 + hh
        return index

    steps = n_pairs + stages - 1
    n_chunks = seq // RET_CHUNK
    _, side_rows, side_cols = side.shape
    side_blocks = 1
    while (side_blocks * 2 <= min(steps, SIDE_CAST_BLOCKS)
           and side_rows % (side_blocks * 2 * n_chunks * BF16_TILE_ROWS) == 0):
        side_blocks *= 2
    side_br = side_rows // side_blocks

    def side_index(s):
        return jnp.minimum(s, side_blocks - 1)

    return pl.pallas_call(
        functools.partial(_retention_kernel, n_chunks=n_chunks, n_pairs=n_pairs),
        grid=(steps,),
        in_specs=[pl.BlockSpec(memory_space=pltpu.SMEM),
                  pl.BlockSpec((seq, RET_DK), block(0, 0)),
                  pl.BlockSpec((seq, RET_DK), block(0, kq)),
                  pl.BlockSpec((seq, RET_DV), block(1, kv)),
                  pl.BlockSpec((seq, RET_DK // 2), lambda s: (0, 0)),
                  pl.BlockSpec((seq, RET_DK // 2), lambda s: (0, 0)),
                  pl.BlockSpec((None, side_br, side_cols), lambda s: (side_layer, side_index(s), 0))],
        out_specs=[pl.BlockSpec((seq, RET_DV), block(1, 0)),
                   pl.BlockSpec((side_br, side_cols), lambda s: (side_index(s), 0))],
        out_shape=[jax.ShapeDtypeStruct((n, RET_V), BF16),
                   jax.ShapeDtypeStruct((side_rows, side_cols), BF16)],
        scratch_shapes=[pltpu.VMEM((2, seq, RET_DK), BF16),
                        pltpu.VMEM((2, seq, RET_DK), BF16),
                        pltpu.VMEM((2, seq, RET_DK), BF16),
                        pltpu.VMEM((2, seq, RET_DK), BF16),
                        pltpu.VMEM((RET_CHUNK, RET_CHUNK), F32),
                        pltpu.VMEM((RET_CHUNK, RET_DK), F32),
                        pltpu.VMEM((RET_CHUNK, RET_DK), F32),
                        pltpu.VMEM((RET_DK, RET_DV), F32),
                        pltpu.VMEM((RET_DK, RET_DV), BF16)],
        compiler_params=pltpu.CompilerParams(dimension_semantics=("arbitrary",)),
        name="retention",
    )(log_g, proj, proj, proj, cos, sin, side)


def _pool_kernel(x_ref, gm_ref, pw_ref, sc_ref, gf_ref, rt_ref,
                 xo_ref, hn_ref, meta_ref, meta_t_ref, cnt_out_ref, hbuf, cnt_ref, *, tiles_per_seq):
    tm = x_ref.shape[0]
    step = pl.program_id(0)
    ti = step % tiles_per_seq
    x = x_ref[...]
    hn = _rms(x, gm_ref[...])

    @pl.when(ti == 0)
    def _():
        hbuf[0:POOL_HALO, :] = jnp.zeros((POOL_HALO, D_MODEL), F32)

    @pl.when(ti != 0)
    def _():
        hbuf[0:POOL_HALO, :] = hbuf[tm:tm + POOL_HALO, :]

    hbuf[POOL_HALO:POOL_HALO + tm, :] = hn

    t = ti * tm + lax.broadcasted_iota(jnp.int32, (tm, 1), 0)
    for gi, w in enumerate(POOL_WINDOWS):
        lo, hi = gi * POOL_G, (gi + 1) * POOL_G
        cur = hn[:, lo:hi]
        acc = hbuf[:, lo:hi]
        shift = 1
        while shift < w:
            acc = acc + pltpu.roll(acc, shift, 0)
            shift *= 2
        acc = acc[POOL_HALO:, :]
        inv = 1.0 / jnp.minimum(t + 1, w).astype(F32)
        mix = (acc * inv - cur).astype(BF16)
        o = _dot(mix, pw_ref[gi]) * sc_ref[:, lo:hi]
        xo_ref[:, lo:hi] = x[:, lo:hi] + o

    hn2 = _rms(xo_ref[...], gf_ref[...])
    hn_ref[...] = hn2

    rt = rt_ref[...]
    hn_hi = hn2.astype(BF16)
    hn_lo = (hn2 - hn_hi.astype(F32)).astype(BF16)
    rt_hi = rt.astype(BF16)
    rt_lo = (rt - rt_hi.astype(F32)).astype(BF16)
    logits = _dot(hn_hi, rt_hi) + (_dot(hn_hi, rt_lo) + _dot(hn_lo, rt_hi))
    lt = logits.T[:N_EXPERTS, :]
    expert = lax.broadcasted_iota(jnp.int32, lt.shape, 0)
    neg = jnp.float32(-jnp.inf)
    m1 = jnp.max(lt, axis=0, keepdims=True)
    i1 = jnp.min(jnp.where(lt == m1, expert, N_EXPERTS), axis=0, keepdims=True)
    lt2 = jnp.where(expert == i1, neg, lt)
    m2 = jnp.max(lt2, axis=0, keepdims=True)
    i2 = jnp.min(jnp.where(lt2 == m2, expert, N_EXPERTS), axis=0, keepdims=True)
    e2 = jnp.exp(m2 - m1)
    g1 = 1.0 / (1.0 + e2)
    g2 = e2 * g1

    @pl.when(step == 0)
    def _():
        cnt_ref[...] = jnp.zeros_like(cnt_ref)

    chosen = jnp.where((expert == i1) | (expert == i2), 1.0, 0.0)
    src = lax.broadcasted_iota(jnp.int32, (tm, tm), 0)
    dst = lax.broadcasted_iota(jnp.int32, (tm, tm), 1)
    earlier = jnp.where(src < dst, 1.0, 0.0).astype(BF16)
    before = _dot(chosen.astype(BF16), earlier) + cnt_ref[:, 0:1]
    r1 = jnp.sum(jnp.where(expert == i1, before, 0.0), axis=0, keepdims=True)
    r2 = jnp.sum(jnp.where(expert == i2, before, 0.0), axis=0, keepdims=True)
    cnt_ref[...] += jnp.sum(chosen, axis=1, keepdims=True)
    cnt_out_ref[...] = cnt_ref[...]

    fields = {META_E1: i1.astype(F32), META_E2: i2.astype(F32), META_R1: r1, META_R2: r2,
              META_G1: g1, META_G2: g2}
    zero_row = jnp.zeros_like(g1)
    meta_t = jnp.concatenate([fields.get(row, zero_row) for row in range(SUBLANES)], axis=0)
    meta_t_ref[...] = meta_t
    meta_ref[...] = jnp.concatenate([meta_t, jnp.zeros((LANES - SUBLANES, tm), F32)], axis=0).T


def pool_router(x, g_mix, pool_w, layer, pool_scale, g_ffn, router, seq, tm=512):
    n, d = x.shape
    tm = min(tm, seq)
    router_p = jnp.zeros((d, LANES), F32).at[:, :N_EXPERTS].set(router)
    ng = len(POOL_WINDOWS)
    return pl.pallas_call(
        functools.partial(_pool_kernel, tiles_per_seq=seq // tm),
        grid=(n // tm,),
        in_specs=[pl.BlockSpec((tm, d), lambda i: (i, 0)),
                  pl.BlockSpec((1, d), lambda i: (0, 0)),
                  pl.BlockSpec((None, ng, POOL_G, POOL_G), lambda i: (layer, 0, 0, 0)),
                  pl.BlockSpec((1, d), lambda i: (0, 0)),
                  pl.BlockSpec((1, d), lambda i: (0, 0)),
                  pl.BlockSpec((d, LANES), lambda i: (0, 0))],
        out_specs=[pl.BlockSpec((tm, d), lambda i: (i, 0)),
                   pl.BlockSpec((tm, d), lambda i: (i, 0)),
                   pl.BlockSpec((tm, LANES), lambda i: (i, 0)),
                   pl.BlockSpec((SUBLANES, tm), lambda i: (0, i)),
                   pl.BlockSpec((SUBLANES, LANES), lambda i: (0, 0))],
        out_shape=[jax.ShapeDtypeStruct((n, d), F32),
                   jax.ShapeDtypeStruct((n, d), F32),
                   jax.ShapeDtypeStruct((n, LANES), F32),
                   jax.ShapeDtypeStruct((SUBLANES, n), F32),
                   jax.ShapeDtypeStruct((SUBLANES, LANES), F32)],
        scratch_shapes=[pltpu.VMEM((POOL_HALO + tm, d), F32),
                        pltpu.VMEM((N_EXPERTS, LANES), F32)],
        compiler_params=pltpu.CompilerParams(dimension_semantics=("arbitrary",)),
        name="pool_router",
    )(x, g_mix.reshape(1, d), pool_w, pool_scale.reshape(1, d), g_ffn.reshape(1, d), router_p)


def moe_num_tiles(n):
    return TOP_K * n // MOE_TILE + N_EXPERTS


def moe_schedule(meta_t, counts_f, n):
    nt = moe_num_tiles(n)
    counts = counts_f[:N_EXPERTS, 0].astype(jnp.int32)
    tiles = (counts + MOE_TILE - 1) // MOE_TILE
    tile_end = jnp.cumsum(tiles)
    starts = (tile_end - tiles) * MOE_TILE
    expert_ids = jnp.arange(N_EXPERTS, dtype=jnp.int32)[:, None]

    def slots(e_row, r_row):
        e = meta_t[e_row].astype(jnp.int32)
        first = jnp.sum(jnp.where(e[None, :] == expert_ids, starts[:, None], 0), axis=0)
        return first + meta_t[r_row].astype(jnp.int32)

    pos = jnp.concatenate([slots(META_E1, META_R1), slots(META_E2, META_R2)])
    total = tile_end[-1]
    t = jnp.arange(nt, dtype=jnp.int32)
    t_used = jnp.minimum(t, total - 1)
    tile_expert = jnp.sum((tile_end[None, :] <= t_used[:, None]).astype(jnp.int32), axis=1)
    tile_expert = jnp.minimum(tile_expert, N_EXPERTS - 1)
    tile_valid = (t < total).astype(jnp.int32)
    pad_lo = starts + counts
    pad_hi = starts + tiles * MOE_TILE
    prev_expert = jnp.concatenate([jnp.full((1,), -1, jnp.int32), tile_expert[:-1]])
    group_first = (tile_expert != prev_expert).astype(jnp.int32)
    group_slot = (jnp.cumsum(group_first) - 1) % 2
    later = (expert_ids.T > expert_ids) & (tiles[None, :] > 0)
    next_used = jnp.min(jnp.where(later, expert_ids.T, N_EXPERTS), axis=1)
    next_used = jnp.where(next_used == N_EXPERTS, -1, next_used)
    next_expert = jnp.sum(jnp.where(tile_expert[None, :] == expert_ids, next_used[:, None], 0), axis=0)
    table = jnp.concatenate([tile_expert, tile_valid, group_first, group_slot, next_expert])
    return pos, table.astype(jnp.int32), pad_lo, pad_hi, total.reshape(1)


def _dispatch_kernel(pos_ref, lo_ref, hi_ref, used_ref, hn_ref, xs_hbm, zeros, sem, *, n, n_tiles):
    tb = hn_ref.shape[0]
    step = pl.program_id(0)
    base = step * tb

    def issue(r, c):
        tok = base + r
        src = hn_ref.at[pl.ds(r, 1), :]
        pltpu.make_async_copy(src, xs_hbm.at[pl.ds(pos_ref[tok], 1), :], sem).start(priority=0)
        pltpu.make_async_copy(src, xs_hbm.at[pl.ds(pos_ref[n + tok], 1), :], sem).start(priority=1)
        return c

    lax.fori_loop(0, tb, issue, 0, unroll=8)
    for _ in range(TOP_K):
        pltpu.make_async_copy(hn_ref, xs_hbm.at[pl.ds(0, tb), :], sem).wait()

    @pl.when(step == pl.num_programs(0) - 1)
    def _():
        zeros[...] = jnp.zeros_like(zeros)

        def pad_copy(slot):
            return pltpu.make_async_copy(zeros.at[pl.ds(0, 1), :], xs_hbm.at[pl.ds(slot, 1), :], sem)

        def issue_pad(s, c):
            pad_copy(s).start()
            return c

        def drain_pad(s, c):
            pad_copy(s).wait()
            return c

        for e in range(N_EXPERTS):
            lax.fori_loop(lo_ref[e], hi_ref[e], issue_pad, 0)
            lax.fori_loop(lo_ref[e], hi_ref[e], drain_pad, 0)

        def tile_copy(t):
            return pltpu.make_async_copy(zeros, xs_hbm.at[pl.ds(t * MOE_TILE, MOE_TILE), :], sem)

        def issue_tile(t, c):
            tile_copy(t).start()
            return c

        def drain_tile(t, c):
            tile_copy(t).wait()
            return c

        lax.fori_loop(used_ref[0], n_tiles, issue_tile, 0)
        lax.fori_loop(used_ref[0], n_tiles, drain_tile, 0)


def moe_dispatch(hn, pos, pad_lo, pad_hi, used, tb=512):
    n, d = hn.shape
    tb = min(tb, n)
    nt = moe_num_tiles(n)
    grid_spec = pltpu.PrefetchScalarGridSpec(
        num_scalar_prefetch=4,
        grid=(n // tb,),
        in_specs=[pl.BlockSpec((tb, d), lambda i, *_: (i, 0))],
        out_specs=pl.BlockSpec(memory_space=pl.ANY),
        scratch_shapes=[pltpu.VMEM((MOE_TILE, d), F32), pltpu.SemaphoreType.DMA],
    )
    return pl.pallas_call(
        functools.partial(_dispatch_kernel, n=n, n_tiles=nt),
        grid_spec=grid_spec,
        out_shape=jax.ShapeDtypeStruct((nt * MOE_TILE, d), F32),
        compiler_params=pltpu.CompilerParams(dimension_semantics=("arbitrary",)),
        name="moe_dispatch",
    )(pos, pad_lo, pad_hi, used, hn)


def _expert_kernel(tab_ref, x_ref, wgu_hbm, wd_hbm, o_ref, wa_buf, wu_buf, wd_buf, sems):
    t = pl.program_id(0)
    nt = pl.num_programs(0)
    f = wd_buf.shape[1]

    def field(k, tile):
        return tab_ref[k * nt + tile]

    def weight_copies(e, slot):
        return (pltpu.make_async_copy(wgu_hbm.at[e, :, pl.ds(0, f)], wa_buf.at[slot], sems.at[slot]),
                pltpu.make_async_copy(wgu_hbm.at[e, :, pl.ds(f, f)], wu_buf.at[slot], sems.at[slot]),
                pltpu.make_async_copy(wd_hbm.at[e], wd_buf.at[slot], sems.at[slot]))

    slot = field(TILE_SLOT, t)

    @pl.when(t == 0)
    def _():
        for cp in weight_copies(field(TILE_EXPERT, 0), field(TILE_SLOT, 0)):
            cp.start()

    @pl.when(field(TILE_FIRST, t) == 1)
    def _():
        for cp in weight_copies(field(TILE_EXPERT, t), slot):
            cp.wait()
        nxt = field(TILE_NEXT, t)

        @pl.when(nxt >= 0)
        def _():
            for cp in weight_copies(nxt, 1 - slot):
                cp.start()

    valid = field(TILE_VALID, t) == 1

    @pl.when(valid)
    def _():
        x = x_ref[...].astype(BF16)
        a = _dot(x, wa_buf[slot])
        u = _dot(x, wu_buf[slot])
        o_ref[...] = _dot((_silu(a) * u).astype(BF16), wd_buf[slot])

    @pl.when(jnp.logical_not(valid))
    def _():
        o_ref[...] = jnp.zeros_like(o_ref)


def moe_experts(xs, w_gu, w_down, table):
    rows, d = xs.shape
    f = w_down.shape[1]
    nt = rows // MOE_TILE
    grid_spec = pltpu.PrefetchScalarGridSpec(
        num_scalar_prefetch=1,
        grid=(nt,),
        in_specs=[pl.BlockSpec((MOE_TILE, d), lambda t, tab: (t, 0)),
                  pl.BlockSpec(memory_space=pl.ANY),
                  pl.BlockSpec(memory_space=pl.ANY)],
        out_specs=pl.BlockSpec((MOE_TILE, d), lambda t, tab: (t, 0)),
        scratch_shapes=[pltpu.VMEM((2, d, f), BF16), pltpu.VMEM((2, d, f), BF16),
                        pltpu.VMEM((2, f, d), BF16), pltpu.SemaphoreType.DMA((2,))],
    )
    return pl.pallas_call(
        _expert_kernel,
        grid_spec=grid_spec,
        out_shape=jax.ShapeDtypeStruct((rows, d), F32),
        compiler_params=pltpu.CompilerParams(dimension_semantics=("arbitrary",),
                                             vmem_limit_bytes=56 * 1024 * 1024),
        name="moe_experts",
    )(table, xs, w_gu, w_down)


def _combine_kernel(pos_ref, x_ref, meta_ref, g_ref, y_hbm, *refs, n, emit_x, emit_norm):
    ybuf, sem = refs[-2], refs[-1]
    tc = x_ref.shape[0]
    step = pl.program_id(0)
    slot = step % 2

    def start_gather(s, into):
        def issue(r, c):
            tok = s * tc + r
            for k in range(TOP_K):
                pltpu.make_async_copy(y_hbm.at[pl.ds(pos_ref[k * n + tok], 1), :],
                                      ybuf.at[into, k, pl.ds(r, 1), :],
                                      sem.at[into]).start(priority=k)
            return c

        lax.fori_loop(0, tc, issue, 0, unroll=8)

    @pl.when(step == 0)
    def _():
        start_gather(0, 0)

    @pl.when(step + 1 < pl.num_programs(0))
    def _():
        start_gather(step + 1, 1 - slot)

    for k in range(TOP_K):
        pltpu.make_async_copy(y_hbm.at[pl.ds(0, tc), :], ybuf.at[slot, k], sem.at[slot]).wait()

    meta = meta_ref[...]
    g1 = meta[:, META_G1:META_G1 + 1]
    g2 = meta[:, META_G2:META_G2 + 1]
    xn = x_ref[...] + g1 * ybuf[slot, 0] + g2 * ybuf[slot, 1]
    _emit_residual_outputs(xn, g_ref, refs, emit_x, emit_norm)


def moe_combine(x, y, meta, pos, g, *, emit_x=True, norm_dtype=None, tc=256):
    n, d = x.shape
    tc = min(tc, n)
    out_shape, out_specs = _residual_out_shapes(n, d, tc, emit_x, norm_dtype, lambda i, p: (i, 0))
    grid_spec = pltpu.PrefetchScalarGridSpec(
        num_scalar_prefetch=1,
        grid=(n // tc,),
        in_specs=[pl.BlockSpec((tc, d), lambda i, p: (i, 0)),
                  pl.BlockSpec((tc, LANES), lambda i, p: (i, 0)),
                  pl.BlockSpec((1, d), lambda i, p: (0, 0)),
                  pl.BlockSpec(memory_space=pl.ANY)],
        out_specs=out_specs,
        scratch_shapes=[pltpu.VMEM((2, TOP_K, tc, d), F32), pltpu.SemaphoreType.DMA((2,))],
    )
    return pl.pallas_call(
        functools.partial(_combine_kernel, n=n, emit_x=emit_x, emit_norm=norm_dtype is not None),
        grid_spec=grid_spec,
        out_shape=out_shape,
        compiler_params=pltpu.CompilerParams(dimension_semantics=("arbitrary",)),
        name="moe_combine",
    )(pos, x, meta, g.reshape(1, d), y)


def kernel(x, ret_w_in, ret_norm_g, ret_w_out, pool_w, pool_scale, ffn_w_gu, ffn_w_down,
           moe_router, moe_w_gu, moe_w_down, norm_mix_g, norm_ffn_g, norm_final_g):
    batch, seq, d = x.shape
    n = batch * seq
    xf = x.reshape(n, d)
    pool_w = pool_w.astype(BF16)
    n_moe, n_exp, _, f2 = moe_w_gu.shape
    f = moe_w_down.shape[2]
    moe_gu_rows = moe_w_gu.reshape(n_moe, n_exp * d, f2)
    moe_down_rows = moe_w_down.reshape(n_moe, n_exp * f, d)
    hn = rmsnorm(xf, norm_mix_g[0], BF16)
    for i in range(DEPTH):
        j = i // 2
        last = i == DEPTH - 1
        if i % 2 == 0:
            proj, w_out_b = retention_in_proj(hn, ret_w_in, j, [(ret_w_out, j)])
            y, moe_gu_b = retention(proj, batch, seq, moe_gu_rows, j)
            gate_block = (2 * RET_QK + RET_V) // RET_V
            xf, xg, ssq = down_residual(y, w_out_b, xf, norm_ffn_g[i],
                                        head_gate=(proj, gate_block, ret_norm_g[j]))
            hid, moe_down_b, ffn_down_b = swiglu_up(xg, ssq, ffn_w_gu, j,
                                                    [(moe_down_rows, j), (ffn_w_down, j)])
            (xf,) = down_residual(hid, ffn_down_b, xf)
            moe_gu_b = moe_gu_b.reshape(n_exp, d, f2)
            moe_down_b = moe_down_b.reshape(n_exp, f, d)
        else:
            xf, hn, meta, meta_t, counts = pool_router(xf, norm_mix_g[i], pool_w, j, pool_scale[j],
                                               norm_ffn_g[i], moe_router[j], seq)
            pos, tile_table, pad_lo, pad_hi, used = moe_schedule(meta_t, counts, n)
            xs = moe_dispatch(hn, pos, pad_lo, pad_hi, used)
            ys = moe_experts(xs, moe_gu_b, moe_down_b, tile_table)
            if last:
                (out,) = moe_combine(xf, ys, meta, pos, norm_final_g, emit_x=False, norm_dtype=F32)
            else:
                xf, hn = moe_combine(xf, ys, meta, pos, norm_mix_g[i + 1], norm_dtype=BF16)
    return out.reshape(batch, seq, d)
```

```python
import functools

import jax
import jax.numpy as jnp
from jax import lax
from jax.experimental import pallas as pl
from jax.experimental.pallas import tpu as pltpu

BF16 = jnp.bfloat16
F32 = jnp.float32

D_MODEL = 2048
DEPTH = 4
RET_HEADS = 8
RET_DK = D_MODEL // RET_HEADS
RET_DV = 2 * D_MODEL // RET_HEADS
RET_QK = RET_HEADS * RET_DK
RET_V = RET_HEADS * RET_DV
RET_CHUNK = 256
RET_PREP_ROWS = 32
ROPE_BASE = 10000.0
POOL_WINDOWS = (2, 4, 8, 16)
POOL_G = D_MODEL // len(POOL_WINDOWS)
POOL_HALO = 16
N_EXPERTS = 8
TOP_K = 2
NORM_EPS = 1e-6
LANES = 128
SUBLANES = 8
BF16_TILE_ROWS = 16
MOE_TILE = 256
SIDE_CAST_BLOCKS = 64
X_RING = 3
DOWN_VMEM_LIMIT = 60 * 1024 * 1024
EXPERT_VMEM_LIMIT = 56 * 1024 * 1024
META_E1, META_E2, META_R1, META_R2, META_G1, META_G2 = range(6)
TILE_EXPERT, TILE_VALID, TILE_FIRST, TILE_SLOT, TILE_NEXT = range(5)


def _rms(x, g):
    ms = jnp.mean(x * x, axis=-1, keepdims=True)
    return x * lax.rsqrt(ms + NORM_EPS) * g


def _silu(a):
    return a / (1.0 + jnp.exp(-a))


def _dot(a, b):
    return jnp.dot(a, b, preferred_element_type=F32)


def _rmsnorm_kernel(x_ref, g_ref, o_ref):
    o_ref[...] = _rms(x_ref[...], g_ref[...]).astype(o_ref.dtype)


def rmsnorm(x, g, out_dtype, tm=512):
    n, d = x.shape
    return pl.pallas_call(
        _rmsnorm_kernel,
        grid=(n // tm,),
        in_specs=[pl.BlockSpec((tm, d), lambda i: (i, 0)),
                  pl.BlockSpec((1, d), lambda i: (0, 0))],
        out_specs=pl.BlockSpec((tm, d), lambda i: (i, 0)),
        out_shape=jax.ShapeDtypeStruct((n, d), out_dtype),
        name="rmsnorm",
    )(x, g.reshape(1, d))


def _side_cast_specs(sides, steps, n_inner):
    in_specs, out_specs, shapes = [], [], []
    for side, side_layer in sides:
        _, rows, cols = side.shape
        n_blocks = 1
        while (n_blocks * 2 <= min(steps, SIDE_CAST_BLOCKS)
               and rows % (n_blocks * 2 * BF16_TILE_ROWS) == 0):
            n_blocks *= 2
        br = rows // n_blocks

        def index(j, i, n_blocks=n_blocks):
            return jnp.minimum(j * n_inner + i, n_blocks - 1)

        in_specs.append(pl.BlockSpec((None, br, cols),
                                     lambda j, i, l=side_layer, ix=index: (l, ix(j, i), 0)))
        out_specs.append(pl.BlockSpec((br, cols), lambda j, i, ix=index: (ix(j, i), 0)))
        shapes.append(jax.ShapeDtypeStruct((rows, cols), BF16))
    return in_specs, out_specs, shapes


def _cast_sides(refs, n_sides):
    for s_ref, so_ref in zip(refs[:n_sides], refs[n_sides + 1:2 * n_sides + 1]):
        so_ref[...] = s_ref[...].astype(BF16)


class _RowTileRing:
    def __init__(self, x_hbm, buf, sems, n_row_tiles):
        self.x_hbm, self.buf, self.sems, self.n_row_tiles = x_hbm, buf, sems, n_row_tiles

    def _copy(self, s):
        tm = self.buf.shape[1]
        rows = pl.ds(pl.multiple_of((s % self.n_row_tiles) * tm, tm), tm)
        return pltpu.make_async_copy(self.x_hbm.at[rows, :], self.buf.at[s % X_RING],
                                     self.sems.at[s % X_RING])

    def advance(self, step, total):
        @pl.when(step == 0)
        def _():
            for s in range(X_RING - 1):
                self._copy(s).start()

        @pl.when(step + X_RING - 1 < total)
        def _():
            self._copy(step + X_RING - 1).start()

    def tile(self, step):
        self._copy(step).wait()
        return self.buf[step % X_RING]


def _ring_scratch(tm, k):
    return [pltpu.VMEM((X_RING, tm, k), BF16), pltpu.SemaphoreType.DMA((X_RING,))]


def _in_proj_kernel(x_hbm, w_ref, perm_ref, *refs, n_qk_slabs, n_sides):
    o_ref = refs[n_sides]
    wb_ref, xbuf, xsem = refs[-3:]
    ni = pl.num_programs(1)
    step = pl.program_id(0) * ni + pl.program_id(1)
    ring = _RowTileRing(x_hbm, xbuf, xsem, ni)
    ring.advance(step, pl.num_programs(0) * ni)

    @pl.when(pl.program_id(1) == 0)
    def _():
        is_qk = pl.program_id(0) < n_qk_slabs

        @pl.when(is_qk)
        def _():
            for lo in range(0, wb_ref.shape[1], RET_DK):
                head = w_ref[:, lo:lo + RET_DK].astype(BF16)
                wb_ref[:, lo:lo + RET_DK] = _dot(head, perm_ref[...]).astype(BF16)

        @pl.when(jnp.logical_not(is_qk))
        def _():
            wb_ref[...] = w_ref[...].astype(BF16)

    o_ref[...] = _dot(ring.tile(step), wb_ref[...]).astype(o_ref.dtype)
    _cast_sides(refs, n_sides)


def retention_in_proj(x, w, layer, sides, tm=1024, tn=1024):
    n, k = x.shape
    f = w.shape[2]
    tm = min(tm, n)
    nj, ni = f // tn, n // tm
    assert f % tn == 0 and n % tm == 0 and nj * ni >= X_RING - 1
    half = RET_DK // 2
    src = jnp.arange(RET_DK)
    dst = jnp.where(src % 2 == 0, src // 2, half + src // 2)
    perm = (dst[:, None] == jnp.arange(RET_DK)[None, :]).astype(BF16)
    s_in, s_out, s_shapes = _side_cast_specs(sides, nj * ni, ni)
    return pl.pallas_call(
        functools.partial(_in_proj_kernel, n_qk_slabs=2 * RET_QK // tn, n_sides=len(sides)),
        grid=(nj, ni),
        in_specs=[pl.BlockSpec(memory_space=pl.ANY),
                  pl.BlockSpec((None, k, tn), lambda j, i: (layer, 0, j)),
                  pl.BlockSpec((RET_DK, RET_DK), lambda j, i: (0, 0)),
                  *s_in],
        out_specs=[pl.BlockSpec((tm, tn), lambda j, i: (i, j)), *s_out],
        out_shape=[jax.ShapeDtypeStruct((n, f), BF16), *s_shapes],
        scratch_shapes=[pltpu.VMEM((k, tn), BF16), *_ring_scratch(tm, k)],
        compiler_params=pltpu.CompilerParams(dimension_semantics=("arbitrary", "arbitrary")),
        name="retention_in_proj",
    )(x, w, perm, *(side for side, _ in sides))


def _swiglu_kernel(x_hbm, ssq_ref, wa_ref, wu_ref, *refs, n_sides):
    o_ref = refs[n_sides]
    wab_ref, wub_ref, xbuf, xsem = refs[-4:]
    ni = pl.num_programs(1)
    step = pl.program_id(0) * ni + pl.program_id(1)
    ring = _RowTileRing(x_hbm, xbuf, xsem, ni)
    ring.advance(step, pl.num_programs(0) * ni)

    @pl.when(pl.program_id(1) == 0)
    def _():
        wab_ref[...] = wa_ref[...].astype(BF16)
        wub_ref[...] = wu_ref[...].astype(BF16)

    x = ring.tile(step)
    rowscale = lax.rsqrt(jnp.sum(ssq_ref[...], axis=0) / x.shape[1] + NORM_EPS)
    a = _dot(x, wab_ref[...]) * rowscale
    u = _dot(x, wub_ref[...]) * rowscale
    o_ref[...] = (_silu(a) * u).astype(o_ref.dtype)
    _cast_sides(refs, n_sides)


def swiglu_up(xg, ssq, w_gu, layer, sides, tm=1024, tn=512):
    n, k = xg.shape
    f = w_gu.shape[2] // 2
    tm = min(tm, n)
    nj, ni = f // tn, n // tm
    assert f % tn == 0 and n % tm == 0 and nj * ni >= X_RING - 1
    ns = ssq.shape[0]
    s_in, s_out, s_shapes = _side_cast_specs(sides, nj * ni, ni)
    return pl.pallas_call(
        functools.partial(_swiglu_kernel, n_sides=len(sides)),
        grid=(nj, ni),
        in_specs=[pl.BlockSpec(memory_space=pl.ANY),
                  pl.BlockSpec((ns, tm, 1), lambda j, i: (0, i, 0)),
                  pl.BlockSpec((None, k, tn), lambda j, i: (layer, 0, j)),
                  pl.BlockSpec((None, k, tn), lambda j, i: (layer, 0, nj + j)),
                  *s_in],
        out_specs=[pl.BlockSpec((tm, tn), lambda j, i: (i, j)), *s_out],
        out_shape=[jax.ShapeDtypeStruct((n, f), BF16), *s_shapes],
        scratch_shapes=[pltpu.VMEM((k, tn), BF16), pltpu.VMEM((k, tn), BF16),
                        *_ring_scratch(tm, k)],
        compiler_params=pltpu.CompilerParams(dimension_semantics=("arbitrary", "arbitrary")),
        name="swiglu_up",
    )(xg, ssq, w_gu, w_gu, *(side for side, _ in sides))


def _emit_residual_outputs(xn, g_ref, refs, emit_x, emit_norm):
    pos = 0
    if emit_x:
        refs[pos][...] = xn
        pos += 1
    if emit_norm:
        refs[pos][...] = _rms(xn, g_ref[...]).astype(refs[pos].dtype)


def _residual_out_shapes(n, d, tm, emit_x, norm_dtype, index_map):
    out_shape, out_specs = [], []
    if emit_x:
        out_shape.append(jax.ShapeDtypeStruct((n, d), F32))
        out_specs.append(pl.BlockSpec((tm, d), index_map))
    if norm_dtype is not None:
        out_shape.append(jax.ShapeDtypeStruct((n, d), norm_dtype))
        out_specs.append(pl.BlockSpec((tm, d), index_map))
    return out_shape, out_specs


def _down_kernel(x_ref, w_ref, r_ref, g_ref, *refs, emit_scaled, head_gated):
    if head_gated:
        gate_ref, gain_ref, xs_ref = refs[0], refs[1], refs[-1]
        refs = refs[2:-1]
        for lo in range(0, x_ref.shape[1], RET_DV):
            o = x_ref[:, lo:lo + RET_DV].astype(F32)
            ms = jnp.mean(o * o, axis=-1, keepdims=True)
            o = o * lax.rsqrt(ms + NORM_EPS) * gain_ref[:, lo:lo + RET_DV]
            gate = gate_ref[:, lo:lo + RET_DV].astype(F32)
            xs_ref[:, lo:lo + RET_DV] = (o * _silu(gate)).astype(BF16)
        x = xs_ref[...]
    else:
        x = x_ref[...]
    xn = r_ref[...] + _dot(x, w_ref[...])
    refs[0][...] = xn
    if emit_scaled:
        refs[1][...] = (xn * g_ref[...]).astype(BF16)
        refs[2][...] = jnp.sum(xn * xn, axis=-1, keepdims=True)[None]


def down_residual(h, w, resid, g=None, *, head_gate=None, tm=512):
    n, kk = h.shape
    d = w.shape[1]
    tm = min(tm, n)
    emit_scaled = g is not None
    if g is None:
        g = jnp.ones((d,), F32)
    extra_in, extra_specs, extra_scratch = [], [], []
    if head_gate is not None:
        proj, gate_block, head_gain = head_gate
        extra_in = [proj, head_gain.reshape(1, kk)]
        extra_specs = [pl.BlockSpec((tm, kk), lambda i: (i, gate_block)),
                       pl.BlockSpec((1, kk), lambda i: (0, 0))]
        extra_scratch = [pltpu.VMEM((tm, kk), BF16)]
    out_shape = [jax.ShapeDtypeStruct((n, d), F32)]
    out_specs = [pl.BlockSpec((tm, d), lambda i: (i, 0))]
    if emit_scaled:
        out_shape += [jax.ShapeDtypeStruct((n, d), BF16), jax.ShapeDtypeStruct((1, n, 1), F32)]
        out_specs += [pl.BlockSpec((tm, d), lambda i: (i, 0)),
                      pl.BlockSpec((1, tm, 1), lambda i: (0, i, 0))]
    return pl.pallas_call(
        functools.partial(_down_kernel, emit_scaled=emit_scaled, head_gated=head_gate is not None),
        grid=(n // tm,),
        in_specs=[pl.BlockSpec((tm, kk), lambda i: (i, 0)),
                  pl.BlockSpec((kk, d), lambda i: (0, 0), pipeline_mode=pl.Buffered(1)),
                  pl.BlockSpec((tm, d), lambda i: (i, 0)),
                  pl.BlockSpec((1, d), lambda i: (0, 0)),
                  *extra_specs],
        out_specs=out_specs,
        out_shape=out_shape,
        scratch_shapes=extra_scratch,
        compiler_params=pltpu.CompilerParams(dimension_semantics=("arbitrary",),
                                             vmem_limit_bytes=DOWN_VMEM_LIMIT),
        name="down_residual",
    )(h, w, resid, g.reshape(1, d), *extra_in)


def _retention_kernel(lg_ref, q_ref, k_ref, v_ref, cos_ref, sin_ref, o_ref,
                      qr_ref, qd_ref, kr_ref, kd_ref, mask_ref, qdec_ref, kdec_ref,
                      state_ref, sb_ref, *, n_chunks, n_pairs):
    c = RET_CHUNK
    step = pl.program_id(0)
    slot_prep = step % 2
    slot_recur = 1 - slot_prep

    @pl.when(step == 0)
    def _():
        for ref in (qr_ref, qd_ref, kr_ref, kd_ref):
            ref[...] = jnp.zeros_like(ref)

    lg_prep = lg_ref[jnp.minimum(step, n_pairs - 1) % RET_HEADS]
    lg = lg_ref[jnp.clip(step - 1, 0, n_pairs - 1) % RET_HEADS]
    ii = lax.broadcasted_iota(jnp.int32, (c, c), 0)
    jj = lax.broadcasted_iota(jnp.int32, (c, c), 1)
    rel = (ii - jj).astype(F32)
    mask_ref[...] = jnp.where(rel >= 0.0, jnp.exp(jnp.maximum(rel, 0.0) * lg), 0.0)
    idx = lax.broadcasted_iota(jnp.int32, (c, 1), 0).astype(F32)
    k_scale = RET_DK ** -0.5
    qdec_ref[...] = jnp.broadcast_to(jnp.exp((idx + 1.0) * lg_prep), (c, RET_DK))
    kdec_ref[...] = jnp.broadcast_to(jnp.exp((c - 1.0 - idx) * lg_prep) * k_scale, (c, RET_DK))
    chunk_decay = jnp.exp(jnp.full((1, 1), float(c), F32) * lg)
    half = RET_DK // 2

    def rotary(xb, cos, sin):
        x = xb.astype(F32)
        x1, x2 = x[:, :half], x[:, half:]
        return jnp.concatenate([x1 * cos - x2 * sin, x2 * cos + x1 * sin], axis=1)

    def rows(ci):
        return pl.ds(pl.multiple_of(ci * c, c), c)

    state_ref[...] = jnp.zeros_like(state_ref)
    sb_ref[...] = jnp.zeros_like(sb_ref)

    def body(ci, carry):
        r = rows(ci)

        v = v_ref[r, :]
        scores = lax.dot_general(qr_ref[slot_recur, r, :], kr_ref[slot_recur, r, :],
                                 (((1,), (1,)), ((), ())),
                                 preferred_element_type=F32) * mask_ref[...]
        o_ref[r, :] = (_dot(scores.astype(BF16), v)
                       + _dot(qd_ref[slot_recur, r, :], sb_ref[...])).astype(o_ref.dtype)
        upd = lax.dot_general(kd_ref[slot_recur, r, :], v, (((0,), (0,)), ((), ())),
                              preferred_element_type=F32)
        state = state_ref[...] * chunk_decay + upd
        state_ref[...] = state
        sb_ref[...] = state.astype(BF16)

        for lo in range(0, c, RET_PREP_ROWS):
            rr = pl.ds(pl.multiple_of(ci * c + lo, RET_PREP_ROWS), RET_PREP_ROWS)
            dec = pl.ds(lo, RET_PREP_ROWS)
            cos = cos_ref[rr, :]
            sin = sin_ref[rr, :]
            q = rotary(q_ref[rr, :], cos, sin)
            k = rotary(k_ref[rr, :], cos, sin)
            qr_ref[slot_prep, rr, :] = q.astype(BF16)
            qd_ref[slot_prep, rr, :] = (q * qdec_ref[dec, :]).astype(BF16)
            kr_ref[slot_prep, rr, :] = (k * k_scale).astype(BF16)
            kd_ref[slot_prep, rr, :] = (k * kdec_ref[dec, :]).astype(BF16)
        return carry

    lax.fori_loop(0, n_chunks, body, 0, unroll=2)


def retention(proj, batch, seq):
    n = batch * seq
    h = RET_HEADS
    log_g = jnp.log1p(-jnp.exp2(-5.0 - jnp.arange(h, dtype=F32)))
    freq = 1.0 / (ROPE_BASE ** jnp.linspace(0.0, 1.0, RET_DK // 2, dtype=F32))
    ang = jnp.arange(seq).astype(F32)[:, None] * freq[None, :]
    cos = jnp.cos(ang)
    sin = jnp.sin(ang)
    kq = RET_QK // RET_DK
    kv = 2 * RET_QK // RET_DV
    n_pairs = batch * h
    stages = 2

    def pair(step, stage):
        p = jnp.clip(step - stage, 0, n_pairs - 1)
        return p // h, p % h

    def block(stage, col0):
        def index(s):
            b, hh = pair(s, stage)
            return b, col0 + hh
        return index

    return pl.pallas_call(
        functools.partial(_retention_kernel, n_chunks=seq // RET_CHUNK, n_pairs=n_pairs),
        grid=(n_pairs + stages - 1,),
        in_specs=[pl.BlockSpec(memory_space=pltpu.SMEM),
                  pl.BlockSpec((seq, RET_DK), block(0, 0)),
                  pl.BlockSpec((seq, RET_DK), block(0, kq)),
                  pl.BlockSpec((seq, RET_DV), block(1, kv)),
                  pl.BlockSpec((seq, RET_DK // 2), lambda s: (0, 0)),
                  pl.BlockSpec((seq, RET_DK // 2), lambda s: (0, 0))],
        out_specs=pl.BlockSpec((seq, RET_DV), block(1, 0)),
        out_shape=jax.ShapeDtypeStruct((n, RET_V), BF16),
        scratch_shapes=[pltpu.VMEM((2, seq, RET_DK), BF16),
                        pltpu.VMEM((2, seq, RET_DK), BF16),
                        pltpu.VMEM((2, seq, RET_DK), BF16),
                        pltpu.VMEM((2, seq, RET_DK), BF16),
                        pltpu.VMEM((RET_CHUNK, RET_CHUNK), F32),
                        pltpu.VMEM((RET_CHUNK, RET_DK), F32),
                        pltpu.VMEM((RET_CHUNK, RET_DK), F32),
                        pltpu.VMEM((RET_DK, RET_DV), F32),
                        pltpu.VMEM((RET_DK, RET_DV), BF16)],
        compiler_params=pltpu.CompilerParams(dimension_semantics=("arbitrary",)),
        name="retention",
    )(log_g, proj, proj, proj, cos, sin)


def _pool_kernel(x_ref, gm_ref, pw_ref, sc_ref, gf_ref, rt_ref,
                 xo_ref, hn_ref, meta_ref, meta_t_ref, cnt_out_ref, hbuf, cnt_ref, *, tiles_per_seq):
    tm = x_ref.shape[0]
    step = pl.program_id(0)
    ti = step % tiles_per_seq
    x = x_ref[...]
    hn = _rms(x, gm_ref[...])

    @pl.when(ti == 0)
    def _():
        hbuf[0:POOL_HALO, :] = jnp.zeros((POOL_HALO, D_MODEL), F32)

    @pl.when(ti != 0)
    def _():
        hbuf[0:POOL_HALO, :] = hbuf[tm:tm + POOL_HALO, :]

    hbuf[POOL_HALO:POOL_HALO + tm, :] = hn

    t = ti * tm + lax.broadcasted_iota(jnp.int32, (tm, 1), 0)
    for gi, w in enumerate(POOL_WINDOWS):
        lo, hi = gi * POOL_G, (gi + 1) * POOL_G
        cur = hn[:, lo:hi]
        acc = hbuf[:, lo:hi]
        shift = 1
        while shift < w:
            acc = acc + pltpu.roll(acc, shift, 0)
            shift *= 2
        acc = acc[POOL_HALO:, :]
        inv = 1.0 / jnp.minimum(t + 1, w).astype(F32)
        mix = (acc * inv - cur).astype(BF16)
        o = _dot(mix, pw_ref[gi]) * sc_ref[:, lo:hi]
        xo_ref[:, lo:hi] = x[:, lo:hi] + o

    hn2 = _rms(xo_ref[...], gf_ref[...])
    hn_ref[...] = hn2

    rt = rt_ref[...]
    hn_hi = hn2.astype(BF16)
    hn_lo = (hn2 - hn_hi.astype(F32)).astype(BF16)
    rt_hi = rt.astype(BF16)
    rt_lo = (rt - rt_hi.astype(F32)).astype(BF16)
    logits = _dot(hn_hi, rt_hi) + (_dot(hn_hi, rt_lo) + _dot(hn_lo, rt_hi))
    lt = logits.T[:N_EXPERTS, :]
    expert = lax.broadcasted_iota(jnp.int32, lt.shape, 0)
    neg = jnp.float32(-jnp.inf)
    m1 = jnp.max(lt, axis=0, keepdims=True)
    i1 = jnp.min(jnp.where(lt == m1, expert, N_EXPERTS), axis=0, keepdims=True)
    lt2 = jnp.where(expert == i1, neg, lt)
    m2 = jnp.max(lt2, axis=0, keepdims=True)
    i2 = jnp.min(jnp.where(lt2 == m2, expert, N_EXPERTS), axis=0, keepdims=True)
    e2 = jnp.exp(m2 - m1)
    g1 = 1.0 / (1.0 + e2)
    g2 = e2 * g1

    @pl.when(step == 0)
    def _():
        cnt_ref[...] = jnp.zeros_like(cnt_ref)

    chosen = jnp.where((expert == i1) | (expert == i2), 1.0, 0.0)
    src = lax.broadcasted_iota(jnp.int32, (tm, tm), 0)
    dst = lax.broadcasted_iota(jnp.int32, (tm, tm), 1)
    earlier = jnp.where(src < dst, 1.0, 0.0).astype(BF16)
    before = _dot(chosen.astype(BF16), earlier) + cnt_ref[:, 0:1]
    r1 = jnp.sum(jnp.where(expert == i1, before, 0.0), axis=0, keepdims=True)
    r2 = jnp.sum(jnp.where(expert == i2, before, 0.0), axis=0, keepdims=True)
    cnt_ref[...] += jnp.sum(chosen, axis=1, keepdims=True)
    cnt_out_ref[...] = cnt_ref[...]

    fields = {META_E1: i1.astype(F32), META_E2: i2.astype(F32), META_R1: r1, META_R2: r2,
              META_G1: g1, META_G2: g2}
    zero_row = jnp.zeros_like(g1)
    meta_t = jnp.concatenate([fields.get(row, zero_row) for row in range(SUBLANES)], axis=0)
    meta_t_ref[...] = meta_t
    meta_ref[...] = jnp.concatenate([meta_t, jnp.zeros((LANES - SUBLANES, tm), F32)], axis=0).T


def pool_router(x, g_mix, pool_w, layer, pool_scale, g_ffn, router, seq, tm=512):
    n, d = x.shape
    tm = min(tm, seq)
    router_p = jnp.zeros((d, LANES), F32).at[:, :N_EXPERTS].set(router)
    ng = len(POOL_WINDOWS)
    return pl.pallas_call(
        functools.partial(_pool_kernel, tiles_per_seq=seq // tm),
        grid=(n // tm,),
        in_specs=[pl.BlockSpec((tm, d), lambda i: (i, 0)),
                  pl.BlockSpec((1, d), lambda i: (0, 0)),
                  pl.BlockSpec((None, ng, POOL_G, POOL_G), lambda i: (layer, 0, 0, 0)),
                  pl.BlockSpec((1, d), lambda i: (0, 0)),
                  pl.BlockSpec((1, d), lambda i: (0, 0)),
                  pl.BlockSpec((d, LANES), lambda i: (0, 0))],
        out_specs=[pl.BlockSpec((tm, d), lambda i: (i, 0)),
                   pl.BlockSpec((tm, d), lambda i: (i, 0)),
                   pl.BlockSpec((tm, LANES), lambda i: (i, 0)),
                   pl.BlockSpec((SUBLANES, tm), lambda i: (0, i)),
                   pl.BlockSpec((SUBLANES, LANES), lambda i: (0, 0))],
        out_shape=[jax.ShapeDtypeStruct((n, d), F32),
                   jax.ShapeDtypeStruct((n, d), F32),
                   jax.ShapeDtypeStruct((n, LANES), F32),
                   jax.ShapeDtypeStruct((SUBLANES, n), F32),
                   jax.ShapeDtypeStruct((SUBLANES, LANES), F32)],
        scratch_shapes=[pltpu.VMEM((POOL_HALO + tm, d), F32),
                        pltpu.VMEM((N_EXPERTS, LANES), F32)],
        compiler_params=pltpu.CompilerParams(dimension_semantics=("arbitrary",)),
        name="pool_router",
    )(x, g_mix.reshape(1, d), pool_w, pool_scale.reshape(1, d), g_ffn.reshape(1, d), router_p)


def moe_num_tiles(n):
    return TOP_K * n // MOE_TILE + N_EXPERTS


def moe_schedule(meta_t, counts_f, n):
    nt = moe_num_tiles(n)
    counts = counts_f[:N_EXPERTS, 0].astype(jnp.int32)
    tiles = (counts + MOE_TILE - 1) // MOE_TILE
    tile_end = jnp.cumsum(tiles)
    starts = (tile_end - tiles) * MOE_TILE
    expert_ids = jnp.arange(N_EXPERTS, dtype=jnp.int32)[:, None]

    def slots(e_row, r_row):
        e = meta_t[e_row].astype(jnp.int32)
        first = jnp.sum(jnp.where(e[None, :] == expert_ids, starts[:, None], 0), axis=0)
        return first + meta_t[r_row].astype(jnp.int32)

    pos = jnp.concatenate([slots(META_E1, META_R1), slots(META_E2, META_R2)])
    total = tile_end[-1]
    t = jnp.arange(nt, dtype=jnp.int32)
    t_used = jnp.minimum(t, total - 1)
    tile_expert = jnp.sum((tile_end[None, :] <= t_used[:, None]).astype(jnp.int32), axis=1)
    tile_expert = jnp.minimum(tile_expert, N_EXPERTS - 1)
    tile_valid = (t < total).astype(jnp.int32)
    pad_lo = starts + counts
    pad_hi = starts + tiles * MOE_TILE
    prev_expert = jnp.concatenate([jnp.full((1,), -1, jnp.int32), tile_expert[:-1]])
    group_first = (tile_expert != prev_expert).astype(jnp.int32)
    group_slot = (jnp.cumsum(group_first) - 1) % 2
    later = (expert_ids.T > expert_ids) & (tiles[None, :] > 0)
    next_used = jnp.min(jnp.where(later, expert_ids.T, N_EXPERTS), axis=1)
    next_used = jnp.where(next_used == N_EXPERTS, -1, next_used)
    next_expert = jnp.sum(jnp.where(tile_expert[None, :] == expert_ids, next_used[:, None], 0), axis=0)
    table = jnp.concatenate([tile_expert, tile_valid, group_first, group_slot, next_expert])
    return pos, table.astype(jnp.int32), pad_lo, pad_hi, total.reshape(1)


def _dispatch_kernel(pos_ref, lo_ref, hi_ref, used_ref, hn_ref, xs_hbm, zeros, sem, *, n, n_tiles):
    tb = hn_ref.shape[0]
    step = pl.program_id(0)
    base = step * tb

    def issue(r, c):
        tok = base + r
        src = hn_ref.at[pl.ds(r, 1), :]
        pltpu.make_async_copy(src, xs_hbm.at[pl.ds(pos_ref[tok], 1), :], sem).start()
        pltpu.make_async_copy(src, xs_hbm.at[pl.ds(pos_ref[n + tok], 1), :], sem).start()
        return c

    lax.fori_loop(0, tb, issue, 0, unroll=8)
    for _ in range(TOP_K):
        pltpu.make_async_copy(hn_ref, xs_hbm.at[pl.ds(0, tb), :], sem).wait()

    @pl.when(step == pl.num_programs(0) - 1)
    def _():
        zeros[...] = jnp.zeros_like(zeros)

        def pad_copy(slot):
            return pltpu.make_async_copy(zeros.at[pl.ds(0, 1), :], xs_hbm.at[pl.ds(slot, 1), :], sem)

        def issue_pad(s, c):
            pad_copy(s).start()
            return c

        def drain_pad(s, c):
            pad_copy(s).wait()
            return c

        for e in range(N_EXPERTS):
            lax.fori_loop(lo_ref[e], hi_ref[e], issue_pad, 0)
            lax.fori_loop(lo_ref[e], hi_ref[e], drain_pad, 0)

        def tile_copy(t):
            return pltpu.make_async_copy(zeros, xs_hbm.at[pl.ds(t * MOE_TILE, MOE_TILE), :], sem)

        def issue_tile(t, c):
            tile_copy(t).start()
            return c

        def drain_tile(t, c):
            tile_copy(t).wait()
            return c

        lax.fori_loop(used_ref[0], n_tiles, issue_tile, 0)
        lax.fori_loop(used_ref[0], n_tiles, drain_tile, 0)


def moe_dispatch(hn, pos, pad_lo, pad_hi, used, tb=512):
    n, d = hn.shape
    tb = min(tb, n)
    nt = moe_num_tiles(n)
    grid_spec = pltpu.PrefetchScalarGridSpec(
        num_scalar_prefetch=4,
        grid=(n // tb,),
        in_specs=[pl.BlockSpec((tb, d), lambda i, *_: (i, 0))],
        out_specs=pl.BlockSpec(memory_space=pl.ANY),
        scratch_shapes=[pltpu.VMEM((MOE_TILE, d), F32), pltpu.SemaphoreType.DMA],
    )
    return pl.pallas_call(
        functools.partial(_dispatch_kernel, n=n, n_tiles=nt),
        grid_spec=grid_spec,
        out_shape=jax.ShapeDtypeStruct((nt * MOE_TILE, d), F32),
        compiler_params=pltpu.CompilerParams(dimension_semantics=("arbitrary",)),
        name="moe_dispatch",
    )(pos, pad_lo, pad_hi, used, hn)


def _expert_kernel(tab_ref, x_ref, wgu_hbm, wd_hbm, o_ref, wa_buf, wu_buf, wd_buf, sems):
    t = pl.program_id(0)
    nt = pl.num_programs(0)
    f = wd_buf.shape[1]

    def field(k, tile):
        return tab_ref[k * nt + tile]

    def weight_copies(e, slot):
        return (pltpu.make_async_copy(wgu_hbm.at[e, :, pl.ds(0, f)], wa_buf.at[slot], sems.at[slot]),
                pltpu.make_async_copy(wgu_hbm.at[e, :, pl.ds(f, f)], wu_buf.at[slot], sems.at[slot]),
                pltpu.make_async_copy(wd_hbm.at[e], wd_buf.at[slot], sems.at[slot]))

    slot = field(TILE_SLOT, t)

    @pl.when(t == 0)
    def _():
        for cp in weight_copies(field(TILE_EXPERT, 0), field(TILE_SLOT, 0)):
            cp.start()

    @pl.when(field(TILE_FIRST, t) == 1)
    def _():
        for cp in weight_copies(field(TILE_EXPERT, t), slot):
            cp.wait()
        nxt = field(TILE_NEXT, t)

        @pl.when(nxt >= 0)
        def _():
            for cp in weight_copies(nxt, 1 - slot):
                cp.start()

    valid = field(TILE_VALID, t) == 1

    @pl.when(valid)
    def _():
        x = x_ref[...].astype(BF16)
        a = _dot(x, wa_buf[slot])
        u = _dot(x, wu_buf[slot])
        o_ref[...] = _dot((_silu(a) * u).astype(BF16), wd_buf[slot])

    @pl.when(jnp.logical_not(valid))
    def _():
        o_ref[...] = jnp.zeros_like(o_ref)


def moe_experts(xs, w_gu, w_down, table):
    rows, d = xs.shape
    f = w_down.shape[1]
    nt = rows // MOE_TILE
    grid_spec = pltpu.PrefetchScalarGridSpec(
        num_scalar_prefetch=1,
        grid=(nt,),
        in_specs=[pl.BlockSpec((MOE_TILE, d), lambda t, tab: (t, 0)),
                  pl.BlockSpec(memory_space=pl.ANY),
                  pl.BlockSpec(memory_space=pl.ANY)],
        out_specs=pl.BlockSpec((MOE_TILE, d), lambda t, tab: (t, 0)),
        scratch_shapes=[pltpu.VMEM((2, d, f), BF16), pltpu.VMEM((2, d, f), BF16),
                        pltpu.VMEM((2, f, d), BF16), pltpu.SemaphoreType.DMA((2,))],
    )
    return pl.pallas_call(
        _expert_kernel,
        grid_spec=grid_spec,
        out_shape=jax.ShapeDtypeStruct((rows, d), F32),
        compiler_params=pltpu.CompilerParams(dimension_semantics=("arbitrary",),
                                             vmem_limit_bytes=EXPERT_VMEM_LIMIT),
        name="moe_experts",
    )(table, xs, w_gu, w_down)


def _combine_kernel(pos_ref, x_ref, meta_ref, g_ref, y_hbm, *refs, n, emit_x, emit_norm):
    ybuf, sem = refs[-2], refs[-1]
    tc = x_ref.shape[0]
    step = pl.program_id(0)
    slot = step % 2

    def start_gather(s, into):
        def issue(r, c):
            tok = s * tc + r
            for k in range(TOP_K):
                pltpu.make_async_copy(y_hbm.at[pl.ds(pos_ref[k * n + tok], 1), :],
                                      ybuf.at[into, k, pl.ds(r, 1), :], sem.at[into]).start()
            return c

        lax.fori_loop(0, tc, issue, 0, unroll=8)

    @pl.when(step == 0)
    def _():
        start_gather(0, 0)

    @pl.when(step + 1 < pl.num_programs(0))
    def _():
        start_gather(step + 1, 1 - slot)

    for k in range(TOP_K):
        pltpu.make_async_copy(y_hbm.at[pl.ds(0, tc), :], ybuf.at[slot, k], sem.at[slot]).wait()

    meta = meta_ref[...]
    g1 = meta[:, META_G1:META_G1 + 1]
    g2 = meta[:, META_G2:META_G2 + 1]
    xn = x_ref[...] + g1 * ybuf[slot, 0] + g2 * ybuf[slot, 1]
    _emit_residual_outputs(xn, g_ref, refs, emit_x, emit_norm)


def moe_combine(x, y, meta, pos, g, *, emit_x=True, norm_dtype=None, tc=256):
    n, d = x.shape
    tc = min(tc, n)
    out_shape, out_specs = _residual_out_shapes(n, d, tc, emit_x, norm_dtype, lambda i, p: (i, 0))
    grid_spec = pltpu.PrefetchScalarGridSpec(
        num_scalar_prefetch=1,
        grid=(n // tc,),
        in_specs=[pl.BlockSpec((tc, d), lambda i, p: (i, 0)),
                  pl.BlockSpec((tc, LANES), lambda i, p: (i, 0)),
                  pl.BlockSpec((1, d), lambda i, p: (0, 0)),
                  pl.BlockSpec(memory_space=pl.ANY)],
        out_specs=out_specs,
        scratch_shapes=[pltpu.VMEM((2, TOP_K, tc, d), F32), pltpu.SemaphoreType.DMA((2,))],
    )
    return pl.pallas_call(
        functools.partial(_combine_kernel, n=n, emit_x=emit_x, emit_norm=norm_dtype is not None),
        grid_spec=grid_spec,
        out_shape=out_shape,
        compiler_params=pltpu.CompilerParams(dimension_semantics=("arbitrary",)),
        name="moe_combine",
    )(pos, x, meta, g.reshape(1, d), y)


def kernel(x, ret_w_in, ret_norm_g, ret_w_out, pool_w, pool_scale, ffn_w_gu, ffn_w_down,
           moe_router, moe_w_gu, moe_w_down, norm_mix_g, norm_ffn_g, norm_final_g):
    batch, seq, d = x.shape
    n = batch * seq
    n_moe, n_exp, _, f2 = moe_w_gu.shape
    f = moe_w_down.shape[2]
    assert d == D_MODEL and n_exp == N_EXPERTS and seq % RET_CHUNK == 0 and seq >= POOL_HALO
    assert norm_mix_g.shape[0] == DEPTH and DEPTH % 2 == 0, "the trunk ends on a pooling / MoE layer"
    xf = x.reshape(n, d)
    pool_w = pool_w.astype(BF16)
    moe_gu_rows = moe_w_gu.reshape(n_moe, n_exp * d, f2)
    moe_down_rows = moe_w_down.reshape(n_moe, n_exp * f, d)
    hn = rmsnorm(xf, norm_mix_g[0], BF16)
    for i in range(DEPTH):
        j = i // 2
        last = i == DEPTH - 1
        if i % 2 == 0:
            proj, moe_gu_b, w_out_b = retention_in_proj(hn, ret_w_in, j,
                                                        [(moe_gu_rows, j), (ret_w_out, j)])
            y = retention(proj, batch, seq)
            gate_block = (2 * RET_QK + RET_V) // RET_V
            xf, xg, ssq = down_residual(y, w_out_b, xf, norm_ffn_g[i],
                                        head_gate=(proj, gate_block, ret_norm_g[j]))
            hid, moe_down_b, ffn_down_b = swiglu_up(xg, ssq, ffn_w_gu, j,
                                                    [(moe_down_rows, j), (ffn_w_down, j)])
            (xf,) = down_residual(hid, ffn_down_b, xf)
            moe_gu_b = moe_gu_b.reshape(n_exp, d, f2)
            moe_down_b = moe_down_b.reshape(n_exp, f, d)
        else:
            xf, hn, meta, meta_t, counts = pool_router(xf, norm_mix_g[i], pool_w, j, pool_scale[j],
                                                       norm_ffn_g[i], moe_router[j], seq)
            pos, tile_table, pad_lo, pad_hi, used = moe_schedule(meta_t, counts, n)
            xs = moe_dispatch(hn, pos, pad_lo, pad_hi, used)
            ys = moe_experts(xs, moe_gu_b, moe_down_b, tile_table)
            if last:
                (out,) = moe_combine(xf, ys, meta, pos, norm_final_g, emit_x=False, norm_dtype=F32)
            else:
                xf, hn = moe_combine(xf, ys, meta, pos, norm_mix_g[i + 1], norm_dtype=BF16)
    return out.reshape(batch, seq, d)
```

```python
import functools

import jax
import jax.numpy as jnp
from jax import lax
from jax.experimental import pallas as pl
from jax.experimental.pallas import tpu as pltpu

BF16 = jnp.bfloat16
F32 = jnp.float32

D_MODEL = 2048
DEPTH = 4
RET_HEADS = 8
RET_DK = D_MODEL // RET_HEADS
RET_DV = 2 * D_MODEL // RET_HEADS
RET_QK = RET_HEADS * RET_DK
RET_V = RET_HEADS * RET_DV
RET_CHUNK = 256
RET_PREP_ROWS = 32
ROPE_BASE = 10000.0
POOL_WINDOWS = (2, 4, 8, 16)
POOL_G = D_MODEL // len(POOL_WINDOWS)
POOL_HALO = 16
N_EXPERTS = 8
TOP_K = 2
NORM_EPS = 1e-6
LANES = 128
SUBLANES = 8
BF16_TILE_ROWS = 16
MOE_TILE = 256
SIDE_CAST_BLOCKS = 64
X_RING = 3
DOWN_VMEM_LIMIT = 60 * 1024 * 1024
EXPERT_VMEM_LIMIT = 56 * 1024 * 1024
META_E1, META_E2, META_R1, META_R2, META_G1, META_G2 = range(6)
TILE_EXPERT, TILE_VALID, TILE_FIRST, TILE_SLOT, TILE_NEXT = range(5)


def _rms(x, g):
    ms = jnp.mean(x * x, axis=-1, keepdims=True)
    return x * lax.rsqrt(ms + NORM_EPS) * g


def _silu(a):
    return a / (1.0 + jnp.exp(-a))


def _dot(a, b):
    return jnp.dot(a, b, preferred_element_type=F32)


def _rmsnorm_kernel(x_ref, g_ref, o_ref):
    o_ref[...] = _rms(x_ref[...], g_ref[...]).astype(o_ref.dtype)


def rmsnorm(x, g, out_dtype, tm=512):
    n, d = x.shape
    return pl.pallas_call(
        _rmsnorm_kernel,
        grid=(n // tm,),
        in_specs=[pl.BlockSpec((tm, d), lambda i: (i, 0)),
                  pl.BlockSpec((1, d), lambda i: (0, 0))],
        out_specs=pl.BlockSpec((tm, d), lambda i: (i, 0)),
        out_shape=jax.ShapeDtypeStruct((n, d), out_dtype),
        name="rmsnorm",
    )(x, g.reshape(1, d))


def _side_cast_specs(sides, steps, n_inner):
    in_specs, out_specs, shapes = [], [], []
    for side, side_layer in sides:
        _, rows, cols = side.shape
        n_blocks = 1
        while (n_blocks * 2 <= min(steps, SIDE_CAST_BLOCKS)
               and rows % (n_blocks * 2 * BF16_TILE_ROWS) == 0):
            n_blocks *= 2
        br = rows // n_blocks

        def index(j, i, n_blocks=n_blocks):
            return jnp.minimum(j * n_inner + i, n_blocks - 1)

        in_specs.append(pl.BlockSpec((None, br, cols),
                                     lambda j, i, l=side_layer, ix=index: (l, ix(j, i), 0)))
        out_specs.append(pl.BlockSpec((br, cols), lambda j, i, ix=index: (ix(j, i), 0)))
        shapes.append(jax.ShapeDtypeStruct((rows, cols), BF16))
    return in_specs, out_specs, shapes


def _cast_sides(refs, n_sides):
    for s_ref, so_ref in zip(refs[:n_sides], refs[n_sides + 1:2 * n_sides + 1]):
        so_ref[...] = s_ref[...].astype(BF16)


class _RowTileRing:
    def __init__(self, x_hbm, buf, sems, n_row_tiles):
        self.x_hbm, self.buf, self.sems, self.n_row_tiles = x_hbm, buf, sems, n_row_tiles

    def _copy(self, s):
        tm = self.buf.shape[1]
        rows = pl.ds(pl.multiple_of((s % self.n_row_tiles) * tm, tm), tm)
        return pltpu.make_async_copy(self.x_hbm.at[rows, :], self.buf.at[s % X_RING],
                                     self.sems.at[s % X_RING])

    def advance(self, step, total):
        @pl.when(step == 0)
        def _():
            for s in range(X_RING - 1):
                self._copy(s).start()

        @pl.when(step + X_RING - 1 < total)
        def _():
            self._copy(step + X_RING - 1).start()

    def tile(self, step):
        self._copy(step).wait()
        return self.buf[step % X_RING]


def _ring_scratch(tm, k):
    return [pltpu.VMEM((X_RING, tm, k), BF16), pltpu.SemaphoreType.DMA((X_RING,))]


def _in_proj_kernel(x_hbm, w_ref, perm_ref, *refs, n_qk_slabs, n_sides):
    o_ref = refs[n_sides]
    wb_ref, xbuf, xsem = refs[-3:]
    ni = pl.num_programs(1)
    step = pl.program_id(0) * ni + pl.program_id(1)
    ring = _RowTileRing(x_hbm, xbuf, xsem, ni)
    ring.advance(step, pl.num_programs(0) * ni)

    @pl.when(pl.program_id(1) == 0)
    def _():
        is_qk = pl.program_id(0) < n_qk_slabs

        @pl.when(is_qk)
        def _():
            for lo in range(0, wb_ref.shape[1], RET_DK):
                head = w_ref[:, lo:lo + RET_DK].astype(BF16)
                wb_ref[:, lo:lo + RET_DK] = _dot(head, perm_ref[...]).astype(BF16)

        @pl.when(jnp.logical_not(is_qk))
        def _():
            wb_ref[...] = w_ref[...].astype(BF16)

    o_ref[...] = _dot(ring.tile(step), wb_ref[...]).astype(o_ref.dtype)
    _cast_sides(refs, n_sides)


def retention_in_proj(x, w, layer, sides, tm=1024, tn=1024):
    n, k = x.shape
    f = w.shape[2]
    tm = min(tm, n)
    nj, ni = f // tn, n // tm
    assert f % tn == 0 and n % tm == 0 and nj * ni >= X_RING - 1
    half = RET_DK // 2
    src = jnp.arange(RET_DK)
    dst = jnp.where(src % 2 == 0, src // 2, half + src // 2)
    perm = (dst[:, None] == jnp.arange(RET_DK)[None, :]).astype(BF16)
    s_in, s_out, s_shapes = _side_cast_specs(sides, nj * ni, ni)
    return pl.pallas_call(
        functools.partial(_in_proj_kernel, n_qk_slabs=2 * RET_QK // tn, n_sides=len(sides)),
        grid=(nj, ni),
        in_specs=[pl.BlockSpec(memory_space=pl.ANY),
                  pl.BlockSpec((None, k, tn), lambda j, i: (layer, 0, j)),
                  pl.BlockSpec((RET_DK, RET_DK), lambda j, i: (0, 0)),
                  *s_in],
        out_specs=[pl.BlockSpec((tm, tn), lambda j, i: (i, j)), *s_out],
        out_shape=[jax.ShapeDtypeStruct((n, f), BF16), *s_shapes],
        scratch_shapes=[pltpu.VMEM((k, tn), BF16), *_ring_scratch(tm, k)],
        compiler_params=pltpu.CompilerParams(dimension_semantics=("arbitrary", "arbitrary")),
        name="retention_in_proj",
    )(x, w, perm, *(side for side, _ in sides))


def _swiglu_kernel(x_hbm, ssq_ref, wa_ref, wu_ref, *refs, n_sides):
    o_ref = refs[n_sides]
    wab_ref, wub_ref, xbuf, xsem = refs[-4:]
    ni = pl.num_programs(1)
    step = pl.program_id(0) * ni + pl.program_id(1)
    ring = _RowTileRing(x_hbm, xbuf, xsem, ni)
    ring.advance(step, pl.num_programs(0) * ni)

    @pl.when(pl.program_id(1) == 0)
    def _():
        wab_ref[...] = wa_ref[...].astype(BF16)
        wub_ref[...] = wu_ref[...].astype(BF16)

    x = ring.tile(step)
    rowscale = lax.rsqrt(jnp.sum(ssq_ref[...], axis=0) / x.shape[1] + NORM_EPS)
    a = _dot(x, wab_ref[...]) * rowscale
    u = _dot(x, wub_ref[...]) * rowscale
    o_ref[...] = (_silu(a) * u).astype(o_ref.dtype)
    _cast_sides(refs, n_sides)


def swiglu_up(xg, ssq, w_gu, layer, sides, tm=1024, tn=512):
    n, k = xg.shape
    f = w_gu.shape[2] // 2
    tm = min(tm, n)
    nj, ni = f // tn, n // tm
    assert f % tn == 0 and n % tm == 0 and nj * ni >= X_RING - 1
    ns = ssq.shape[0]
    s_in, s_out, s_shapes = _side_cast_specs(sides, nj * ni, ni)
    return pl.pallas_call(
        functools.partial(_swiglu_kernel, n_sides=len(sides)),
        grid=(nj, ni),
        in_specs=[pl.BlockSpec(memory_space=pl.ANY),
                  pl.BlockSpec((ns, tm, 1), lambda j, i: (0, i, 0)),
                  pl.BlockSpec((None, k, tn), lambda j, i: (layer, 0, j)),
                  pl.BlockSpec((None, k, tn), lambda j, i: (layer, 0, nj + j)),
                  *s_in],
        out_specs=[pl.BlockSpec((tm, tn), lambda j, i: (i, j)), *s_out],
        out_shape=[jax.ShapeDtypeStruct((n, f), BF16), *s_shapes],
        scratch_shapes=[pltpu.VMEM((k, tn), BF16), pltpu.VMEM((k, tn), BF16),
                        *_ring_scratch(tm, k)],
        compiler_params=pltpu.CompilerParams(dimension_semantics=("arbitrary", "arbitrary")),
        name="swiglu_up",
    )(xg, ssq, w_gu, w_gu, *(side for side, _ in sides))


def _emit_residual_outputs(xn, g_ref, refs, emit_x, emit_norm):
    pos = 0
    if emit_x:
        refs[pos][...] = xn
        pos += 1
    if emit_norm:
        refs[pos][...] = _rms(xn, g_ref[...]).astype(refs[pos].dtype)


def _residual_out_shapes(n, d, tm, emit_x, norm_dtype, index_map):
    out_shape, out_specs = [], []
    if emit_x:
        out_shape.append(jax.ShapeDtypeStruct((n, d), F32))
        out_specs.append(pl.BlockSpec((tm, d), index_map))
    if norm_dtype is not None:
        out_shape.append(jax.ShapeDtypeStruct((n, d), norm_dtype))
        out_specs.append(pl.BlockSpec((tm, d), index_map))
    return out_shape, out_specs


def _down_kernel(x_ref, w_ref, r_ref, g_ref, *refs, emit_scaled, head_gated):
    if head_gated:
        gate_ref, gain_ref, xs_ref = refs[0], refs[1], refs[-1]
        refs = refs[2:-1]
        for lo in range(0, x_ref.shape[1], RET_DV):
            o = x_ref[:, lo:lo + RET_DV].astype(F32)
            ms = jnp.mean(o * o, axis=-1, keepdims=True)
            o = o * lax.rsqrt(ms + NORM_EPS) * gain_ref[:, lo:lo + RET_DV]
            gate = gate_ref[:, lo:lo + RET_DV].astype(F32)
            xs_ref[:, lo:lo + RET_DV] = (o * _silu(gate)).astype(BF16)
        x = xs_ref[...]
    else:
        x = x_ref[...]
    xn = r_ref[...] + _dot(x, w_ref[...])
    refs[0][...] = xn
    if emit_scaled:
        refs[1][...] = (xn * g_ref[...]).astype(BF16)
        refs[2][...] = jnp.sum(xn * xn, axis=-1, keepdims=True)[None]


def down_residual(h, w, resid, g=None, *, head_gate=None, tm=512):
    n, kk = h.shape
    d = w.shape[1]
    tm = min(tm, n)
    emit_scaled = g is not None
    if g is None:
        g = jnp.ones((d,), F32)
    extra_in, extra_specs, extra_scratch = [], [], []
    if head_gate is not None:
        proj, gate_block, head_gain = head_gate
        extra_in = [proj, head_gain.reshape(1, kk)]
        extra_specs = [pl.BlockSpec((tm, kk), lambda i: (i, gate_block)),
                       pl.BlockSpec((1, kk), lambda i: (0, 0))]
        extra_scratch = [pltpu.VMEM((tm, kk), BF16)]
    out_shape = [jax.ShapeDtypeStruct((n, d), F32)]
    out_specs = [pl.BlockSpec((tm, d), lambda i: (i, 0))]
    if emit_scaled:
        out_shape += [jax.ShapeDtypeStruct((n, d), BF16), jax.ShapeDtypeStruct((1, n, 1), F32)]
        out_specs += [pl.BlockSpec((tm, d), lambda i: (i, 0)),
                      pl.BlockSpec((1, tm, 1), lambda i: (0, i, 0))]
    return pl.pallas_call(
        functools.partial(_down_kernel, emit_scaled=emit_scaled, head_gated=head_gate is not None),
        grid=(n // tm,),
        in_specs=[pl.BlockSpec((tm, kk), lambda i: (i, 0)),
                  pl.BlockSpec((kk, d), lambda i: (0, 0), pipeline_mode=pl.Buffered(1)),
                  pl.BlockSpec((tm, d), lambda i: (i, 0)),
                  pl.BlockSpec((1, d), lambda i: (0, 0)),
                  *extra_specs],
        out_specs=out_specs,
        out_shape=out_shape,
        scratch_shapes=extra_scratch,
        compiler_params=pltpu.CompilerParams(dimension_semantics=("arbitrary",),
                                             vmem_limit_bytes=DOWN_VMEM_LIMIT),
        name="down_residual",
    )(h, w, resid, g.reshape(1, d), *extra_in)


def _retention_kernel(lg_ref, q_ref, k_ref, v_ref, cos_ref, sin_ref, o_ref,
                      qr_ref, qd_ref, kr_ref, kd_ref, mask_ref, qdec_ref, kdec_ref,
                      state_ref, sb_ref, *, n_chunks, n_pairs):
    c = RET_CHUNK
    step = pl.program_id(0)
    slot_prep = step % 2
    slot_recur = 1 - slot_prep

    @pl.when(step == 0)
    def _():
        for ref in (qr_ref, qd_ref, kr_ref, kd_ref):
            ref[...] = jnp.zeros_like(ref)

    lg_prep = lg_ref[jnp.minimum(step, n_pairs - 1) % RET_HEADS]
    lg = lg_ref[jnp.clip(step - 1, 0, n_pairs - 1) % RET_HEADS]
    ii = lax.broadcasted_iota(jnp.int32, (c, c), 0)
    jj = lax.broadcasted_iota(jnp.int32, (c, c), 1)
    rel = (ii - jj).astype(F32)
    mask_ref[...] = jnp.where(rel >= 0.0, jnp.exp(jnp.maximum(rel, 0.0) * lg), 0.0)
    idx = lax.broadcasted_iota(jnp.int32, (c, 1), 0).astype(F32)
    k_scale = RET_DK ** -0.5
    qdec_ref[...] = jnp.broadcast_to(jnp.exp((idx + 1.0) * lg_prep), (c, RET_DK))
    kdec_ref[...] = jnp.broadcast_to(jnp.exp((c - 1.0 - idx) * lg_prep) * k_scale, (c, RET_DK))
    chunk_decay = jnp.exp(jnp.full((1, 1), float(c), F32) * lg)
    half = RET_DK // 2

    def rotary(xb, cos, sin):
        x = xb.astype(F32)
        x1, x2 = x[:, :half], x[:, half:]
        return jnp.concatenate([x1 * cos - x2 * sin, x2 * cos + x1 * sin], axis=1)

    def rows(ci):
        return pl.ds(pl.multiple_of(ci * c, c), c)

    state_ref[...] = jnp.zeros_like(state_ref)
    sb_ref[...] = jnp.zeros_like(sb_ref)

    def body(ci, carry):
        r = rows(ci)

        v = v_ref[r, :]
        scores = lax.dot_general(qr_ref[slot_recur, r, :], kr_ref[slot_recur, r, :],
                                 (((1,), (1,)), ((), ())),
                                 preferred_element_type=F32) * mask_ref[...]
        o_ref[r, :] = (_dot(scores.astype(BF16), v)
                       + _dot(qd_ref[slot_recur, r, :], sb_ref[...])).astype(o_ref.dtype)
        upd = lax.dot_general(kd_ref[slot_recur, r, :], v, (((0,), (0,)), ((), ())),
                              preferred_element_type=F32)
        state = state_ref[...] * chunk_decay + upd
        state_ref[...] = state
        sb_ref[...] = state.astype(BF16)

        for lo in range(0, c, RET_PREP_ROWS):
            rr = pl.ds(pl.multiple_of(ci * c + lo, RET_PREP_ROWS), RET_PREP_ROWS)
            dec = pl.ds(lo, RET_PREP_ROWS)
            cos = cos_ref[rr, :]
            sin = sin_ref[rr, :]
            q = rotary(q_ref[rr, :], cos, sin)
            k = rotary(k_ref[rr, :], cos, sin)
            qr_ref[slot_prep, rr, :] = q.astype(BF16)
            qd_ref[slot_prep, rr, :] = (q * qdec_ref[dec, :]).astype(BF16)
            kr_ref[slot_prep, rr, :] = (k * k_scale).astype(BF16)
            kd_ref[slot_prep, rr, :] = (k * kdec_ref[dec, :]).astype(BF16)
        return carry

    lax.fori_loop(0, n_chunks, body, 0, unroll=2)


def retention(proj, batch, seq):
    n = batch * seq
    h = RET_HEADS
    log_g = jnp.log1p(-jnp.exp2(-5.0 - jnp.arange(h, dtype=F32)))
    freq = 1.0 / (ROPE_BASE ** jnp.linspace(0.0, 1.0, RET_DK // 2, dtype=F32))
    ang = jnp.arange(seq).astype(F32)[:, None] * freq[None, :]
    cos = jnp.cos(ang)
    sin = jnp.sin(ang)
    kq = RET_QK // RET_DK
    kv = 2 * RET_QK // RET_DV
    n_pairs = batch * h
    stages = 2

    def pair(step, stage):
        p = jnp.clip(step - stage, 0, n_pairs - 1)
        return p // h, p % h

    def block(stage, col0):
        def index(s):
            b, hh = pair(s, stage)
            return b, col0 + hh
        return index

    return pl.pallas_call(
        functools.partial(_retention_kernel, n_chunks=seq // RET_CHUNK, n_pairs=n_pairs),
        grid=(n_pairs + stages - 1,),
        in_specs=[pl.BlockSpec(memory_space=pltpu.SMEM),
                  pl.BlockSpec((seq, RET_DK), block(0, 0)),
                  pl.BlockSpec((seq, RET_DK), block(0, kq)),
                  pl.BlockSpec((seq, RET_DV), block(1, kv)),
                  pl.BlockSpec((seq, RET_DK // 2), lambda s: (0, 0)),
                  pl.BlockSpec((seq, RET_DK // 2), lambda s: (0, 0))],
        out_specs=pl.BlockSpec((seq, RET_DV), block(1, 0)),
        out_shape=jax.ShapeDtypeStruct((n, RET_V), BF16),
        scratch_shapes=[pltpu.VMEM((2, seq, RET_DK), BF16),
                        pltpu.VMEM((2, seq, RET_DK), BF16),
                        pltpu.VMEM((2, seq, RET_DK), BF16),
                        pltpu.VMEM((2, seq, RET_DK), BF16),
                        pltpu.VMEM((RET_CHUNK, RET_CHUNK), F32),
                        pltpu.VMEM((RET_CHUNK, RET_DK), F32),
                        pltpu.VMEM((RET_CHUNK, RET_DK), F32),
                        pltpu.VMEM((RET_DK, RET_DV), F32),
                        pltpu.VMEM((RET_DK, RET_DV), BF16)],
        compiler_params=pltpu.CompilerParams(dimension_semantics=("arbitrary",)),
        name="retention",
    )(log_g, proj, proj, proj, cos, sin)


def _pool_kernel(x_ref, gm_ref, pw_ref, sc_ref, gf_ref, rt_ref,
                 xo_ref, hn_ref, meta_ref, meta_t_ref, cnt_out_ref, hbuf, cnt_ref, *, tiles_per_seq):
    tm = x_ref.shape[0]
    step = pl.program_id(0)
    ti = step % tiles_per_seq
    x = x_ref[...]
    hn = _rms(x, gm_ref[...])

    @pl.when(ti == 0)
    def _():
        hbuf[0:POOL_HALO, :] = jnp.zeros((POOL_HALO, D_MODEL), F32)

    @pl.when(ti != 0)
    def _():
        hbuf[0:POOL_HALO, :] = hbuf[tm:tm + POOL_HALO, :]

    hbuf[POOL_HALO:POOL_HALO + tm, :] = hn

    t = ti * tm + lax.broadcasted_iota(jnp.int32, (tm, 1), 0)
    for gi, w in enumerate(POOL_WINDOWS):
        lo, hi = gi * POOL_G, (gi + 1) * POOL_G
        cur = hn[:, lo:hi]
        acc = hbuf[:, lo:hi]
        shift = 1
        while shift < w:
            acc = acc + pltpu.roll(acc, shift, 0)
            shift *= 2
        acc = acc[POOL_HALO:, :]
        inv = 1.0 / jnp.minimum(t + 1, w).astype(F32)
        mix = (acc * inv - cur).astype(BF16)
        o = _dot(mix, pw_ref[gi]) * sc_ref[:, lo:hi]
        xo_ref[:, lo:hi] = x[:, lo:hi] + o

    hn2 = _rms(xo_ref[...], gf_ref[...])
    hn_ref[...] = hn2

    rt = rt_ref[...]
    hn_hi = hn2.astype(BF16)
    hn_lo = (hn2 - hn_hi.astype(F32)).astype(BF16)
    rt_hi = rt.astype(BF16)
    rt_lo = (rt - rt_hi.astype(F32)).astype(BF16)
    hi_both = _dot(hn_hi, jnp.concatenate([rt_hi, rt_lo], axis=1))
    logits = hi_both[:, :LANES] + (hi_both[:, LANES:] + _dot(hn_lo, rt_hi))
    lt = logits.T[:N_EXPERTS, :]
    expert = lax.broadcasted_iota(jnp.int32, lt.shape, 0)
    neg = jnp.float32(-jnp.inf)
    m1 = jnp.max(lt, axis=0, keepdims=True)
    i1 = jnp.min(jnp.where(lt == m1, expert, N_EXPERTS), axis=0, keepdims=True)
    lt2 = jnp.where(expert == i1, neg, lt)
    m2 = jnp.max(lt2, axis=0, keepdims=True)
    i2 = jnp.min(jnp.where(lt2 == m2, expert, N_EXPERTS), axis=0, keepdims=True)
    e2 = jnp.exp(m2 - m1)
    g1 = 1.0 / (1.0 + e2)
    g2 = e2 * g1

    @pl.when(step == 0)
    def _():
        cnt_ref[...] = jnp.zeros_like(cnt_ref)

    chosen = jnp.where((expert == i1) | (expert == i2), 1.0, 0.0)
    src = lax.broadcasted_iota(jnp.int32, (tm, tm), 0)
    dst = lax.broadcasted_iota(jnp.int32, (tm, tm), 1)
    earlier = jnp.where(src < dst, 1.0, 0.0).astype(BF16)
    before = _dot(chosen.astype(BF16), earlier) + cnt_ref[:, 0:1]
    r1 = jnp.sum(jnp.where(expert == i1, before, 0.0), axis=0, keepdims=True)
    r2 = jnp.sum(jnp.where(expert == i2, before, 0.0), axis=0, keepdims=True)
    cnt_ref[...] += jnp.sum(chosen, axis=1, keepdims=True)
    cnt_out_ref[...] = cnt_ref[...]

    fields = {META_E1: i1.astype(F32), META_E2: i2.astype(F32), META_R1: r1, META_R2: r2,
              META_G1: g1, META_G2: g2}
    zero_row = jnp.zeros_like(g1)
    meta_t = jnp.concatenate([fields.get(row, zero_row) for row in range(SUBLANES)], axis=0)
    meta_t_ref[...] = meta_t
    meta_ref[...] = jnp.concatenate([meta_t, jnp.zeros((LANES - SUBLANES, tm), F32)], axis=0).T


def pool_router(x, g_mix, pool_w, layer, pool_scale, g_ffn, router, seq, tm=512):
    n, d = x.shape
    tm = min(tm, seq)
    router_p = jnp.zeros((d, LANES), F32).at[:, :N_EXPERTS].set(router)
    ng = len(POOL_WINDOWS)
    return pl.pallas_call(
        functools.partial(_pool_kernel, tiles_per_seq=seq // tm),
        grid=(n // tm,),
        in_specs=[pl.BlockSpec((tm, d), lambda i: (i, 0)),
                  pl.BlockSpec((1, d), lambda i: (0, 0)),
                  pl.BlockSpec((None, ng, POOL_G, POOL_G), lambda i: (layer, 0, 0, 0)),
                  pl.BlockSpec((1, d), lambda i: (0, 0)),
                  pl.BlockSpec((1, d), lambda i: (0, 0)),
                  pl.BlockSpec((d, LANES), lambda i: (0, 0))],
        out_specs=[pl.BlockSpec((tm, d), lambda i: (i, 0)),
                   pl.BlockSpec((tm, d), lambda i: (i, 0)),
                   pl.BlockSpec((tm, LANES), lambda i: (i, 0)),
                   pl.BlockSpec((SUBLANES, tm), lambda i: (0, i)),
                   pl.BlockSpec((SUBLANES, LANES), lambda i: (0, 0))],
        out_shape=[jax.ShapeDtypeStruct((n, d), F32),
                   jax.ShapeDtypeStruct((n, d), F32),
                   jax.ShapeDtypeStruct((n, LANES), F32),
                   jax.ShapeDtypeStruct((SUBLANES, n), F32),
                   jax.ShapeDtypeStruct((SUBLANES, LANES), F32)],
        scratch_shapes=[pltpu.VMEM((POOL_HALO + tm, d), F32),
                        pltpu.VMEM((N_EXPERTS, LANES), F32)],
        compiler_params=pltpu.CompilerParams(dimension_semantics=("arbitrary",)),
        name="pool_router",
    )(x, g_mix.reshape(1, d), pool_w, pool_scale.reshape(1, d), g_ffn.reshape(1, d), router_p)


def moe_num_tiles(n):
    return TOP_K * n // MOE_TILE + N_EXPERTS


def moe_schedule(meta_t, counts_f, n):
    nt = moe_num_tiles(n)
    counts = counts_f[:N_EXPERTS, 0].astype(jnp.int32)
    tiles = (counts + MOE_TILE - 1) // MOE_TILE
    tile_end = jnp.cumsum(tiles)
    starts = (tile_end - tiles) * MOE_TILE
    expert_ids = jnp.arange(N_EXPERTS, dtype=jnp.int32)[:, None]

    def slots(e_row, r_row):
        e = meta_t[e_row].astype(jnp.int32)
        first = jnp.sum(jnp.where(e[None, :] == expert_ids, starts[:, None], 0), axis=0)
        return first + meta_t[r_row].astype(jnp.int32)

    pos = jnp.concatenate([slots(META_E1, META_R1), slots(META_E2, META_R2)])
    total = tile_end[-1]
    t = jnp.arange(nt, dtype=jnp.int32)
    t_used = jnp.minimum(t, total - 1)
    tile_expert = jnp.sum((tile_end[None, :] <= t_used[:, None]).astype(jnp.int32), axis=1)
    tile_expert = jnp.minimum(tile_expert, N_EXPERTS - 1)
    tile_valid = (t < total).astype(jnp.int32)
    pad_lo = starts + counts
    pad_hi = starts + tiles * MOE_TILE
    prev_expert = jnp.concatenate([jnp.full((1,), -1, jnp.int32), tile_expert[:-1]])
    group_first = (tile_expert != prev_expert).astype(jnp.int32)
    group_slot = (jnp.cumsum(group_first) - 1) % 2
    later = (expert_ids.T > expert_ids) & (tiles[None, :] > 0)
    next_used = jnp.min(jnp.where(later, expert_ids.T, N_EXPERTS), axis=1)
    next_used = jnp.where(next_used == N_EXPERTS, -1, next_used)
    next_expert = jnp.sum(jnp.where(tile_expert[None, :] == expert_ids, next_used[:, None], 0), axis=0)
    table = jnp.concatenate([tile_expert, tile_valid, group_first, group_slot, next_expert])
    return pos, table.astype(jnp.int32), pad_lo, pad_hi, total.reshape(1)


def _dispatch_kernel(pos_ref, lo_ref, hi_ref, used_ref, hn_ref, xs_hbm, zeros, sem, *, n, n_tiles):
    tb = hn_ref.shape[0]
    step = pl.program_id(0)
    base = step * tb

    def issue(r, c):
        tok = base + r
        src = hn_ref.at[pl.ds(r, 1), :]
        pltpu.make_async_copy(src, xs_hbm.at[pl.ds(pos_ref[tok], 1), :], sem).start()
        pltpu.make_async_copy(src, xs_hbm.at[pl.ds(pos_ref[n + tok], 1), :], sem).start()
        return c

    lax.fori_loop(0, tb, issue, 0, unroll=8)
    for _ in range(TOP_K):
        pltpu.make_async_copy(hn_ref, xs_hbm.at[pl.ds(0, tb), :], sem).wait()

    @pl.when(step == pl.num_programs(0) - 1)
    def _():
        zeros[...] = jnp.zeros_like(zeros)

        def pad_copy(slot):
            return pltpu.make_async_copy(zeros.at[pl.ds(0, 1), :], xs_hbm.at[pl.ds(slot, 1), :], sem)

        def issue_pad(s, c):
            pad_copy(s).start()
            return c

        def drain_pad(s, c):
            pad_copy(s).wait()
            return c

        for e in range(N_EXPERTS):
            lax.fori_loop(lo_ref[e], hi_ref[e], issue_pad, 0)
            lax.fori_loop(lo_ref[e], hi_ref[e], drain_pad, 0)

        def tile_copy(t):
            return pltpu.make_async_copy(zeros, xs_hbm.at[pl.ds(t * MOE_TILE, MOE_TILE), :], sem)

        def issue_tile(t, c):
            tile_copy(t).start()
            return c

        def drain_tile(t, c):
            tile_copy(t).wait()
            return c

        lax.fori_loop(used_ref[0], n_tiles, issue_tile, 0)
        lax.fori_loop(used_ref[0], n_tiles, drain_tile, 0)


def moe_dispatch(hn, pos, pad_lo, pad_hi, used, tb=1024):
    n, d = hn.shape
    tb = min(tb, n)
    nt = moe_num_tiles(n)
    grid_spec = pltpu.PrefetchScalarGridSpec(
        num_scalar_prefetch=4,
        grid=(n // tb,),
        in_specs=[pl.BlockSpec((tb, d), lambda i, *_: (i, 0))],
        out_specs=pl.BlockSpec(memory_space=pl.ANY),
        scratch_shapes=[pltpu.VMEM((MOE_TILE, d), F32), pltpu.SemaphoreType.DMA],
    )
    return pl.pallas_call(
        functools.partial(_dispatch_kernel, n=n, n_tiles=nt),
        grid_spec=grid_spec,
        out_shape=jax.ShapeDtypeStruct((nt * MOE_TILE, d), F32),
        compiler_params=pltpu.CompilerParams(dimension_semantics=("arbitrary",)),
        name="moe_dispatch",
    )(pos, pad_lo, pad_hi, used, hn)


def _expert_kernel(tab_ref, x_ref, wgu_hbm, wd_hbm, o_ref, wa_buf, wu_buf, wd_buf, sems):
    t = pl.program_id(0)
    nt = pl.num_programs(0)
    f = wd_buf.shape[1]

    def field(k, tile):
        return tab_ref[k * nt + tile]

    def weight_copies(e, slot):
        return (pltpu.make_async_copy(wgu_hbm.at[e, :, pl.ds(0, f)], wa_buf.at[slot], sems.at[slot]),
                pltpu.make_async_copy(wgu_hbm.at[e, :, pl.ds(f, f)], wu_buf.at[slot], sems.at[slot]),
                pltpu.make_async_copy(wd_hbm.at[e], wd_buf.at[slot], sems.at[slot]))

    slot = field(TILE_SLOT, t)

    @pl.when(t == 0)
    def _():
        for cp in weight_copies(field(TILE_EXPERT, 0), field(TILE_SLOT, 0)):
            cp.start()

    @pl.when(field(TILE_FIRST, t) == 1)
    def _():
        for cp in weight_copies(field(TILE_EXPERT, t), slot):
            cp.wait()
        nxt = field(TILE_NEXT, t)

        @pl.when(nxt >= 0)
        def _():
            for cp in weight_copies(nxt, 1 - slot):
                cp.start()

    valid = field(TILE_VALID, t) == 1

    @pl.when(valid)
    def _():
        x = x_ref[...].astype(BF16)
        a = _dot(x, wa_buf[slot])
        u = _dot(x, wu_buf[slot])
        o_ref[...] = _dot((_silu(a) * u).astype(BF16), wd_buf[slot])

    @pl.when(jnp.logical_not(valid))
    def _():
        o_ref[...] = jnp.zeros_like(o_ref)


def moe_experts(xs, w_gu, w_down, table):
    rows, d = xs.shape
    f = w_down.shape[1]
    nt = rows // MOE_TILE
    grid_spec = pltpu.PrefetchScalarGridSpec(
        num_scalar_prefetch=1,
        grid=(nt,),
        in_specs=[pl.BlockSpec((MOE_TILE, d), lambda t, tab: (t, 0)),
                  pl.BlockSpec(memory_space=pl.ANY),
                  pl.BlockSpec(memory_space=pl.ANY)],
        out_specs=pl.BlockSpec((MOE_TILE, d), lambda t, tab: (t, 0)),
        scratch_shapes=[pltpu.VMEM((2, d, f), BF16), pltpu.VMEM((2, d, f), BF16),
                        pltpu.VMEM((2, f, d), BF16), pltpu.SemaphoreType.DMA((2,))],
    )
    return pl.pallas_call(
        _expert_kernel,
        grid_spec=grid_spec,
        out_shape=jax.ShapeDtypeStruct((rows, d), F32),
        compiler_params=pltpu.CompilerParams(dimension_semantics=("arbitrary",),
                                             vmem_limit_bytes=EXPERT_VMEM_LIMIT),
        name="moe_experts",
    )(table, xs, w_gu, w_down)


def _combine_kernel(pos_ref, x_ref, meta_ref, g_ref, y_hbm, *refs, n, emit_x, emit_norm):
    ybuf, sem = refs[-2], refs[-1]
    tc = x_ref.shape[0]
    step = pl.program_id(0)
    slot = step % 2

    def start_gather(s, into):
        def issue(r, c):
            tok = s * tc + r
            for k in range(TOP_K):
                pltpu.make_async_copy(y_hbm.at[pl.ds(pos_ref[k * n + tok], 1), :],
                                      ybuf.at[into, k, pl.ds(r, 1), :], sem.at[into]).start()
            return c

        lax.fori_loop(0, tc, issue, 0, unroll=8)

    @pl.when(step == 0)
    def _():
        start_gather(0, 0)

    @pl.when(step + 1 < pl.num_programs(0))
    def _():
        start_gather(step + 1, 1 - slot)

    for k in range(TOP_K):
        pltpu.make_async_copy(y_hbm.at[pl.ds(0, tc), :], ybuf.at[slot, k], sem.at[slot]).wait()

    meta = meta_ref[...]
    g1 = meta[:, META_G1:META_G1 + 1]
    g2 = meta[:, META_G2:META_G2 + 1]
    xn = x_ref[...] + g1 * ybuf[slot, 0] + g2 * ybuf[slot, 1]
    _emit_residual_outputs(xn, g_ref, refs, emit_x, emit_norm)


def moe_combine(x, y, meta, pos, g, *, emit_x=True, norm_dtype=None, tc=512):
    n, d = x.shape
    tc = min(tc, n)
    out_shape, out_specs = _residual_out_shapes(n, d, tc, emit_x, norm_dtype, lambda i, p: (i, 0))
    grid_spec = pltpu.PrefetchScalarGridSpec(
        num_scalar_prefetch=1,
        grid=(n // tc,),
        in_specs=[pl.BlockSpec((tc, d), lambda i, p: (i, 0)),
                  pl.BlockSpec((tc, LANES), lambda i, p: (i, 0)),
                  pl.BlockSpec((1, d), lambda i, p: (0, 0)),
                  pl.BlockSpec(memory_space=pl.ANY)],
        out_specs=out_specs,
        scratch_shapes=[pltpu.VMEM((2, TOP_K, tc, d), F32), pltpu.SemaphoreType.DMA((2,))],
    )
    return pl.pallas_call(
        functools.partial(_combine_kernel, n=n, emit_x=emit_x, emit_norm=norm_dtype is not None),
        grid_spec=grid_spec,
        out_shape=out_shape,
        compiler_params=pltpu.CompilerParams(dimension_semantics=("arbitrary",)),
        name="moe_combine",
    )(pos, x, meta, g.reshape(1, d), y)


def kernel(x, ret_w_in, ret_norm_g, ret_w_out, pool_w, pool_scale, ffn_w_gu, ffn_w_down,
           moe_router, moe_w_gu, moe_w_down, norm_mix_g, norm_ffn_g, norm_final_g):
    batch, seq, d = x.shape
    n = batch * seq
    n_moe, n_exp, _, f2 = moe_w_gu.shape
    f = moe_w_down.shape[2]
    assert d == D_MODEL and n_exp == N_EXPERTS and seq % RET_CHUNK == 0 and seq >= POOL_HALO
    assert norm_mix_g.shape[0] == DEPTH and DEPTH % 2 == 0, "the trunk ends on a pooling / MoE layer"
    xf = x.reshape(n, d)
    pool_w = pool_w.astype(BF16)
    moe_gu_rows = moe_w_gu.reshape(n_moe, n_exp * d, f2)
    moe_down_rows = moe_w_down.reshape(n_moe, n_exp * f, d)
    hn = rmsnorm(xf, norm_mix_g[0], BF16)
    for i in range(DEPTH):
        j = i // 2
        last = i == DEPTH - 1
        if i % 2 == 0:
            proj, moe_gu_b, w_out_b = retention_in_proj(hn, ret_w_in, j,
                                                        [(moe_gu_rows, j), (ret_w_out, j)])
            y = retention(proj, batch, seq)
            gate_block = (2 * RET_QK + RET_V) // RET_V
            xf, xg, ssq = down_residual(y, w_out_b, xf, norm_ffn_g[i],
                                        head_gate=(proj, gate_block, ret_norm_g[j]))
            hid, moe_down_b, ffn_down_b = swiglu_up(xg, ssq, ffn_w_gu, j,
                                                    [(moe_down_rows, j), (ffn_w_down, j)])
            (xf,) = down_residual(hid, ffn_down_b, xf)
            moe_gu_b = moe_gu_b.reshape(n_exp, d, f2)
            moe_down_b = moe_down_b.reshape(n_exp, f, d)
        else:
            xf, hn, meta, meta_t, counts = pool_router(xf, norm_mix_g[i], pool_w, j, pool_scale[j],
                                                       norm_ffn_g[i], moe_router[j], seq)
            pos, tile_table, pad_lo, pad_hi, used = moe_schedule(meta_t, counts, n)
            xs = moe_dispatch(hn, pos, pad_lo, pad_hi, used)
            ys = moe_experts(xs, moe_gu_b, moe_down_b, tile_table)
            if last:
                (out,) = moe_combine(xf, ys, meta, pos, norm_final_g, emit_x=False, norm_dtype=F32)
            else:
                xf, hn = moe_combine(xf, ys, meta, pos, norm_mix_g[i + 1], norm_dtype=BF16)
    return out.reshape(batch, seq, d)
```

```python
import functools

import jax
import jax.numpy as jnp
from jax import lax
from jax.experimental import pallas as pl
from jax.experimental.pallas import tpu as pltpu

BF16 = jnp.bfloat16
F32 = jnp.float32

D_MODEL = 2048
DEPTH = 4
RET_HEADS = 8
RET_DK = D_MODEL // RET_HEADS
RET_DV = 2 * D_MODEL // RET_HEADS
RET_QK = RET_HEADS * RET_DK
RET_V = RET_HEADS * RET_DV
RET_CHUNK = 256
RET_PREP_ROWS = 32
ROPE_BASE = 10000.0
POOL_WINDOWS = (2, 4, 8, 16)
POOL_G = D_MODEL // len(POOL_WINDOWS)
POOL_HALO = 16
N_EXPERTS = 8
TOP_K = 2
NORM_EPS = 1e-6
LANES = 128
SUBLANES = 8
BF16_TILE_ROWS = 16
MOE_TILE = 256
SIDE_CAST_BLOCKS = 64
X_RING = 3
DOWN_VMEM_LIMIT = 60 * 1024 * 1024
EXPERT_VMEM_LIMIT = 56 * 1024 * 1024
META_E1, META_E2, META_R1, META_R2, META_G1, META_G2 = range(6)
TILE_EXPERT, TILE_VALID, TILE_FIRST, TILE_SLOT, TILE_NEXT = range(5)


def _rms(x, g):
    ms = jnp.mean(x * x, axis=-1, keepdims=True)
    return x * lax.rsqrt(ms + NORM_EPS) * g


def _silu(a):
    return a / (1.0 + jnp.exp(-a))


def _dot(a, b):
    return jnp.dot(a, b, preferred_element_type=F32)


def _rmsnorm_kernel(x_ref, g_ref, o_ref):
    o_ref[...] = _rms(x_ref[...], g_ref[...]).astype(o_ref.dtype)


def rmsnorm(x, g, out_dtype, tm=512):
    n, d = x.shape
    return pl.pallas_call(
        _rmsnorm_kernel,
        grid=(n // tm,),
        in_specs=[pl.BlockSpec((tm, d), lambda i: (i, 0)),
                  pl.BlockSpec((1, d), lambda i: (0, 0))],
        out_specs=pl.BlockSpec((tm, d), lambda i: (i, 0)),
        out_shape=jax.ShapeDtypeStruct((n, d), out_dtype),
        name="rmsnorm",
    )(x, g.reshape(1, d))


def _side_cast_specs(sides, steps, n_inner):
    in_specs, out_specs, shapes = [], [], []
    for side, side_layer in sides:
        _, rows, cols = side.shape
        n_blocks = 1
        while (n_blocks * 2 <= min(steps, SIDE_CAST_BLOCKS)
               and rows % (n_blocks * 2 * BF16_TILE_ROWS) == 0):
            n_blocks *= 2
        br = rows // n_blocks

        def index(j, i, n_blocks=n_blocks):
            return jnp.minimum(j * n_inner + i, n_blocks - 1)

        in_specs.append(pl.BlockSpec((None, br, cols),
                                     lambda j, i, l=side_layer, ix=index: (l, ix(j, i), 0)))
        out_specs.append(pl.BlockSpec((br, cols), lambda j, i, ix=index: (ix(j, i), 0)))
        shapes.append(jax.ShapeDtypeStruct((rows, cols), BF16))
    return in_specs, out_specs, shapes


def _cast_sides(refs, n_sides):
    for s_ref, so_ref in zip(refs[:n_sides], refs[n_sides + 1:2 * n_sides + 1]):
        so_ref[...] = s_ref[...].astype(BF16)


class _RowTileRing:
    def __init__(self, x_hbm, buf, sems, n_row_tiles):
        self.x_hbm, self.buf, self.sems, self.n_row_tiles = x_hbm, buf, sems, n_row_tiles

    def _copy(self, s):
        tm = self.buf.shape[1]
        rows = pl.ds(pl.multiple_of((s % self.n_row_tiles) * tm, tm), tm)
        return pltpu.make_async_copy(self.x_hbm.at[rows, :], self.buf.at[s % X_RING],
                                     self.sems.at[s % X_RING])

    def advance(self, step, total):
        @pl.when(step == 0)
        def _():
            for s in range(X_RING - 1):
                self._copy(s).start()

        @pl.when(step + X_RING - 1 < total)
        def _():
            self._copy(step + X_RING - 1).start()

    def tile(self, step):
        self._copy(step).wait()
        return self.buf[step % X_RING]


def _ring_scratch(tm, k):
    return [pltpu.VMEM((X_RING, tm, k), BF16), pltpu.SemaphoreType.DMA((X_RING,))]


def _in_proj_kernel(x_hbm, w_ref, perm_ref, *refs, n_qk_slabs, n_sides):
    o_ref = refs[n_sides]
    wb_ref, xbuf, xsem = refs[-3:]
    ni = pl.num_programs(1)
    step = pl.program_id(0) * ni + pl.program_id(1)
    ring = _RowTileRing(x_hbm, xbuf, xsem, ni)
    ring.advance(step, pl.num_programs(0) * ni)

    @pl.when(pl.program_id(1) == 0)
    def _():
        is_qk = pl.program_id(0) < n_qk_slabs

        @pl.when(is_qk)
        def _():
            for lo in range(0, wb_ref.shape[1], RET_DK):
                head = w_ref[:, lo:lo + RET_DK].astype(BF16)
                wb_ref[:, lo:lo + RET_DK] = _dot(head, perm_ref[...]).astype(BF16)

        @pl.when(jnp.logical_not(is_qk))
        def _():
            wb_ref[...] = w_ref[...].astype(BF16)

    o_ref[...] = _dot(ring.tile(step), wb_ref[...]).astype(o_ref.dtype)
    _cast_sides(refs, n_sides)


def retention_in_proj(x, w, layer, sides, tm=1024, tn=1024):
    n, k = x.shape
    f = w.shape[2]
    tm = min(tm, n)
    nj, ni = f // tn, n // tm
    assert f % tn == 0 and n % tm == 0 and nj * ni >= X_RING - 1
    half = RET_DK // 2
    src = jnp.arange(RET_DK)
    dst = jnp.where(src % 2 == 0, src // 2, half + src // 2)
    perm = (dst[:, None] == jnp.arange(RET_DK)[None, :]).astype(BF16)
    s_in, s_out, s_shapes = _side_cast_specs(sides, nj * ni, ni)
    return pl.pallas_call(
        functools.partial(_in_proj_kernel, n_qk_slabs=2 * RET_QK // tn, n_sides=len(sides)),
        grid=(nj, ni),
        in_specs=[pl.BlockSpec(memory_space=pl.ANY),
                  pl.BlockSpec((None, k, tn), lambda j, i: (layer, 0, j)),
                  pl.BlockSpec((RET_DK, RET_DK), lambda j, i: (0, 0)),
                  *s_in],
        out_specs=[pl.BlockSpec((tm, tn), lambda j, i: (i, j)), *s_out],
        out_shape=[jax.ShapeDtypeStruct((n, f), BF16), *s_shapes],
        scratch_shapes=[pltpu.VMEM((k, tn), BF16), *_ring_scratch(tm, k)],
        compiler_params=pltpu.CompilerParams(dimension_semantics=("arbitrary", "arbitrary")),
        name="retention_in_proj",
    )(x, w, perm, *(side for side, _ in sides))


def _swiglu_kernel(x_hbm, ssq_ref, wa_ref, wu_ref, *refs, n_sides):
    o_ref = refs[n_sides]
    wab_ref, wub_ref, xbuf, xsem = refs[-4:]
    ni = pl.num_programs(1)
    step = pl.program_id(0) * ni + pl.program_id(1)
    ring = _RowTileRing(x_hbm, xbuf, xsem, ni)
    ring.advance(step, pl.num_programs(0) * ni)

    @pl.when(pl.program_id(1) == 0)
    def _():
        wab_ref[...] = wa_ref[...].astype(BF16)
        wub_ref[...] = wu_ref[...].astype(BF16)

    x = ring.tile(step)
    rowscale = lax.rsqrt(jnp.sum(ssq_ref[...], axis=0) / x.shape[1] + NORM_EPS)
    a = _dot(x, wab_ref[...]) * rowscale
    u = _dot(x, wub_ref[...]) * rowscale
    o_ref[...] = (_silu(a) * u).astype(o_ref.dtype)
    _cast_sides(refs, n_sides)


def swiglu_up(xg, ssq, w_gu, layer, sides, tm=1024, tn=512):
    n, k = xg.shape
    f = w_gu.shape[2] // 2
    tm = min(tm, n)
    nj, ni = f // tn, n // tm
    assert f % tn == 0 and n % tm == 0 and nj * ni >= X_RING - 1
    ns = ssq.shape[0]
    s_in, s_out, s_shapes = _side_cast_specs(sides, nj * ni, ni)
    return pl.pallas_call(
        functools.partial(_swiglu_kernel, n_sides=len(sides)),
        grid=(nj, ni),
        in_specs=[pl.BlockSpec(memory_space=pl.ANY),
                  pl.BlockSpec((ns, tm, 1), lambda j, i: (0, i, 0)),
                  pl.BlockSpec((None, k, tn), lambda j, i: (layer, 0, j)),
                  pl.BlockSpec((None, k, tn), lambda j, i: (layer, 0, nj + j)),
                  *s_in],
        out_specs=[pl.BlockSpec((tm, tn), lambda j, i: (i, j)), *s_out],
        out_shape=[jax.ShapeDtypeStruct((n, f), BF16), *s_shapes],
        scratch_shapes=[pltpu.VMEM((k, tn), BF16), pltpu.VMEM((k, tn), BF16),
                        *_ring_scratch(tm, k)],
        compiler_params=pltpu.CompilerParams(dimension_semantics=("arbitrary", "arbitrary")),
        name="swiglu_up",
    )(xg, ssq, w_gu, w_gu, *(side for side, _ in sides))


def _emit_residual_outputs(xn, g_ref, refs, emit_x, emit_norm):
    pos = 0
    if emit_x:
        refs[pos][...] = xn
        pos += 1
    if emit_norm:
        refs[pos][...] = _rms(xn, g_ref[...]).astype(refs[pos].dtype)


def _residual_out_shapes(n, d, tm, emit_x, norm_dtype, index_map):
    out_shape, out_specs = [], []
    if emit_x:
        out_shape.append(jax.ShapeDtypeStruct((n, d), F32))
        out_specs.append(pl.BlockSpec((tm, d), index_map))
    if norm_dtype is not None:
        out_shape.append(jax.ShapeDtypeStruct((n, d), norm_dtype))
        out_specs.append(pl.BlockSpec((tm, d), index_map))
    return out_shape, out_specs


def _down_kernel(x_ref, w_ref, r_ref, g_ref, *refs, emit_scaled, head_gated):
    if head_gated:
        gate_ref, gain_ref, xs_ref = refs[0], refs[1], refs[-1]
        refs = refs[2:-1]
        for lo in range(0, x_ref.shape[1], RET_DV):
            o = x_ref[:, lo:lo + RET_DV].astype(F32)
            ms = jnp.mean(o * o, axis=-1, keepdims=True)
            o = o * lax.rsqrt(ms + NORM_EPS) * gain_ref[:, lo:lo + RET_DV]
            gate = gate_ref[:, lo:lo + RET_DV].astype(F32)
            xs_ref[:, lo:lo + RET_DV] = (o * _silu(gate)).astype(BF16)
        x = xs_ref[...]
    else:
        x = x_ref[...]
    xn = r_ref[...] + _dot(x, w_ref[...])
    refs[0][...] = xn
    if emit_scaled:
        refs[1][...] = (xn * g_ref[...]).astype(BF16)
        refs[2][...] = jnp.sum(xn * xn, axis=-1, keepdims=True)[None]


def down_residual(h, w, resid, g=None, *, head_gate=None, tm=512):
    n, kk = h.shape
    d = w.shape[1]
    tm = min(tm, n)
    emit_scaled = g is not None
    if g is None:
        g = jnp.ones((d,), F32)
    extra_in, extra_specs, extra_scratch = [], [], []
    if head_gate is not None:
        proj, gate_block, head_gain = head_gate
        extra_in = [proj, head_gain.reshape(1, kk)]
        extra_specs = [pl.BlockSpec((tm, kk), lambda i: (i, gate_block)),
                       pl.BlockSpec((1, kk), lambda i: (0, 0))]
        extra_scratch = [pltpu.VMEM((tm, kk), BF16)]
    out_shape = [jax.ShapeDtypeStruct((n, d), F32)]
    out_specs = [pl.BlockSpec((tm, d), lambda i: (i, 0))]
    if emit_scaled:
        out_shape += [jax.ShapeDtypeStruct((n, d), BF16), jax.ShapeDtypeStruct((1, n, 1), F32)]
        out_specs += [pl.BlockSpec((tm, d), lambda i: (i, 0)),
                      pl.BlockSpec((1, tm, 1), lambda i: (0, i, 0))]
    return pl.pallas_call(
        functools.partial(_down_kernel, emit_scaled=emit_scaled, head_gated=head_gate is not None),
        grid=(n // tm,),
        in_specs=[pl.BlockSpec((tm, kk), lambda i: (i, 0)),
                  pl.BlockSpec((kk, d), lambda i: (0, 0), pipeline_mode=pl.Buffered(1)),
                  pl.BlockSpec((tm, d), lambda i: (i, 0)),
                  pl.BlockSpec((1, d), lambda i: (0, 0)),
                  *extra_specs],
        out_specs=out_specs,
        out_shape=out_shape,
        scratch_shapes=extra_scratch,
        compiler_params=pltpu.CompilerParams(dimension_semantics=("arbitrary",),
                                             vmem_limit_bytes=DOWN_VMEM_LIMIT),
        name="down_residual",
    )(h, w, resid, g.reshape(1, d), *extra_in)


def _retention_kernel(lg_ref, q_ref, k_ref, v_ref, cos_ref, sin_ref, o_ref,
                      qr_ref, qd_ref, kr_ref, kd_ref, mask_ref, qdec_ref, kdec_ref,
                      state_ref, sb_ref, *, n_chunks, n_pairs):
    c = RET_CHUNK
    step = pl.program_id(0)
    slot_prep = step % 2
    slot_recur = 1 - slot_prep

    @pl.when(step == 0)
    def _():
        for ref in (qr_ref, qd_ref, kr_ref, kd_ref):
            ref[...] = jnp.zeros_like(ref)

    lg_prep = lg_ref[jnp.minimum(step, n_pairs - 1) % RET_HEADS]
    lg = lg_ref[jnp.clip(step - 1, 0, n_pairs - 1) % RET_HEADS]
    ii = lax.broadcasted_iota(jnp.int32, (c, c), 0)
    jj = lax.broadcasted_iota(jnp.int32, (c, c), 1)
    rel = (ii - jj).astype(F32)
    mask_ref[...] = jnp.where(rel >= 0.0, jnp.exp(jnp.maximum(rel, 0.0) * lg), 0.0)
    idx = lax.broadcasted_iota(jnp.int32, (c, 1), 0).astype(F32)
    k_scale = RET_DK ** -0.5
    qdec_ref[...] = jnp.broadcast_to(jnp.exp((idx + 1.0) * lg_prep), (c, RET_DK))
    kdec_ref[...] = jnp.broadcast_to(jnp.exp((c - 1.0 - idx) * lg_prep) * k_scale, (c, RET_DK))
    chunk_decay = jnp.exp(jnp.full((1, 1), float(c), F32) * lg)
    half = RET_DK // 2

    def rotary(xb, cos, sin):
        x = xb.astype(F32)
        x1, x2 = x[:, :half], x[:, half:]
        return jnp.concatenate([x1 * cos - x2 * sin, x2 * cos + x1 * sin], axis=1)

    def rows(ci):
        return pl.ds(pl.multiple_of(ci * c, c), c)

    state_ref[...] = jnp.zeros_like(state_ref)
    sb_ref[...] = jnp.zeros_like(sb_ref)

    def body(ci, carry):
        r = rows(ci)

        v = v_ref[r, :]
        scores = lax.dot_general(qr_ref[slot_recur, r, :], kr_ref[slot_recur, r, :],
                                 (((1,), (1,)), ((), ())),
                                 preferred_element_type=F32) * mask_ref[...]
        o_ref[r, :] = (_dot(scores.astype(BF16), v)
                       + _dot(qd_ref[slot_recur, r, :], sb_ref[...])).astype(o_ref.dtype)
        upd = lax.dot_general(kd_ref[slot_recur, r, :], v, (((0,), (0,)), ((), ())),
                              preferred_element_type=F32)
        state = state_ref[...] * chunk_decay + upd
        state_ref[...] = state
        sb_ref[...] = state.astype(BF16)

        for lo in range(0, c, RET_PREP_ROWS):
            rr = pl.ds(pl.multiple_of(ci * c + lo, RET_PREP_ROWS), RET_PREP_ROWS)
            dec = pl.ds(lo, RET_PREP_ROWS)
            cos = cos_ref[rr, :]
            sin = sin_ref[rr, :]
            q = rotary(q_ref[rr, :], cos, sin)
            k = rotary(k_ref[rr, :], cos, sin)
            qr_ref[slot_prep, rr, :] = q.astype(BF16)
            qd_ref[slot_prep, rr, :] = (q * qdec_ref[dec, :]).astype(BF16)
            kr_ref[slot_prep, rr, :] = (k * k_scale).astype(BF16)
            kd_ref[slot_prep, rr, :] = (k * kdec_ref[dec, :]).astype(BF16)
        return carry

    lax.fori_loop(0, n_chunks, body, 0, unroll=2)


def retention(proj, batch, seq):
    n = batch * seq
    h = RET_HEADS
    log_g = jnp.log1p(-jnp.exp2(-5.0 - jnp.arange(h, dtype=F32)))
    freq = 1.0 / (ROPE_BASE ** jnp.linspace(0.0, 1.0, RET_DK // 2, dtype=F32))
    ang = jnp.arange(seq).astype(F32)[:, None] * freq[None, :]
    cos = jnp.cos(ang)
    sin = jnp.sin(ang)
    kq = RET_QK // RET_DK
    kv = 2 * RET_QK // RET_DV
    n_pairs = batch * h
    stages = 2

    def pair(step, stage):
        p = jnp.clip(step - stage, 0, n_pairs - 1)
        return p // h, p % h

    def block(stage, col0):
        def index(s):
            b, hh = pair(s, stage)
            return b, col0 + hh
        return index

    return pl.pallas_call(
        functools.partial(_retention_kernel, n_chunks=seq // RET_CHUNK, n_pairs=n_pairs),
        grid=(n_pairs + stages - 1,),
        in_specs=[pl.BlockSpec(memory_space=pltpu.SMEM),
                  pl.BlockSpec((seq, RET_DK), block(0, 0)),
                  pl.BlockSpec((seq, RET_DK), block(0, kq)),
                  pl.BlockSpec((seq, RET_DV), block(1, kv)),
                  pl.BlockSpec((seq, RET_DK // 2), lambda s: (0, 0)),
                  pl.BlockSpec((seq, RET_DK // 2), lambda s: (0, 0))],
        out_specs=pl.BlockSpec((seq, RET_DV), block(1, 0)),
        out_shape=jax.ShapeDtypeStruct((n, RET_V), BF16),
        scratch_shapes=[pltpu.VMEM((2, seq, RET_DK), BF16),
                        pltpu.VMEM((2, seq, RET_DK), BF16),
                        pltpu.VMEM((2, seq, RET_DK), BF16),
                        pltpu.VMEM((2, seq, RET_DK), BF16),
                        pltpu.VMEM((RET_CHUNK, RET_CHUNK), F32),
                        pltpu.VMEM((RET_CHUNK, RET_DK), F32),
                        pltpu.VMEM((RET_CHUNK, RET_DK), F32),
                        pltpu.VMEM((RET_DK, RET_DV), F32),
                        pltpu.VMEM((RET_DK, RET_DV), BF16)],
        compiler_params=pltpu.CompilerParams(dimension_semantics=("arbitrary",)),
        name="retention",
    )(log_g, proj, proj, proj, cos, sin)


def _pool_kernel(x_ref, gm_ref, pw_ref, sc_ref, gf_ref, rt_ref,
                 xo_ref, hn_ref, meta_ref, meta_t_ref, cnt_out_ref, hbuf, cnt_ref, *, tiles_per_seq):
    tm = x_ref.shape[0]
    step = pl.program_id(0)
    ti = step % tiles_per_seq
    x = x_ref[...]
    hn = _rms(x, gm_ref[...])

    @pl.when(ti == 0)
    def _():
        hbuf[0:POOL_HALO, :] = jnp.zeros((POOL_HALO, D_MODEL), F32)

    @pl.when(ti != 0)
    def _():
        hbuf[0:POOL_HALO, :] = hbuf[tm:tm + POOL_HALO, :]

    hbuf[POOL_HALO:POOL_HALO + tm, :] = hn

    t = ti * tm + lax.broadcasted_iota(jnp.int32, (tm, 1), 0)
    for gi, w in enumerate(POOL_WINDOWS):
        lo, hi = gi * POOL_G, (gi + 1) * POOL_G
        cur = hn[:, lo:hi]
        acc = hbuf[:, lo:hi]
        shift = 1
        while shift < w:
            acc = acc + pltpu.roll(acc, shift, 0)
            shift *= 2
        acc = acc[POOL_HALO:, :]
        inv = 1.0 / jnp.minimum(t + 1, w).astype(F32)
        mix = (acc * inv - cur).astype(BF16)
        o = _dot(mix, pw_ref[gi]) * sc_ref[:, lo:hi]
        xo_ref[:, lo:hi] = x[:, lo:hi] + o

    hn2 = _rms(xo_ref[...], gf_ref[...])
    hn_ref[...] = hn2

    rt = rt_ref[...]
    hn_hi = hn2.astype(BF16)
    hn_lo = (hn2 - hn_hi.astype(F32)).astype(BF16)
    rt_hi = rt.astype(BF16)
    rt_lo = (rt - rt_hi.astype(F32)).astype(BF16)
    hi_both = _dot(hn_hi, jnp.concatenate([rt_hi, rt_lo], axis=1))
    logits = hi_both[:, :LANES] + (hi_both[:, LANES:] + _dot(hn_lo, rt_hi))
    lt = logits.T[:N_EXPERTS, :]
    expert = lax.broadcasted_iota(jnp.int32, lt.shape, 0)
    neg = jnp.float32(-jnp.inf)
    m1 = jnp.max(lt, axis=0, keepdims=True)
    i1 = jnp.min(jnp.where(lt == m1, expert, N_EXPERTS), axis=0, keepdims=True)
    lt2 = jnp.where(expert == i1, neg, lt)
    m2 = jnp.max(lt2, axis=0, keepdims=True)
    i2 = jnp.min(jnp.where(lt2 == m2, expert, N_EXPERTS), axis=0, keepdims=True)
    e2 = jnp.exp(m2 - m1)
    g1 = 1.0 / (1.0 + e2)
    g2 = e2 * g1

    @pl.when(step == 0)
    def _():
        cnt_ref[...] = jnp.zeros_like(cnt_ref)

    chosen = jnp.where((expert == i1) | (expert == i2), 1.0, 0.0)
    src = lax.broadcasted_iota(jnp.int32, (tm, tm), 0)
    dst = lax.broadcasted_iota(jnp.int32, (tm, tm), 1)
    earlier = jnp.where(src < dst, 1.0, 0.0).astype(BF16)
    before = _dot(chosen.astype(BF16), earlier) + cnt_ref[:, 0:1]
    r1 = jnp.sum(jnp.where(expert == i1, before, 0.0), axis=0, keepdims=True)
    r2 = jnp.sum(jnp.where(expert == i2, before, 0.0), axis=0, keepdims=True)
    cnt_ref[...] += jnp.sum(chosen, axis=1, keepdims=True)
    cnt_out_ref[...] = cnt_ref[...]

    fields = {META_E1: i1.astype(F32), META_E2: i2.astype(F32), META_R1: r1, META_R2: r2,
              META_G1: g1, META_G2: g2}
    zero_row = jnp.zeros_like(g1)
    meta_t = jnp.concatenate([fields.get(row, zero_row) for row in range(SUBLANES)], axis=0)
    meta_t_ref[...] = meta_t
    meta_ref[...] = jnp.concatenate([meta_t, jnp.zeros((LANES - SUBLANES, tm), F32)], axis=0).T


def pool_router(x, g_mix, pool_w, layer, pool_scale, g_ffn, router, seq, tm=512):
    n, d = x.shape
    tm = min(tm, seq)
    router_p = jnp.zeros((d, LANES), F32).at[:, :N_EXPERTS].set(router)
    ng = len(POOL_WINDOWS)
    return pl.pallas_call(
        functools.partial(_pool_kernel, tiles_per_seq=seq // tm),
        grid=(n // tm,),
        in_specs=[pl.BlockSpec((tm, d), lambda i: (i, 0)),
                  pl.BlockSpec((1, d), lambda i: (0, 0)),
                  pl.BlockSpec((None, ng, POOL_G, POOL_G), lambda i: (layer, 0, 0, 0)),
                  pl.BlockSpec((1, d), lambda i: (0, 0)),
                  pl.BlockSpec((1, d), lambda i: (0, 0)),
                  pl.BlockSpec((d, LANES), lambda i: (0, 0))],
        out_specs=[pl.BlockSpec((tm, d), lambda i: (i, 0)),
                   pl.BlockSpec((tm, d), lambda i: (i, 0)),
                   pl.BlockSpec((tm, LANES), lambda i: (i, 0)),
                   pl.BlockSpec((SUBLANES, tm), lambda i: (0, i)),
                   pl.BlockSpec((SUBLANES, LANES), lambda i: (0, 0))],
        out_shape=[jax.ShapeDtypeStruct((n, d), F32),
                   jax.ShapeDtypeStruct((n, d), F32),
                   jax.ShapeDtypeStruct((n, LANES), F32),
                   jax.ShapeDtypeStruct((SUBLANES, n), F32),
                   jax.ShapeDtypeStruct((SUBLANES, LANES), F32)],
        scratch_shapes=[pltpu.VMEM((POOL_HALO + tm, d), F32),
                        pltpu.VMEM((N_EXPERTS, LANES), F32)],
        compiler_params=pltpu.CompilerParams(dimension_semantics=("arbitrary",)),
        name="pool_router",
    )(x, g_mix.reshape(1, d), pool_w, pool_scale.reshape(1, d), g_ffn.reshape(1, d), router_p)


def moe_num_tiles(n):
    return TOP_K * n // MOE_TILE + N_EXPERTS


def moe_schedule(meta_t, counts_f, n):
    nt = moe_num_tiles(n)
    counts = counts_f[:N_EXPERTS, 0].astype(jnp.int32)
    tiles = (counts + MOE_TILE - 1) // MOE_TILE
    tile_end = jnp.cumsum(tiles)
    starts = (tile_end - tiles) * MOE_TILE
    expert_ids = jnp.arange(N_EXPERTS, dtype=jnp.int32)[:, None]

    def slots(e_row, r_row):
        e = meta_t[e_row].astype(jnp.int32)
        first = jnp.sum(jnp.where(e[None, :] == expert_ids, starts[:, None], 0), axis=0)
        return first + meta_t[r_row].astype(jnp.int32)

    pos = jnp.concatenate([slots(META_E1, META_R1), slots(META_E2, META_R2)])
    total = tile_end[-1]
    t = jnp.arange(nt, dtype=jnp.int32)
    t_used = jnp.minimum(t, total - 1)
    tile_expert = jnp.sum((tile_end[None, :] <= t_used[:, None]).astype(jnp.int32), axis=1)
    tile_expert = jnp.minimum(tile_expert, N_EXPERTS - 1)
    tile_valid = (t < total).astype(jnp.int32)
    pad_lo = starts + counts
    pad_hi = starts + tiles * MOE_TILE
    prev_expert = jnp.concatenate([jnp.full((1,), -1, jnp.int32), tile_expert[:-1]])
    group_first = (tile_expert != prev_expert).astype(jnp.int32)
    group_slot = (jnp.cumsum(group_first) - 1) % 2
    later = (expert_ids.T > expert_ids) & (tiles[None, :] > 0)
    next_used = jnp.min(jnp.where(later, expert_ids.T, N_EXPERTS), axis=1)
    next_used = jnp.where(next_used == N_EXPERTS, -1, next_used)
    next_expert = jnp.sum(jnp.where(tile_expert[None, :] == expert_ids, next_used[:, None], 0), axis=0)
    table = jnp.concatenate([tile_expert, tile_valid, group_first, group_slot, next_expert])
    return pos, table.astype(jnp.int32), pad_lo, pad_hi, total.reshape(1)


def _dispatch_kernel(pos_ref, lo_ref, hi_ref, used_ref, hn_ref, xs_hbm, zeros, sem, *, n, n_tiles):
    tb = hn_ref.shape[0]
    step = pl.program_id(0)
    base = step * tb

    def issue(r, c):
        tok = base + r
        src = hn_ref.at[pl.ds(r, 1), :]
        pltpu.make_async_copy(src, xs_hbm.at[pl.ds(pos_ref[tok], 1), :], sem).start()
        pltpu.make_async_copy(src, xs_hbm.at[pl.ds(pos_ref[n + tok], 1), :], sem).start()
        return c

    lax.fori_loop(0, tb, issue, 0, unroll=8)
    for _ in range(TOP_K):
        pltpu.make_async_copy(hn_ref, xs_hbm.at[pl.ds(0, tb), :], sem).wait()

    @pl.when(step == pl.num_programs(0) - 1)
    def _():
        zeros[...] = jnp.zeros_like(zeros)

        def pad_copy(slot):
            return pltpu.make_async_copy(zeros.at[pl.ds(0, 1), :], xs_hbm.at[pl.ds(slot, 1), :], sem)

        def issue_pad(s, c):
            pad_copy(s).start()
            return c

        def drain_pad(s, c):
            pad_copy(s).wait()
            return c

        for e in range(N_EXPERTS):
            lax.fori_loop(lo_ref[e], hi_ref[e], issue_pad, 0)
            lax.fori_loop(lo_ref[e], hi_ref[e], drain_pad, 0)

        def tile_copy(t):
            return pltpu.make_async_copy(zeros, xs_hbm.at[pl.ds(t * MOE_TILE, MOE_TILE), :], sem)

        def issue_tile(t, c):
            tile_copy(t).start()
            return c

        def drain_tile(t, c):
            tile_copy(t).wait()
            return c

        lax.fori_loop(used_ref[0], n_tiles, issue_tile, 0)
        lax.fori_loop(used_ref[0], n_tiles, drain_tile, 0)


def moe_dispatch(hn, pos, pad_lo, pad_hi, used, tb=1024):
    n, d = hn.shape
    tb = min(tb, n)
    nt = moe_num_tiles(n)
    grid_spec = pltpu.PrefetchScalarGridSpec(
        num_scalar_prefetch=4,
        grid=(n // tb,),
        in_specs=[pl.BlockSpec((tb, d), lambda i, *_: (i, 0))],
        out_specs=pl.BlockSpec(memory_space=pl.ANY),
        scratch_shapes=[pltpu.VMEM((MOE_TILE, d), F32), pltpu.SemaphoreType.DMA],
    )
    return pl.pallas_call(
        functools.partial(_dispatch_kernel, n=n, n_tiles=nt),
        grid_spec=grid_spec,
        out_shape=jax.ShapeDtypeStruct((nt * MOE_TILE, d), F32),
        compiler_params=pltpu.CompilerParams(dimension_semantics=("arbitrary",)),
        name="moe_dispatch",
    )(pos, pad_lo, pad_hi, used, hn)


def _expert_kernel(tab_ref, x_ref, wgu_hbm, wd_hbm, o_ref, wa_buf, wu_buf, wd_buf, sems):
    t = pl.program_id(0)
    nt = pl.num_programs(0)
    f = wd_buf.shape[1]

    def field(k, tile):
        return tab_ref[k * nt + tile]

    def weight_copies(e, slot):
        return (pltpu.make_async_copy(wgu_hbm.at[e, :, pl.ds(0, f)], wa_buf.at[slot], sems.at[slot]),
                pltpu.make_async_copy(wgu_hbm.at[e, :, pl.ds(f, f)], wu_buf.at[slot], sems.at[slot]),
                pltpu.make_async_copy(wd_hbm.at[e], wd_buf.at[slot], sems.at[slot]))

    slot = field(TILE_SLOT, t)

    @pl.when(t == 0)
    def _():
        for cp in weight_copies(field(TILE_EXPERT, 0), field(TILE_SLOT, 0)):
            cp.start()

    @pl.when(field(TILE_FIRST, t) == 1)
    def _():
        for cp in weight_copies(field(TILE_EXPERT, t), slot):
            cp.wait()
        nxt = field(TILE_NEXT, t)

        @pl.when(nxt >= 0)
        def _():
            for cp in weight_copies(nxt, 1 - slot):
                cp.start()

    valid = field(TILE_VALID, t) == 1

    @pl.when(valid)
    def _():
        x = x_ref[...].astype(BF16)
        a = _dot(x, wa_buf[slot])
        u = _dot(x, wu_buf[slot])
        o_ref[...] = _dot((_silu(a) * u).astype(BF16), wd_buf[slot])

    @pl.when(jnp.logical_not(valid))
    def _():
        o_ref[...] = jnp.zeros_like(o_ref)


def moe_experts(xs, w_gu, w_down, table):
    rows, d = xs.shape
    f = w_down.shape[1]
    nt = rows // MOE_TILE
    grid_spec = pltpu.PrefetchScalarGridSpec(
        num_scalar_prefetch=1,
        grid=(nt,),
        in_specs=[pl.BlockSpec((MOE_TILE, d), lambda t, tab: (t, 0)),
                  pl.BlockSpec(memory_space=pl.ANY),
                  pl.BlockSpec(memory_space=pl.ANY)],
        out_specs=pl.BlockSpec((MOE_TILE, d), lambda t, tab: (t, 0)),
        scratch_shapes=[pltpu.VMEM((2, d, f), BF16), pltpu.VMEM((2, d, f), BF16),
                        pltpu.VMEM((2, f, d), BF16), pltpu.SemaphoreType.DMA((2,))],
    )
    return pl.pallas_call(
        _expert_kernel,
        grid_spec=grid_spec,
        out_shape=jax.ShapeDtypeStruct((rows, d), F32),
        compiler_params=pltpu.CompilerParams(dimension_semantics=("arbitrary",),
                                             vmem_limit_bytes=EXPERT_VMEM_LIMIT),
        name="moe_experts",
    )(table, xs, w_gu, w_down)


def _combine_kernel(pos_ref, x_ref, meta_ref, g_ref, y_hbm, *refs, n, emit_x, emit_norm):
    ybuf, sem = refs[-2], refs[-1]
    tc = x_ref.shape[0]
    step = pl.program_id(0)
    slot = step % 2

    def start_gather(s, into):
        def issue(r, c):
            tok = s * tc + r
            for k in range(TOP_K):
                pltpu.make_async_copy(y_hbm.at[pl.ds(pos_ref[k * n + tok], 1), :],
                                      ybuf.at[into, k, pl.ds(r, 1), :], sem.at[into]).start()
            return c

        lax.fori_loop(0, tc, issue, 0, unroll=8)

    @pl.when(step == 0)
    def _():
        start_gather(0, 0)

    @pl.when(step + 1 < pl.num_programs(0))
    def _():
        start_gather(step + 1, 1 - slot)

    for k in range(TOP_K):
        pltpu.make_async_copy(y_hbm.at[pl.ds(0, tc), :], ybuf.at[slot, k], sem.at[slot]).wait()

    meta = meta_ref[...]
    g1 = meta[:, META_G1:META_G1 + 1]
    g2 = meta[:, META_G2:META_G2 + 1]
    xn = x_ref[...] + g1 * ybuf[slot, 0] + g2 * ybuf[slot, 1]
    _emit_residual_outputs(xn, g_ref, refs, emit_x, emit_norm)


def moe_combine(x, y, meta, pos, g, *, emit_x=True, norm_dtype=None, tc=256):
    n, d = x.shape
    tc = min(tc, n)
    out_shape, out_specs = _residual_out_shapes(n, d, tc, emit_x, norm_dtype, lambda i, p: (i, 0))
    grid_spec = pltpu.PrefetchScalarGridSpec(
        num_scalar_prefetch=1,
        grid=(n // tc,),
        in_specs=[pl.BlockSpec((tc, d), lambda i, p: (i, 0)),
                  pl.BlockSpec((tc, LANES), lambda i, p: (i, 0)),
                  pl.BlockSpec((1, d), lambda i, p: (0, 0)),
                  pl.BlockSpec(memory_space=pl.ANY)],
        out_specs=out_specs,
        scratch_shapes=[pltpu.VMEM((2, TOP_K, tc, d), F32), pltpu.SemaphoreType.DMA((2,))],
    )
    return pl.pallas_call(
        functools.partial(_combine_kernel, n=n, emit_x=emit_x, emit_norm=norm_dtype is not None),
        grid_spec=grid_spec,
        out_shape=out_shape,
        compiler_params=pltpu.CompilerParams(dimension_semantics=("arbitrary",)),
        name="moe_combine",
    )(pos, x, meta, g.reshape(1, d), y)


def kernel(x, ret_w_in, ret_norm_g, ret_w_out, pool_w, pool_scale, ffn_w_gu, ffn_w_down,
           moe_router, moe_w_gu, moe_w_down, norm_mix_g, norm_ffn_g, norm_final_g):
    batch, seq, d = x.shape
    n = batch * seq
    n_moe, n_exp, _, f2 = moe_w_gu.shape
    f = moe_w_down.shape[2]
    assert d == D_MODEL and n_exp == N_EXPERTS and seq % RET_CHUNK == 0 and seq >= POOL_HALO
    assert norm_mix_g.shape[0] == DEPTH and DEPTH % 2 == 0, "the trunk ends on a pooling / MoE layer"
    xf = x.reshape(n, d)
    pool_w = pool_w.astype(BF16)
    moe_gu_rows = moe_w_gu.reshape(n_moe, n_exp * d, f2)
    moe_down_rows = moe_w_down.reshape(n_moe, n_exp * f, d)
    hn = rmsnorm(xf, norm_mix_g[0], BF16)
    for i in range(DEPTH):
        j = i // 2
        last = i == DEPTH - 1
        if i % 2 == 0:
            proj, moe_gu_b, w_out_b = retention_in_proj(hn, ret_w_in, j,
                                                        [(moe_gu_rows, j), (ret_w_out, j)])
            y = retention(proj, batch, seq)
            gate_block = (2 * RET_QK + RET_V) // RET_V
            xf, xg, ssq = down_residual(y, w_out_b, xf, norm_ffn_g[i],
                                        head_gate=(proj, gate_block, ret_norm_g[j]))
            hid, moe_down_b, ffn_down_b = swiglu_up(xg, ssq, ffn_w_gu, j,
                                                    [(moe_down_rows, j), (ffn_w_down, j)])
            (xf,) = down_residual(hid, ffn_down_b, xf)
            moe_gu_b = moe_gu_b.reshape(n_exp, d, f2)
            moe_down_b = moe_down_b.reshape(n_exp, f, d)
        else:
            xf, hn, meta, meta_t, counts = pool_router(xf, norm_mix_g[i], pool_w, j, pool_scale[j],
                                                       norm_ffn_g[i], moe_router[j], seq)
            pos, tile_table, pad_lo, pad_hi, used = moe_schedule(meta_t, counts, n)
            xs = moe_dispatch(hn, pos, pad_lo, pad_hi, used)
            ys = moe_experts(xs, moe_gu_b, moe_down_b, tile_table)
            if last:
                (out,) = moe_combine(xf, ys, meta, pos, norm_final_g, emit_x=False, norm_dtype=F32)
            else:
                xf, hn = moe_combine(xf, ys, meta, pos, norm_mix_g[i + 1], norm_dtype=BF16)
    return out.reshape(batch, seq, d)
```

```python
import functools

import jax
import jax.numpy as jnp
from jax import lax
from jax.experimental import pallas as pl
from jax.experimental.pallas import tpu as pltpu

BF16 = jnp.bfloat16
F32 = jnp.float32

D_MODEL = 2048
DEPTH = 4
RET_HEADS = 8
RET_DK = D_MODEL // RET_HEADS
RET_DV = 2 * D_MODEL // RET_HEADS
RET_QK = RET_HEADS * RET_DK
RET_V = RET_HEADS * RET_DV
RET_CHUNK = 256
RET_PREP_ROWS = 32
ROPE_BASE = 10000.0
POOL_WINDOWS = (2, 4, 8, 16)
POOL_G = D_MODEL // len(POOL_WINDOWS)
POOL_HALO = 16
N_EXPERTS = 8
TOP_K = 2
NORM_EPS = 1e-6
LANES = 128
SUBLANES = 8
BF16_TILE_ROWS = 16
MOE_TILE = 256
SIDE_CAST_BLOCKS = 64
X_RING = 3
DOWN_VMEM_LIMIT = 60 * 1024 * 1024
EXPERT_VMEM_LIMIT = 56 * 1024 * 1024
META_E1, META_E2, META_R1, META_R2, META_G1, META_G2 = range(6)
TILE_EXPERT, TILE_VALID, TILE_FIRST, TILE_SLOT, TILE_NEXT = range(5)


def _rms(x, g):
    ms = jnp.mean(x * x, axis=-1, keepdims=True)
    return x * lax.rsqrt(ms + NORM_EPS) * g


def _silu(a):
    return a / (1.0 + jnp.exp(-a))


def _dot(a, b):
    return jnp.dot(a, b, preferred_element_type=F32)


def _rmsnorm_kernel(x_ref, g_ref, o_ref):
    o_ref[...] = _rms(x_ref[...], g_ref[...]).astype(o_ref.dtype)


def rmsnorm(x, g, out_dtype, tm=512):
    n, d = x.shape
    return pl.pallas_call(
        _rmsnorm_kernel,
        grid=(n // tm,),
        in_specs=[pl.BlockSpec((tm, d), lambda i: (i, 0)),
                  pl.BlockSpec((1, d), lambda i: (0, 0))],
        out_specs=pl.BlockSpec((tm, d), lambda i: (i, 0)),
        out_shape=jax.ShapeDtypeStruct((n, d), out_dtype),
        name="rmsnorm",
    )(x, g.reshape(1, d))


def _side_cast_specs(sides, steps, n_inner):
    in_specs, out_specs, shapes = [], [], []
    for side, side_layer in sides:
        _, rows, cols = side.shape
        n_blocks = 1
        while (n_blocks * 2 <= min(steps, SIDE_CAST_BLOCKS)
               and rows % (n_blocks * 2 * BF16_TILE_ROWS) == 0):
            n_blocks *= 2
        br = rows // n_blocks

        def index(j, i, n_blocks=n_blocks):
            return jnp.minimum(j * n_inner + i, n_blocks - 1)

        in_specs.append(pl.BlockSpec((None, br, cols),
                                     lambda j, i, l=side_layer, ix=index: (l, ix(j, i), 0)))
        out_specs.append(pl.BlockSpec((br, cols), lambda j, i, ix=index: (ix(j, i), 0)))
        shapes.append(jax.ShapeDtypeStruct((rows, cols), BF16))
    return in_specs, out_specs, shapes


def _cast_sides(refs, n_sides):
    for s_ref, so_ref in zip(refs[:n_sides], refs[n_sides + 1:2 * n_sides + 1]):
        so_ref[...] = s_ref[...].astype(BF16)


class _RowTileRing:
    def __init__(self, x_hbm, buf, sems, n_row_tiles):
        self.x_hbm, self.buf, self.sems, self.n_row_tiles = x_hbm, buf, sems, n_row_tiles

    def _copy(self, s):
        tm = self.buf.shape[1]
        rows = pl.ds(pl.multiple_of((s % self.n_row_tiles) * tm, tm), tm)
        return pltpu.make_async_copy(self.x_hbm.at[rows, :], self.buf.at[s % X_RING],
                                     self.sems.at[s % X_RING])

    def advance(self, step, total):
        @pl.when(step == 0)
        def _():
            for s in range(X_RING - 1):
                self._copy(s).start()

        @pl.when(step + X_RING - 1 < total)
        def _():
            self._copy(step + X_RING - 1).start()

    def tile(self, step):
        self._copy(step).wait()
        return self.buf[step % X_RING]


def _ring_scratch(tm, k):
    return [pltpu.VMEM((X_RING, tm, k), BF16), pltpu.SemaphoreType.DMA((X_RING,))]


def _in_proj_kernel(x_hbm, w_ref, perm_ref, *refs, n_qk_slabs, n_sides):
    o_ref = refs[n_sides]
    wb_ref, xbuf, xsem = refs[-3:]
    ni = pl.num_programs(1)
    step = pl.program_id(0) * ni + pl.program_id(1)
    ring = _RowTileRing(x_hbm, xbuf, xsem, ni)
    ring.advance(step, pl.num_programs(0) * ni)

    @pl.when(pl.program_id(1) == 0)
    def _():
        is_qk = pl.program_id(0) < n_qk_slabs

        @pl.when(is_qk)
        def _():
            for lo in range(0, wb_ref.shape[1], RET_DK):
                head = w_ref[:, lo:lo + RET_DK].astype(BF16)
                wb_ref[:, lo:lo + RET_DK] = _dot(head, perm_ref[...]).astype(BF16)

        @pl.when(jnp.logical_not(is_qk))
        def _():
            wb_ref[...] = w_ref[...].astype(BF16)

    o_ref[...] = _dot(ring.tile(step), wb_ref[...]).astype(o_ref.dtype)
    _cast_sides(refs, n_sides)


def retention_in_proj(x, w, layer, sides, tm=1024, tn=1024):
    n, k = x.shape
    f = w.shape[2]
    tm = min(tm, n)
    nj, ni = f // tn, n // tm
    assert f % tn == 0 and n % tm == 0 and nj * ni >= X_RING - 1
    half = RET_DK // 2
    src = jnp.arange(RET_DK)
    dst = jnp.where(src % 2 == 0, src // 2, half + src // 2)
    perm = (dst[:, None] == jnp.arange(RET_DK)[None, :]).astype(BF16)
    s_in, s_out, s_shapes = _side_cast_specs(sides, nj * ni, ni)
    return pl.pallas_call(
        functools.partial(_in_proj_kernel, n_qk_slabs=2 * RET_QK // tn, n_sides=len(sides)),
        grid=(nj, ni),
        in_specs=[pl.BlockSpec(memory_space=pl.ANY),
                  pl.BlockSpec((None, k, tn), lambda j, i: (layer, 0, j)),
                  pl.BlockSpec((RET_DK, RET_DK), lambda j, i: (0, 0)),
                  *s_in],
        out_specs=[pl.BlockSpec((tm, tn), lambda j, i: (i, j)), *s_out],
        out_shape=[jax.ShapeDtypeStruct((n, f), BF16), *s_shapes],
        scratch_shapes=[pltpu.VMEM((k, tn), BF16), *_ring_scratch(tm, k)],
        compiler_params=pltpu.CompilerParams(dimension_semantics=("arbitrary", "arbitrary")),
        name="retention_in_proj",
    )(x, w, perm, *(side for side, _ in sides))


def _swiglu_kernel(x_hbm, ssq_ref, wa_ref, wu_ref, *refs, n_sides):
    o_ref = refs[n_sides]
    wab_ref, wub_ref, xbuf, xsem = refs[-4:]
    ni = pl.num_programs(1)
    step = pl.program_id(0) * ni + pl.program_id(1)
    ring = _RowTileRing(x_hbm, xbuf, xsem, ni)
    ring.advance(step, pl.num_programs(0) * ni)

    @pl.when(pl.program_id(1) == 0)
    def _():
        wab_ref[...] = wa_ref[...].astype(BF16)
        wub_ref[...] = wu_ref[...].astype(BF16)

    x = ring.tile(step)
    rowscale = lax.rsqrt(jnp.sum(ssq_ref[...], axis=0) / x.shape[1] + NORM_EPS)
    a = _dot(x, wab_ref[...]) * rowscale
    u = _dot(x, wub_ref[...]) * rowscale
    o_ref[...] = (_silu(a) * u).astype(o_ref.dtype)
    _cast_sides(refs, n_sides)


def swiglu_up(xg, ssq, w_gu, layer, sides, tm=1024, tn=512):
    n, k = xg.shape
    f = w_gu.shape[2] // 2
    tm = min(tm, n)
    nj, ni = f // tn, n // tm
    assert f % tn == 0 and n % tm == 0 and nj * ni >= X_RING - 1
    ns = ssq.shape[0]
    s_in, s_out, s_shapes = _side_cast_specs(sides, nj * ni, ni)
    return pl.pallas_call(
        functools.partial(_swiglu_kernel, n_sides=len(sides)),
        grid=(nj, ni),
        in_specs=[pl.BlockSpec(memory_space=pl.ANY),
                  pl.BlockSpec((ns, tm, 1), lambda j, i: (0, i, 0)),
                  pl.BlockSpec((None, k, tn), lambda j, i: (layer, 0, j)),
                  pl.BlockSpec((None, k, tn), lambda j, i: (layer, 0, nj + j)),
                  *s_in],
        out_specs=[pl.BlockSpec((tm, tn), lambda j, i: (i, j)), *s_out],
        out_shape=[jax.ShapeDtypeStruct((n, f), BF16), *s_shapes],
        scratch_shapes=[pltpu.VMEM((k, tn), BF16), pltpu.VMEM((k, tn), BF16),
                        *_ring_scratch(tm, k)],
        compiler_params=pltpu.CompilerParams(dimension_semantics=("arbitrary", "arbitrary")),
        name="swiglu_up",
    )(xg, ssq, w_gu, w_gu, *(side for side, _ in sides))


def _emit_residual_outputs(xn, g_ref, refs, emit_x, emit_norm):
    pos = 0
    if emit_x:
        refs[pos][...] = xn
        pos += 1
    if emit_norm:
        refs[pos][...] = _rms(xn, g_ref[...]).astype(refs[pos].dtype)


def _residual_out_shapes(n, d, tm, emit_x, norm_dtype, index_map):
    out_shape, out_specs = [], []
    if emit_x:
        out_shape.append(jax.ShapeDtypeStruct((n, d), F32))
        out_specs.append(pl.BlockSpec((tm, d), index_map))
    if norm_dtype is not None:
        out_shape.append(jax.ShapeDtypeStruct((n, d), norm_dtype))
        out_specs.append(pl.BlockSpec((tm, d), index_map))
    return out_shape, out_specs


def _down_kernel(x_ref, w_ref, r_ref, g_ref, *refs, emit_scaled, head_gated):
    if head_gated:
        gate_ref, xs_ref = refs[0], refs[-1]
        refs = refs[1:-1]
        for lo in range(0, x_ref.shape[1], RET_DV):
            gate = gate_ref[:, lo:lo + RET_DV].astype(F32)
            xs_ref[:, lo:lo + RET_DV] = (x_ref[:, lo:lo + RET_DV].astype(F32)
                                         * _silu(gate)).astype(BF16)
        x = xs_ref[...]
    else:
        x = x_ref[...]
    xn = r_ref[...] + _dot(x, w_ref[...])
    refs[0][...] = xn
    if emit_scaled:
        refs[1][...] = (xn * g_ref[...]).astype(BF16)
        refs[2][...] = jnp.sum(xn * xn, axis=-1, keepdims=True)[None]


def down_residual(h, w, resid, g=None, *, head_gate=None, tm=512):
    n, kk = h.shape
    d = w.shape[1]
    tm = min(tm, n)
    emit_scaled = g is not None
    if g is None:
        g = jnp.ones((d,), F32)
    extra_in, extra_specs, extra_scratch = [], [], []
    if head_gate is not None:
        proj, gate_block = head_gate
        extra_in = [proj]
        extra_specs = [pl.BlockSpec((tm, kk), lambda i: (i, gate_block))]
        extra_scratch = [pltpu.VMEM((tm, kk), BF16)]
    out_shape = [jax.ShapeDtypeStruct((n, d), F32)]
    out_specs = [pl.BlockSpec((tm, d), lambda i: (i, 0))]
    if emit_scaled:
        out_shape += [jax.ShapeDtypeStruct((n, d), BF16), jax.ShapeDtypeStruct((1, n, 1), F32)]
        out_specs += [pl.BlockSpec((tm, d), lambda i: (i, 0)),
                      pl.BlockSpec((1, tm, 1), lambda i: (0, i, 0))]
    return pl.pallas_call(
        functools.partial(_down_kernel, emit_scaled=emit_scaled, head_gated=head_gate is not None),
        grid=(n // tm,),
        in_specs=[pl.BlockSpec((tm, kk), lambda i: (i, 0)),
                  pl.BlockSpec((kk, d), lambda i: (0, 0), pipeline_mode=pl.Buffered(1)),
                  pl.BlockSpec((tm, d), lambda i: (i, 0)),
                  pl.BlockSpec((1, d), lambda i: (0, 0)),
                  *extra_specs],
        out_specs=out_specs,
        out_shape=out_shape,
        scratch_shapes=extra_scratch,
        compiler_params=pltpu.CompilerParams(dimension_semantics=("arbitrary",),
                                             vmem_limit_bytes=DOWN_VMEM_LIMIT),
        name="down_residual",
    )(h, w, resid, g.reshape(1, d), *extra_in)


def _retention_kernel(lg_ref, q_ref, k_ref, v_ref, cos_ref, sin_ref, gain_ref, o_ref,
                      qr_ref, qd_ref, kr_ref, kd_ref, mask_ref, qdec_ref, kdec_ref,
                      state_ref, sb_ref, *, n_chunks, n_pairs):
    c = RET_CHUNK
    step = pl.program_id(0)
    slot_prep = step % 2
    slot_recur = 1 - slot_prep

    @pl.when(step == 0)
    def _():
        for ref in (qr_ref, qd_ref, kr_ref, kd_ref):
            ref[...] = jnp.zeros_like(ref)

    lg_prep = lg_ref[jnp.minimum(step, n_pairs - 1) % RET_HEADS]
    lg = lg_ref[jnp.clip(step - 1, 0, n_pairs - 1) % RET_HEADS]
    ii = lax.broadcasted_iota(jnp.int32, (c, c), 0)
    jj = lax.broadcasted_iota(jnp.int32, (c, c), 1)
    rel = (ii - jj).astype(F32)
    mask_ref[...] = jnp.where(rel >= 0.0, jnp.exp(jnp.maximum(rel, 0.0) * lg), 0.0)
    idx = lax.broadcasted_iota(jnp.int32, (c, 1), 0).astype(F32)
    k_scale = RET_DK ** -0.5
    qdec_ref[...] = jnp.broadcast_to(jnp.exp((idx + 1.0) * lg_prep), (c, RET_DK))
    kdec_ref[...] = jnp.broadcast_to(jnp.exp((c - 1.0 - idx) * lg_prep) * k_scale, (c, RET_DK))
    chunk_decay = jnp.exp(jnp.full((1, 1), float(c), F32) * lg)
    half = RET_DK // 2

    def rotary(xb, cos, sin):
        x = xb.astype(F32)
        x1, x2 = x[:, :half], x[:, half:]
        return jnp.concatenate([x1 * cos - x2 * sin, x2 * cos + x1 * sin], axis=1)

    def rows(ci):
        return pl.ds(pl.multiple_of(ci * c, c), c)

    state_ref[...] = jnp.zeros_like(state_ref)
    sb_ref[...] = jnp.zeros_like(sb_ref)

    def body(ci, carry):
        r = rows(ci)

        v = v_ref[r, :]
        scores = lax.dot_general(qr_ref[slot_recur, r, :], kr_ref[slot_recur, r, :],
                                 (((1,), (1,)), ((), ())),
                                 preferred_element_type=F32) * mask_ref[...]
        o = _dot(scores.astype(BF16), v) + _dot(qd_ref[slot_recur, r, :], sb_ref[...])
        ms = jnp.mean(o * o, axis=-1, keepdims=True)
        o_ref[r, :] = (o * lax.rsqrt(ms + NORM_EPS) * gain_ref[...]).astype(o_ref.dtype)
        upd = lax.dot_general(kd_ref[slot_recur, r, :], v, (((0,), (0,)), ((), ())),
                              preferred_element_type=F32)
        state = state_ref[...] * chunk_decay + upd
        state_ref[...] = state
        sb_ref[...] = state.astype(BF16)

        for lo in range(0, c, RET_PREP_ROWS):
            rr = pl.ds(pl.multiple_of(ci * c + lo, RET_PREP_ROWS), RET_PREP_ROWS)
            dec = pl.ds(lo, RET_PREP_ROWS)
            cos = cos_ref[rr, :]
            sin = sin_ref[rr, :]
            q = rotary(q_ref[rr, :], cos, sin)
            k = rotary(k_ref[rr, :], cos, sin)
            qr_ref[slot_prep, rr, :] = q.astype(BF16)
            qd_ref[slot_prep, rr, :] = (q * qdec_ref[dec, :]).astype(BF16)
            kr_ref[slot_prep, rr, :] = (k * k_scale).astype(BF16)
            kd_ref[slot_prep, rr, :] = (k * kdec_ref[dec, :]).astype(BF16)
        return carry

    lax.fori_loop(0, n_chunks, body, 0, unroll=2)


def retention(proj, norm_g, layer, batch, seq):
    n = batch * seq
    h = RET_HEADS
    log_g = jnp.log1p(-jnp.exp2(-5.0 - jnp.arange(h, dtype=F32)))
    freq = 1.0 / (ROPE_BASE ** jnp.linspace(0.0, 1.0, RET_DK // 2, dtype=F32))
    ang = jnp.arange(seq).astype(F32)[:, None] * freq[None, :]
    cos = jnp.cos(ang)
    sin = jnp.sin(ang)
    kq = RET_QK // RET_DK
    kv = 2 * RET_QK // RET_DV
    n_pairs = batch * h
    stages = 2

    def pair(step, stage):
        p = jnp.clip(step - stage, 0, n_pairs - 1)
        return p // h, p % h

    def block(stage, col0):
        def index(s):
            b, hh = pair(s, stage)
            return b, col0 + hh
        return index

    return pl.pallas_call(
        functools.partial(_retention_kernel, n_chunks=seq // RET_CHUNK, n_pairs=n_pairs),
        grid=(n_pairs + stages - 1,),
        in_specs=[pl.BlockSpec(memory_space=pltpu.SMEM),
                  pl.BlockSpec((seq, RET_DK), block(0, 0)),
                  pl.BlockSpec((seq, RET_DK), block(0, kq)),
                  pl.BlockSpec((seq, RET_DV), block(1, kv)),
                  pl.BlockSpec((seq, RET_DK // 2), lambda s: (0, 0)),
                  pl.BlockSpec((seq, RET_DK // 2), lambda s: (0, 0)),
                  pl.BlockSpec((None, 1, RET_DV), lambda s: (layer, 0, pair(s, 1)[1]))],
        out_specs=pl.BlockSpec((seq, RET_DV), block(1, 0)),
        out_shape=jax.ShapeDtypeStruct((n, RET_V), BF16),
        scratch_shapes=[pltpu.VMEM((2, seq, RET_DK), BF16),
                        pltpu.VMEM((2, seq, RET_DK), BF16),
                        pltpu.VMEM((2, seq, RET_DK), BF16),
                        pltpu.VMEM((2, seq, RET_DK), BF16),
                        pltpu.VMEM((RET_CHUNK, RET_CHUNK), F32),
                        pltpu.VMEM((RET_CHUNK, RET_DK), F32),
                        pltpu.VMEM((RET_CHUNK, RET_DK), F32),
                        pltpu.VMEM((RET_DK, RET_DV), F32),
                        pltpu.VMEM((RET_DK, RET_DV), BF16)],
        compiler_params=pltpu.CompilerParams(dimension_semantics=("arbitrary",)),
        name="retention",
    )(log_g, proj, proj, proj, cos, sin, norm_g.reshape(norm_g.shape[0], 1, RET_V))


def _pool_kernel(x_ref, gm_ref, pw_ref, sc_ref, gf_ref, rt_ref,
                 xo_ref, hn_ref, meta_ref, meta_t_ref, cnt_out_ref, hbuf, cnt_ref, *, tiles_per_seq):
    tm = x_ref.shape[0]
    step = pl.program_id(0)
    ti = step % tiles_per_seq
    x = x_ref[...]
    hn = _rms(x, gm_ref[...])

    @pl.when(ti == 0)
    def _():
        hbuf[0:POOL_HALO, :] = jnp.zeros((POOL_HALO, D_MODEL), F32)

    @pl.when(ti != 0)
    def _():
        hbuf[0:POOL_HALO, :] = hbuf[tm:tm + POOL_HALO, :]

    hbuf[POOL_HALO:POOL_HALO + tm, :] = hn

    t = ti * tm + lax.broadcasted_iota(jnp.int32, (tm, 1), 0)
    for gi, w in enumerate(POOL_WINDOWS):
        lo, hi = gi * POOL_G, (gi + 1) * POOL_G
        cur = hn[:, lo:hi]
        acc = hbuf[:, lo:hi]
        shift = 1
        while shift < w:
            acc = acc + pltpu.roll(acc, shift, 0)
            shift *= 2
        acc = acc[POOL_HALO:, :]
        inv = 1.0 / jnp.minimum(t + 1, w).astype(F32)
        mix = (acc * inv - cur).astype(BF16)
        o = _dot(mix, pw_ref[gi]) * sc_ref[:, lo:hi]
        xo_ref[:, lo:hi] = x[:, lo:hi] + o

    hn2 = _rms(xo_ref[...], gf_ref[...])
    hn_ref[...] = hn2

    rt = rt_ref[...]
    hn_hi = hn2.astype(BF16)
    hn_lo = (hn2 - hn_hi.astype(F32)).astype(BF16)
    rt_hi = rt.astype(BF16)
    rt_lo = (rt - rt_hi.astype(F32)).astype(BF16)
    hi_both = _dot(hn_hi, jnp.concatenate([rt_hi, rt_lo], axis=1))
    logits = hi_both[:, :LANES] + (hi_both[:, LANES:] + _dot(hn_lo, rt_hi))
    lt = logits.T[:N_EXPERTS, :]
    expert = lax.broadcasted_iota(jnp.int32, lt.shape, 0)
    neg = jnp.float32(-jnp.inf)
    m1 = jnp.max(lt, axis=0, keepdims=True)
    i1 = jnp.min(jnp.where(lt == m1, expert, N_EXPERTS), axis=0, keepdims=True)
    lt2 = jnp.where(expert == i1, neg, lt)
    m2 = jnp.max(lt2, axis=0, keepdims=True)
    i2 = jnp.min(jnp.where(lt2 == m2, expert, N_EXPERTS), axis=0, keepdims=True)
    e2 = jnp.exp(m2 - m1)
    g1 = 1.0 / (1.0 + e2)
    g2 = e2 * g1

    @pl.when(step == 0)
    def _():
        cnt_ref[...] = jnp.zeros_like(cnt_ref)

    chosen = jnp.where((expert == i1) | (expert == i2), 1.0, 0.0)
    src = lax.broadcasted_iota(jnp.int32, (tm, tm), 0)
    dst = lax.broadcasted_iota(jnp.int32, (tm, tm), 1)
    earlier = jnp.where(src < dst, 1.0, 0.0).astype(BF16)
    before = _dot(chosen.astype(BF16), earlier) + cnt_ref[:, 0:1]
    r1 = jnp.sum(jnp.where(expert == i1, before, 0.0), axis=0, keepdims=True)
    r2 = jnp.sum(jnp.where(expert == i2, before, 0.0), axis=0, keepdims=True)
    cnt_ref[...] += jnp.sum(chosen, axis=1, keepdims=True)
    cnt_out_ref[...] = cnt_ref[...]

    fields = {META_E1: i1.astype(F32), META_E2: i2.astype(F32), META_R1: r1, META_R2: r2,
              META_G1: g1, META_G2: g2}
    zero_row = jnp.zeros_like(g1)
    meta_t = jnp.concatenate([fields.get(row, zero_row) for row in range(SUBLANES)], axis=0)
    meta_t_ref[...] = meta_t
    meta_ref[...] = jnp.concatenate([meta_t, jnp.zeros((LANES - SUBLANES, tm), F32)], axis=0).T


def pool_router(x, g_mix, pool_w, layer, pool_scale, g_ffn, router, seq, tm=512):
    n, d = x.shape
    tm = min(tm, seq)
    router_p = jnp.zeros((d, LANES), F32).at[:, :N_EXPERTS].set(router)
    ng = len(POOL_WINDOWS)
    return pl.pallas_call(
        functools.partial(_pool_kernel, tiles_per_seq=seq // tm),
        grid=(n // tm,),
        in_specs=[pl.BlockSpec((tm, d), lambda i: (i, 0)),
                  pl.BlockSpec((1, d), lambda i: (0, 0)),
                  pl.BlockSpec((None, ng, POOL_G, POOL_G), lambda i: (layer, 0, 0, 0)),
                  pl.BlockSpec((1, d), lambda i: (0, 0)),
                  pl.BlockSpec((1, d), lambda i: (0, 0)),
                  pl.BlockSpec((d, LANES), lambda i: (0, 0))],
        out_specs=[pl.BlockSpec((tm, d), lambda i: (i, 0)),
                   pl.BlockSpec((tm, d), lambda i: (i, 0)),
                   pl.BlockSpec((tm, LANES), lambda i: (i, 0)),
                   pl.BlockSpec((SUBLANES, tm), lambda i: (0, i)),
                   pl.BlockSpec((SUBLANES, LANES), lambda i: (0, 0))],
        out_shape=[jax.ShapeDtypeStruct((n, d), F32),
                   jax.ShapeDtypeStruct((n, d), F32),
                   jax.ShapeDtypeStruct((n, LANES), F32),
                   jax.ShapeDtypeStruct((SUBLANES, n), F32),
                   jax.ShapeDtypeStruct((SUBLANES, LANES), F32)],
        scratch_shapes=[pltpu.VMEM((POOL_HALO + tm, d), F32),
                        pltpu.VMEM((N_EXPERTS, LANES), F32)],
        compiler_params=pltpu.CompilerParams(dimension_semantics=("arbitrary",)),
        name="pool_router",
    )(x, g_mix.reshape(1, d), pool_w, pool_scale.reshape(1, d), g_ffn.reshape(1, d), router_p)


def moe_num_tiles(n):
    return TOP_K * n // MOE_TILE + N_EXPERTS


def moe_schedule(meta_t, counts_f, n):
    nt = moe_num_tiles(n)
    counts = counts_f[:N_EXPERTS, 0].astype(jnp.int32)
    tiles = (counts + MOE_TILE - 1) // MOE_TILE
    tile_end = jnp.cumsum(tiles)
    starts = (tile_end - tiles) * MOE_TILE
    expert_ids = jnp.arange(N_EXPERTS, dtype=jnp.int32)[:, None]

    def slots(e_row, r_row):
        e = meta_t[e_row].astype(jnp.int32)
        first = jnp.sum(jnp.where(e[None, :] == expert_ids, starts[:, None], 0), axis=0)
        return first + meta_t[r_row].astype(jnp.int32)

    pos = jnp.concatenate([slots(META_E1, META_R1), slots(META_E2, META_R2)])
    total = tile_end[-1]
    t = jnp.arange(nt, dtype=jnp.int32)
    t_used = jnp.minimum(t, total - 1)
    tile_expert = jnp.sum((tile_end[None, :] <= t_used[:, None]).astype(jnp.int32), axis=1)
    tile_expert = jnp.minimum(tile_expert, N_EXPERTS - 1)
    tile_valid = (t < total).astype(jnp.int32)
    pad_lo = starts + counts
    pad_hi = starts + tiles * MOE_TILE
    prev_expert = jnp.concatenate([jnp.full((1,), -1, jnp.int32), tile_expert[:-1]])
    group_first = (tile_expert != prev_expert).astype(jnp.int32)
    group_slot = (jnp.cumsum(group_first) - 1) % 2
    later = (expert_ids.T > expert_ids) & (tiles[None, :] > 0)
    next_used = jnp.min(jnp.where(later, expert_ids.T, N_EXPERTS), axis=1)
    next_used = jnp.where(next_used == N_EXPERTS, -1, next_used)
    next_expert = jnp.sum(jnp.where(tile_expert[None, :] == expert_ids, next_used[:, None], 0), axis=0)
    table = jnp.concatenate([tile_expert, tile_valid, group_first, group_slot, next_expert])
    return pos, table.astype(jnp.int32), pad_lo, pad_hi, total.reshape(1)


def _dispatch_kernel(pos_ref, lo_ref, hi_ref, used_ref, hn_ref, xs_hbm, zeros, sem, *, n, n_tiles):
    tb = hn_ref.shape[0]
    step = pl.program_id(0)
    base = step * tb

    def issue(r, c):
        tok = base + r
        src = hn_ref.at[pl.ds(r, 1), :]
        pltpu.make_async_copy(src, xs_hbm.at[pl.ds(pos_ref[tok], 1), :], sem).start()
        pltpu.make_async_copy(src, xs_hbm.at[pl.ds(pos_ref[n + tok], 1), :], sem).start()
        return c

    lax.fori_loop(0, tb, issue, 0, unroll=8)
    for _ in range(TOP_K):
        pltpu.make_async_copy(hn_ref, xs_hbm.at[pl.ds(0, tb), :], sem).wait()

    @pl.when(step == pl.num_programs(0) - 1)
    def _():
        zeros[...] = jnp.zeros_like(zeros)

        def pad_copy(slot):
            return pltpu.make_async_copy(zeros.at[pl.ds(0, 1), :], xs_hbm.at[pl.ds(slot, 1), :], sem)

        def issue_pad(s, c):
            pad_copy(s).start()
            return c

        def drain_pad(s, c):
            pad_copy(s).wait()
            return c

        for e in range(N_EXPERTS):
            lax.fori_loop(lo_ref[e], hi_ref[e], issue_pad, 0)
            lax.fori_loop(lo_ref[e], hi_ref[e], drain_pad, 0)

        def tile_copy(t):
            return pltpu.make_async_copy(zeros, xs_hbm.at[pl.ds(t * MOE_TILE, MOE_TILE), :], sem)

        def issue_tile(t, c):
            tile_copy(t).start()
            return c

        def drain_tile(t, c):
            tile_copy(t).wait()
            return c

        lax.fori_loop(used_ref[0], n_tiles, issue_tile, 0)
        lax.fori_loop(used_ref[0], n_tiles, drain_tile, 0)


def moe_dispatch(hn, pos, pad_lo, pad_hi, used, tb=1024):
    n, d = hn.shape
    tb = min(tb, n)
    nt = moe_num_tiles(n)
    grid_spec = pltpu.PrefetchScalarGridSpec(
        num_scalar_prefetch=4,
        grid=(n // tb,),
        in_specs=[pl.BlockSpec((tb, d), lambda i, *_: (i, 0))],
        out_specs=pl.BlockSpec(memory_space=pl.ANY),
        scratch_shapes=[pltpu.VMEM((MOE_TILE, d), F32), pltpu.SemaphoreType.DMA],
    )
    return pl.pallas_call(
        functools.partial(_dispatch_kernel, n=n, n_tiles=nt),
        grid_spec=grid_spec,
        out_shape=jax.ShapeDtypeStruct((nt * MOE_TILE, d), F32),
        compiler_params=pltpu.CompilerParams(dimension_semantics=("arbitrary",)),
        name="moe_dispatch",
    )(pos, pad_lo, pad_hi, used, hn)


def _expert_kernel(tab_ref, x_ref, wgu_hbm, wd_hbm, o_ref, wa_buf, wu_buf, wd_buf, sems):
    t = pl.program_id(0)
    nt = pl.num_programs(0)
    f = wd_buf.shape[1]

    def field(k, tile):
        return tab_ref[k * nt + tile]

    def weight_copies(e, slot):
        return (pltpu.make_async_copy(wgu_hbm.at[e, :, pl.ds(0, f)], wa_buf.at[slot], sems.at[slot]),
                pltpu.make_async_copy(wgu_hbm.at[e, :, pl.ds(f, f)], wu_buf.at[slot], sems.at[slot]),
                pltpu.make_async_copy(wd_hbm.at[e], wd_buf.at[slot], sems.at[slot]))

    slot = field(TILE_SLOT, t)

    @pl.when(t == 0)
    def _():
        for cp in weight_copies(field(TILE_EXPERT, 0), field(TILE_SLOT, 0)):
            cp.start()

    @pl.when(field(TILE_FIRST, t) == 1)
    def _():
        for cp in weight_copies(field(TILE_EXPERT, t), slot):
            cp.wait()
        nxt = field(TILE_NEXT, t)

        @pl.when(nxt >= 0)
        def _():
            for cp in weight_copies(nxt, 1 - slot):
                cp.start()

    valid = field(TILE_VALID, t) == 1

    @pl.when(valid)
    def _():
        x = x_ref[...].astype(BF16)
        a = _dot(x, wa_buf[slot])
        u = _dot(x, wu_buf[slot])
        o_ref[...] = _dot((_silu(a) * u).astype(BF16), wd_buf[slot])

    @pl.when(jnp.logical_not(valid))
    def _():
        o_ref[...] = jnp.zeros_like(o_ref)


def moe_experts(xs, w_gu, w_down, table):
    rows, d = xs.shape
    f = w_down.shape[1]
    nt = rows // MOE_TILE
    grid_spec = pltpu.PrefetchScalarGridSpec(
        num_scalar_prefetch=1,
        grid=(nt,),
        in_specs=[pl.BlockSpec((MOE_TILE, d), lambda t, tab: (t, 0)),
                  pl.BlockSpec(memory_space=pl.ANY),
                  pl.BlockSpec(memory_space=pl.ANY)],
        out_specs=pl.BlockSpec((MOE_TILE, d), lambda t, tab: (t, 0)),
        scratch_shapes=[pltpu.VMEM((2, d, f), BF16), pltpu.VMEM((2, d, f), BF16),
                        pltpu.VMEM((2, f, d), BF16), pltpu.SemaphoreType.DMA((2,))],
    )
    return pl.pallas_call(
        _expert_kernel,
        grid_spec=grid_spec,
        out_shape=jax.ShapeDtypeStruct((rows, d), F32),
        compiler_params=pltpu.CompilerParams(dimension_semantics=("arbitrary",),
                                             vmem_limit_bytes=EXPERT_VMEM_LIMIT),
        name="moe_experts",
    )(table, xs, w_gu, w_down)


def _combine_kernel(pos_ref, x_ref, meta_ref, g_ref, y_hbm, *refs, n, emit_x, emit_norm):
    ybuf, sem = refs[-2], refs[-1]
    tc = x_ref.shape[0]
    step = pl.program_id(0)
    slot = step % 2

    def start_gather(s, into):
        def issue(r, c):
            tok = s * tc + r
            for k in range(TOP_K):
                pltpu.make_async_copy(y_hbm.at[pl.ds(pos_ref[k * n + tok], 1), :],
                                      ybuf.at[into, k, pl.ds(r, 1), :], sem.at[into]).start()
            return c

        lax.fori_loop(0, tc, issue, 0, unroll=8)

    @pl.when(step == 0)
    def _():
        start_gather(0, 0)

    @pl.when(step + 1 < pl.num_programs(0))
    def _():
        start_gather(step + 1, 1 - slot)

    for k in range(TOP_K):
        pltpu.make_async_copy(y_hbm.at[pl.ds(0, tc), :], ybuf.at[slot, k], sem.at[slot]).wait()

    meta = meta_ref[...]
    g1 = meta[:, META_G1:META_G1 + 1]
    g2 = meta[:, META_G2:META_G2 + 1]
    xn = x_ref[...] + g1 * ybuf[slot, 0] + g2 * ybuf[slot, 1]
    _emit_residual_outputs(xn, g_ref, refs, emit_x, emit_norm)


def moe_combine(x, y, meta, pos, g, *, emit_x=True, norm_dtype=None, tc=256):
    n, d = x.shape
    tc = min(tc, n)
    out_shape, out_specs = _residual_out_shapes(n, d, tc, emit_x, norm_dtype, lambda i, p: (i, 0))
    grid_spec = pltpu.PrefetchScalarGridSpec(
        num_scalar_prefetch=1,
        grid=(n // tc,),
        in_specs=[pl.BlockSpec((tc, d), lambda i, p: (i, 0)),
                  pl.BlockSpec((tc, LANES), lambda i, p: (i, 0)),
                  pl.BlockSpec((1, d), lambda i, p: (0, 0)),
                  pl.BlockSpec(memory_space=pl.ANY)],
        out_specs=out_specs,
        scratch_shapes=[pltpu.VMEM((2, TOP_K, tc, d), F32), pltpu.SemaphoreType.DMA((2,))],
    )
    return pl.pallas_call(
        functools.partial(_combine_kernel, n=n, emit_x=emit_x, emit_norm=norm_dtype is not None),
        grid_spec=grid_spec,
        out_shape=out_shape,
        compiler_params=pltpu.CompilerParams(dimension_semantics=("arbitrary",)),
        name="moe_combine",
    )(pos, x, meta, g.reshape(1, d), y)


def kernel(x, ret_w_in, ret_norm_g, ret_w_out, pool_w, pool_scale, ffn_w_gu, ffn_w_down,
           moe_router, moe_w_gu, moe_w_down, norm_mix_g, norm_ffn_g, norm_final_g):
    batch, seq, d = x.shape
    n = batch * seq
    n_moe, n_exp, _, f2 = moe_w_gu.shape
    f = moe_w_down.shape[2]
    assert d == D_MODEL and n_exp == N_EXPERTS and seq % RET_CHUNK == 0 and seq >= POOL_HALO
    assert norm_mix_g.shape[0] == DEPTH and DEPTH % 2 == 0, "the trunk ends on a pooling / MoE layer"
    xf = x.reshape(n, d)
    pool_w = pool_w.astype(BF16)
    moe_gu_rows = moe_w_gu.reshape(n_moe, n_exp * d, f2)
    moe_down_rows = moe_w_down.reshape(n_moe, n_exp * f, d)
    hn = rmsnorm(xf, norm_mix_g[0], BF16)
    for i in range(DEPTH):
        j = i // 2
        last = i == DEPTH - 1
        if i % 2 == 0:
            proj, moe_gu_b, w_out_b = retention_in_proj(hn, ret_w_in, j,
                                                        [(moe_gu_rows, j), (ret_w_out, j)])
            y = retention(proj, ret_norm_g, j, batch, seq)
            gate_block = (2 * RET_QK + RET_V) // RET_V
            xf, xg, ssq = down_residual(y, w_out_b, xf, norm_ffn_g[i], head_gate=(proj, gate_block))
            hid, moe_down_b, ffn_down_b = swiglu_up(xg, ssq, ffn_w_gu, j,
                                                    [(moe_down_rows, j), (ffn_w_down, j)])
            (xf,) = down_residual(hid, ffn_down_b, xf)
            moe_gu_b = moe_gu_b.reshape(n_exp, d, f2)
            moe_down_b = moe_down_b.reshape(n_exp, f, d)
        else:
            xf, hn, meta, meta_t, counts = pool_router(xf, norm_mix_g[i], pool_w, j, pool_scale[j],
                                                       norm_ffn_g[i], moe_router[j], seq)
            pos, tile_table, pad_lo, pad_hi, used = moe_schedule(meta_t, counts, n)
            xs = moe_dispatch(hn, pos, pad_lo, pad_hi, used)
            ys = moe_experts(xs, moe_gu_b, moe_down_b, tile_table)
            if last:
                (out,) = moe_combine(xf, ys, meta, pos, norm_final_g, emit_x=False, norm_dtype=F32)
            else:
                xf, hn = moe_combine(xf, ys, meta, pos, norm_mix_g[i + 1], norm_dtype=BF16)
    return out.reshape(batch, seq, d)
```
